```python
import jax, jax.numpy as jnp
from jax import lax
import numpy as np

D_MODEL = 1024
BATCH = 8
SEQ = 8192
DEPTH = 4

N_EVEN = (DEPTH + 1) // 2
N_ODD = DEPTH // 2
D_A = D_MODEL // 2
A_HEADS = 8
CONV_WIDTH_A = 3
POOL_WINDOWS = (2, 4, 8, 16)
D_B = D_MODEL // 2
B_GROUP = D_B // len(POOL_WINDOWS)
D_IN_AB = 3 * D_A + D_B
CONF_KERNEL = 31
D_FF = 256 * (-(-(8 * D_MODEL) // (3 * 256)))
N_MOD = 6
RMS_EPS = 1e-6
LN_EPS = 1e-5

kernel_name = "hybrid_conv_pool_conformer_encoder"


def rms_norm(x, g):
    xf = x.astype(jnp.float32)
    y = xf * lax.rsqrt(jnp.mean(xf * xf, axis=-1, keepdims=True) + RMS_EPS)
    return (y * g.astype(jnp.float32)).astype(x.dtype)


def layer_norm(x, g, b):
    xf = x.astype(jnp.float32)
    mu = jnp.mean(xf, axis=-1, keepdims=True)
    var = jnp.mean(jnp.square(xf - mu), axis=-1, keepdims=True)
    y = (xf - mu) * lax.rsqrt(var + LN_EPS)
    return (y * g.astype(jnp.float32) + b.astype(jnp.float32)).astype(x.dtype)


def modulate(h, shift, scale):
    return h * (1 + scale[:, None, :]) + shift[:, None, :]


def depthwise_conv(x, w):
    k = w.shape[0]
    left = (k - 1) // 2
    return lax.conv_general_dilated(
        x, w[:, None, :].astype(x.dtype), window_strides=(1,),
        padding=[(left, k - 1 - left)],
        dimension_numbers=('NWC', 'WIO', 'NWC'),
        feature_group_count=x.shape[-1])


def centred_window_mean(x, w):
    L = x.shape[1]
    left = w // 2
    right = w - 1 - left
    xp = jnp.pad(x.astype(jnp.float32), ((0, 0), (left + 1, right), (0, 0)))
    cs = jnp.cumsum(xp, axis=1)
    s = cs[:, w:w + L] - cs[:, :L]
    t = jnp.arange(L)
    cnt = (jnp.minimum(t + right, L - 1) - jnp.maximum(t - left, 0) + 1).astype(jnp.float32)
    return (s / cnt[None, :, None]).astype(x.dtype)


def conv_pool_mixer(h, w_in, conv_a, w_pool, pool_scale, w_out):
    u = jnp.einsum('bsd,de->bse', h, w_in)
    b_gate, c_gate, v, p = jnp.split(u, [D_A, 2 * D_A, 3 * D_A], axis=-1)
    y_a = b_gate * depthwise_conv(c_gate * v, conv_a)
    groups = jnp.split(p, len(POOL_WINDOWS), axis=-1)
    pooled = jnp.stack([centred_window_mean(g, w) - g for g, w in zip(groups, POOL_WINDOWS)],
                       axis=2)
    y_b = jnp.einsum('bsgc,gce->bsge', pooled, w_pool)
    y_b = y_b.reshape(h.shape[0], h.shape[1], D_B) * pool_scale
    return jnp.einsum('bse,ed->bsd', jnp.concatenate([y_a, y_b], axis=-1), w_out)


def conformer_conv(h, w_pw1, b_pw1, w_dw, b_dw, ln_g, ln_b, w_pw2, b_pw2):
    u = jnp.einsum('bsd,de->bse', h, w_pw1) + b_pw1
    a, g = jnp.split(u, 2, axis=-1)
    z = a * jax.nn.sigmoid(g)
    z = depthwise_conv(z, w_dw) + b_dw
    z = jax.nn.silu(layer_norm(z, ln_g, ln_b))
    return jnp.einsum('bsd,de->bse', z, w_pw2) + b_pw2


def swiglu(h, w_gate, w_up, w_down):
    a = jnp.einsum('bsd,df->bsf', h, w_gate)
    b = jnp.einsum('bsd,df->bsf', h, w_up)
    return jnp.einsum('bsf,fd->bsd', jax.nn.silu(a) * b, w_down)


def _fwd_setup_inputs(seed: int = 0) -> dict:
    key = jax.random.key(seed)
    ks = iter(jax.random.split(key, 32))
    D = D_MODEL
    f32 = jnp.float32

    def nrm(shape, scale):
        return jax.random.normal(next(ks), shape, f32) * scale

    return {
        "x": nrm((BATCH, SEQ, D), 1.0),
        "c": nrm((BATCH, D), 1.0),
        "norm_mix_g": 1.0 + nrm((DEPTH, D), 0.05),
        "norm_ffn_g": 1.0 + nrm((DEPTH, D), 0.05),
        "w_mod": nrm((DEPTH, D, N_MOD * D), 0.5 * D ** -0.5),
        "b_mod": nrm((DEPTH, N_MOD * D), 0.02),
        "ab_w_in": nrm((N_EVEN, D, D_IN_AB), D ** -0.5),
        "ab_conv": nrm((N_EVEN, CONV_WIDTH_A, D_A), CONV_WIDTH_A ** -0.5),
        "ab_w_pool": nrm((N_EVEN, len(POOL_WINDOWS), B_GROUP, B_GROUP), B_GROUP ** -0.5),
        "ab_pool_scale": 1.0 + nrm((N_EVEN, D_B), 0.1),
        "ab_w_out": nrm((N_EVEN, D_A + D_B, D), (D_A + D_B) ** -0.5),
        "cf_w_pw1": nrm((N_ODD, D, 2 * D), D ** -0.5),
        "cf_b_pw1": nrm((N_ODD, 2 * D), 0.02),
        "cf_w_dw": nrm((N_ODD, CONF_KERNEL, D), CONF_KERNEL ** -0.5),
        "cf_b_dw": nrm((N_ODD, D), 0.02),
        "cf_ln_g": 1.0 + nrm((N_ODD, D), 0.05),
        "cf_ln_b": nrm((N_ODD, D), 0.02),
        "cf_w_pw2": nrm((N_ODD, D, D), D ** -0.5),
        "cf_b_pw2": nrm((N_ODD, D), 0.02),
        "ffn_w_gate": nrm((DEPTH, D, D_FF), D ** -0.5),
        "ffn_w_up": nrm((DEPTH, D, D_FF), D ** -0.5),
        "ffn_w_down": nrm((DEPTH, D_FF, D), D_FF ** -0.5),
        "final_norm_g": 1.0 + nrm((D,), 0.05),
    }


def _fwd_reference(x, c, norm_mix_g, norm_ffn_g, w_mod, b_mod,
              ab_w_in, ab_conv, ab_w_pool, ab_pool_scale, ab_w_out,
              cf_w_pw1, cf_b_pw1, cf_w_dw, cf_b_dw, cf_ln_g, cf_ln_b, cf_w_pw2, cf_b_pw2,
              ffn_w_gate, ffn_w_up, ffn_w_down, final_norm_g):
    c_act = jax.nn.silu(c)
    for layer in range(DEPTH):
        mod = jnp.einsum('bd,de->be', c_act, w_mod[layer]) + b_mod[layer]
        sh1, sc1, g1, sh2, sc2, g2 = jnp.split(mod, N_MOD, axis=-1)
        h = modulate(rms_norm(x, norm_mix_g[layer]), sh1, sc1)
        i = layer // 2
        if layer % 2 == 0:
            y = conv_pool_mixer(h, ab_w_in[i], ab_conv[i], ab_w_pool[i],
                                ab_pool_scale[i], ab_w_out[i])
        else:
            y = conformer_conv(h, cf_w_pw1[i], cf_b_pw1[i], cf_w_dw[i], cf_b_dw[i],
                               cf_ln_g[i], cf_ln_b[i], cf_w_pw2[i], cf_b_pw2[i])
        x = x + g1[:, None, :] * y
        h = modulate(rms_norm(x, norm_ffn_g[layer]), sh2, sc2)
        x = x + g2[:, None, :] * swiglu(h, ffn_w_gate[layer], ffn_w_up[layer], ffn_w_down[layer])
    return rms_norm(x, final_norm_g)


import jax as _jax
import jax.numpy as _jnp

TWIN_FORMAT = 'train_step'
FWD_PARAMS = ['x', 'c', 'norm_mix_g', 'norm_ffn_g', 'w_mod', 'b_mod', 'ab_w_in', 'ab_conv', 'ab_w_pool', 'ab_pool_scale', 'ab_w_out', 'cf_w_pw1', 'cf_b_pw1', 'cf_w_dw', 'cf_b_dw', 'cf_ln_g', 'cf_ln_b', 'cf_w_pw2', 'cf_b_pw2', 'ffn_w_gate', 'ffn_w_up', 'ffn_w_down', 'final_norm_g']
TWIN_WEIGHTS = ['norm_mix_g', 'norm_ffn_g', 'w_mod', 'b_mod', 'ab_w_in', 'ab_conv', 'ab_w_pool', 'ab_pool_scale', 'ab_w_out', 'cf_w_pw1', 'cf_b_pw1', 'cf_w_dw', 'cf_b_dw', 'cf_ln_g', 'cf_ln_b', 'cf_w_pw2', 'cf_b_pw2', 'ffn_w_gate', 'ffn_w_up', 'ffn_w_down', 'final_norm_g']
TWIN_DIFF_INPUT = 'x'
TWIN_INPUTS = ['x', 'c', 'norm_mix_g', 'norm_ffn_g', 'w_mod', 'b_mod', 'ab_w_in', 'ab_conv', 'ab_w_pool', 'ab_pool_scale', 'ab_w_out', 'cf_w_pw1', 'cf_b_pw1', 'cf_w_dw', 'cf_b_dw', 'cf_ln_g', 'cf_ln_b', 'cf_w_pw2', 'cf_b_pw2', 'ffn_w_gate', 'ffn_w_up', 'ffn_w_down', 'final_norm_g', 'loss_target', 'm_norm_mix_g', 'm_norm_ffn_g', 'm_w_mod', 'm_b_mod', 'm_ab_w_in', 'm_ab_conv', 'm_ab_w_pool', 'm_ab_pool_scale', 'm_ab_w_out', 'm_cf_w_pw1', 'm_cf_b_pw1', 'm_cf_w_dw', 'm_cf_b_dw', 'm_cf_ln_g', 'm_cf_ln_b', 'm_cf_w_pw2', 'm_cf_b_pw2', 'm_ffn_w_gate', 'm_ffn_w_up', 'm_ffn_w_down', 'm_final_norm_g', 'v_norm_mix_g', 'v_norm_ffn_g', 'v_w_mod', 'v_b_mod', 'v_ab_w_in', 'v_ab_conv', 'v_ab_w_pool', 'v_ab_pool_scale', 'v_ab_w_out', 'v_cf_w_pw1', 'v_cf_b_pw1', 'v_cf_w_dw', 'v_cf_b_dw', 'v_cf_ln_g', 'v_cf_ln_b', 'v_cf_w_pw2', 'v_cf_b_pw2', 'v_ffn_w_gate', 'v_ffn_w_up', 'v_ffn_w_down', 'v_final_norm_g']
TWIN_OUTPUTS = ['loss', 'grad_x', 'grad_norm_mix_g', 'grad_norm_ffn_g', 'grad_w_mod', 'grad_b_mod', 'grad_ab_w_in', 'grad_ab_conv', 'grad_ab_w_pool', 'grad_ab_pool_scale', 'grad_ab_w_out', 'grad_cf_w_pw1', 'grad_cf_b_pw1', 'grad_cf_w_dw', 'grad_cf_b_dw', 'grad_cf_ln_g', 'grad_cf_ln_b', 'grad_cf_w_pw2', 'grad_cf_b_pw2', 'grad_ffn_w_gate', 'grad_ffn_w_up', 'grad_ffn_w_down', 'grad_final_norm_g', 'delta_norm_mix_g', 'delta_norm_ffn_g', 'delta_w_mod', 'delta_b_mod', 'delta_ab_w_in', 'delta_ab_conv', 'delta_ab_w_pool', 'delta_ab_pool_scale', 'delta_ab_w_out', 'delta_cf_w_pw1', 'delta_cf_b_pw1', 'delta_cf_w_dw', 'delta_cf_b_dw', 'delta_cf_ln_g', 'delta_cf_ln_b', 'delta_cf_w_pw2', 'delta_cf_b_pw2', 'delta_ffn_w_gate', 'delta_ffn_w_up', 'delta_ffn_w_down', 'delta_final_norm_g', 'new_m_norm_mix_g', 'new_m_norm_ffn_g', 'new_m_w_mod', 'new_m_b_mod', 'new_m_ab_w_in', 'new_m_ab_conv', 'new_m_ab_w_pool', 'new_m_ab_pool_scale', 'new_m_ab_w_out', 'new_m_cf_w_pw1', 'new_m_cf_b_pw1', 'new_m_cf_w_dw', 'new_m_cf_b_dw', 'new_m_cf_ln_g', 'new_m_cf_ln_b', 'new_m_cf_w_pw2', 'new_m_cf_b_pw2', 'new_m_ffn_w_gate', 'new_m_ffn_w_up', 'new_m_ffn_w_down', 'new_m_final_norm_g', 'new_v_norm_mix_g', 'new_v_norm_ffn_g', 'new_v_w_mod', 'new_v_b_mod', 'new_v_ab_w_in', 'new_v_ab_conv', 'new_v_ab_w_pool', 'new_v_ab_pool_scale', 'new_v_ab_w_out', 'new_v_cf_w_pw1', 'new_v_cf_b_pw1', 'new_v_cf_w_dw', 'new_v_cf_b_dw', 'new_v_cf_ln_g', 'new_v_cf_ln_b', 'new_v_cf_w_pw2', 'new_v_cf_b_pw2', 'new_v_ffn_w_gate', 'new_v_ffn_w_up', 'new_v_ffn_w_down', 'new_v_final_norm_g']
TWIN_LEAF_KINDS = {'loss': 'loss', 'grad_x': 'grad_x', 'grad_norm_mix_g': 'grad_w', 'grad_norm_ffn_g': 'grad_w', 'grad_w_mod': 'grad_w', 'grad_b_mod': 'grad_w', 'grad_ab_w_in': 'grad_w', 'grad_ab_conv': 'grad_w', 'grad_ab_w_pool': 'grad_w', 'grad_ab_pool_scale': 'grad_w', 'grad_ab_w_out': 'grad_w', 'grad_cf_w_pw1': 'grad_w', 'grad_cf_b_pw1': 'grad_w', 'grad_cf_w_dw': 'grad_w', 'grad_cf_b_dw': 'grad_w', 'grad_cf_ln_g': 'grad_w', 'grad_cf_ln_b': 'grad_w', 'grad_cf_w_pw2': 'grad_w', 'grad_cf_b_pw2': 'grad_w', 'grad_ffn_w_gate': 'grad_w', 'grad_ffn_w_up': 'grad_w', 'grad_ffn_w_down': 'grad_w', 'grad_final_norm_g': 'grad_w', 'delta_norm_mix_g': 'delta_w', 'delta_norm_ffn_g': 'delta_w', 'delta_w_mod': 'delta_w', 'delta_b_mod': 'delta_w', 'delta_ab_w_in': 'delta_w', 'delta_ab_conv': 'delta_w', 'delta_ab_w_pool': 'delta_w', 'delta_ab_pool_scale': 'delta_w', 'delta_ab_w_out': 'delta_w', 'delta_cf_w_pw1': 'delta_w', 'delta_cf_b_pw1': 'delta_w', 'delta_cf_w_dw': 'delta_w', 'delta_cf_b_dw': 'delta_w', 'delta_cf_ln_g': 'delta_w', 'delta_cf_ln_b': 'delta_w', 'delta_cf_w_pw2': 'delta_w', 'delta_cf_b_pw2': 'delta_w', 'delta_ffn_w_gate': 'delta_w', 'delta_ffn_w_up': 'delta_w', 'delta_ffn_w_down': 'delta_w', 'delta_final_norm_g': 'delta_w', 'new_m_norm_mix_g': 'new_m', 'new_m_norm_ffn_g': 'new_m', 'new_m_w_mod': 'new_m', 'new_m_b_mod': 'new_m', 'new_m_ab_w_in': 'new_m', 'new_m_ab_conv': 'new_m', 'new_m_ab_w_pool': 'new_m', 'new_m_ab_pool_scale': 'new_m', 'new_m_ab_w_out': 'new_m', 'new_m_cf_w_pw1': 'new_m', 'new_m_cf_b_pw1': 'new_m', 'new_m_cf_w_dw': 'new_m', 'new_m_cf_b_dw': 'new_m', 'new_m_cf_ln_g': 'new_m', 'new_m_cf_ln_b': 'new_m', 'new_m_cf_w_pw2': 'new_m', 'new_m_cf_b_pw2': 'new_m', 'new_m_ffn_w_gate': 'new_m', 'new_m_ffn_w_up': 'new_m', 'new_m_ffn_w_down': 'new_m', 'new_m_final_norm_g': 'new_m', 'new_v_norm_mix_g': 'new_v', 'new_v_norm_ffn_g': 'new_v', 'new_v_w_mod': 'new_v', 'new_v_b_mod': 'new_v', 'new_v_ab_w_in': 'new_v', 'new_v_ab_conv': 'new_v', 'new_v_ab_w_pool': 'new_v', 'new_v_ab_pool_scale': 'new_v', 'new_v_ab_w_out': 'new_v', 'new_v_cf_w_pw1': 'new_v', 'new_v_cf_b_pw1': 'new_v', 'new_v_cf_w_dw': 'new_v', 'new_v_cf_b_dw': 'new_v', 'new_v_cf_ln_g': 'new_v', 'new_v_cf_ln_b': 'new_v', 'new_v_cf_w_pw2': 'new_v', 'new_v_cf_b_pw2': 'new_v', 'new_v_ffn_w_gate': 'new_v', 'new_v_ffn_w_up': 'new_v', 'new_v_ffn_w_down': 'new_v', 'new_v_final_norm_g': 'new_v'}


def _forward(args):
    return _fwd_reference(*[args[k] for k in FWD_PARAMS])


def _output_shape():
    def fwd():
        inp = _fwd_setup_inputs(0)
        return _fwd_reference(*[inp[k] for k in FWD_PARAMS])
    out = _jax.eval_shape(fwd)
    return out.shape, out.dtype

N_MICROBATCH = 1
ADAM_LR = 0.001
ADAM_B1 = 0.9
ADAM_B2 = 0.999
ADAM_EPS = 1e-08
ADAM_WD = 0.01
ADAM_STEP = 10
PER_EXAMPLE_BATCH_AXIS = {'x': 0, 'c': 0, 'loss_target': 0}
SHARED_INPUTS = []
_WEIGHT_DTYPES = {'norm_mix_g': _jnp.float32, 'norm_ffn_g': _jnp.float32, 'w_mod': _jnp.float32, 'b_mod': _jnp.float32, 'ab_w_in': _jnp.float32, 'ab_conv': _jnp.float32, 'ab_w_pool': _jnp.float32, 'ab_pool_scale': _jnp.float32, 'ab_w_out': _jnp.float32, 'cf_w_pw1': _jnp.float32, 'cf_b_pw1': _jnp.float32, 'cf_w_dw': _jnp.float32, 'cf_b_dw': _jnp.float32, 'cf_ln_g': _jnp.float32, 'cf_ln_b': _jnp.float32, 'cf_w_pw2': _jnp.float32, 'cf_b_pw2': _jnp.float32, 'ffn_w_gate': _jnp.float32, 'ffn_w_up': _jnp.float32, 'ffn_w_down': _jnp.float32, 'final_norm_g': _jnp.float32}
MOMENT_SCALE = {'norm_mix_g': 9.239856e-02, 'norm_ffn_g': 7.043462e-02, 'w_mod': 1.047082e-01, 'b_mod': 1.957711e-01, 'ab_w_in': 9.045340e-02, 'ab_conv': 9.753053e-02, 'ab_w_pool': 7.023006e-02, 'ab_pool_scale': 7.023616e-02, 'ab_w_out': 8.528318e-02, 'cf_w_pw1': 3.346746e-02, 'cf_b_pw1': 3.507827e-02, 'cf_w_dw': 4.343235e-02, 'cf_b_dw': 8.701803e-02, 'cf_ln_g': 5.291030e-02, 'cf_ln_b': 5.296556e-02, 'cf_w_pw2': 4.415566e-02, 'cf_b_pw2': 9.247948e-02, 'ffn_w_gate': 3.148811e-02, 'ffn_w_up': 3.045876e-02, 'ffn_w_down': 5.063283e-02, 'final_norm_g': 6.427724e+01}


def _to_microbatches(a, axis):
    t = _jnp.moveaxis(a, axis, 0)
    t = t.reshape((N_MICROBATCH, t.shape[0] // N_MICROBATCH) + t.shape[1:])
    return _jnp.moveaxis(t, 1, axis + 1)


def setup_inputs(seed: int = 0) -> dict:
    inp = _fwd_setup_inputs(seed)
    key = _jax.random.fold_in(_jax.random.key(seed), 7919)
    shape, _ = _output_shape()
    out = dict(inp)
    out["loss_target"] = _jax.random.normal(_jax.random.fold_in(key, 0), shape, _jnp.float32)
    for i, name in enumerate(TWIN_WEIGHTS):
        w = inp[name].astype(_jnp.float32)
        if MOMENT_SCALE is None:
            s = _jnp.sqrt(_jnp.mean(_jnp.square(w)) + 1e-30)
        else:
            s = MOMENT_SCALE[name]
        km, kv = _jax.random.split(_jax.random.fold_in(key, i + 1))
        out[name] = w
        out["m_" + name] = s * _jax.random.normal(km, w.shape, _jnp.float32)
        out["v_" + name] = (s * s) * _jax.random.uniform(kv, w.shape, _jnp.float32, 0.5, 1.5)
    if N_MICROBATCH > 1:
        for name, axis in PER_EXAMPLE_BATCH_AXIS.items():
            out[name] = _to_microbatches(out[name], axis)
    return {'x': out['x'], 'c': out['c'], 'norm_mix_g': out['norm_mix_g'], 'norm_ffn_g': out['norm_ffn_g'], 'w_mod': out['w_mod'], 'b_mod': out['b_mod'], 'ab_w_in': out['ab_w_in'], 'ab_conv': out['ab_conv'], 'ab_w_pool': out['ab_w_pool'], 'ab_pool_scale': out['ab_pool_scale'], 'ab_w_out': out['ab_w_out'], 'cf_w_pw1': out['cf_w_pw1'], 'cf_b_pw1': out['cf_b_pw1'], 'cf_w_dw': out['cf_w_dw'], 'cf_b_dw': out['cf_b_dw'], 'cf_ln_g': out['cf_ln_g'], 'cf_ln_b': out['cf_ln_b'], 'cf_w_pw2': out['cf_w_pw2'], 'cf_b_pw2': out['cf_b_pw2'], 'ffn_w_gate': out['ffn_w_gate'], 'ffn_w_up': out['ffn_w_up'], 'ffn_w_down': out['ffn_w_down'], 'final_norm_g': out['final_norm_g'], 'loss_target': out['loss_target'], 'm_norm_mix_g': out['m_norm_mix_g'], 'm_norm_ffn_g': out['m_norm_ffn_g'], 'm_w_mod': out['m_w_mod'], 'm_b_mod': out['m_b_mod'], 'm_ab_w_in': out['m_ab_w_in'], 'm_ab_conv': out['m_ab_conv'], 'm_ab_w_pool': out['m_ab_w_pool'], 'm_ab_pool_scale': out['m_ab_pool_scale'], 'm_ab_w_out': out['m_ab_w_out'], 'm_cf_w_pw1': out['m_cf_w_pw1'], 'm_cf_b_pw1': out['m_cf_b_pw1'], 'm_cf_w_dw': out['m_cf_w_dw'], 'm_cf_b_dw': out['m_cf_b_dw'], 'm_cf_ln_g': out['m_cf_ln_g'], 'm_cf_ln_b': out['m_cf_ln_b'], 'm_cf_w_pw2': out['m_cf_w_pw2'], 'm_cf_b_pw2': out['m_cf_b_pw2'], 'm_ffn_w_gate': out['m_ffn_w_gate'], 'm_ffn_w_up': out['m_ffn_w_up'], 'm_ffn_w_down': out['m_ffn_w_down'], 'm_final_norm_g': out['m_final_norm_g'], 'v_norm_mix_g': out['v_norm_mix_g'], 'v_norm_ffn_g': out['v_norm_ffn_g'], 'v_w_mod': out['v_w_mod'], 'v_b_mod': out['v_b_mod'], 'v_ab_w_in': out['v_ab_w_in'], 'v_ab_conv': out['v_ab_conv'], 'v_ab_w_pool': out['v_ab_w_pool'], 'v_ab_pool_scale': out['v_ab_pool_scale'], 'v_ab_w_out': out['v_ab_w_out'], 'v_cf_w_pw1': out['v_cf_w_pw1'], 'v_cf_b_pw1': out['v_cf_b_pw1'], 'v_cf_w_dw': out['v_cf_w_dw'], 'v_cf_b_dw': out['v_cf_b_dw'], 'v_cf_ln_g': out['v_cf_ln_g'], 'v_cf_ln_b': out['v_cf_ln_b'], 'v_cf_w_pw2': out['v_cf_w_pw2'], 'v_cf_b_pw2': out['v_cf_b_pw2'], 'v_ffn_w_gate': out['v_ffn_w_gate'], 'v_ffn_w_up': out['v_ffn_w_up'], 'v_ffn_w_down': out['v_ffn_w_down'], 'v_final_norm_g': out['v_final_norm_g']}


def _loss(weights, diff, rest, loss_target):
    with _jax.named_scope("forward"):
        args = {**rest, TWIN_DIFF_INPUT: diff, **{k: w.astype(_WEIGHT_DTYPES[k]) for k, w in weights.items()}}
        y = _forward(args)
    with _jax.named_scope("loss_head"):
        err = _jnp.square(y.astype(_jnp.float32) - loss_target)
        return 0.5 * _jnp.sum(_jnp.mean(err, axis=-1)) if err.ndim else 0.5 * err


def _adamw(w, g, m, v):
    m = ADAM_B1 * m + (1.0 - ADAM_B1) * g
    v = ADAM_B2 * v + (1.0 - ADAM_B2) * _jnp.square(g)
    m_hat = m / (1.0 - ADAM_B1 ** ADAM_STEP)
    v_hat = v / (1.0 - ADAM_B2 ** ADAM_STEP)
    delta = -ADAM_LR * (m_hat / (_jnp.sqrt(v_hat) + ADAM_EPS) + ADAM_WD * w)
    return delta, m, v


def reference(x, c, norm_mix_g, norm_ffn_g, w_mod, b_mod, ab_w_in, ab_conv, ab_w_pool, ab_pool_scale, ab_w_out, cf_w_pw1, cf_b_pw1, cf_w_dw, cf_b_dw, cf_ln_g, cf_ln_b, cf_w_pw2, cf_b_pw2, ffn_w_gate, ffn_w_up, ffn_w_down, final_norm_g, loss_target, m_norm_mix_g, m_norm_ffn_g, m_w_mod, m_b_mod, m_ab_w_in, m_ab_conv, m_ab_w_pool, m_ab_pool_scale, m_ab_w_out, m_cf_w_pw1, m_cf_b_pw1, m_cf_w_dw, m_cf_b_dw, m_cf_ln_g, m_cf_ln_b, m_cf_w_pw2, m_cf_b_pw2, m_ffn_w_gate, m_ffn_w_up, m_ffn_w_down, m_final_norm_g, v_norm_mix_g, v_norm_ffn_g, v_w_mod, v_b_mod, v_ab_w_in, v_ab_conv, v_ab_w_pool, v_ab_pool_scale, v_ab_w_out, v_cf_w_pw1, v_cf_b_pw1, v_cf_w_dw, v_cf_b_dw, v_cf_ln_g, v_cf_ln_b, v_cf_w_pw2, v_cf_b_pw2, v_ffn_w_gate, v_ffn_w_up, v_ffn_w_down, v_final_norm_g):
    given = dict(x=x, c=c, norm_mix_g=norm_mix_g, norm_ffn_g=norm_ffn_g, w_mod=w_mod, b_mod=b_mod, ab_w_in=ab_w_in, ab_conv=ab_conv, ab_w_pool=ab_w_pool, ab_pool_scale=ab_pool_scale, ab_w_out=ab_w_out, cf_w_pw1=cf_w_pw1, cf_b_pw1=cf_b_pw1, cf_w_dw=cf_w_dw, cf_b_dw=cf_b_dw, cf_ln_g=cf_ln_g, cf_ln_b=cf_ln_b, cf_w_pw2=cf_w_pw2, cf_b_pw2=cf_b_pw2, ffn_w_gate=ffn_w_gate, ffn_w_up=ffn_w_up, ffn_w_down=ffn_w_down, final_norm_g=final_norm_g, loss_target=loss_target, m_norm_mix_g=m_norm_mix_g, m_norm_ffn_g=m_norm_ffn_g, m_w_mod=m_w_mod, m_b_mod=m_b_mod, m_ab_w_in=m_ab_w_in, m_ab_conv=m_ab_conv, m_ab_w_pool=m_ab_w_pool, m_ab_pool_scale=m_ab_pool_scale, m_ab_w_out=m_ab_w_out, m_cf_w_pw1=m_cf_w_pw1, m_cf_b_pw1=m_cf_b_pw1, m_cf_w_dw=m_cf_w_dw, m_cf_b_dw=m_cf_b_dw, m_cf_ln_g=m_cf_ln_g, m_cf_ln_b=m_cf_ln_b, m_cf_w_pw2=m_cf_w_pw2, m_cf_b_pw2=m_cf_b_pw2, m_ffn_w_gate=m_ffn_w_gate, m_ffn_w_up=m_ffn_w_up, m_ffn_w_down=m_ffn_w_down, m_final_norm_g=m_final_norm_g, v_norm_mix_g=v_norm_mix_g, v_norm_ffn_g=v_norm_ffn_g, v_w_mod=v_w_mod, v_b_mod=v_b_mod, v_ab_w_in=v_ab_w_in, v_ab_conv=v_ab_conv, v_ab_w_pool=v_ab_w_pool, v_ab_pool_scale=v_ab_pool_scale, v_ab_w_out=v_ab_w_out, v_cf_w_pw1=v_cf_w_pw1, v_cf_b_pw1=v_cf_b_pw1, v_cf_w_dw=v_cf_w_dw, v_cf_b_dw=v_cf_b_dw, v_cf_ln_g=v_cf_ln_g, v_cf_ln_b=v_cf_ln_b, v_cf_w_pw2=v_cf_w_pw2, v_cf_b_pw2=v_cf_b_pw2, v_ffn_w_gate=v_ffn_w_gate, v_ffn_w_up=v_ffn_w_up, v_ffn_w_down=v_ffn_w_down, v_final_norm_g=v_final_norm_g)
    weights = {n: given[n] for n in TWIN_WEIGHTS}
    shared = {n: given[n] for n in SHARED_INPUTS}
    per_example = {n: given[n] for n in ['x', 'c']}
    grad_fn = _jax.value_and_grad(_loss, argnums=(0, 1))

    def one_microbatch(ex, loss_target):
        ex = dict(ex)
        diff = ex.pop(TWIN_DIFF_INPUT)
        return grad_fn(weights, diff, {**shared, **ex}, loss_target)

    if N_MICROBATCH == 1:
        loss, (grad_w, grad_x) = one_microbatch(per_example, given["loss_target"])
    else:
        def body(carry, xs):
            loss_sum, grad_sum = carry
            l_k, (gw_k, gx_k) = one_microbatch(xs[0], xs[1])
            with _jax.named_scope("update"):
                return (loss_sum + l_k, _jax.tree.map(_jnp.add, grad_sum, gw_k)), gx_k

        init = (_jnp.zeros((), _jnp.float32), _jax.tree.map(_jnp.zeros_like, weights))
        (loss, grad_w), grad_x = _jax.lax.scan(body, init, (per_example, given["loss_target"]))
    with _jax.named_scope("update"):
        delta_w, new_m, new_v = {}, {}, {}
        for n in TWIN_WEIGHTS:
            delta_w[n], new_m[n], new_v[n] = _adamw(weights[n], grad_w[n], given["m_" + n], given["v_" + n])
    return (loss, grad_x, *[grad_w[n] for n in TWIN_WEIGHTS], *[delta_w[n] for n in TWIN_WEIGHTS],
            *[new_m[n] for n in TWIN_WEIGHTS], *[new_v[n] for n in TWIN_WEIGHTS])
```

```python
import functools

import jax
import jax.numpy as jnp
from jax import lax
from jax.experimental import pallas as pl
from jax.experimental.pallas import tpu as pltpu

F32 = jnp.float32
BF16 = jnp.bfloat16
MESH = pl.DeviceIdType.MESH

NDEV = 8
DEPTH = 4
D = 1024
FF = 2816
FS = FF // NDEV
DA = 512
PG = 128
POOL = ((2, 1, 0), (4, 2, 1), (8, 4, 3), (16, 8, 7))
CONF_K = 31
CONF_L = 15
N_MOD = 6
MODW = N_MOD * D // NDEV
RMS_EPS = 1e-6
LN_EPS = 1e-5

ROWS_IN, ROWS_OUT = 2 * D // NDEV, D // NDEV
OFF_IN, OFF_OUT = 0, ROWS_IN
OFF_G = OFF_OUT + ROWS_OUT
OFF_U = OFF_G + FS
OFF_D = OFF_U + FS
LROWS = OFF_D + FS

HALO = 16
VMEM_LIMIT = 60 * 1024 * 1024

ADAM_LR, ADAM_B1, ADAM_B2, ADAM_EPS, ADAM_WD, ADAM_STEP = 1e-3, 0.9, 0.999, 1e-8, 0.01, 10
ADAM_C1 = 1.0 / (1.0 - ADAM_B1 ** ADAM_STEP)
ADAM_C2 = 1.0 / (1.0 - ADAM_B2 ** ADAM_STEP)


def _nn(a, b):
    return jnp.dot(a, b, preferred_element_type=F32)


def _nt(a, b):
    return lax.dot_general(a, b, (((1,), (1,)), ((), ())), preferred_element_type=F32)


def _tn(a, b):
    return lax.dot_general(a, b, (((0,), (0,)), ((), ())), preferred_element_type=F32)


def _sig(x):
    return 1.0 / (1.0 + jnp.exp(-x))


def _rowsum(x):
    return jnp.sum(x, axis=0, keepdims=True)


def _lanemean(x):
    return jnp.mean(x, axis=-1, keepdims=True)


def _norm_fwd(x, g, sc, sh):
    r = lax.rsqrt(_lanemean(x * x) + RMS_EPS)
    n = x * r
    return n * (g * (1.0 + sc)) + sh, n, r


def _norm_bwd(dh, n, r, g, sc):
    dn = dh * (g * (1.0 + sc))
    return r * (dn - n * _lanemean(dn * n))


def _adam(w, g, m, v):
    m2 = ADAM_B1 * m + (1.0 - ADAM_B1) * g
    v2 = ADAM_B2 * v + (1.0 - ADAM_B2) * (g * g)
    delta = -ADAM_LR * ((m2 * ADAM_C1) / (jnp.sqrt(v2 * ADAM_C2) + ADAM_EPS) + ADAM_WD * w)
    return delta, m2, v2


def _load_weights(g_ref, layer, specs, sems):
    cps = [pltpu.make_async_copy(g_ref.at[:, layer, pl.ds(off, rows), :], dst, sems.at[k])
           for k, (off, rows, dst) in enumerate(specs)]
    for cp in cps:
        cp.start()
    for cp in cps:
        cp.wait()


def _cparams(*sem):
    return pltpu.CompilerParams(dimension_semantics=sem if sem else None, vmem_limit_bytes=VMEM_LIMIT)


def _tile(tm, w):
    return pl.BlockSpec((tm, w), lambda i: (i, 0))


def _full(shape):
    nd = len(shape)
    return pl.BlockSpec(shape, lambda i: (0,) * nd)


def _halo_specs(tm, w, total_rows):
    tb = tm // HALO
    nb = total_rows // HALO
    prev = pl.BlockSpec((HALO, w), lambda i: (jnp.maximum(i * tb - 1, 0), 0))
    nxt = pl.BlockSpec((HALO, w), lambda i: (jnp.minimum((i + 1) * tb, nb - 1), 0))
    return prev, nxt


ANY = pl.BlockSpec(memory_space=pl.ANY)


def _peers():
    x, y, c = lax.axis_index("x"), lax.axis_index("y"), lax.axis_index("c")
    return x, y, c


def _gather_small(v, name):
    m_per, n = v.shape

    def body(x_ref, out_ref, send_sems, recv_sems, local_sem):
        x, y, c = _peers()
        me, sibling = (x, y, c), (x, y, 1 - c)
        chips = [(1 - x, y), (x, 1 - y), (1 - x, 1 - y)]

        def rows(px, py, pc):
            return out_ref.at[pl.ds((4 * px + 2 * py + pc) * m_per, m_per), :]

        def copy(k, block, to, src=None):
            return pltpu.make_async_remote_copy(
                src_ref=rows(*block) if src is None else src, dst_ref=rows(*block),
                send_sem=send_sems.at[k], recv_sem=recv_sems.at[k], device_id=to, device_id_type=MESH)

        mine = pltpu.make_async_copy(x_ref, rows(*me), local_sem)
        mine.start()
        first = [copy(0, me, sibling, src=x_ref)]
        first += [copy(1 + j, me, (*chip, c), src=x_ref) for j, chip in enumerate(chips)]
        for cp in first:
            cp.start()
        passed = [copy(4 + j, (*chip, c), sibling) for j, chip in enumerate(chips)]
        for j, chip in enumerate(chips):
            copy(1 + j, (*chip, c), me).wait_recv()
            passed[j].start()
        copy(0, sibling, me).wait_recv()
        for j, chip in enumerate(chips):
            copy(4 + j, (*chip, 1 - c), me).wait_recv()
        for cp in first + passed:
            cp.wait_send()
        mine.wait()

    return pl.pallas_call(
        body, name=name,
        out_shape=jax.ShapeDtypeStruct((NDEV * m_per, n), v.dtype),
        in_specs=[pl.BlockSpec(memory_space=pltpu.VMEM)],
        out_specs=pl.BlockSpec(memory_space=pltpu.VMEM),
        scratch_shapes=[pltpu.SemaphoreType.DMA((7,)), pltpu.SemaphoreType.DMA((7,)), pltpu.SemaphoreType.DMA],
        compiler_params=pltpu.CompilerParams(vmem_limit_bytes=VMEM_LIMIT),
    )(v)


def _gather_weights(p):
    def body(p_ref, out_ref, send_sems, recv_sems, local_sem):
        x, y, c = _peers()
        me, sibling = (x, y, c), (x, y, 1 - c)
        chips = [(1 - x, y), (x, 1 - y), (1 - x, 1 - y)]

        def slab(px, py, pc):
            return out_ref.at[4 * px + 2 * py + pc]

        def copy(k, block, to, src=None):
            return pltpu.make_async_remote_copy(
                src_ref=slab(*block) if src is None else src, dst_ref=slab(*block),
                send_sem=send_sems.at[k], recv_sem=recv_sems.at[k], device_id=to, device_id_type=MESH)

        mine = pltpu.make_async_copy(p_ref, slab(*me), local_sem)
        mine.start()
        first = [copy(0, me, sibling, src=p_ref)]
        first += [copy(1 + j, me, (*chip, c), src=p_ref) for j, chip in enumerate(chips)]
        for cp in first:
            cp.start()
        passed = [copy(4 + j, (*chip, c), sibling) for j, chip in enumerate(chips)]
        for j, chip in enumerate(chips):
            copy(1 + j, (*chip, c), me).wait_recv()
            passed[j].start()
        copy(0, sibling, me).wait_recv()
        for j, chip in enumerate(chips):
            copy(4 + j, (*chip, 1 - c), me).wait_recv()
        for cp in first + passed:
            cp.wait_send()
        mine.wait()

    return pl.pallas_call(
        body, name="gather_weights",
        out_shape=jax.ShapeDtypeStruct((NDEV,) + p.shape, p.dtype),
        in_specs=[ANY], out_specs=ANY,
        scratch_shapes=[pltpu.SemaphoreType.DMA((7,)), pltpu.SemaphoreType.DMA((7,)), pltpu.SemaphoreType.DMA],
    )(p)


def _scatter_grads(items):
    arrays = [it[0] for it in items]
    n_in = len(arrays)

    def body(*refs):
        in_refs, out_ref = refs[:n_in], refs[n_in]
        send_sems, recv_sems, local_sem = refs[n_in + 1:]
        x, y, c = _peers()
        me = 4 * x + 2 * y + c

        def part(k, dev):
            _, layer, off, rows = items[k]
            src = in_refs[k].at[pl.ds(pl.multiple_of(dev * rows, 16), rows), :]
            return src, out_ref.at[me, layer, pl.ds(off, rows), :]

        for k in range(n_in):
            src, dst = part(k, me)
            pltpu.make_async_copy(src, dst, local_sem).start()
        for r in range(1, NDEV):
            px = 1 - x if r & 4 else x
            py = 1 - y if r & 2 else y
            pc = 1 - c if r & 1 else c
            for k in range(n_in):
                src, dst = part(k, 4 * px + 2 * py + pc)
                pltpu.make_async_remote_copy(
                    src_ref=src, dst_ref=dst, send_sem=send_sems.at[r - 1], recv_sem=recv_sems.at[r - 1],
                    device_id=(px, py, pc), device_id_type=MESH).start()
        whole = out_ref.at[0]
        for r in range(1, NDEV):
            done = pltpu.make_async_remote_copy(
                src_ref=whole, dst_ref=whole, send_sem=send_sems.at[r - 1], recv_sem=recv_sems.at[r - 1],
                device_id=(x, y, c), device_id_type=MESH)
            done.wait_recv()
            done.wait_send()
        pltpu.make_async_copy(whole, whole, local_sem).wait()

    return pl.pallas_call(
        body, name="scatter_grads",
        out_shape=jax.ShapeDtypeStruct((NDEV, DEPTH, LROWS, D), BF16),
        in_specs=[ANY] * n_in, out_specs=ANY,
        scratch_shapes=[pltpu.SemaphoreType.DMA((7,)), pltpu.SemaphoreType.DMA((7,)), pltpu.SemaphoreType.DMA],
    )(*arrays)


def _mod_fwd(c_all, w_mod, b_sl):
    def body(c_ref, w_ref, b_ref, o_ref, ca_ref):
        cv = c_ref[...]
        ca = cv * _sig(cv)
        ca_ref[...] = ca
        o_ref[...] = jnp.dot(ca, w_ref[0], preferred_element_type=F32, precision=lax.Precision.HIGHEST) + b_ref[0]

    return pl.pallas_call(
        body, name="mod_fwd", grid=(DEPTH,),
        in_specs=[_full((NDEV, D)), pl.BlockSpec((1, D, MODW), lambda l: (l, 0, 0)),
                  pl.BlockSpec((1, 1, MODW), lambda l: (l, 0, 0))],
        out_specs=[pl.BlockSpec((NDEV, MODW), lambda l: (0, l)), _full((NDEV, D))],
        out_shape=[jax.ShapeDtypeStruct((NDEV, DEPTH * MODW), F32), jax.ShapeDtypeStruct((NDEV, D), F32)],
        compiler_params=_cparams("arbitrary"),
    )(c_all, w_mod, b_sl)


def _mod_bwd_adam(ca_t, dmod, w, m, v):
    def body(ct_ref, dm_ref, w_ref, m_ref, v_ref, g_ref, d_ref, mo_ref, vo_ref):
        g = jnp.dot(ct_ref[...], dm_ref[0], preferred_element_type=F32, precision=lax.Precision.HIGHEST)
        delta, m2, v2 = _adam(w_ref[0], g, m_ref[0], v_ref[0])
        g_ref[0], d_ref[0], mo_ref[0], vo_ref[0] = g, delta, m2, v2

    blk = pl.BlockSpec((1, D, MODW), lambda l: (l, 0, 0))
    sds = jax.ShapeDtypeStruct(w.shape, F32)
    return pl.pallas_call(
        body, name="mod_bwd_adam", grid=(DEPTH,),
        in_specs=[_full((D, NDEV)), pl.BlockSpec((1, NDEV, MODW), lambda l: (l, 0, 0)), blk, blk, blk],
        out_specs=[blk] * 4, out_shape=[sds] * 4,
        compiler_params=_cparams("arbitrary"),
    )(ca_t, dmod, w, m, v)


def _inproj(x, vec, bias, gw, layer, tm):
    t = x.shape[0]

    def body(x_ref, vec_ref, b_ref, g_ref, u_ref, w_s, sems):
        @pl.when(pl.program_id(0) == 0)
        def _():
            _load_weights(g_ref, layer, [(OFF_IN, ROWS_IN, w_s)], sems)

        h, _, _ = _norm_fwd(x_ref[...], vec_ref[0:1], vec_ref[1:2], vec_ref[2:3])
        w = w_s[...].reshape(2 * D, D)
        u_ref[...] = (_nt(h.astype(BF16), w) + b_ref[...]).astype(BF16)

    return pl.pallas_call(
        body, name=f"inproj_l{layer}", grid=(t // tm,),
        in_specs=[_tile(tm, D), _full((8, D)), _full((1, 2 * D)), ANY],
        out_specs=_tile(tm, 2 * D), out_shape=jax.ShapeDtypeStruct((t, 2 * D), BF16),
        scratch_shapes=[pltpu.VMEM((NDEV, ROWS_IN, D), BF16), pltpu.SemaphoreType.DMA((1,))],
        compiler_params=_cparams("arbitrary"),
    )(x, vec, bias, gw)


def _fill_even(qe, pe, be, part_ref, lo, rows, valid):
    cg = part_ref[:, DA:2 * DA].astype(F32)
    v = part_ref[:, 2 * DA:3 * DA].astype(F32)
    q = cg * v
    p = part_ref[:, 3 * DA:4 * DA].astype(F32)
    if valid is not None:
        q = jnp.where(valid, q, 0.0)
        p = jnp.where(valid, p, 0.0)
    qe[lo:lo + rows, :] = q
    pe[lo:lo + rows, :] = p
    if be is not None:
        b = part_ref[:, 0:DA].astype(F32)
        be[lo:lo + rows, :] = b if valid is None else jnp.where(valid, b, 0.0)


def _conv3(ca_ref, qe, tm):
    return (ca_ref[0:1] * qe[HALO - 1:HALO - 1 + tm] + ca_ref[1:2] * qe[HALO:HALO + tm]
            + ca_ref[2:3] * qe[HALO + 1:HALO + 1 + tm])


def _pool_counts(t0, rows, first_row, left, right, t):
    tg = t0 + first_row + lax.broadcasted_iota(jnp.int32, (rows, 1), 0)
    cnt = jnp.minimum(tg + right, t - 1) - jnp.maximum(tg - left, 0) + 1
    return jnp.maximum(cnt, 1).astype(F32)


def _pool_minus_id(pe, gi, left, right, inv_cnt, tm):
    c0 = gi * PG
    s = pe[HALO - left:HALO - left + tm, c0:c0 + PG]
    for j in range(-left + 1, right + 1):
        s = s + pe[HALO + j:HALO + j + tm, c0:c0 + PG]
    return s * inv_cnt - pe[HALO:HALO + tm, c0:c0 + PG]


def _mix_even_fwd(u, x, vec, ca, wp, ps, gw, layer, tm):
    t = x.shape[0]
    n = t // tm
    e = tm + 2 * HALO

    def body(u_ref, up_ref, un_ref, x_ref, vec_ref, ca_ref, wp_ref, ps_ref, g_ref, xo_ref, y_ref, w_s, qe, pe, sems):
        i = pl.program_id(0)

        @pl.when(i == 0)
        def _():
            _load_weights(g_ref, layer, [(OFF_OUT, ROWS_OUT, w_s)], sems)

        _fill_even(qe, pe, None, up_ref, 0, HALO, i > 0)
        _fill_even(qe, pe, None, u_ref, HALO, tm, None)
        _fill_even(qe, pe, None, un_ref, HALO + tm, HALO, i < n - 1)
        ya = u_ref[:, 0:DA].astype(F32) * _conv3(ca_ref, qe, tm)
        parts = [ya]
        for gi, (_, left, right) in enumerate(POOL):
            inv = 1.0 / _pool_counts(i * tm, tm, 0, left, right, t)
            pm = _pool_minus_id(pe, gi, left, right, inv, tm)
            parts.append(_nn(pm.astype(BF16), wp_ref[gi]) * ps_ref[0:1, gi * PG:(gi + 1) * PG])
        cat = jnp.concatenate(parts, axis=-1).astype(BF16)
        y = _nn(cat, w_s[...].reshape(D, D))
        y_ref[...] = y.astype(BF16)
        xo_ref[...] = x_ref[...] + vec_ref[3:4] * y

    prev, nxt = _halo_specs(tm, 2 * D, t)
    return pl.pallas_call(
        body, name=f"mix_even_fwd_l{layer}", grid=(n,),
        in_specs=[_tile(tm, 2 * D), prev, nxt, _tile(tm, D), _full((8, D)), _full((3, DA)),
                  _full((4, PG, PG)), _full((1, DA)), ANY],
        out_specs=[_tile(tm, D), _tile(tm, D)],
        out_shape=[jax.ShapeDtypeStruct((t, D), F32), jax.ShapeDtypeStruct((t, D), BF16)],
        scratch_shapes=[pltpu.VMEM((NDEV, ROWS_OUT, D), BF16), pltpu.VMEM((e, DA), F32), pltpu.VMEM((e, DA), F32),
                        pltpu.SemaphoreType.DMA((1,))],
        compiler_params=_cparams("arbitrary"),
    )(u, u, u, x, vec, ca, wp, ps, gw)


def _mix_even_bwd(dxo, u, x, y, vec, ca, wp, ps, gw, layer, tm):
    t = x.shape[0]
    n = t // tm
    e = tm + 2 * HALO

    def body(dxo_ref, dp_ref, dn_ref, u_ref, up_ref, un_ref, x_ref, y_ref, vec_ref, ca_ref, wp_ref, ps_ref, g_ref,
             dxi_ref, du_ref, h_ref, cat_ref, dy_ref, sums_ref, dwp_ref,
             wo_s, wi_s, dye, qe, pe, be, dce, epe, sems):
        i = pl.program_id(0)

        @pl.when(i == 0)
        def _():
            _load_weights(g_ref, layer, [(OFF_OUT, ROWS_OUT, wo_s), (OFF_IN, ROWS_IN, wi_s)], sems)
            sums_ref[...] = jnp.zeros_like(sums_ref)
            dwp_ref[...] = jnp.zeros_like(dwp_ref)

        gate = vec_ref[3:4]
        dxo_m = dxo_ref[...]
        dye[0:HALO, :] = jnp.where(i > 0, gate * dp_ref[...], 0.0).astype(BF16)
        dye[HALO:HALO + tm, :] = (gate * dxo_m).astype(BF16)
        dye[HALO + tm:e, :] = jnp.where(i < n - 1, gate * dn_ref[...], 0.0).astype(BF16)
        _fill_even(qe, pe, be, up_ref, 0, HALO, i > 0)
        _fill_even(qe, pe, be, u_ref, HALO, tm, None)
        _fill_even(qe, pe, be, un_ref, HALO + tm, HALO, i < n - 1)
        sums_ref[0:1, :] += _rowsum(dxo_m * y_ref[...].astype(F32))

        dcat = _nt(dye[...], wo_s[...].reshape(D, D))
        dce[...] = dcat[:, 0:DA] * be[...]
        cq = _conv3(ca_ref, qe, tm)
        bg = be[HALO:HALO + tm]
        dc_m = dce[HALO:HALO + tm]
        dbg = dcat[HALO:HALO + tm, 0:DA] * cq
        dq = (ca_ref[0:1] * dce[HALO + 1:HALO + 1 + tm] + ca_ref[1:2] * dc_m
              + ca_ref[2:3] * dce[HALO - 1:HALO - 1 + tm])
        cg = u_ref[:, DA:2 * DA].astype(F32)
        v = u_ref[:, 2 * DA:3 * DA].astype(F32)
        for k in range(3):
            sums_ref[4 + k:5 + k, 0:DA] += _rowsum(dc_m * qe[HALO - 1 + k:HALO - 1 + k + tm])
        du_parts = [dbg, dq * v, dq * cg]
        cat_parts = [bg * cq]
        for gi, (_, left, right) in enumerate(POOL):
            c0 = gi * PG
            scale = ps_ref[0:1, c0:c0 + PG]
            dyb = dcat[:, DA + c0:DA + c0 + PG]
            dybs = (dyb * scale).astype(BF16)
            dpm = _nt(dybs, wp_ref[gi])
            inv_e = 1.0 / _pool_counts(i * tm, e, -HALO, left, right, t)
            epe[:, c0:c0 + PG] = dpm * inv_e
            s_adj = epe[HALO - right:HALO - right + tm, c0:c0 + PG]
            for j in range(-right + 1, left + 1):
                s_adj = s_adj + epe[HALO + j:HALO + j + tm, c0:c0 + PG]
            du_parts.append(s_adj - dpm[HALO:HALO + tm])
            inv_m = 1.0 / _pool_counts(i * tm, tm, 0, left, right, t)
            pm = _pool_minus_id(pe, gi, left, right, inv_m, tm).astype(BF16)
            ybpre = _nn(pm, wp_ref[gi])
            sums_ref[7:8, c0:c0 + PG] += _rowsum(dyb[HALO:HALO + tm] * ybpre)
            dwp_ref[gi] += _tn(pm, dybs[HALO:HALO + tm])
            cat_parts.append(ybpre * scale)
        du = jnp.concatenate(du_parts, axis=-1).astype(BF16)
        du_ref[...] = du
        cat_ref[...] = jnp.concatenate(cat_parts, axis=-1).astype(BF16)
        dy_ref[...] = dye[HALO:HALO + tm, :]
        dh = _nn(du, wi_s[...].reshape(2 * D, D))
        g, sc, sh = vec_ref[0:1], vec_ref[1:2], vec_ref[2:3]
        h, nrm, r = _norm_fwd(x_ref[...], g, sc, sh)
        h_ref[...] = h.astype(BF16)
        dxi_ref[...] = dxo_m + _norm_bwd(dh, nrm, r, g, sc)
        sums_ref[1:2, :] += _rowsum(dh)
        sums_ref[2:3, :] += _rowsum(dh * nrm)

        @pl.when(i == n - 1)
        def _():
            p = sums_ref[2:3, :]
            sums_ref[3:4, :] = p * (1.0 + sc)
            sums_ref[2:3, :] = p * g

    prev_u, nxt_u = _halo_specs(tm, 2 * D, t)
    prev_d, nxt_d = _halo_specs(tm, D, t)
    return pl.pallas_call(
        body, name=f"mix_even_bwd_l{layer}", grid=(n,),
        in_specs=[_tile(tm, D), prev_d, nxt_d, _tile(tm, 2 * D), prev_u, nxt_u, _tile(tm, D), _tile(tm, D),
                  _full((8, D)), _full((3, DA)), _full((4, PG, PG)), _full((1, DA)), ANY],
        out_specs=[_tile(tm, D), _tile(tm, 2 * D), _tile(tm, D), _tile(tm, D), _tile(tm, D),
                   _full((16, D)), _full((4, PG, PG))],
        out_shape=[jax.ShapeDtypeStruct((t, D), F32), jax.ShapeDtypeStruct((t, 2 * D), BF16),
                   jax.ShapeDtypeStruct((t, D), BF16), jax.ShapeDtypeStruct((t, D), BF16),
                   jax.ShapeDtypeStruct((t, D), BF16), jax.ShapeDtypeStruct((16, D), F32),
                   jax.ShapeDtypeStruct((4, PG, PG), F32)],
        scratch_shapes=[pltpu.VMEM((NDEV, ROWS_OUT, D), BF16), pltpu.VMEM((NDEV, ROWS_IN, D), BF16),
                        pltpu.VMEM((e, D), BF16), pltpu.VMEM((e, DA), F32), pltpu.VMEM((e, DA), F32),
                        pltpu.VMEM((e, DA), F32), pltpu.VMEM((e, DA), F32), pltpu.VMEM((e, DA), F32),
                        pltpu.SemaphoreType.DMA((2,))],
        compiler_params=_cparams("arbitrary"),
    )(dxo, dxo, dxo, u, u, u, x, y, vec, ca, wp, ps, gw)


def _fill_glu(ze, part_ref, lo, rows, valid):
    a = part_ref[:, 0:D].astype(F32)
    g = part_ref[:, D:2 * D].astype(F32)
    z = a * _sig(g)
    ze[lo:lo + rows, :] = z if valid is None else jnp.where(valid, z, 0.0)


def _layer_norm_parts(z2):
    mu = _lanemean(z2)
    d = z2 - mu
    rstd = lax.rsqrt(_lanemean(d * d) + LN_EPS)
    return d * rstd, rstd


def _mix_odd_fwd(u, x, vec, wdw, sm, gw, layer, tm):
    t = x.shape[0]
    n = t // tm
    e = tm + 2 * HALO

    def body(u_ref, up_ref, un_ref, x_ref, vec_ref, wdw_ref, sm_ref, g_ref, xo_ref, y_ref, z2_ref, w_s, ze, sems):
        i = pl.program_id(0)

        @pl.when(i == 0)
        def _():
            _load_weights(g_ref, layer, [(OFF_OUT, ROWS_OUT, w_s)], sems)

        _fill_glu(ze, up_ref, 0, HALO, i > 0)
        _fill_glu(ze, u_ref, HALO, tm, None)
        _fill_glu(ze, un_ref, HALO + tm, HALO, i < n - 1)
        z2 = sm_ref[0:1] + wdw_ref[0:1] * ze[1:1 + tm]
        for k in range(1, CONF_K):
            z2 = z2 + wdw_ref[k:k + 1] * ze[1 + k:1 + k + tm]
        z2_ref[...] = z2.astype(BF16)
        zn, _ = _layer_norm_parts(z2)
        lo = zn * sm_ref[1:2] + sm_ref[2:3]
        z3 = lo * _sig(lo)
        y = _nn(z3.astype(BF16), w_s[...].reshape(D, D)) + sm_ref[3:4]
        y_ref[...] = y.astype(BF16)
        xo_ref[...] = x_ref[...] + vec_ref[3:4] * y

    prev, nxt = _halo_specs(tm, 2 * D, t)
    return pl.pallas_call(
        body, name=f"mix_odd_fwd_l{layer}", grid=(n,),
        in_specs=[_tile(tm, 2 * D), prev, nxt, _tile(tm, D), _full((8, D)), _full((32, D)), _full((8, D)), ANY],
        out_specs=[_tile(tm, D), _tile(tm, D), _tile(tm, D)],
        out_shape=[jax.ShapeDtypeStruct((t, D), F32), jax.ShapeDtypeStruct((t, D), BF16),
                   jax.ShapeDtypeStruct((t, D), BF16)],
        scratch_shapes=[pltpu.VMEM((NDEV, ROWS_OUT, D), BF16), pltpu.VMEM((e, D), F32), pltpu.SemaphoreType.DMA((1,))],
        compiler_params=_cparams("arbitrary"),
    )(u, u, u, x, vec, wdw, sm, gw)


def _mix_odd_bwd1(dxo, y, z2, vec, sm, gw, layer, tm):
    t = dxo.shape[0]
    n = t // tm

    def body(dxo_ref, y_ref, z2_ref, vec_ref, sm_ref, g_ref, dy_ref, z3_ref, dz2_ref, sums_ref, w_s, sems):
        i = pl.program_id(0)

        @pl.when(i == 0)
        def _():
            _load_weights(g_ref, layer, [(OFF_OUT, ROWS_OUT, w_s)], sems)
            sums_ref[...] = jnp.zeros_like(sums_ref)

        dxo_m = dxo_ref[...]
        dy = vec_ref[3:4] * dxo_m
        dyb = dy.astype(BF16)
        dy_ref[...] = dyb
        sums_ref[0:1, :] += _rowsum(dxo_m * y_ref[...].astype(F32))
        sums_ref[4:5, :] += _rowsum(dy)
        dz3 = _nt(dyb, w_s[...].reshape(D, D))
        zn, rstd = _layer_norm_parts(z2_ref[...].astype(F32))
        lo = zn * sm_ref[1:2] + sm_ref[2:3]
        sg = _sig(lo)
        z3_ref[...] = (lo * sg).astype(BF16)
        dlo = dz3 * (sg * (1.0 + lo * (1.0 - sg)))
        sums_ref[5:6, :] += _rowsum(dlo * zn)
        sums_ref[6:7, :] += _rowsum(dlo)
        dzn = dlo * sm_ref[1:2]
        dz2 = rstd * (dzn - _lanemean(dzn) - zn * _lanemean(dzn * zn))
        sums_ref[7:8, :] += _rowsum(dz2)
        dz2_ref[...] = dz2.astype(BF16)

    return pl.pallas_call(
        body, name=f"mix_odd_bwd1_l{layer}", grid=(n,),
        in_specs=[_tile(tm, D), _tile(tm, D), _tile(tm, D), _full((8, D)), _full((8, D)), ANY],
        out_specs=[_tile(tm, D), _tile(tm, D), _tile(tm, D), _full((16, D))],
        out_shape=[jax.ShapeDtypeStruct((t, D), BF16)] * 3 + [jax.ShapeDtypeStruct((16, D), F32)],
        scratch_shapes=[pltpu.VMEM((NDEV, ROWS_OUT, D), BF16), pltpu.SemaphoreType.DMA((1,))],
        compiler_params=_cparams("arbitrary"),
    )(dxo, y, z2, vec, sm, gw)


def _mix_odd_bwd2(dz2, u, x, dxo, vec, wdw, gw, layer, tm):
    t = x.shape[0]
    n = t // tm
    e = tm + 2 * HALO

    def body(dz_ref, dzp_ref, dzn_ref, u_ref, up_ref, un_ref, x_ref, dxo_ref, vec_ref, wdw_ref, g_ref,
             dxi_ref, du_ref, h_ref, sums_ref, dw_ref, w_s, ze, de, sems):
        i = pl.program_id(0)

        @pl.when(i == 0)
        def _():
            _load_weights(g_ref, layer, [(OFF_IN, ROWS_IN, w_s)], sems)
            sums_ref[...] = jnp.zeros_like(sums_ref)
            dw_ref[...] = jnp.zeros_like(dw_ref)

        _fill_glu(ze, up_ref, 0, HALO, i > 0)
        _fill_glu(ze, u_ref, HALO, tm, None)
        _fill_glu(ze, un_ref, HALO + tm, HALO, i < n - 1)
        de[0:HALO, :] = jnp.where(i > 0, dzp_ref[...].astype(F32), 0.0)
        dz2_m = dz_ref[...].astype(F32)
        de[HALO:HALO + tm, :] = dz2_m
        de[HALO + tm:e, :] = jnp.where(i < n - 1, dzn_ref[...].astype(F32), 0.0)
        dz = wdw_ref[0:1] * de[CONF_K:CONF_K + tm]
        for k in range(1, CONF_K):
            dz = dz + wdw_ref[k:k + 1] * de[CONF_K - k:CONF_K - k + tm]
        for k in range(CONF_K):
            dw_ref[k:k + 1, :] += _rowsum(dz2_m * ze[1 + k:1 + k + tm])
        a = u_ref[:, 0:D].astype(F32)
        gg = u_ref[:, D:2 * D].astype(F32)
        sg = _sig(gg)
        da = dz * sg
        dg = dz * a * (sg * (1.0 - sg))
        sums_ref[8:9, :] += _rowsum(da)
        sums_ref[9:10, :] += _rowsum(dg)
        du = jnp.concatenate([da, dg], axis=-1).astype(BF16)
        du_ref[...] = du
        dh = _nn(du, w_s[...].reshape(2 * D, D))
        g, sc, sh = vec_ref[0:1], vec_ref[1:2], vec_ref[2:3]
        h, nrm, r = _norm_fwd(x_ref[...], g, sc, sh)
        h_ref[...] = h.astype(BF16)
        dxi_ref[...] = dxo_ref[...] + _norm_bwd(dh, nrm, r, g, sc)
        sums_ref[1:2, :] += _rowsum(dh)
        sums_ref[2:3, :] += _rowsum(dh * nrm)

        @pl.when(i == n - 1)
        def _():
            p = sums_ref[2:3, :]
            sums_ref[3:4, :] = p * (1.0 + sc)
            sums_ref[2:3, :] = p * g

    prev_u, nxt_u = _halo_specs(tm, 2 * D, t)
    prev_d, nxt_d = _halo_specs(tm, D, t)
    return pl.pallas_call(
        body, name=f"mix_odd_bwd2_l{layer}", grid=(n,),
        in_specs=[_tile(tm, D), prev_d, nxt_d, _tile(tm, 2 * D), prev_u, nxt_u, _tile(tm, D), _tile(tm, D),
                  _full((8, D)), _full((32, D)), ANY],
        out_specs=[_tile(tm, D), _tile(tm, 2 * D), _tile(tm, D), _full((16, D)), _full((32, D))],
        out_shape=[jax.ShapeDtypeStruct((t, D), F32), jax.ShapeDtypeStruct((t, 2 * D), BF16),
                   jax.ShapeDtypeStruct((t, D), BF16), jax.ShapeDtypeStruct((16, D), F32),
                   jax.ShapeDtypeStruct((32, D), F32)],
        scratch_shapes=[pltpu.VMEM((NDEV, ROWS_IN, D), BF16), pltpu.VMEM((e, D), F32), pltpu.VMEM((e, D), F32),
                        pltpu.SemaphoreType.DMA((1,))],
        compiler_params=_cparams("arbitrary"),
    )(dz2, dz2, dz2, u, u, u, x, dxo, vec, wdw, gw)


FCH = FF // 2


def _ffn_fwd(x, vec, gw, layer, tm):
    t = x.shape[0]

    def body(x_ref, vec_ref, g_ref, xo_ref, a_ref, b_ref, y_ref, wg_s, wu_s, wd_s, sems):
        @pl.when(pl.program_id(0) == 0)
        def _():
            _load_weights(g_ref, layer, [(OFF_G, FS, wg_s), (OFF_U, FS, wu_s), (OFF_D, FS, wd_s)], sems)

        xv = x_ref[...]
        h, _, _ = _norm_fwd(xv, vec_ref[0:1], vec_ref[1:2], vec_ref[2:3])
        hb = h.astype(BF16)
        y = jnp.zeros((tm, D), F32)
        for ch in range(2):
            a = _nt(hb, wg_s[4 * ch:4 * ch + 4].reshape(FCH, D))
            b = _nt(hb, wu_s[4 * ch:4 * ch + 4].reshape(FCH, D))
            a_ref[:, ch * FCH:(ch + 1) * FCH] = a.astype(BF16)
            b_ref[:, ch * FCH:(ch + 1) * FCH] = b.astype(BF16)
            s = (a * _sig(a)) * b
            y = y + _nn(s.astype(BF16), wd_s[4 * ch:4 * ch + 4].reshape(FCH, D))
        y_ref[...] = y.astype(BF16)
        xo_ref[...] = xv + vec_ref[3:4] * y

    wsc = pltpu.VMEM((NDEV, FS, D), BF16)
    return pl.pallas_call(
        body, name=f"ffn_fwd_l{layer}", grid=(t // tm,),
        in_specs=[_tile(tm, D), _full((8, D)), ANY],
        out_specs=[_tile(tm, D), _tile(tm, FF), _tile(tm, FF), _tile(tm, D)],
        out_shape=[jax.ShapeDtypeStruct((t, D), F32), jax.ShapeDtypeStruct((t, FF), BF16),
                   jax.ShapeDtypeStruct((t, FF), BF16), jax.ShapeDtypeStruct((t, D), BF16)],
        scratch_shapes=[wsc, wsc, wsc, pltpu.SemaphoreType.DMA((3,))],
        compiler_params=_cparams("arbitrary"),
    )(x, vec, gw)


def _ffn_bwd_hidden(dxo, y, a, b, vec, gw, layer, tm):
    t = dxo.shape[0]

    def body(dxo_ref, y_ref, a_ref, b_ref, vec_ref, g_ref, dy_ref, da_ref, db_ref, sums_ref, wd_s, sems):
        @pl.when(pl.program_id(0) == 0)
        def _():
            _load_weights(g_ref, layer, [(OFF_D, FS, wd_s)], sems)
            sums_ref[...] = jnp.zeros_like(sums_ref)

        dxo_m = dxo_ref[...]
        sums_ref[0:1, :] += _rowsum(dxo_m * y_ref[...].astype(F32))
        dyb = (vec_ref[3:4] * dxo_m).astype(BF16)
        dy_ref[...] = dyb
        for ch in range(2):
            cols = slice(ch * FCH, (ch + 1) * FCH)
            ds = _nt(dyb, wd_s[4 * ch:4 * ch + 4].reshape(FCH, D))
            av = a_ref[:, cols].astype(F32)
            bv = b_ref[:, cols].astype(F32)
            sg = _sig(av)
            db_ref[:, cols] = (ds * (av * sg)).astype(BF16)
            da_ref[:, cols] = (ds * bv * (sg * (1.0 + av * (1.0 - sg)))).astype(BF16)

    return pl.pallas_call(
        body, name=f"ffn_bwd_hidden_l{layer}", grid=(t // tm,),
        in_specs=[_tile(tm, D), _tile(tm, D), _tile(tm, FF), _tile(tm, FF), _full((8, D)), ANY],
        out_specs=[_tile(tm, D), _tile(tm, FF), _tile(tm, FF), _full((8, D))],
        out_shape=[jax.ShapeDtypeStruct((t, D), BF16), jax.ShapeDtypeStruct((t, FF), BF16),
                   jax.ShapeDtypeStruct((t, FF), BF16), jax.ShapeDtypeStruct((8, D), F32)],
        scratch_shapes=[pltpu.VMEM((NDEV, FS, D), BF16), pltpu.SemaphoreType.DMA((1,))],
        compiler_params=_cparams("arbitrary"),
    )(dxo, y, a, b, vec, gw)


def _ffn_bwd_input(da, db, x, dxo, vec, gw, layer, tm):
    t = x.shape[0]
    n = t // tm

    def body(da_ref, db_ref, x_ref, dxo_ref, vec_ref, g_ref, dxi_ref, h_ref, sums_ref, wg_s, wu_s, sems):
        i = pl.program_id(0)

        @pl.when(i == 0)
        def _():
            _load_weights(g_ref, layer, [(OFF_G, FS, wg_s), (OFF_U, FS, wu_s)], sems)
            sums_ref[...] = jnp.zeros_like(sums_ref)

        dh = _nn(da_ref[...], wg_s[...].reshape(FF, D)) + _nn(db_ref[...], wu_s[...].reshape(FF, D))
        g, sc, sh = vec_ref[0:1], vec_ref[1:2], vec_ref[2:3]
        h, nrm, r = _norm_fwd(x_ref[...], g, sc, sh)
        h_ref[...] = h.astype(BF16)
        dxi_ref[...] = dxo_ref[...] + _norm_bwd(dh, nrm, r, g, sc)
        sums_ref[1:2, :] += _rowsum(dh)
        sums_ref[2:3, :] += _rowsum(dh * nrm)

        @pl.when(i == n - 1)
        def _():
            p = sums_ref[2:3, :]
            sums_ref[3:4, :] = p * (1.0 + sc)
            sums_ref[2:3, :] = p * g

    wsc = pltpu.VMEM((NDEV, FS, D), BF16)
    return pl.pallas_call(
        body, name=f"ffn_bwd_input_l{layer}", grid=(n,),
        in_specs=[_tile(tm, FF), _tile(tm, FF), _tile(tm, D), _tile(tm, D), _full((8, D)), ANY],
        out_specs=[_tile(tm, D), _tile(tm, D), _full((8, D))],
        out_shape=[jax.ShapeDtypeStruct((t, D), F32), jax.ShapeDtypeStruct((t, D), BF16),
                   jax.ShapeDtypeStruct((8, D), F32)],
        scratch_shapes=[wsc, wsc, pltpu.SemaphoreType.DMA((2,))],
        compiler_params=_cparams("arbitrary"),
    )(da, db, x, dxo, vec, gw)


def _wgrad(lhs, rhs, name, tk, lhs2=None):
    t, m = lhs.shape
    n = t // tk

    def body(*refs):
        if lhs2 is None:
            l_ref, r_ref, o_ref, acc = refs
            lv = l_ref[...]
        else:
            l_ref, l2_ref, r_ref, o_ref, acc = refs
            av = l_ref[...].astype(F32)
            lv = ((av * _sig(av)) * l2_ref[...].astype(F32)).astype(BF16)
        i = pl.program_id(0)

        @pl.when(i == 0)
        def _():
            acc[...] = jnp.zeros_like(acc)

        acc[...] += _tn(lv, r_ref[...])

        @pl.when(i == n - 1)
        def _():
            o_ref[...] = acc[...].astype(BF16)

    ins = [lhs] + ([] if lhs2 is None else [lhs2]) + [rhs]
    specs = [_tile(tk, m)] * (len(ins) - 1) + [_tile(tk, D)]
    return pl.pallas_call(
        body, name=name, grid=(n,),
        in_specs=specs, out_specs=_full((m, D)), out_shape=jax.ShapeDtypeStruct((m, D), BF16),
        scratch_shapes=[pltpu.VMEM((m, D), F32)],
        compiler_params=_cparams("arbitrary"),
    )(*ins)


def _final(x, tgt, gf, tm):
    t = x.shape[0]

    def body(x_ref, t_ref, g_ref, dx_ref, sums_ref):
        @pl.when(pl.program_id(0) == 0)
        def _():
            sums_ref[...] = jnp.zeros_like(sums_ref)

        xv = x_ref[...]
        r = lax.rsqrt(_lanemean(xv * xv) + RMS_EPS)
        nrm = xv * r
        g = g_ref[...]
        err = nrm * g - t_ref[...]
        sums_ref[1:2, :] += _rowsum(err * err) * (0.5 / D)
        dout = err * (1.0 / D)
        sums_ref[0:1, :] += _rowsum(dout * nrm)
        dn = dout * g
        dx_ref[...] = r * (dn - nrm * _lanemean(dn * nrm))

    return pl.pallas_call(
        body, name="loss_head", grid=(t // tm,),
        in_specs=[_tile(tm, D), _tile(tm, D), _full((1, D))],
        out_specs=[_tile(tm, D), _full((8, D))],
        out_shape=[jax.ShapeDtypeStruct((t, D), F32), jax.ShapeDtypeStruct((8, D), F32)],
        compiler_params=_cparams("arbitrary"),
    )(x, tgt, gf)


ADAM_ROWS = LROWS // 5


def _adam_big(recv, w, m, v):
    def body(r_ref, w_ref, m_ref, v_ref, g_ref, d_ref, mo_ref, vo_ref):
        g = r_ref[0, 0].astype(F32)
        for s in range(1, NDEV):
            g = g + r_ref[s, 0].astype(F32)
        delta, m2, v2 = _adam(w_ref[0], g, m_ref[0], v_ref[0])
        g_ref[0], d_ref[0], mo_ref[0], vo_ref[0] = g, delta, m2, v2

    blk = pl.BlockSpec((1, ADAM_ROWS, D), lambda l, j: (l, j, 0))
    sds = jax.ShapeDtypeStruct(w.shape, F32)
    return pl.pallas_call(
        body, name="adam_big", grid=(DEPTH, LROWS // ADAM_ROWS),
        in_specs=[pl.BlockSpec((NDEV, 1, ADAM_ROWS, D), lambda l, j: (0, l, j, 0)), blk, blk, blk],
        out_specs=[blk] * 4, out_shape=[sds] * 4,
        compiler_params=_cparams("arbitrary", "arbitrary"),
    )(recv, w, m, v)


def _sum_small(gathered, rows):
    def body(g_ref, o_ref):
        acc = g_ref[0:rows, :]
        for s in range(1, NDEV):
            acc = acc + g_ref[s * rows:(s + 1) * rows, :]
        o_ref[...] = acc

    return pl.pallas_call(
        body, name="sum_small",
        in_specs=[pl.BlockSpec(memory_space=pltpu.VMEM)], out_specs=pl.BlockSpec(memory_space=pltpu.VMEM),
        out_shape=jax.ShapeDtypeStruct((rows, D), F32),
        compiler_params=pltpu.CompilerParams(vmem_limit_bytes=VMEM_LIMIT),
    )(gathered)


def _adam_small(params):
    k = len(params)

    def body(*refs):
        ins, outs = refs[:4 * k], refs[4 * k:]
        for j in range(k):
            w_ref, g_ref, m_ref, v_ref = ins[4 * j:4 * j + 4]
            delta, m2, v2 = _adam(w_ref[...], g_ref[...], m_ref[...], v_ref[...])
            outs[3 * j][...], outs[3 * j + 1][...], outs[3 * j + 2][...] = delta, m2, v2

    flat = [a for p in params for a in p]
    shapes = [jax.ShapeDtypeStruct(p[0].shape, F32) for p in params for _ in range(3)]
    vm = pl.BlockSpec(memory_space=pltpu.VMEM)
    res = pl.pallas_call(
        body, name="adam_small", in_specs=[vm] * len(flat), out_specs=[vm] * len(shapes), out_shape=shapes,
        compiler_params=pltpu.CompilerParams(vmem_limit_bytes=VMEM_LIMIT),
    )(*flat)
    return [tuple(res[3 * j:3 * j + 3]) for j in range(k)]


def _pack(ab_in, ab_out, pw1, pw2, wg, wu, wd):
    ins = jnp.swapaxes(jnp.stack([ab_in[0], pw1[0], ab_in[1], pw1[1]]), 1, 2)
    outs = jnp.stack([ab_out[0], pw2[0], ab_out[1], pw2[1]])
    return jnp.concatenate([ins, outs, jnp.swapaxes(wg, 1, 2), jnp.swapaxes(wu, 1, 2), wd], axis=1)


def _unpack(p):
    ins = jnp.swapaxes(p[:, OFF_IN:OFF_OUT], 1, 2)
    outs = p[:, OFF_OUT:OFF_G]
    return (ins[0::2], outs[0::2], ins[1::2], outs[1::2], jnp.swapaxes(p[:, OFF_G:OFF_U], 1, 2),
            jnp.swapaxes(p[:, OFF_U:OFF_D], 1, 2), p[:, OFF_D:LROWS])


def _unshard(flat, lead, per):
    k = len(lead)
    a = flat.reshape((NDEV,) + tuple(lead) + (per,))
    a = jnp.transpose(a, tuple(range(1, k + 1)) + (0, k + 1))
    return a.reshape(tuple(lead) + (NDEV * per,))


def _rows_of(a):
    f = a.reshape(-1)
    pad = (-f.shape[0]) % D
    if pad:
        f = jnp.concatenate([f, jnp.zeros((pad,), f.dtype)])
    return f.reshape(-1, D)


def _pad_rows(a, rows):
    return jnp.concatenate([a, jnp.zeros((rows - a.shape[0],) + a.shape[1:], a.dtype)], axis=0)


def kernel(x, c, norm_mix_g, norm_ffn_g, w_mod, b_mod, ab_w_in, ab_conv, ab_w_pool, ab_pool_scale, ab_w_out, cf_w_pw1, cf_b_pw1, cf_w_dw, cf_b_dw, cf_ln_g, cf_ln_b, cf_w_pw2, cf_b_pw2, ffn_w_gate, ffn_w_up, ffn_w_down, final_norm_g, loss_target, m_norm_mix_g, m_norm_ffn_g, m_w_mod, m_b_mod, m_ab_w_in, m_ab_conv, m_ab_w_pool, m_ab_pool_scale, m_ab_w_out, m_cf_w_pw1, m_cf_b_pw1, m_cf_w_dw, m_cf_b_dw, m_cf_ln_g, m_cf_ln_b, m_cf_w_pw2, m_cf_b_pw2, m_ffn_w_gate, m_ffn_w_up, m_ffn_w_down, m_final_norm_g, v_norm_mix_g, v_norm_ffn_g, v_w_mod, v_b_mod, v_ab_w_in, v_ab_conv, v_ab_w_pool, v_ab_pool_scale, v_ab_w_out, v_cf_w_pw1, v_cf_b_pw1, v_cf_w_dw, v_cf_b_dw, v_cf_ln_g, v_cf_ln_b, v_cf_w_pw2, v_cf_b_pw2, v_ffn_w_gate, v_ffn_w_up, v_ffn_w_down, v_final_norm_g):
    t = x.shape[1]
    tm = 512 if t % 512 == 0 else t // 2
    tk = 512 if t % 512 == 0 else t // 2
    tmo = tm // 2
    me = 4 * lax.axis_index("x") + 2 * lax.axis_index("y") + lax.axis_index("c")
    xs, tgt = x[0], loss_target[0]

    sharded = [ab_conv, cf_b_pw1, cf_w_dw, cf_b_dw, cf_ln_g, cf_ln_b, cf_b_pw2]
    flat = jnp.concatenate([a.reshape(-1) for a in sharded])
    n_flat = flat.shape[0]
    g1 = _gather_small(_pad_rows(jnp.concatenate([c, _rows_of(flat)], axis=0), 16), "gather_cond")
    g1 = g1.reshape(NDEV, 16, D)
    c_all = g1[:, 0, :]
    flat_all = g1[:, 1:, :].reshape(NDEV, -1)[:, :n_flat]
    full, o = [], 0
    for a in sharded:
        lead, per = a.shape[:-1], a.shape[-1]
        size = a.size
        full.append(_unshard(flat_all[:, o:o + size], lead, per))
        o += size
    ab_conv_f, b_pw1_f, w_dw_f, b_dw_f, ln_g_f, ln_b_f, b_pw2_f = full

    b_sl = lax.dynamic_slice_in_dim(b_mod, me * MODW, MODW, axis=1).reshape(DEPTH, 1, MODW)
    mod_part, c_act = _mod_fwd(c_all, w_mod, b_sl)
    g2 = _gather_small(mod_part, "gather_mod").reshape(NDEV, NDEV, DEPTH, MODW)
    mod = jnp.transpose(lax.dynamic_index_in_dim(g2, me, axis=1, keepdims=False), (1, 0, 2)).reshape(DEPTH, N_MOD, D)
    zeros4 = jnp.zeros((4, D), F32)

    def vec_of(g, layer, k):
        return jnp.concatenate([g[layer][None], mod[layer, k + 1][None], mod[layer, k][None],
                                mod[layer, k + 2][None], zeros4], axis=0)

    vmix = [vec_of(norm_mix_g, l, 0) for l in range(DEPTH)]
    vffn = [vec_of(norm_ffn_g, l, 3) for l in range(DEPTH)]

    w_pack = _pack(ab_w_in, ab_w_out, cf_w_pw1, cf_w_pw2, ffn_w_gate, ffn_w_up, ffn_w_down)
    gw = _gather_weights(w_pack.astype(BF16))

    wp16 = ab_w_pool.astype(BF16)
    wdw32 = [_pad_rows(w_dw_f[i], 32) for i in range(2)]
    sm_odd = [jnp.concatenate([b_dw_f[i][None], ln_g_f[i][None], ln_b_f[i][None], b_pw2_f[i][None], zeros4], axis=0)
              for i in range(2)]
    zero_bias = jnp.zeros((1, 2 * D), F32)

    saved = []
    xc = xs
    for l in range(DEPTH):
        i = l // 2
        if l % 2 == 0:
            u = _inproj(xc, vmix[l], zero_bias, gw, l, tm)
            x_mid, y_mix = _mix_even_fwd(u, xc, vmix[l], ab_conv_f[i], wp16[i], ab_pool_scale[i][None], gw, l, tm)
            z2 = None
        else:
            u = _inproj(xc, vmix[l], b_pw1_f[i][None], gw, l, tm)
            x_mid, y_mix, z2 = _mix_odd_fwd(u, xc, vmix[l], wdw32[i], sm_odd[i], gw, l, tmo)
        x_out, a, b, y_ffn = _ffn_fwd(x_mid, vffn[l], gw, l, tm)
        saved.append((xc, u, y_mix, z2, x_mid, a, b, y_ffn))
        xc = x_out

    dx, fsum = _final(xc, tgt, final_norm_g[None], tm)
    loss = lax.psum(jnp.sum(fsum[1]), ("x", "y", "c"))
    d_final_g = fsum[0]

    items = []
    dmod = [None] * DEPTH
    d_mix_g, d_ffn_g = [None] * DEPTH, [None] * DEPTH
    d_conv, d_pool, d_pscale = [None] * 2, [None] * 2, [None] * 2
    d_bpw1, d_wdw, d_bdw, d_lng, d_lnb, d_bpw2 = ([None] * 2 for _ in range(6))
    for l in reversed(range(DEPTH)):
        i = l // 2
        x_in, u, y_mix, z2, x_mid, a, b, y_ffn = saved[l]
        dy, da, db, s_h = _ffn_bwd_hidden(dx, y_ffn, a, b, vffn[l], gw, l, tm)
        dx_mid, h2, s_i = _ffn_bwd_input(da, db, x_mid, dx, vffn[l], gw, l, tm)
        items.append((_wgrad(da, h2, f"wgrad_gate_l{l}", tk), l, OFF_G, FS))
        items.append((_wgrad(db, h2, f"wgrad_up_l{l}", tk), l, OFF_U, FS))
        items.append((_wgrad(a, dy, f"wgrad_down_l{l}", tk, lhs2=b), l, OFF_D, FS))
        d_ffn_g[l] = s_i[3]
        mod_ffn = [s_i[1], s_i[2], s_h[0]]
        if l % 2 == 0:
            dx, du, h, cat, dym, s_m, dwp = _mix_even_bwd(dx_mid, u, x_in, y_mix, vmix[l], ab_conv_f[i], wp16[i],
                                                          ab_pool_scale[i][None], gw, l, tm)
            items.append((_wgrad(cat, dym, f"wgrad_out_l{l}", tk), l, OFF_OUT, ROWS_OUT))
            d_conv[i], d_pool[i], d_pscale[i] = s_m[4:7, :DA], dwp, s_m[7, :DA]
            mod_mix = [s_m[1], s_m[2], s_m[0]]
            d_mix_g[l] = s_m[3]
        else:
            dym, z3, dz2, s_1 = _mix_odd_bwd1(dx_mid, y_mix, z2, vmix[l], sm_odd[i], gw, l, tm)
            dx, du, h, s_2, dwdw = _mix_odd_bwd2(dz2, u, x_in, dx_mid, vmix[l], wdw32[i], gw, l, tmo)
            items.append((_wgrad(z3, dym, f"wgrad_out_l{l}", tk), l, OFF_OUT, ROWS_OUT))
            d_bpw1[i], d_wdw[i], d_bdw[i] = s_2[8:10].reshape(2 * D), dwdw[:CONF_K], s_1[7]
            d_lng[i], d_lnb[i], d_bpw2[i] = s_1[5], s_1[6], s_1[4]
            mod_mix = [s_2[1], s_2[2], s_1[0]]
            d_mix_g[l] = s_2[3]
        items.append((_wgrad(du, h, f"wgrad_in_l{l}", tk), l, OFF_IN, ROWS_IN))
        dmod[l] = jnp.stack(mod_mix + mod_ffn)
    grad_x = dx[None]

    recv = _scatter_grads(items)
    m_pack = _pack(m_ab_w_in, m_ab_w_out, m_cf_w_pw1, m_cf_w_pw2, m_ffn_w_gate, m_ffn_w_up, m_ffn_w_down)
    v_pack = _pack(v_ab_w_in, v_ab_w_out, v_cf_w_pw1, v_cf_w_pw2, v_ffn_w_gate, v_ffn_w_up, v_ffn_w_down)
    big = [_unpack(p) for p in _adam_big(recv, w_pack, m_pack, v_pack)]

    small = [jnp.stack(dmod), jnp.stack(d_mix_g), jnp.stack(d_ffn_g), d_final_g, jnp.stack(d_pool),
             jnp.stack(d_pscale), jnp.stack(d_conv), jnp.stack(d_bpw1), jnp.stack(d_wdw), jnp.stack(d_bdw),
             jnp.stack(d_lng), jnp.stack(d_lnb), jnp.stack(d_bpw2)]
    small_rows = [_rows_of(a) for a in small]
    n_rows = sum(a.shape[0] for a in small_rows)
    pad_rows = -(-n_rows // 8) * 8
    g3 = _gather_small(_pad_rows(jnp.concatenate(small_rows, axis=0), pad_rows), "gather_small_grads")
    summed = _sum_small(g3, pad_rows)
    outs, o = [], 0
    for a, r in zip(small, small_rows):
        outs.append(summed[o:o + r.shape[0]].reshape(-1)[:a.size].reshape(a.shape))
        o += r.shape[0]
    (g_bmod, g_mix_g, g_ffn_g, g_final, g_pool, g_pscale, g_conv, g_bpw1, g_wdw, g_bdw, g_lng, g_lnb, g_bpw2) = outs
    g_bmod = g_bmod.reshape(DEPTH, N_MOD * D)

    def my_shard(a):
        per = a.shape[-1] // NDEV
        return lax.dynamic_slice_in_dim(a, me * per, per, axis=a.ndim - 1)

    g_conv, g_bpw1, g_wdw, g_bdw, g_lng, g_lnb, g_bpw2 = [
        my_shard(a) for a in (g_conv, g_bpw1, g_wdw, g_bdw, g_lng, g_lnb, g_bpw2)]

    dmod_all = g3.reshape(NDEV, pad_rows, D)[:, :DEPTH * N_MOD, :].reshape(NDEV, DEPTH, N_MOD * D)
    dmod_mine = jnp.transpose(lax.dynamic_slice_in_dim(dmod_all, me * MODW, MODW, axis=2), (1, 0, 2))
    g_wmod, d_wmod, nm_wmod, nv_wmod = _mod_bwd_adam(c_act.T, dmod_mine, w_mod, m_w_mod, v_w_mod)

    small_params = [
        (norm_mix_g, g_mix_g, m_norm_mix_g, v_norm_mix_g), (norm_ffn_g, g_ffn_g, m_norm_ffn_g, v_norm_ffn_g),
        (b_mod, g_bmod, m_b_mod, v_b_mod), (ab_conv, g_conv, m_ab_conv, v_ab_conv),
        (ab_w_pool, g_pool, m_ab_w_pool, v_ab_w_pool), (ab_pool_scale, g_pscale, m_ab_pool_scale, v_ab_pool_scale),
        (cf_b_pw1, g_bpw1, m_cf_b_pw1, v_cf_b_pw1), (cf_w_dw, g_wdw, m_cf_w_dw, v_cf_w_dw),
        (cf_b_dw, g_bdw, m_cf_b_dw, v_cf_b_dw), (cf_ln_g, g_lng, m_cf_ln_g, v_cf_ln_g),
        (cf_ln_b, g_lnb, m_cf_ln_b, v_cf_ln_b), (cf_b_pw2, g_bpw2, m_cf_b_pw2, v_cf_b_pw2),
        (final_norm_g, g_final, m_final_norm_g, v_final_norm_g)]

    def two_d(a):
        return a.reshape(-1, a.shape[-1])

    upd = _adam_small([tuple(two_d(a) for a in p) for p in small_params])
    upd = [tuple(r.reshape(p[0].shape) for r in u) for u, p in zip(upd, small_params)]
    (s_mix, s_ffn, s_bmod, s_conv, s_pool, s_pscale, s_bpw1, s_wdw, s_bdw, s_lng, s_lnb, s_bpw2, s_final) = upd
    small_g = [p[1] for p in small_params]
    (q_mix, q_ffn, q_bmod, q_conv, q_pool, q_pscale, q_bpw1, q_wdw, q_bdw, q_lng, q_lnb, q_bpw2, q_final) = small_g

    def ordered(k):
        ab_in, ab_out, pw1, pw2, wg, wu, wd = big[k]
        if k == 0:
            sm = dict(mix=q_mix, ffn=q_ffn, bmod=q_bmod, conv=q_conv, pool=q_pool, pscale=q_pscale, bpw1=q_bpw1,
                      wdw=q_wdw, bdw=q_bdw, lng=q_lng, lnb=q_lnb, bpw2=q_bpw2, final=q_final)
            wmod = g_wmod
        else:
            j = k - 1
            sm = dict(mix=s_mix[j], ffn=s_ffn[j], bmod=s_bmod[j], conv=s_conv[j], pool=s_pool[j], pscale=s_pscale[j],
                      bpw1=s_bpw1[j], wdw=s_wdw[j], bdw=s_bdw[j], lng=s_lng[j], lnb=s_lnb[j], bpw2=s_bpw2[j],
                      final=s_final[j])
            wmod = (d_wmod, nm_wmod, nv_wmod)[j]
        return [sm["mix"], sm["ffn"], wmod, sm["bmod"], ab_in, sm["conv"], sm["pool"], sm["pscale"], ab_out,
                pw1, sm["bpw1"], sm["wdw"], sm["bdw"], sm["lng"], sm["lnb"], pw2, sm["bpw2"], wg, wu, wd, sm["final"]]

    return (loss, grad_x, *ordered(0), *ordered(1), *ordered(2), *ordered(3))
```

```python
import functools

import jax
import jax.numpy as jnp
from jax import lax
from jax.experimental import pallas as pl
from jax.experimental.pallas import tpu as pltpu

F32 = jnp.float32
BF16 = jnp.bfloat16
MESH = pl.DeviceIdType.MESH

NDEV = 8
DEPTH = 4
D = 1024
FF = 2816
FS = FF // NDEV
DA = 512
PG = 128
POOL = ((2, 1, 0), (4, 2, 1), (8, 4, 3), (16, 8, 7))
CONF_K = 31
CONF_L = 15
N_MOD = 6
MODW = N_MOD * D // NDEV
RMS_EPS = 1e-6
LN_EPS = 1e-5

ROWS_IN, ROWS_OUT = 2 * D // NDEV, D // NDEV
OFF_IN, OFF_OUT = 0, ROWS_IN
OFF_G = OFF_OUT + ROWS_OUT
OFF_U = OFF_G + FS
OFF_D = OFF_U + FS
LROWS = OFF_D + FS

HALO = 16
VMEM_LIMIT = 60 * 1024 * 1024

ADAM_LR, ADAM_B1, ADAM_B2, ADAM_EPS, ADAM_WD, ADAM_STEP = 1e-3, 0.9, 0.999, 1e-8, 0.01, 10
ADAM_C1 = 1.0 / (1.0 - ADAM_B1 ** ADAM_STEP)
ADAM_C2 = 1.0 / (1.0 - ADAM_B2 ** ADAM_STEP)


def _nn(a, b):
    return jnp.dot(a, b, preferred_element_type=F32)


def _nt(a, b):
    return lax.dot_general(a, b, (((1,), (1,)), ((), ())), preferred_element_type=F32)


def _tn(a, b):
    return lax.dot_general(a, b, (((0,), (0,)), ((), ())), preferred_element_type=F32)


def _sig(x):
    return 1.0 / (1.0 + jnp.exp(-x))


def _rowsum(x):
    return jnp.sum(x, axis=0, keepdims=True)


def _lanemean(x):
    return jnp.mean(x, axis=-1, keepdims=True)


def _norm_fwd(x, g, sc, sh):
    r = lax.rsqrt(_lanemean(x * x) + RMS_EPS)
    n = x * r
    return n * (g * (1.0 + sc)) + sh, n, r


def _norm_bwd(dh, n, r, g, sc):
    dn = dh * (g * (1.0 + sc))
    return r * (dn - n * _lanemean(dn * n))


def _adam(w, g, m, v):
    m2 = ADAM_B1 * m + (1.0 - ADAM_B1) * g
    v2 = ADAM_B2 * v + (1.0 - ADAM_B2) * (g * g)
    delta = -ADAM_LR * ((m2 * ADAM_C1) / (jnp.sqrt(v2 * ADAM_C2) + ADAM_EPS) + ADAM_WD * w)
    return delta, m2, v2


def _load_weights(g_ref, specs, sems):
    cps = [pltpu.make_async_copy(g_ref.at[:, pl.ds(off, rows), :], dst, sems.at[k])
           for k, (off, rows, dst) in enumerate(specs)]
    for cp in cps:
        cp.start()
    for cp in cps:
        cp.wait()


def _cparams(*sem):
    return pltpu.CompilerParams(dimension_semantics=sem if sem else None, vmem_limit_bytes=VMEM_LIMIT)


def _tile(tm, w):
    return pl.BlockSpec((tm, w), lambda i: (i, 0))


def _full(shape):
    nd = len(shape)
    return pl.BlockSpec(shape, lambda i: (0,) * nd)


def _halo_specs(tm, w, total_rows):
    tb = tm // HALO
    nb = total_rows // HALO
    prev = pl.BlockSpec((HALO, w), lambda i: (jnp.maximum(i * tb - 1, 0), 0))
    nxt = pl.BlockSpec((HALO, w), lambda i: (jnp.minimum((i + 1) * tb, nb - 1), 0))
    return prev, nxt


ANY = pl.BlockSpec(memory_space=pl.ANY)


def _peers():
    x, y, c = lax.axis_index("x"), lax.axis_index("y"), lax.axis_index("c")
    return x, y, c


def _gather_small(v, name):
    m_per, n = v.shape

    def body(x_ref, out_ref, send_sems, recv_sems, local_sem):
        x, y, c = _peers()
        me, sibling = (x, y, c), (x, y, 1 - c)
        chips = [(1 - x, y), (x, 1 - y), (1 - x, 1 - y)]

        def rows(px, py, pc):
            return out_ref.at[pl.ds((4 * px + 2 * py + pc) * m_per, m_per), :]

        def copy(k, block, to, src=None):
            return pltpu.make_async_remote_copy(
                src_ref=rows(*block) if src is None else src, dst_ref=rows(*block),
                send_sem=send_sems.at[k], recv_sem=recv_sems.at[k], device_id=to, device_id_type=MESH)

        mine = pltpu.make_async_copy(x_ref, rows(*me), local_sem)
        mine.start()
        first = [copy(0, me, sibling, src=x_ref)]
        first += [copy(1 + j, me, (*chip, c), src=x_ref) for j, chip in enumerate(chips)]
        for cp in first:
            cp.start()
        passed = [copy(4 + j, (*chip, c), sibling) for j, chip in enumerate(chips)]
        for j, chip in enumerate(chips):
            copy(1 + j, (*chip, c), me).wait_recv()
            passed[j].start()
        copy(0, sibling, me).wait_recv()
        for j, chip in enumerate(chips):
            copy(4 + j, (*chip, 1 - c), me).wait_recv()
        for cp in first + passed:
            cp.wait_send()
        mine.wait()

    return pl.pallas_call(
        body, name=name,
        out_shape=jax.ShapeDtypeStruct((NDEV * m_per, n), v.dtype),
        in_specs=[pl.BlockSpec(memory_space=pltpu.VMEM)],
        out_specs=pl.BlockSpec(memory_space=pltpu.VMEM),
        scratch_shapes=[pltpu.SemaphoreType.DMA((7,)), pltpu.SemaphoreType.DMA((7,)), pltpu.SemaphoreType.DMA],
        compiler_params=pltpu.CompilerParams(vmem_limit_bytes=VMEM_LIMIT),
    )(v)


class _Comm:
    def __init__(self, ins, outs, aliases, bind):
        self.ins, self.outs, self.aliases, self.bind = ins, outs, aliases, bind


COMM_SEMS = [pltpu.SemaphoreType.DMA((7,)), pltpu.SemaphoreType.DMA((7,)), pltpu.SemaphoreType.DMA]


def _gather_hooks(p_ref, out_ref, send_sems, recv_sems, local_sem):
    def parts():
        x, y, c = _peers()
        me, sibling = (x, y, c), (x, y, 1 - c)
        chips = [(1 - x, y), (x, 1 - y), (1 - x, 1 - y)]

        def slab(px, py, pc):
            return out_ref.at[4 * px + 2 * py + pc]

        def copy(k, block, to, src=None):
            return pltpu.make_async_remote_copy(
                src_ref=slab(*block) if src is None else src, dst_ref=slab(*block),
                send_sem=send_sems.at[k], recv_sem=recv_sems.at[k], device_id=to, device_id_type=MESH)

        mine = pltpu.make_async_copy(p_ref, slab(*me), local_sem)
        first = [copy(0, me, sibling, src=p_ref)]
        first += [copy(1 + j, me, (*chip, c), src=p_ref) for j, chip in enumerate(chips)]
        passed = [copy(4 + j, (*chip, c), sibling) for j, chip in enumerate(chips)]
        from_chips = [copy(1 + j, (*chip, c), me) for j, chip in enumerate(chips)]
        from_sibling = [copy(0, sibling, me)] + [copy(4 + j, (*chip, 1 - c), me) for j, chip in enumerate(chips)]
        return mine, first, passed, from_chips, from_sibling

    def start():
        mine, first, _, _, _ = parts()
        mine.start()
        for cp in first:
            cp.start()

    def middle():
        _, _, passed, from_chips, _ = parts()
        for arrived, onward in zip(from_chips, passed):
            arrived.wait_recv()
            onward.start()

    def end():
        mine, first, passed, _, from_sibling = parts()
        for cp in from_sibling:
            cp.wait_recv()
        for cp in first + passed:
            cp.wait_send()
        mine.wait()

    return start, middle, end


def _gather_comm(p):
    return _Comm([p], [jax.ShapeDtypeStruct((NDEV,) + p.shape, p.dtype)], {},
                 lambda cins, couts, sems: _gather_hooks(cins[0], couts[0], *sems))


def _gather_weights(p):
    def body(p_ref, out_ref, send_sems, recv_sems, local_sem):
        for hook in _gather_hooks(p_ref, out_ref, send_sems, recv_sems, local_sem):
            hook()

    return pl.pallas_call(
        body, name="gather_weights",
        out_shape=jax.ShapeDtypeStruct((NDEV,) + p.shape, p.dtype),
        in_specs=[ANY], out_specs=ANY, scratch_shapes=COMM_SEMS,
    )(p)


def _scatter_hooks(src_refs, specs, r_ref, layer, send_sems, recv_sems, local_sem):
    off0 = specs[0][0]
    total = sum(rows for _, rows in specs)

    def start():
        x, y, c = _peers()
        me = 4 * x + 2 * y + c

        def part(k, dev):
            off, rows = specs[k]
            src = src_refs[k].at[pl.ds(pl.multiple_of(dev * rows, 16), rows), :]
            return src, r_ref.at[me, layer, pl.ds(off, rows), :]

        for k in range(len(specs)):
            src, dst = part(k, me)
            pltpu.make_async_copy(src, dst, local_sem).start()
        for r in range(1, NDEV):
            px = 1 - x if r & 4 else x
            py = 1 - y if r & 2 else y
            pc = 1 - c if r & 1 else c
            for k in range(len(specs)):
                src, dst = part(k, 4 * px + 2 * py + pc)
                pltpu.make_async_remote_copy(
                    src_ref=src, dst_ref=dst, send_sem=send_sems.at[r - 1], recv_sem=recv_sems.at[r - 1],
                    device_id=(px, py, pc), device_id_type=MESH).start()

    def end():
        x, y, c = _peers()
        whole = r_ref.at[0, layer, pl.ds(off0, total), :]
        for r in range(1, NDEV):
            done = pltpu.make_async_remote_copy(
                src_ref=whole, dst_ref=whole, send_sem=send_sems.at[r - 1], recv_sem=recv_sems.at[r - 1],
                device_id=(x, y, c), device_id_type=MESH)
            done.wait_recv()
            done.wait_send()
        pltpu.make_async_copy(whole, whole, local_sem).wait()

    return start, None, end


def _scatter_comm(srcs, specs, recv, layer):
    k = len(srcs)
    return _Comm(list(srcs) + [recv], [jax.ShapeDtypeStruct(recv.shape, recv.dtype)], {k: 0},
                 lambda cins, couts, sems: _scatter_hooks(cins[:k], specs, couts[0], layer, *sems))


def _scatter_grads(srcs, specs, recv, layer):
    k = len(srcs)

    def body(*refs):
        start, _, end = _scatter_hooks(refs[:k], specs, refs[k + 1], layer, *refs[k + 2:])
        start()
        end()

    return pl.pallas_call(
        body, name=f"scatter_grads_l{layer}",
        out_shape=jax.ShapeDtypeStruct(recv.shape, recv.dtype),
        in_specs=[ANY] * (k + 1), out_specs=ANY, scratch_shapes=COMM_SEMS,
        input_output_aliases={k: 0},
    )(*srcs, recv)


def _empty_recv():
    def body(o_ref):
        del o_ref

    return pl.pallas_call(body, name="recv_buffer", out_specs=ANY,
                          out_shape=jax.ShapeDtypeStruct((NDEV, DEPTH, LROWS, D), BF16))()


def _host_call(inner, comm, *, name, grid, in_specs, out_specs, out_shape, scratch_shapes, operands):
    if comm is None:
        return pl.pallas_call(
            inner, name=name, grid=grid, in_specs=in_specs, out_specs=out_specs, out_shape=out_shape,
            scratch_shapes=scratch_shapes, compiler_params=_cparams("arbitrary"))(*operands)
    n_in, n_out, n_s = len(in_specs), len(out_specs), len(scratch_shapes)
    k_in, k_out = len(comm.ins), len(comm.outs)
    steps = grid[0]

    def body(*refs):
        ins, cins = refs[:n_in], refs[n_in:n_in + k_in]
        o0 = n_in + k_in
        outs, couts = refs[o0:o0 + n_out], refs[o0 + n_out:o0 + n_out + k_out]
        s0 = o0 + n_out + k_out
        scr, sems = refs[s0:s0 + n_s], refs[s0 + n_s:]
        start, middle, end = comm.bind(cins, couts, sems)
        i = pl.program_id(0)
        pl.when(i == 0)(start)
        if middle is not None:
            pl.when(i == steps // 2)(middle)
        inner(*ins, *outs, *scr)
        pl.when(i == steps - 1)(end)

    return pl.pallas_call(
        body, name=name, grid=grid, in_specs=list(in_specs) + [ANY] * k_in,
        out_specs=list(out_specs) + [ANY] * k_out, out_shape=list(out_shape) + list(comm.outs),
        scratch_shapes=list(scratch_shapes) + COMM_SEMS,
        input_output_aliases={n_in + a: n_out + b for a, b in comm.aliases.items()},
        compiler_params=_cparams("arbitrary"))(*operands, *comm.ins)


def _mod_fwd(c_all, w_mod, b_sl):
    def body(c_ref, w_ref, b_ref, o_ref, ca_ref):
        cv = c_ref[...]
        ca = cv * _sig(cv)
        ca_ref[...] = ca
        o_ref[...] = jnp.dot(ca, w_ref[0], preferred_element_type=F32, precision=lax.Precision.HIGHEST) + b_ref[0]

    return pl.pallas_call(
        body, name="mod_fwd", grid=(DEPTH,),
        in_specs=[_full((NDEV, D)), pl.BlockSpec((1, D, MODW), lambda l: (l, 0, 0)),
                  pl.BlockSpec((1, 1, MODW), lambda l: (l, 0, 0))],
        out_specs=[pl.BlockSpec((NDEV, MODW), lambda l: (0, l)), _full((NDEV, D))],
        out_shape=[jax.ShapeDtypeStruct((NDEV, DEPTH * MODW), F32), jax.ShapeDtypeStruct((NDEV, D), F32)],
        compiler_params=_cparams("arbitrary"),
    )(c_all, w_mod, b_sl)


def _mod_bwd_adam(ca_t, dmod, w, m, v):
    def body(ct_ref, dm_ref, w_ref, m_ref, v_ref, g_ref, d_ref, mo_ref, vo_ref):
        g = jnp.dot(ct_ref[...], dm_ref[0], preferred_element_type=F32, precision=lax.Precision.HIGHEST)
        delta, m2, v2 = _adam(w_ref[0], g, m_ref[0], v_ref[0])
        g_ref[0], d_ref[0], mo_ref[0], vo_ref[0] = g, delta, m2, v2

    blk = pl.BlockSpec((1, D, MODW), lambda l: (l, 0, 0))
    sds = jax.ShapeDtypeStruct(w.shape, F32)
    return pl.pallas_call(
        body, name="mod_bwd_adam", grid=(DEPTH,),
        in_specs=[_full((D, NDEV)), pl.BlockSpec((1, NDEV, MODW), lambda l: (l, 0, 0)), blk, blk, blk],
        out_specs=[blk] * 4, out_shape=[sds] * 4,
        compiler_params=_cparams("arbitrary"),
    )(ca_t, dmod, w, m, v)


def _inproj(x, vec, bias, gw, layer, tm):
    t = x.shape[0]

    def body(x_ref, vec_ref, b_ref, g_ref, u_ref, w_s, sems):
        @pl.when(pl.program_id(0) == 0)
        def _():
            _load_weights(g_ref, [(OFF_IN, ROWS_IN, w_s)], sems)

        h, _, _ = _norm_fwd(x_ref[...], vec_ref[0:1], vec_ref[1:2], vec_ref[2:3])
        w = w_s[...].reshape(2 * D, D)
        u_ref[...] = (_nt(h.astype(BF16), w) + b_ref[...]).astype(BF16)

    return pl.pallas_call(
        body, name=f"inproj_l{layer}", grid=(t // tm,),
        in_specs=[_tile(tm, D), _full((8, D)), _full((1, 2 * D)), ANY],
        out_specs=_tile(tm, 2 * D), out_shape=jax.ShapeDtypeStruct((t, 2 * D), BF16),
        scratch_shapes=[pltpu.VMEM((NDEV, ROWS_IN, D), BF16), pltpu.SemaphoreType.DMA((1,))],
        compiler_params=_cparams("arbitrary"),
    )(x, vec, bias, gw)


def _fill_even(qe, pe, be, part_ref, lo, rows, valid):
    cg = part_ref[:, DA:2 * DA].astype(F32)
    v = part_ref[:, 2 * DA:3 * DA].astype(F32)
    q = cg * v
    p = part_ref[:, 3 * DA:4 * DA].astype(F32)
    if valid is not None:
        q = jnp.where(valid, q, 0.0)
        p = jnp.where(valid, p, 0.0)
    qe[lo:lo + rows, :] = q
    pe[lo:lo + rows, :] = p
    if be is not None:
        b = part_ref[:, 0:DA].astype(F32)
        be[lo:lo + rows, :] = b if valid is None else jnp.where(valid, b, 0.0)


def _conv3(ca_ref, qe, tm):
    return (ca_ref[0:1] * qe[HALO - 1:HALO - 1 + tm] + ca_ref[1:2] * qe[HALO:HALO + tm]
            + ca_ref[2:3] * qe[HALO + 1:HALO + 1 + tm])


def _pool_counts(t0, rows, first_row, left, right, t):
    tg = t0 + first_row + lax.broadcasted_iota(jnp.int32, (rows, 1), 0)
    cnt = jnp.minimum(tg + right, t - 1) - jnp.maximum(tg - left, 0) + 1
    return jnp.maximum(cnt, 1).astype(F32)


def _pool_minus_id(pe, gi, left, right, inv_cnt, tm):
    c0 = gi * PG
    s = pe[HALO - left:HALO - left + tm, c0:c0 + PG]
    for j in range(-left + 1, right + 1):
        s = s + pe[HALO + j:HALO + j + tm, c0:c0 + PG]
    return s * inv_cnt - pe[HALO:HALO + tm, c0:c0 + PG]


def _mix_even_fwd(u, x, vec, ca, wp, ps, gw, layer, tm):
    t = x.shape[0]
    n = t // tm
    e = tm + 2 * HALO

    def body(u_ref, up_ref, un_ref, x_ref, vec_ref, ca_ref, wp_ref, ps_ref, g_ref, xo_ref, y_ref, w_s, qe, pe, sems):
        i = pl.program_id(0)

        @pl.when(i == 0)
        def _():
            _load_weights(g_ref, [(OFF_OUT, ROWS_OUT, w_s)], sems)

        _fill_even(qe, pe, None, up_ref, 0, HALO, i > 0)
        _fill_even(qe, pe, None, u_ref, HALO, tm, None)
        _fill_even(qe, pe, None, un_ref, HALO + tm, HALO, i < n - 1)
        ya = u_ref[:, 0:DA].astype(F32) * _conv3(ca_ref, qe, tm)
        parts = [ya]
        for gi, (_, left, right) in enumerate(POOL):
            inv = 1.0 / _pool_counts(i * tm, tm, 0, left, right, t)
            pm = _pool_minus_id(pe, gi, left, right, inv, tm)
            parts.append(_nn(pm.astype(BF16), wp_ref[gi]) * ps_ref[0:1, gi * PG:(gi + 1) * PG])
        cat = jnp.concatenate(parts, axis=-1).astype(BF16)
        y = _nn(cat, w_s[...].reshape(D, D))
        y_ref[...] = y.astype(BF16)
        xo_ref[...] = x_ref[...] + vec_ref[3:4] * y

    prev, nxt = _halo_specs(tm, 2 * D, t)
    return pl.pallas_call(
        body, name=f"mix_even_fwd_l{layer}", grid=(n,),
        in_specs=[_tile(tm, 2 * D), prev, nxt, _tile(tm, D), _full((8, D)), _full((3, DA)),
                  _full((4, PG, PG)), _full((1, DA)), ANY],
        out_specs=[_tile(tm, D), _tile(tm, D)],
        out_shape=[jax.ShapeDtypeStruct((t, D), F32), jax.ShapeDtypeStruct((t, D), BF16)],
        scratch_shapes=[pltpu.VMEM((NDEV, ROWS_OUT, D), BF16), pltpu.VMEM((e, DA), F32), pltpu.VMEM((e, DA), F32),
                        pltpu.SemaphoreType.DMA((1,))],
        compiler_params=_cparams("arbitrary"),
    )(u, u, u, x, vec, ca, wp, ps, gw)


def _mix_even_bwd(dxo, u, x, y, vec, ca, wp, ps, gw, layer, tm, comm=None):
    t = x.shape[0]
    n = t // tm
    e = tm + 2 * HALO

    def body(dxo_ref, dp_ref, dn_ref, u_ref, up_ref, un_ref, x_ref, y_ref, vec_ref, ca_ref, wp_ref, ps_ref, g_ref,
             dxi_ref, du_ref, h_ref, cat_ref, dy_ref, sums_ref, dwp_ref,
             wo_s, wi_s, dye, qe, pe, be, dce, epe, sems):
        i = pl.program_id(0)

        @pl.when(i == 0)
        def _():
            _load_weights(g_ref, [(OFF_OUT, ROWS_OUT, wo_s), (OFF_IN, ROWS_IN, wi_s)], sems)
            sums_ref[...] = jnp.zeros_like(sums_ref)
            dwp_ref[...] = jnp.zeros_like(dwp_ref)

        gate = vec_ref[3:4]
        dxo_m = dxo_ref[...]
        dye[0:HALO, :] = jnp.where(i > 0, gate * dp_ref[...], 0.0).astype(BF16)
        dye[HALO:HALO + tm, :] = (gate * dxo_m).astype(BF16)
        dye[HALO + tm:e, :] = jnp.where(i < n - 1, gate * dn_ref[...], 0.0).astype(BF16)
        _fill_even(qe, pe, be, up_ref, 0, HALO, i > 0)
        _fill_even(qe, pe, be, u_ref, HALO, tm, None)
        _fill_even(qe, pe, be, un_ref, HALO + tm, HALO, i < n - 1)
        sums_ref[0:1, :] += _rowsum(dxo_m * y_ref[...].astype(F32))

        dcat = _nt(dye[...], wo_s[...].reshape(D, D))
        dce[...] = dcat[:, 0:DA] * be[...]
        cq = _conv3(ca_ref, qe, tm)
        bg = be[HALO:HALO + tm]
        dc_m = dce[HALO:HALO + tm]
        dbg = dcat[HALO:HALO + tm, 0:DA] * cq
        dq = (ca_ref[0:1] * dce[HALO + 1:HALO + 1 + tm] + ca_ref[1:2] * dc_m
              + ca_ref[2:3] * dce[HALO - 1:HALO - 1 + tm])
        cg = u_ref[:, DA:2 * DA].astype(F32)
        v = u_ref[:, 2 * DA:3 * DA].astype(F32)
        for k in range(3):
            sums_ref[4 + k:5 + k, 0:DA] += _rowsum(dc_m * qe[HALO - 1 + k:HALO - 1 + k + tm])
        du_parts = [dbg, dq * v, dq * cg]
        cat_parts = [bg * cq]
        for gi, (_, left, right) in enumerate(POOL):
            c0 = gi * PG
            scale = ps_ref[0:1, c0:c0 + PG]
            dyb = dcat[:, DA + c0:DA + c0 + PG]
            dybs = (dyb * scale).astype(BF16)
            dpm = _nt(dybs, wp_ref[gi])
            inv_e = 1.0 / _pool_counts(i * tm, e, -HALO, left, right, t)
            epe[:, c0:c0 + PG] = dpm * inv_e
            s_adj = epe[HALO - right:HALO - right + tm, c0:c0 + PG]
            for j in range(-right + 1, left + 1):
                s_adj = s_adj + epe[HALO + j:HALO + j + tm, c0:c0 + PG]
            du_parts.append(s_adj - dpm[HALO:HALO + tm])
            inv_m = 1.0 / _pool_counts(i * tm, tm, 0, left, right, t)
            pm = _pool_minus_id(pe, gi, left, right, inv_m, tm).astype(BF16)
            ybpre = _nn(pm, wp_ref[gi])
            sums_ref[7:8, c0:c0 + PG] += _rowsum(dyb[HALO:HALO + tm] * ybpre)
            dwp_ref[gi] += _tn(pm, dybs[HALO:HALO + tm])
            cat_parts.append(ybpre * scale)
        du = jnp.concatenate(du_parts, axis=-1).astype(BF16)
        du_ref[...] = du
        cat_ref[...] = jnp.concatenate(cat_parts, axis=-1).astype(BF16)
        dy_ref[...] = dye[HALO:HALO + tm, :]
        dh = _nn(du, wi_s[...].reshape(2 * D, D))
        g, sc, sh = vec_ref[0:1], vec_ref[1:2], vec_ref[2:3]
        h, nrm, r = _norm_fwd(x_ref[...], g, sc, sh)
        h_ref[...] = h.astype(BF16)
        dxi_ref[...] = dxo_m + _norm_bwd(dh, nrm, r, g, sc)
        sums_ref[1:2, :] += _rowsum(dh)
        sums_ref[2:3, :] += _rowsum(dh * nrm)

        @pl.when(i == n - 1)
        def _():
            p = sums_ref[2:3, :]
            sums_ref[3:4, :] = p * (1.0 + sc)
            sums_ref[2:3, :] = p * g

    prev_u, nxt_u = _halo_specs(tm, 2 * D, t)
    prev_d, nxt_d = _halo_specs(tm, D, t)
    return _host_call(
        body, comm, name=f"mix_even_bwd_l{layer}", grid=(n,),
        in_specs=[_tile(tm, D), prev_d, nxt_d, _tile(tm, 2 * D), prev_u, nxt_u, _tile(tm, D), _tile(tm, D),
                  _full((8, D)), _full((3, DA)), _full((4, PG, PG)), _full((1, DA)), ANY],
        out_specs=[_tile(tm, D), _tile(tm, 2 * D), _tile(tm, D), _tile(tm, D), _tile(tm, D),
                   _full((16, D)), _full((4, PG, PG))],
        out_shape=[jax.ShapeDtypeStruct((t, D), F32), jax.ShapeDtypeStruct((t, 2 * D), BF16),
                   jax.ShapeDtypeStruct((t, D), BF16), jax.ShapeDtypeStruct((t, D), BF16),
                   jax.ShapeDtypeStruct((t, D), BF16), jax.ShapeDtypeStruct((16, D), F32),
                   jax.ShapeDtypeStruct((4, PG, PG), F32)],
        scratch_shapes=[pltpu.VMEM((NDEV, ROWS_OUT, D), BF16), pltpu.VMEM((NDEV, ROWS_IN, D), BF16),
                        pltpu.VMEM((e, D), BF16), pltpu.VMEM((e, DA), F32), pltpu.VMEM((e, DA), F32),
                        pltpu.VMEM((e, DA), F32), pltpu.VMEM((e, DA), F32), pltpu.VMEM((e, DA), F32),
                        pltpu.SemaphoreType.DMA((2,))],
        operands=(dxo, dxo, dxo, u, u, u, x, y, vec, ca, wp, ps, gw))


def _fill_glu(ze, part_ref, lo, rows, valid):
    a = part_ref[:, 0:D].astype(F32)
    g = part_ref[:, D:2 * D].astype(F32)
    z = a * _sig(g)
    ze[lo:lo + rows, :] = z if valid is None else jnp.where(valid, z, 0.0)


def _layer_norm_parts(z2):
    mu = _lanemean(z2)
    d = z2 - mu
    rstd = lax.rsqrt(_lanemean(d * d) + LN_EPS)
    return d * rstd, rstd


def _mix_odd_fwd(u, x, vec, wdw, sm, gw, layer, tm):
    t = x.shape[0]
    n = t // tm
    e = tm + 2 * HALO

    def body(u_ref, up_ref, un_ref, x_ref, vec_ref, wdw_ref, sm_ref, g_ref, xo_ref, y_ref, z2_ref, w_s, ze, sems):
        i = pl.program_id(0)

        @pl.when(i == 0)
        def _():
            _load_weights(g_ref, [(OFF_OUT, ROWS_OUT, w_s)], sems)

        _fill_glu(ze, up_ref, 0, HALO, i > 0)
        _fill_glu(ze, u_ref, HALO, tm, None)
        _fill_glu(ze, un_ref, HALO + tm, HALO, i < n - 1)
        z2 = sm_ref[0:1] + wdw_ref[0:1] * ze[1:1 + tm]
        for k in range(1, CONF_K):
            z2 = z2 + wdw_ref[k:k + 1] * ze[1 + k:1 + k + tm]
        z2_ref[...] = z2.astype(BF16)
        zn, _ = _layer_norm_parts(z2)
        lo = zn * sm_ref[1:2] + sm_ref[2:3]
        z3 = lo * _sig(lo)
        y = _nn(z3.astype(BF16), w_s[...].reshape(D, D)) + sm_ref[3:4]
        y_ref[...] = y.astype(BF16)
        xo_ref[...] = x_ref[...] + vec_ref[3:4] * y

    prev, nxt = _halo_specs(tm, 2 * D, t)
    return pl.pallas_call(
        body, name=f"mix_odd_fwd_l{layer}", grid=(n,),
        in_specs=[_tile(tm, 2 * D), prev, nxt, _tile(tm, D), _full((8, D)), _full((32, D)), _full((8, D)), ANY],
        out_specs=[_tile(tm, D), _tile(tm, D), _tile(tm, D)],
        out_shape=[jax.ShapeDtypeStruct((t, D), F32), jax.ShapeDtypeStruct((t, D), BF16),
                   jax.ShapeDtypeStruct((t, D), BF16)],
        scratch_shapes=[pltpu.VMEM((NDEV, ROWS_OUT, D), BF16), pltpu.VMEM((e, D), F32), pltpu.SemaphoreType.DMA((1,))],
        compiler_params=_cparams("arbitrary"),
    )(u, u, u, x, vec, wdw, sm, gw)


def _mix_odd_bwd1(dxo, y, z2, vec, sm, gw, layer, tm):
    t = dxo.shape[0]
    n = t // tm

    def body(dxo_ref, y_ref, z2_ref, vec_ref, sm_ref, g_ref, dy_ref, z3_ref, dz2_ref, sums_ref, w_s, sems):
        i = pl.program_id(0)

        @pl.when(i == 0)
        def _():
            _load_weights(g_ref, [(OFF_OUT, ROWS_OUT, w_s)], sems)
            sums_ref[...] = jnp.zeros_like(sums_ref)

        dxo_m = dxo_ref[...]
        dy = vec_ref[3:4] * dxo_m
        dyb = dy.astype(BF16)
        dy_ref[...] = dyb
        sums_ref[0:1, :] += _rowsum(dxo_m * y_ref[...].astype(F32))
        sums_ref[4:5, :] += _rowsum(dy)
        dz3 = _nt(dyb, w_s[...].reshape(D, D))
        zn, rstd = _layer_norm_parts(z2_ref[...].astype(F32))
        lo = zn * sm_ref[1:2] + sm_ref[2:3]
        sg = _sig(lo)
        z3_ref[...] = (lo * sg).astype(BF16)
        dlo = dz3 * (sg * (1.0 + lo * (1.0 - sg)))
        sums_ref[5:6, :] += _rowsum(dlo * zn)
        sums_ref[6:7, :] += _rowsum(dlo)
        dzn = dlo * sm_ref[1:2]
        dz2 = rstd * (dzn - _lanemean(dzn) - zn * _lanemean(dzn * zn))
        sums_ref[7:8, :] += _rowsum(dz2)
        dz2_ref[...] = dz2.astype(BF16)

    return pl.pallas_call(
        body, name=f"mix_odd_bwd1_l{layer}", grid=(n,),
        in_specs=[_tile(tm, D), _tile(tm, D), _tile(tm, D), _full((8, D)), _full((8, D)), ANY],
        out_specs=[_tile(tm, D), _tile(tm, D), _tile(tm, D), _full((16, D))],
        out_shape=[jax.ShapeDtypeStruct((t, D), BF16)] * 3 + [jax.ShapeDtypeStruct((16, D), F32)],
        scratch_shapes=[pltpu.VMEM((NDEV, ROWS_OUT, D), BF16), pltpu.SemaphoreType.DMA((1,))],
        compiler_params=_cparams("arbitrary"),
    )(dxo, y, z2, vec, sm, gw)


def _mix_odd_bwd2(dz2, u, x, dxo, vec, wdw, gw, layer, tm, comm=None):
    t = x.shape[0]
    n = t // tm
    e = tm + 2 * HALO

    def body(dz_ref, dzp_ref, dzn_ref, u_ref, up_ref, un_ref, x_ref, dxo_ref, vec_ref, wdw_ref, g_ref,
             dxi_ref, du_ref, h_ref, sums_ref, dw_ref, w_s, ze, de, sems):
        i = pl.program_id(0)

        @pl.when(i == 0)
        def _():
            _load_weights(g_ref, [(OFF_IN, ROWS_IN, w_s)], sems)
            sums_ref[...] = jnp.zeros_like(sums_ref)
            dw_ref[...] = jnp.zeros_like(dw_ref)

        _fill_glu(ze, up_ref, 0, HALO, i > 0)
        _fill_glu(ze, u_ref, HALO, tm, None)
        _fill_glu(ze, un_ref, HALO + tm, HALO, i < n - 1)
        de[0:HALO, :] = jnp.where(i > 0, dzp_ref[...].astype(F32), 0.0)
        dz2_m = dz_ref[...].astype(F32)
        de[HALO:HALO + tm, :] = dz2_m
        de[HALO + tm:e, :] = jnp.where(i < n - 1, dzn_ref[...].astype(F32), 0.0)
        dz = wdw_ref[0:1] * de[CONF_K:CONF_K + tm]
        for k in range(1, CONF_K):
            dz = dz + wdw_ref[k:k + 1] * de[CONF_K - k:CONF_K - k + tm]
        for k in range(CONF_K):
            dw_ref[k:k + 1, :] += _rowsum(dz2_m * ze[1 + k:1 + k + tm])
        a = u_ref[:, 0:D].astype(F32)
        gg = u_ref[:, D:2 * D].astype(F32)
        sg = _sig(gg)
        da = dz * sg
        dg = dz * a * (sg * (1.0 - sg))
        sums_ref[8:9, :] += _rowsum(da)
        sums_ref[9:10, :] += _rowsum(dg)
        du = jnp.concatenate([da, dg], axis=-1).astype(BF16)
        du_ref[...] = du
        dh = _nn(du, w_s[...].reshape(2 * D, D))
        g, sc, sh = vec_ref[0:1], vec_ref[1:2], vec_ref[2:3]
        h, nrm, r = _norm_fwd(x_ref[...], g, sc, sh)
        h_ref[...] = h.astype(BF16)
        dxi_ref[...] = dxo_ref[...] + _norm_bwd(dh, nrm, r, g, sc)
        sums_ref[1:2, :] += _rowsum(dh)
        sums_ref[2:3, :] += _rowsum(dh * nrm)

        @pl.when(i == n - 1)
        def _():
            p = sums_ref[2:3, :]
            sums_ref[3:4, :] = p * (1.0 + sc)
            sums_ref[2:3, :] = p * g

    prev_u, nxt_u = _halo_specs(tm, 2 * D, t)
    prev_d, nxt_d = _halo_specs(tm, D, t)
    return _host_call(
        body, comm, name=f"mix_odd_bwd2_l{layer}", grid=(n,),
        in_specs=[_tile(tm, D), prev_d, nxt_d, _tile(tm, 2 * D), prev_u, nxt_u, _tile(tm, D), _tile(tm, D),
                  _full((8, D)), _full((32, D)), ANY],
        out_specs=[_tile(tm, D), _tile(tm, 2 * D), _tile(tm, D), _full((16, D)), _full((32, D))],
        out_shape=[jax.ShapeDtypeStruct((t, D), F32), jax.ShapeDtypeStruct((t, 2 * D), BF16),
                   jax.ShapeDtypeStruct((t, D), BF16), jax.ShapeDtypeStruct((16, D), F32),
                   jax.ShapeDtypeStruct((32, D), F32)],
        scratch_shapes=[pltpu.VMEM((NDEV, ROWS_IN, D), BF16), pltpu.VMEM((e, D), F32), pltpu.VMEM((e, D), F32),
                        pltpu.SemaphoreType.DMA((1,))],
        operands=(dz2, dz2, dz2, u, u, u, x, dxo, vec, wdw, gw))


FCH = FF // 2


def _ffn_fwd(x, vec, gw, layer, tm, comm=None):
    t = x.shape[0]

    def body(x_ref, vec_ref, g_ref, xo_ref, a_ref, b_ref, y_ref, wg_s, wu_s, wd_s, sems):
        @pl.when(pl.program_id(0) == 0)
        def _():
            _load_weights(g_ref, [(OFF_G, FS, wg_s), (OFF_U, FS, wu_s), (OFF_D, FS, wd_s)], sems)

        xv = x_ref[...]
        h, _, _ = _norm_fwd(xv, vec_ref[0:1], vec_ref[1:2], vec_ref[2:3])
        hb = h.astype(BF16)
        y = jnp.zeros((tm, D), F32)
        for ch in range(2):
            a = _nt(hb, wg_s[4 * ch:4 * ch + 4].reshape(FCH, D))
            b = _nt(hb, wu_s[4 * ch:4 * ch + 4].reshape(FCH, D))
            a_ref[:, ch * FCH:(ch + 1) * FCH] = a.astype(BF16)
            b_ref[:, ch * FCH:(ch + 1) * FCH] = b.astype(BF16)
            s = (a * _sig(a)) * b
            y = y + _nn(s.astype(BF16), wd_s[4 * ch:4 * ch + 4].reshape(FCH, D))
        y_ref[...] = y.astype(BF16)
        xo_ref[...] = xv + vec_ref[3:4] * y

    wsc = pltpu.VMEM((NDEV, FS, D), BF16)
    return _host_call(
        body, comm, name=f"ffn_fwd_l{layer}", grid=(t // tm,),
        in_specs=[_tile(tm, D), _full((8, D)), ANY],
        out_specs=[_tile(tm, D), _tile(tm, FF), _tile(tm, FF), _tile(tm, D)],
        out_shape=[jax.ShapeDtypeStruct((t, D), F32), jax.ShapeDtypeStruct((t, FF), BF16),
                   jax.ShapeDtypeStruct((t, FF), BF16), jax.ShapeDtypeStruct((t, D), BF16)],
        scratch_shapes=[wsc, wsc, wsc, pltpu.SemaphoreType.DMA((3,))],
        operands=(x, vec, gw))


def _ffn_bwd_hidden(dxo, y, a, b, vec, gw, layer, tm, comm=None):
    t = dxo.shape[0]

    def body(dxo_ref, y_ref, a_ref, b_ref, vec_ref, g_ref, dy_ref, da_ref, db_ref, sums_ref, wd_s, sems):
        @pl.when(pl.program_id(0) == 0)
        def _():
            _load_weights(g_ref, [(OFF_D, FS, wd_s)], sems)
            sums_ref[...] = jnp.zeros_like(sums_ref)

        dxo_m = dxo_ref[...]
        sums_ref[0:1, :] += _rowsum(dxo_m * y_ref[...].astype(F32))
        dyb = (vec_ref[3:4] * dxo_m).astype(BF16)
        dy_ref[...] = dyb
        for ch in range(2):
            cols = slice(ch * FCH, (ch + 1) * FCH)
            ds = _nt(dyb, wd_s[4 * ch:4 * ch + 4].reshape(FCH, D))
            av = a_ref[:, cols].astype(F32)
            bv = b_ref[:, cols].astype(F32)
            sg = _sig(av)
            db_ref[:, cols] = (ds * (av * sg)).astype(BF16)
            da_ref[:, cols] = (ds * bv * (sg * (1.0 + av * (1.0 - sg)))).astype(BF16)

    return _host_call(
        body, comm, name=f"ffn_bwd_hidden_l{layer}", grid=(t // tm,),
        in_specs=[_tile(tm, D), _tile(tm, D), _tile(tm, FF), _tile(tm, FF), _full((8, D)), ANY],
        out_specs=[_tile(tm, D), _tile(tm, FF), _tile(tm, FF), _full((8, D))],
        out_shape=[jax.ShapeDtypeStruct((t, D), BF16), jax.ShapeDtypeStruct((t, FF), BF16),
                   jax.ShapeDtypeStruct((t, FF), BF16), jax.ShapeDtypeStruct((8, D), F32)],
        scratch_shapes=[pltpu.VMEM((NDEV, FS, D), BF16), pltpu.SemaphoreType.DMA((1,))],
        operands=(dxo, y, a, b, vec, gw))


def _ffn_bwd_input(da, db, x, dxo, vec, gw, layer, tm, comm=None):
    t = x.shape[0]
    n = t // tm

    def body(da_ref, db_ref, x_ref, dxo_ref, vec_ref, g_ref, dxi_ref, h_ref, sums_ref, wg_s, wu_s, sems):
        i = pl.program_id(0)

        @pl.when(i == 0)
        def _():
            _load_weights(g_ref, [(OFF_G, FS, wg_s), (OFF_U, FS, wu_s)], sems)
            sums_ref[...] = jnp.zeros_like(sums_ref)

        dh = _nn(da_ref[...], wg_s[...].reshape(FF, D)) + _nn(db_ref[...], wu_s[...].reshape(FF, D))
        g, sc, sh = vec_ref[0:1], vec_ref[1:2], vec_ref[2:3]
        h, nrm, r = _norm_fwd(x_ref[...], g, sc, sh)
        h_ref[...] = h.astype(BF16)
        dxi_ref[...] = dxo_ref[...] + _norm_bwd(dh, nrm, r, g, sc)
        sums_ref[1:2, :] += _rowsum(dh)
        sums_ref[2:3, :] += _rowsum(dh * nrm)

        @pl.when(i == n - 1)
        def _():
            p = sums_ref[2:3, :]
            sums_ref[3:4, :] = p * (1.0 + sc)
            sums_ref[2:3, :] = p * g

    wsc = pltpu.VMEM((NDEV, FS, D), BF16)
    return _host_call(
        body, comm, name=f"ffn_bwd_input_l{layer}", grid=(n,),
        in_specs=[_tile(tm, FF), _tile(tm, FF), _tile(tm, D), _tile(tm, D), _full((8, D)), ANY],
        out_specs=[_tile(tm, D), _tile(tm, D), _full((8, D))],
        out_shape=[jax.ShapeDtypeStruct((t, D), F32), jax.ShapeDtypeStruct((t, D), BF16),
                   jax.ShapeDtypeStruct((8, D), F32)],
        scratch_shapes=[wsc, wsc, pltpu.SemaphoreType.DMA((2,))],
        operands=(da, db, x, dxo, vec, gw))


def _wgrad(lhs, rhs, name, tk, lhs2=None):
    t, m = lhs.shape
    n = t // tk

    def body(*refs):
        if lhs2 is None:
            l_ref, r_ref, o_ref, acc = refs
            lv = l_ref[...]
        else:
            l_ref, l2_ref, r_ref, o_ref, acc = refs
            av = l_ref[...].astype(F32)
            lv = ((av * _sig(av)) * l2_ref[...].astype(F32)).astype(BF16)
        i = pl.program_id(0)

        @pl.when(i == 0)
        def _():
            acc[...] = jnp.zeros_like(acc)

        acc[...] += _tn(lv, r_ref[...])

        @pl.when(i == n - 1)
        def _():
            o_ref[...] = acc[...].astype(BF16)

    ins = [lhs] + ([] if lhs2 is None else [lhs2]) + [rhs]
    specs = [_tile(tk, m)] * (len(ins) - 1) + [_tile(tk, D)]
    return pl.pallas_call(
        body, name=name, grid=(n,),
        in_specs=specs, out_specs=_full((m, D)), out_shape=jax.ShapeDtypeStruct((m, D), BF16),
        scratch_shapes=[pltpu.VMEM((m, D), F32)],
        compiler_params=_cparams("arbitrary"),
    )(*ins)


def _final(x, tgt, gf, tm):
    t = x.shape[0]

    def body(x_ref, t_ref, g_ref, dx_ref, sums_ref):
        @pl.when(pl.program_id(0) == 0)
        def _():
            sums_ref[...] = jnp.zeros_like(sums_ref)

        xv = x_ref[...]
        r = lax.rsqrt(_lanemean(xv * xv) + RMS_EPS)
        nrm = xv * r
        g = g_ref[...]
        err = nrm * g - t_ref[...]
        sums_ref[1:2, :] += _rowsum(err * err) * (0.5 / D)
        dout = err * (1.0 / D)
        sums_ref[0:1, :] += _rowsum(dout * nrm)
        dn = dout * g
        dx_ref[...] = r * (dn - nrm * _lanemean(dn * nrm))

    return pl.pallas_call(
        body, name="loss_head", grid=(t // tm,),
        in_specs=[_tile(tm, D), _tile(tm, D), _full((1, D))],
        out_specs=[_tile(tm, D), _full((8, D))],
        out_shape=[jax.ShapeDtypeStruct((t, D), F32), jax.ShapeDtypeStruct((8, D), F32)],
        compiler_params=_cparams("arbitrary"),
    )(x, tgt, gf)


ADAM_ROWS = LROWS // 5


def _adam_big(recv, w, m, v):
    def body(r_ref, w_ref, m_ref, v_ref, g_ref, d_ref, mo_ref, vo_ref):
        g = r_ref[0, 0].astype(F32)
        for s in range(1, NDEV):
            g = g + r_ref[s, 0].astype(F32)
        delta, m2, v2 = _adam(w_ref[0], g, m_ref[0], v_ref[0])
        g_ref[0], d_ref[0], mo_ref[0], vo_ref[0] = g, delta, m2, v2

    blk = pl.BlockSpec((1, ADAM_ROWS, D), lambda l, j: (l, j, 0))
    sds = jax.ShapeDtypeStruct(w.shape, F32)
    return pl.pallas_call(
        body, name="adam_big", grid=(DEPTH, LROWS // ADAM_ROWS),
        in_specs=[pl.BlockSpec((NDEV, 1, ADAM_ROWS, D), lambda l, j: (0, l, j, 0)), blk, blk, blk],
        out_specs=[blk] * 4, out_shape=[sds] * 4,
        compiler_params=_cparams("arbitrary", "arbitrary"),
    )(recv, w, m, v)


def _sum_small(gathered, rows):
    def body(g_ref, o_ref):
        acc = g_ref[0:rows, :]
        for s in range(1, NDEV):
            acc = acc + g_ref[s * rows:(s + 1) * rows, :]
        o_ref[...] = acc

    return pl.pallas_call(
        body, name="sum_small",
        in_specs=[pl.BlockSpec(memory_space=pltpu.VMEM)], out_specs=pl.BlockSpec(memory_space=pltpu.VMEM),
        out_shape=jax.ShapeDtypeStruct((rows, D), F32),
        compiler_params=pltpu.CompilerParams(vmem_limit_bytes=VMEM_LIMIT),
    )(gathered)


def _adam_small(params):
    k = len(params)

    def body(*refs):
        ins, outs = refs[:4 * k], refs[4 * k:]
        for j in range(k):
            w_ref, g_ref, m_ref, v_ref = ins[4 * j:4 * j + 4]
            delta, m2, v2 = _adam(w_ref[...], g_ref[...], m_ref[...], v_ref[...])
            outs[3 * j][...], outs[3 * j + 1][...], outs[3 * j + 2][...] = delta, m2, v2

    flat = [a for p in params for a in p]
    shapes = [jax.ShapeDtypeStruct(p[0].shape, F32) for p in params for _ in range(3)]
    vm = pl.BlockSpec(memory_space=pltpu.VMEM)
    res = pl.pallas_call(
        body, name="adam_small", in_specs=[vm] * len(flat), out_specs=[vm] * len(shapes), out_shape=shapes,
        compiler_params=pltpu.CompilerParams(vmem_limit_bytes=VMEM_LIMIT),
    )(*flat)
    return [tuple(res[3 * j:3 * j + 3]) for j in range(k)]


def _pack(ab_in, ab_out, pw1, pw2, wg, wu, wd):
    ins = jnp.swapaxes(jnp.stack([ab_in[0], pw1[0], ab_in[1], pw1[1]]), 1, 2)
    outs = jnp.stack([ab_out[0], pw2[0], ab_out[1], pw2[1]])
    return jnp.concatenate([ins, outs, jnp.swapaxes(wg, 1, 2), jnp.swapaxes(wu, 1, 2), wd], axis=1)


def _unpack(p):
    ins = jnp.swapaxes(p[:, OFF_IN:OFF_OUT], 1, 2)
    outs = p[:, OFF_OUT:OFF_G]
    return (ins[0::2], outs[0::2], ins[1::2], outs[1::2], jnp.swapaxes(p[:, OFF_G:OFF_U], 1, 2),
            jnp.swapaxes(p[:, OFF_U:OFF_D], 1, 2), p[:, OFF_D:LROWS])


def _unshard(flat, lead, per):
    k = len(lead)
    a = flat.reshape((NDEV,) + tuple(lead) + (per,))
    a = jnp.transpose(a, tuple(range(1, k + 1)) + (0, k + 1))
    return a.reshape(tuple(lead) + (NDEV * per,))


def _rows_of(a):
    f = a.reshape(-1)
    pad = (-f.shape[0]) % D
    if pad:
        f = jnp.concatenate([f, jnp.zeros((pad,), f.dtype)])
    return f.reshape(-1, D)


def _pad_rows(a, rows):
    return jnp.concatenate([a, jnp.zeros((rows - a.shape[0],) + a.shape[1:], a.dtype)], axis=0)


def kernel(x, c, norm_mix_g, norm_ffn_g, w_mod, b_mod, ab_w_in, ab_conv, ab_w_pool, ab_pool_scale, ab_w_out, cf_w_pw1, cf_b_pw1, cf_w_dw, cf_b_dw, cf_ln_g, cf_ln_b, cf_w_pw2, cf_b_pw2, ffn_w_gate, ffn_w_up, ffn_w_down, final_norm_g, loss_target, m_norm_mix_g, m_norm_ffn_g, m_w_mod, m_b_mod, m_ab_w_in, m_ab_conv, m_ab_w_pool, m_ab_pool_scale, m_ab_w_out, m_cf_w_pw1, m_cf_b_pw1, m_cf_w_dw, m_cf_b_dw, m_cf_ln_g, m_cf_ln_b, m_cf_w_pw2, m_cf_b_pw2, m_ffn_w_gate, m_ffn_w_up, m_ffn_w_down, m_final_norm_g, v_norm_mix_g, v_norm_ffn_g, v_w_mod, v_b_mod, v_ab_w_in, v_ab_conv, v_ab_w_pool, v_ab_pool_scale, v_ab_w_out, v_cf_w_pw1, v_cf_b_pw1, v_cf_w_dw, v_cf_b_dw, v_cf_ln_g, v_cf_ln_b, v_cf_w_pw2, v_cf_b_pw2, v_ffn_w_gate, v_ffn_w_up, v_ffn_w_down, v_final_norm_g):
    t = x.shape[1]
    tm = 512 if t % 512 == 0 else t // 2
    tk = 512 if t % 512 == 0 else t // 2
    tmo = tm // 2
    me = 4 * lax.axis_index("x") + 2 * lax.axis_index("y") + lax.axis_index("c")
    xs, tgt = x[0], loss_target[0]

    sharded = [ab_conv, cf_b_pw1, cf_w_dw, cf_b_dw, cf_ln_g, cf_ln_b, cf_b_pw2]
    flat = jnp.concatenate([a.reshape(-1) for a in sharded])
    n_flat = flat.shape[0]
    g1 = _gather_small(jnp.concatenate([_pad_rows(c, 8), _pad_rows(_rows_of(flat), 16)], axis=0), "gather_cond")
    g1 = g1.reshape(NDEV, 24, D)
    c_all = g1[:, 0, :]
    flat_all = g1[:, 8:, :].reshape(NDEV, -1)[:, :n_flat]
    full, o = [], 0
    for a in sharded:
        lead, per = a.shape[:-1], a.shape[-1]
        size = a.size
        full.append(_unshard(flat_all[:, o:o + size], lead, per))
        o += size
    ab_conv_f, b_pw1_f, w_dw_f, b_dw_f, ln_g_f, ln_b_f, b_pw2_f = full

    b_sl = lax.dynamic_slice_in_dim(b_mod, me * MODW, MODW, axis=1).reshape(DEPTH, 1, MODW)
    mod_part, c_act = _mod_fwd(c_all, w_mod, b_sl)
    g2 = _gather_small(mod_part, "gather_mod").reshape(NDEV, NDEV, DEPTH, MODW)
    mod = jnp.transpose(lax.dynamic_index_in_dim(g2, me, axis=1, keepdims=False), (1, 0, 2)).reshape(DEPTH, N_MOD, D)
    zeros4 = jnp.zeros((4, D), F32)

    def vec_of(g, layer, k):
        return jnp.concatenate([g[layer][None], mod[layer, k + 1][None], mod[layer, k][None],
                                mod[layer, k + 2][None], zeros4], axis=0)

    vmix = [vec_of(norm_mix_g, l, 0) for l in range(DEPTH)]
    vffn = [vec_of(norm_ffn_g, l, 3) for l in range(DEPTH)]

    w_pack = _pack(ab_w_in, ab_w_out, cf_w_pw1, cf_w_pw2, ffn_w_gate, ffn_w_up, ffn_w_down)
    p16 = w_pack.astype(BF16)
    gw = [_gather_weights(p16[0])]

    wp16 = ab_w_pool.astype(BF16)
    wdw32 = [_pad_rows(w_dw_f[i], 32) for i in range(2)]
    sm_odd = [jnp.concatenate([b_dw_f[i][None], ln_g_f[i][None], ln_b_f[i][None], b_pw2_f[i][None], zeros4], axis=0)
              for i in range(2)]
    zero_bias = jnp.zeros((1, 2 * D), F32)

    saved = []
    xc = xs
    for l in range(DEPTH):
        i = l // 2
        if l % 2 == 0:
            u = _inproj(xc, vmix[l], zero_bias, gw[l], l, tm)
            x_mid, y_mix = _mix_even_fwd(u, xc, vmix[l], ab_conv_f[i], wp16[i], ab_pool_scale[i][None], gw[l], l, tm)
            z2 = None
        else:
            u = _inproj(xc, vmix[l], b_pw1_f[i][None], gw[l], l, tm)
            x_mid, y_mix, z2 = _mix_odd_fwd(u, xc, vmix[l], wdw32[i], sm_odd[i], gw[l], l, tmo)
        if l + 1 < DEPTH:
            x_out, a, b, y_ffn, g_next = _ffn_fwd(x_mid, vffn[l], gw[l], l, tm, _gather_comm(p16[l + 1]))
            gw.append(g_next)
        else:
            x_out, a, b, y_ffn = _ffn_fwd(x_mid, vffn[l], gw[l], l, tm)
        saved.append((xc, u, y_mix, z2, x_mid, a, b, y_ffn))
        xc = x_out

    dx, fsum = _final(xc, tgt, final_norm_g[None], tm)
    loss = lax.psum(jnp.sum(fsum[1]), ("x", "y", "c"))
    d_final_g = fsum[0]

    recv = _empty_recv()
    io_specs = [(OFF_IN, ROWS_IN), (OFF_OUT, ROWS_OUT)]
    pending = None
    dmod = [None] * DEPTH
    d_mix_g, d_ffn_g = [None] * DEPTH, [None] * DEPTH
    d_conv, d_pool, d_pscale = [None] * 2, [None] * 2, [None] * 2
    d_bpw1, d_wdw, d_bdw, d_lng, d_lnb, d_bpw2 = ([None] * 2 for _ in range(6))
    for l in reversed(range(DEPTH)):
        i = l // 2
        x_in, u, y_mix, z2, x_mid, a, b, y_ffn = saved[l]
        if pending is None:
            dy, da, db, s_h = _ffn_bwd_hidden(dx, y_ffn, a, b, vffn[l], gw[l], l, tm)
        else:
            dy, da, db, s_h, recv = _ffn_bwd_hidden(dx, y_ffn, a, b, vffn[l], gw[l], l, tm,
                                                    _scatter_comm(pending, io_specs, recv, l + 1))
        g_down = _wgrad(a, dy, f"wgrad_down_l{l}", tk, lhs2=b)
        dx_mid, h2, s_i, recv = _ffn_bwd_input(da, db, x_mid, dx, vffn[l], gw[l], l, tm,
                                               _scatter_comm([g_down], [(OFF_D, FS)], recv, l))
        g_gate = _wgrad(da, h2, f"wgrad_gate_l{l}", tk)
        g_up = _wgrad(db, h2, f"wgrad_up_l{l}", tk)
        gu_comm = _scatter_comm([g_gate, g_up], [(OFF_G, FS), (OFF_U, FS)], recv, l)
        d_ffn_g[l] = s_i[3]
        mod_ffn = [s_i[1], s_i[2], s_h[0]]
        if l % 2 == 0:
            dx, du, h, cat, dym, s_m, dwp, recv = _mix_even_bwd(dx_mid, u, x_in, y_mix, vmix[l], ab_conv_f[i], wp16[i],
                                                                ab_pool_scale[i][None], gw[l], l, tm, gu_comm)
            g_out = _wgrad(cat, dym, f"wgrad_out_l{l}", tk)
            d_conv[i], d_pool[i], d_pscale[i] = s_m[4:7, :DA], dwp, s_m[7, :DA]
            mod_mix = [s_m[1], s_m[2], s_m[0]]
            d_mix_g[l] = s_m[3]
        else:
            dym, z3, dz2, s_1 = _mix_odd_bwd1(dx_mid, y_mix, z2, vmix[l], sm_odd[i], gw[l], l, tm)
            dx, du, h, s_2, dwdw, recv = _mix_odd_bwd2(dz2, u, x_in, dx_mid, vmix[l], wdw32[i], gw[l], l, tmo, gu_comm)
            g_out = _wgrad(z3, dym, f"wgrad_out_l{l}", tk)
            d_bpw1[i], d_wdw[i], d_bdw[i] = s_2[8:10].reshape(2 * D), dwdw[:CONF_K], s_1[7]
            d_lng[i], d_lnb[i], d_bpw2[i] = s_1[5], s_1[6], s_1[4]
            mod_mix = [s_2[1], s_2[2], s_1[0]]
            d_mix_g[l] = s_2[3]
        pending = [_wgrad(du, h, f"wgrad_in_l{l}", tk), g_out]
        dmod[l] = jnp.stack(mod_mix + mod_ffn)
    grad_x = dx[None]
    recv = _scatter_grads(pending, io_specs, recv, 0)

    m_pack = _pack(m_ab_w_in, m_ab_w_out, m_cf_w_pw1, m_cf_w_pw2, m_ffn_w_gate, m_ffn_w_up, m_ffn_w_down)
    v_pack = _pack(v_ab_w_in, v_ab_w_out, v_cf_w_pw1, v_cf_w_pw2, v_ffn_w_gate, v_ffn_w_up, v_ffn_w_down)
    big = [_unpack(p) for p in _adam_big(recv, w_pack, m_pack, v_pack)]

    small = [jnp.stack(dmod), jnp.stack(d_mix_g), jnp.stack(d_ffn_g), d_final_g, jnp.stack(d_pool),
             jnp.stack(d_pscale), jnp.stack(d_conv), jnp.stack(d_bpw1), jnp.stack(d_wdw), jnp.stack(d_bdw),
             jnp.stack(d_lng), jnp.stack(d_lnb), jnp.stack(d_bpw2)]
    small_rows = [_rows_of(a) for a in small]
    n_rows = sum(a.shape[0] for a in small_rows)
    pad_rows = -(-n_rows // 8) * 8
    g3 = _gather_small(_pad_rows(jnp.concatenate(small_rows, axis=0), pad_rows), "gather_small_grads")
    summed = _sum_small(g3, pad_rows)
    outs, o = [], 0
    for a, r in zip(small, small_rows):
        outs.append(summed[o:o + r.shape[0]].reshape(-1)[:a.size].reshape(a.shape))
        o += r.shape[0]
    (g_bmod, g_mix_g, g_ffn_g, g_final, g_pool, g_pscale, g_conv, g_bpw1, g_wdw, g_bdw, g_lng, g_lnb, g_bpw2) = outs
    g_bmod = g_bmod.reshape(DEPTH, N_MOD * D)

    def my_shard(a):
        per = a.shape[-1] // NDEV
        return lax.dynamic_slice_in_dim(a, me * per, per, axis=a.ndim - 1)

    g_conv, g_bpw1, g_wdw, g_bdw, g_lng, g_lnb, g_bpw2 = [
        my_shard(a) for a in (g_conv, g_bpw1, g_wdw, g_bdw, g_lng, g_lnb, g_bpw2)]

    dmod_all = g3.reshape(NDEV, pad_rows, D)[:, :DEPTH * N_MOD, :].reshape(NDEV, DEPTH, N_MOD * D)
    dmod_mine = jnp.transpose(lax.dynamic_slice_in_dim(dmod_all, me * MODW, MODW, axis=2), (1, 0, 2))
    g_wmod, d_wmod, nm_wmod, nv_wmod = _mod_bwd_adam(c_act.T, dmod_mine, w_mod, m_w_mod, v_w_mod)

    small_params = [
        (norm_mix_g, g_mix_g, m_norm_mix_g, v_norm_mix_g), (norm_ffn_g, g_ffn_g, m_norm_ffn_g, v_norm_ffn_g),
        (b_mod, g_bmod, m_b_mod, v_b_mod), (ab_conv, g_conv, m_ab_conv, v_ab_conv),
        (ab_w_pool, g_pool, m_ab_w_pool, v_ab_w_pool), (ab_pool_scale, g_pscale, m_ab_pool_scale, v_ab_pool_scale),
        (cf_b_pw1, g_bpw1, m_cf_b_pw1, v_cf_b_pw1), (cf_w_dw, g_wdw, m_cf_w_dw, v_cf_w_dw),
        (cf_b_dw, g_bdw, m_cf_b_dw, v_cf_b_dw), (cf_ln_g, g_lng, m_cf_ln_g, v_cf_ln_g),
        (cf_ln_b, g_lnb, m_cf_ln_b, v_cf_ln_b), (cf_b_pw2, g_bpw2, m_cf_b_pw2, v_cf_b_pw2),
        (final_norm_g, g_final, m_final_norm_g, v_final_norm_g)]

    def two_d(a):
        return a.reshape(-1, a.shape[-1])

    upd = _adam_small([tuple(two_d(a) for a in p) for p in small_params])
    upd = [tuple(r.reshape(p[0].shape) for r in u) for u, p in zip(upd, small_params)]
    (s_mix, s_ffn, s_bmod, s_conv, s_pool, s_pscale, s_bpw1, s_wdw, s_bdw, s_lng, s_lnb, s_bpw2, s_final) = upd
    small_g = [p[1] for p in small_params]
    (q_mix, q_ffn, q_bmod, q_conv, q_pool, q_pscale, q_bpw1, q_wdw, q_bdw, q_lng, q_lnb, q_bpw2, q_final) = small_g

    def ordered(k):
        ab_in, ab_out, pw1, pw2, wg, wu, wd = big[k]
        if k == 0:
            sm = dict(mix=q_mix, ffn=q_ffn, bmod=q_bmod, conv=q_conv, pool=q_pool, pscale=q_pscale, bpw1=q_bpw1,
                      wdw=q_wdw, bdw=q_bdw, lng=q_lng, lnb=q_lnb, bpw2=q_bpw2, final=q_final)
            wmod = g_wmod
        else:
            j = k - 1
            sm = dict(mix=s_mix[j], ffn=s_ffn[j], bmod=s_bmod[j], conv=s_conv[j], pool=s_pool[j], pscale=s_pscale[j],
                      bpw1=s_bpw1[j], wdw=s_wdw[j], bdw=s_bdw[j], lng=s_lng[j], lnb=s_lnb[j], bpw2=s_bpw2[j],
                      final=s_final[j])
            wmod = (d_wmod, nm_wmod, nv_wmod)[j]
        return [sm["mix"], sm["ffn"], wmod, sm["bmod"], ab_in, sm["conv"], sm["pool"], sm["pscale"], ab_out,
                pw1, sm["bpw1"], sm["wdw"], sm["bdw"], sm["lng"], sm["lnb"], pw2, sm["bpw2"], wg, wu, wd, sm["final"]]

    return (loss, grad_x, *ordered(0), *ordered(1), *ordered(2), *ordered(3))
```

```python
import functools

import jax
import jax.numpy as jnp
from jax import lax
from jax.experimental import pallas as pl
from jax.experimental.pallas import tpu as pltpu

F32 = jnp.float32
BF16 = jnp.bfloat16
MESH = pl.DeviceIdType.MESH

NDEV = 8
DEPTH = 4
D = 1024
FF = 2816
FS = FF // NDEV
DA = 512
PG = 128
POOL = ((2, 1, 0), (4, 2, 1), (8, 4, 3), (16, 8, 7))
CONF_K = 31
CONF_L = 15
N_MOD = 6
MODW = N_MOD * D // NDEV
RMS_EPS = 1e-6
LN_EPS = 1e-5

ROWS_IN, ROWS_OUT = 2 * D // NDEV, D // NDEV
OFF_IN, OFF_OUT = 0, ROWS_IN
OFF_G = OFF_OUT + ROWS_OUT
OFF_U = OFF_G + FS
OFF_D = OFF_U + FS
LROWS = OFF_D + FS

HALO = 16
VMEM_LIMIT = 60 * 1024 * 1024

ADAM_LR, ADAM_B1, ADAM_B2, ADAM_EPS, ADAM_WD, ADAM_STEP = 1e-3, 0.9, 0.999, 1e-8, 0.01, 10
ADAM_C1 = 1.0 / (1.0 - ADAM_B1 ** ADAM_STEP)
ADAM_C2 = 1.0 / (1.0 - ADAM_B2 ** ADAM_STEP)


def _nn(a, b):
    return jnp.dot(a, b, preferred_element_type=F32)


def _nt(a, b):
    return lax.dot_general(a, b, (((1,), (1,)), ((), ())), preferred_element_type=F32)


def _tn(a, b):
    return lax.dot_general(a, b, (((0,), (0,)), ((), ())), preferred_element_type=F32)


def _sig(x):
    return 1.0 / (1.0 + jnp.exp(-x))


def _rowsum(x):
    return jnp.sum(x, axis=0, keepdims=True)


def _lanemean(x):
    return jnp.mean(x, axis=-1, keepdims=True)


def _norm_fwd(x, g, sc, sh):
    r = lax.rsqrt(_lanemean(x * x) + RMS_EPS)
    n = x * r
    return n * (g * (1.0 + sc)) + sh, n, r


def _norm_bwd(dh, n, r, g, sc):
    dn = dh * (g * (1.0 + sc))
    return r * (dn - n * _lanemean(dn * n))


def _adam(w, g, m, v):
    m2 = ADAM_B1 * m + (1.0 - ADAM_B1) * g
    v2 = ADAM_B2 * v + (1.0 - ADAM_B2) * (g * g)
    delta = -ADAM_LR * ((m2 * ADAM_C1) / (jnp.sqrt(v2 * ADAM_C2) + ADAM_EPS) + ADAM_WD * w)
    return delta, m2, v2


def _load_weights(g_ref, specs, sems):
    cps = [pltpu.make_async_copy(g_ref.at[:, pl.ds(off, rows), :], dst, sems.at[k])
           for k, (off, rows, dst) in enumerate(specs)]
    for cp in cps:
        cp.start()
    for cp in cps:
        cp.wait()


def _cparams(*sem):
    return pltpu.CompilerParams(dimension_semantics=sem if sem else None, vmem_limit_bytes=VMEM_LIMIT)


def _tile(tm, w):
    return pl.BlockSpec((tm, w), lambda i: (i, 0))


def _full(shape):
    nd = len(shape)
    return pl.BlockSpec(shape, lambda i: (0,) * nd)


def _halo_specs(tm, w, total_rows):
    tb = tm // HALO
    nb = total_rows // HALO
    prev = pl.BlockSpec((HALO, w), lambda i: (jnp.maximum(i * tb - 1, 0), 0))
    nxt = pl.BlockSpec((HALO, w), lambda i: (jnp.minimum((i + 1) * tb, nb - 1), 0))
    return prev, nxt


ANY = pl.BlockSpec(memory_space=pl.ANY)


def _peers():
    x, y, c = lax.axis_index("x"), lax.axis_index("y"), lax.axis_index("c")
    return x, y, c


def _gather_small(v, name):
    m_per, n = v.shape

    def body(x_ref, out_ref, send_sems, recv_sems, local_sem):
        x, y, c = _peers()
        me, sibling = (x, y, c), (x, y, 1 - c)
        chips = [(1 - x, y), (x, 1 - y), (1 - x, 1 - y)]

        def rows(px, py, pc):
            return out_ref.at[pl.ds((4 * px + 2 * py + pc) * m_per, m_per), :]

        def copy(k, block, to, src=None):
            return pltpu.make_async_remote_copy(
                src_ref=rows(*block) if src is None else src, dst_ref=rows(*block),
                send_sem=send_sems.at[k], recv_sem=recv_sems.at[k], device_id=to, device_id_type=MESH)

        mine = pltpu.make_async_copy(x_ref, rows(*me), local_sem)
        mine.start()
        first = [copy(0, me, sibling, src=x_ref)]
        first += [copy(1 + j, me, (*chip, c), src=x_ref) for j, chip in enumerate(chips)]
        for cp in first:
            cp.start()
        passed = [copy(4 + j, (*chip, c), sibling) for j, chip in enumerate(chips)]
        for j, chip in enumerate(chips):
            copy(1 + j, (*chip, c), me).wait_recv()
            passed[j].start()
        copy(0, sibling, me).wait_recv()
        for j, chip in enumerate(chips):
            copy(4 + j, (*chip, 1 - c), me).wait_recv()
        for cp in first + passed:
            cp.wait_send()
        mine.wait()

    return pl.pallas_call(
        body, name=name,
        out_shape=jax.ShapeDtypeStruct((NDEV * m_per, n), v.dtype),
        in_specs=[pl.BlockSpec(memory_space=pltpu.VMEM)],
        out_specs=pl.BlockSpec(memory_space=pltpu.VMEM),
        scratch_shapes=[pltpu.SemaphoreType.DMA((7,)), pltpu.SemaphoreType.DMA((7,)), pltpu.SemaphoreType.DMA],
        compiler_params=pltpu.CompilerParams(vmem_limit_bytes=VMEM_LIMIT),
    )(v)


class _Comm:
    def __init__(self, ins, outs, aliases, bind):
        self.ins, self.outs, self.aliases, self.bind = ins, outs, aliases, bind


COMM_SEMS = [pltpu.SemaphoreType.DMA((7,)), pltpu.SemaphoreType.DMA((7,)), pltpu.SemaphoreType.DMA]


def _gather_hooks(p_ref, out_ref, send_sems, recv_sems, local_sem):
    def parts():
        x, y, c = _peers()
        me, sibling = (x, y, c), (x, y, 1 - c)
        chips = [(1 - x, y), (x, 1 - y), (1 - x, 1 - y)]

        def slab(px, py, pc):
            return out_ref.at[4 * px + 2 * py + pc]

        def copy(k, block, to, src=None):
            return pltpu.make_async_remote_copy(
                src_ref=slab(*block) if src is None else src, dst_ref=slab(*block),
                send_sem=send_sems.at[k], recv_sem=recv_sems.at[k], device_id=to, device_id_type=MESH)

        mine = pltpu.make_async_copy(p_ref, slab(*me), local_sem)
        first = [copy(0, me, sibling, src=p_ref)]
        first += [copy(1 + j, me, (*chip, c), src=p_ref) for j, chip in enumerate(chips)]
        passed = [copy(4 + j, (*chip, c), sibling) for j, chip in enumerate(chips)]
        from_chips = [copy(1 + j, (*chip, c), me) for j, chip in enumerate(chips)]
        from_sibling = [copy(0, sibling, me)] + [copy(4 + j, (*chip, 1 - c), me) for j, chip in enumerate(chips)]
        return mine, first, passed, from_chips, from_sibling

    def start():
        mine, first, _, _, _ = parts()
        mine.start()
        for cp in first:
            cp.start()

    def middle():
        _, _, passed, from_chips, _ = parts()
        for arrived, onward in zip(from_chips, passed):
            arrived.wait_recv()
            onward.start()

    def end():
        mine, first, passed, _, from_sibling = parts()
        for cp in from_sibling:
            cp.wait_recv()
        for cp in first + passed:
            cp.wait_send()
        mine.wait()

    return start, middle, end


def _gather_comm(p):
    return _Comm([p], [jax.ShapeDtypeStruct((NDEV,) + p.shape, p.dtype)], {},
                 lambda cins, couts, sems: _gather_hooks(cins[0], couts[0], *sems))


def _gather_weights(p):
    def body(p_ref, out_ref, send_sems, recv_sems, local_sem):
        for hook in _gather_hooks(p_ref, out_ref, send_sems, recv_sems, local_sem):
            hook()

    return pl.pallas_call(
        body, name="gather_weights",
        out_shape=jax.ShapeDtypeStruct((NDEV,) + p.shape, p.dtype),
        in_specs=[ANY], out_specs=ANY, scratch_shapes=COMM_SEMS,
    )(p)


def _scatter_hooks(src_refs, specs, r_ref, layer, send_sems, recv_sems, local_sem):
    off0 = specs[0][0]
    total = sum(rows for _, rows in specs)

    def start():
        x, y, c = _peers()
        me = 4 * x + 2 * y + c

        def part(k, dev):
            off, rows = specs[k]
            src = src_refs[k].at[pl.ds(pl.multiple_of(dev * rows, 16), rows), :]
            return src, r_ref.at[me, layer, pl.ds(off, rows), :]

        for k in range(len(specs)):
            src, dst = part(k, me)
            pltpu.make_async_copy(src, dst, local_sem).start()
        for r in range(1, NDEV):
            px = 1 - x if r & 4 else x
            py = 1 - y if r & 2 else y
            pc = 1 - c if r & 1 else c
            for k in range(len(specs)):
                src, dst = part(k, 4 * px + 2 * py + pc)
                pltpu.make_async_remote_copy(
                    src_ref=src, dst_ref=dst, send_sem=send_sems.at[r - 1], recv_sem=recv_sems.at[r - 1],
                    device_id=(px, py, pc), device_id_type=MESH).start()

    def end():
        x, y, c = _peers()
        whole = r_ref.at[0, layer, pl.ds(off0, total), :]
        for r in range(1, NDEV):
            done = pltpu.make_async_remote_copy(
                src_ref=whole, dst_ref=whole, send_sem=send_sems.at[r - 1], recv_sem=recv_sems.at[r - 1],
                device_id=(x, y, c), device_id_type=MESH)
            done.wait_recv()
            done.wait_send()
        pltpu.make_async_copy(whole, whole, local_sem).wait()

    return start, None, end


def _scatter_comm(srcs, specs, recv, layer):
    k = len(srcs)
    return _Comm(list(srcs) + [recv], [jax.ShapeDtypeStruct(recv.shape, recv.dtype)], {k: 0},
                 lambda cins, couts, sems: _scatter_hooks(cins[:k], specs, couts[0], layer, *sems))


def _scatter_grads(srcs, specs, recv, layer):
    k = len(srcs)

    def body(*refs):
        start, _, end = _scatter_hooks(refs[:k], specs, refs[k + 1], layer, *refs[k + 2:])
        start()
        end()

    return pl.pallas_call(
        body, name=f"scatter_grads_l{layer}",
        out_shape=jax.ShapeDtypeStruct(recv.shape, recv.dtype),
        in_specs=[ANY] * (k + 1), out_specs=ANY, scratch_shapes=COMM_SEMS,
        input_output_aliases={k: 0},
    )(*srcs, recv)


def _empty_recv():
    def body(o_ref):
        del o_ref

    return pl.pallas_call(body, name="recv_buffer", out_specs=ANY,
                          out_shape=jax.ShapeDtypeStruct((NDEV, DEPTH, LROWS, D), BF16))()


def _host_call(inner, comm, *, name, grid, in_specs, out_specs, out_shape, scratch_shapes, operands):
    if comm is None:
        return pl.pallas_call(
            inner, name=name, grid=grid, in_specs=in_specs, out_specs=out_specs, out_shape=out_shape,
            scratch_shapes=scratch_shapes, compiler_params=_cparams("arbitrary"))(*operands)
    n_in, n_out, n_s = len(in_specs), len(out_specs), len(scratch_shapes)
    k_in, k_out = len(comm.ins), len(comm.outs)
    steps = grid[0]

    def body(*refs):
        ins, cins = refs[:n_in], refs[n_in:n_in + k_in]
        o0 = n_in + k_in
        outs, couts = refs[o0:o0 + n_out], refs[o0 + n_out:o0 + n_out + k_out]
        s0 = o0 + n_out + k_out
        scr, sems = refs[s0:s0 + n_s], refs[s0 + n_s:]
        start, middle, end = comm.bind(cins, couts, sems)
        i = pl.program_id(0)
        pl.when(i == 0)(start)
        if middle is not None:
            pl.when(i == steps * 3 // 4)(middle)
        inner(*ins, *outs, *scr)
        pl.when(i == steps - 1)(end)

    return pl.pallas_call(
        body, name=name, grid=grid, in_specs=list(in_specs) + [ANY] * k_in,
        out_specs=list(out_specs) + [ANY] * k_out, out_shape=list(out_shape) + list(comm.outs),
        scratch_shapes=list(scratch_shapes) + COMM_SEMS,
        input_output_aliases={n_in + a: n_out + b for a, b in comm.aliases.items()},
        compiler_params=_cparams("arbitrary"))(*operands, *comm.ins)


def _mod_fwd(c_all, w_mod, b_sl):
    def body(c_ref, w_ref, b_ref, o_ref, ca_ref):
        cv = c_ref[...]
        ca = cv * _sig(cv)
        ca_ref[...] = ca
        o_ref[...] = jnp.dot(ca, w_ref[0], preferred_element_type=F32, precision=lax.Precision.HIGHEST) + b_ref[0]

    return pl.pallas_call(
        body, name="mod_fwd", grid=(DEPTH,),
        in_specs=[_full((NDEV, D)), pl.BlockSpec((1, D, MODW), lambda l: (l, 0, 0)),
                  pl.BlockSpec((1, 1, MODW), lambda l: (l, 0, 0))],
        out_specs=[pl.BlockSpec((NDEV, MODW), lambda l: (0, l)), _full((NDEV, D))],
        out_shape=[jax.ShapeDtypeStruct((NDEV, DEPTH * MODW), F32), jax.ShapeDtypeStruct((NDEV, D), F32)],
        compiler_params=_cparams("arbitrary"),
    )(c_all, w_mod, b_sl)


def _mod_bwd_adam(ca_t, dmod, w, m, v):
    def body(ct_ref, dm_ref, w_ref, m_ref, v_ref, g_ref, d_ref, mo_ref, vo_ref):
        g = jnp.dot(ct_ref[...], dm_ref[0], preferred_element_type=F32, precision=lax.Precision.HIGHEST)
        delta, m2, v2 = _adam(w_ref[0], g, m_ref[0], v_ref[0])
        g_ref[0], d_ref[0], mo_ref[0], vo_ref[0] = g, delta, m2, v2

    blk = pl.BlockSpec((1, D, MODW), lambda l: (l, 0, 0))
    sds = jax.ShapeDtypeStruct(w.shape, F32)
    return pl.pallas_call(
        body, name="mod_bwd_adam", grid=(DEPTH,),
        in_specs=[_full((D, NDEV)), pl.BlockSpec((1, NDEV, MODW), lambda l: (l, 0, 0)), blk, blk, blk],
        out_specs=[blk] * 4, out_shape=[sds] * 4,
        compiler_params=_cparams("arbitrary"),
    )(ca_t, dmod, w, m, v)


def _inproj(x, vec, bias, gw, layer, tm):
    t = x.shape[0]

    def body(x_ref, vec_ref, b_ref, g_ref, u_ref, w_s, sems):
        @pl.when(pl.program_id(0) == 0)
        def _():
            _load_weights(g_ref, [(OFF_IN, ROWS_IN, w_s)], sems)

        h, _, _ = _norm_fwd(x_ref[...], vec_ref[0:1], vec_ref[1:2], vec_ref[2:3])
        w = w_s[...].reshape(2 * D, D)
        u_ref[...] = (_nt(h.astype(BF16), w) + b_ref[...]).astype(BF16)

    return pl.pallas_call(
        body, name=f"inproj_l{layer}", grid=(t // tm,),
        in_specs=[_tile(tm, D), _full((8, D)), _full((1, 2 * D)), ANY],
        out_specs=_tile(tm, 2 * D), out_shape=jax.ShapeDtypeStruct((t, 2 * D), BF16),
        scratch_shapes=[pltpu.VMEM((NDEV, ROWS_IN, D), BF16), pltpu.SemaphoreType.DMA((1,))],
        compiler_params=_cparams("arbitrary"),
    )(x, vec, bias, gw)


def _fill_even(qe, pe, be, part_ref, lo, rows, valid):
    cg = part_ref[:, DA:2 * DA].astype(F32)
    v = part_ref[:, 2 * DA:3 * DA].astype(F32)
    q = cg * v
    p = part_ref[:, 3 * DA:4 * DA].astype(F32)
    if valid is not None:
        q = jnp.where(valid, q, 0.0)
        p = jnp.where(valid, p, 0.0)
    qe[lo:lo + rows, :] = q
    pe[lo:lo + rows, :] = p
    if be is not None:
        b = part_ref[:, 0:DA].astype(F32)
        be[lo:lo + rows, :] = b if valid is None else jnp.where(valid, b, 0.0)


def _conv3(ca_ref, qe, tm):
    return (ca_ref[0:1] * qe[HALO - 1:HALO - 1 + tm] + ca_ref[1:2] * qe[HALO:HALO + tm]
            + ca_ref[2:3] * qe[HALO + 1:HALO + 1 + tm])


def _pool_counts(t0, rows, first_row, left, right, t):
    tg = t0 + first_row + lax.broadcasted_iota(jnp.int32, (rows, 1), 0)
    cnt = jnp.minimum(tg + right, t - 1) - jnp.maximum(tg - left, 0) + 1
    return jnp.maximum(cnt, 1).astype(F32)


def _pool_minus_id(pe, gi, left, right, inv_cnt, tm):
    c0 = gi * PG
    s = pe[HALO - left:HALO - left + tm, c0:c0 + PG]
    for j in range(-left + 1, right + 1):
        s = s + pe[HALO + j:HALO + j + tm, c0:c0 + PG]
    return s * inv_cnt - pe[HALO:HALO + tm, c0:c0 + PG]


def _mix_even_fwd(u, x, vec, ca, wp, ps, gw, layer, tm):
    t = x.shape[0]
    n = t // tm
    e = tm + 2 * HALO

    def body(u_ref, up_ref, un_ref, x_ref, vec_ref, ca_ref, wp_ref, ps_ref, g_ref, xo_ref, y_ref, w_s, qe, pe, sems):
        i = pl.program_id(0)

        @pl.when(i == 0)
        def _():
            _load_weights(g_ref, [(OFF_OUT, ROWS_OUT, w_s)], sems)

        _fill_even(qe, pe, None, up_ref, 0, HALO, i > 0)
        _fill_even(qe, pe, None, u_ref, HALO, tm, None)
        _fill_even(qe, pe, None, un_ref, HALO + tm, HALO, i < n - 1)
        ya = u_ref[:, 0:DA].astype(F32) * _conv3(ca_ref, qe, tm)
        parts = [ya]
        for gi, (_, left, right) in enumerate(POOL):
            inv = 1.0 / _pool_counts(i * tm, tm, 0, left, right, t)
            pm = _pool_minus_id(pe, gi, left, right, inv, tm)
            parts.append(_nn(pm.astype(BF16), wp_ref[gi]) * ps_ref[0:1, gi * PG:(gi + 1) * PG])
        cat = jnp.concatenate(parts, axis=-1).astype(BF16)
        y = _nn(cat, w_s[...].reshape(D, D))
        y_ref[...] = y.astype(BF16)
        xo_ref[...] = x_ref[...] + vec_ref[3:4] * y

    prev, nxt = _halo_specs(tm, 2 * D, t)
    return pl.pallas_call(
        body, name=f"mix_even_fwd_l{layer}", grid=(n,),
        in_specs=[_tile(tm, 2 * D), prev, nxt, _tile(tm, D), _full((8, D)), _full((3, DA)),
                  _full((4, PG, PG)), _full((1, DA)), ANY],
        out_specs=[_tile(tm, D), _tile(tm, D)],
        out_shape=[jax.ShapeDtypeStruct((t, D), F32), jax.ShapeDtypeStruct((t, D), BF16)],
        scratch_shapes=[pltpu.VMEM((NDEV, ROWS_OUT, D), BF16), pltpu.VMEM((e, DA), F32), pltpu.VMEM((e, DA), F32),
                        pltpu.SemaphoreType.DMA((1,))],
        compiler_params=_cparams("arbitrary"),
    )(u, u, u, x, vec, ca, wp, ps, gw)


def _mix_even_bwd(dxo, u, x, y, vec, ca, wp, ps, gw, layer, tm, comm=None):
    t = x.shape[0]
    n = t // tm
    e = tm + 2 * HALO

    def body(dxo_ref, dp_ref, dn_ref, u_ref, up_ref, un_ref, x_ref, y_ref, vec_ref, ca_ref, wp_ref, ps_ref, g_ref,
             dxi_ref, du_ref, h_ref, cat_ref, dy_ref, sums_ref, dwp_ref,
             wo_s, wi_s, dye, qe, pe, be, dce, epe, sems):
        i = pl.program_id(0)

        @pl.when(i == 0)
        def _():
            _load_weights(g_ref, [(OFF_OUT, ROWS_OUT, wo_s), (OFF_IN, ROWS_IN, wi_s)], sems)
            sums_ref[...] = jnp.zeros_like(sums_ref)
            dwp_ref[...] = jnp.zeros_like(dwp_ref)

        gate = vec_ref[3:4]
        dxo_m = dxo_ref[...]
        dye[0:HALO, :] = jnp.where(i > 0, gate * dp_ref[...], 0.0).astype(BF16)
        dye[HALO:HALO + tm, :] = (gate * dxo_m).astype(BF16)
        dye[HALO + tm:e, :] = jnp.where(i < n - 1, gate * dn_ref[...], 0.0).astype(BF16)
        _fill_even(qe, pe, be, up_ref, 0, HALO, i > 0)
        _fill_even(qe, pe, be, u_ref, HALO, tm, None)
        _fill_even(qe, pe, be, un_ref, HALO + tm, HALO, i < n - 1)
        sums_ref[0:1, :] += _rowsum(dxo_m * y_ref[...].astype(F32))

        dcat = _nt(dye[...], wo_s[...].reshape(D, D))
        dce[...] = dcat[:, 0:DA] * be[...]
        cq = _conv3(ca_ref, qe, tm)
        bg = be[HALO:HALO + tm]
        dc_m = dce[HALO:HALO + tm]
        dbg = dcat[HALO:HALO + tm, 0:DA] * cq
        dq = (ca_ref[0:1] * dce[HALO + 1:HALO + 1 + tm] + ca_ref[1:2] * dc_m
              + ca_ref[2:3] * dce[HALO - 1:HALO - 1 + tm])
        cg = u_ref[:, DA:2 * DA].astype(F32)
        v = u_ref[:, 2 * DA:3 * DA].astype(F32)
        for k in range(3):
            sums_ref[4 + k:5 + k, 0:DA] += _rowsum(dc_m * qe[HALO - 1 + k:HALO - 1 + k + tm])
        du_parts = [dbg, dq * v, dq * cg]
        cat_parts = [bg * cq]
        for gi, (_, left, right) in enumerate(POOL):
            c0 = gi * PG
            scale = ps_ref[0:1, c0:c0 + PG]
            dyb = dcat[:, DA + c0:DA + c0 + PG]
            dybs = (dyb * scale).astype(BF16)
            dpm = _nt(dybs, wp_ref[gi])
            inv_e = 1.0 / _pool_counts(i * tm, e, -HALO, left, right, t)
            epe[:, c0:c0 + PG] = dpm * inv_e
            s_adj = epe[HALO - right:HALO - right + tm, c0:c0 + PG]
            for j in range(-right + 1, left + 1):
                s_adj = s_adj + epe[HALO + j:HALO + j + tm, c0:c0 + PG]
            du_parts.append(s_adj - dpm[HALO:HALO + tm])
            inv_m = 1.0 / _pool_counts(i * tm, tm, 0, left, right, t)
            pm = _pool_minus_id(pe, gi, left, right, inv_m, tm).astype(BF16)
            ybpre = _nn(pm, wp_ref[gi])
            sums_ref[7:8, c0:c0 + PG] += _rowsum(dyb[HALO:HALO + tm] * ybpre)
            dwp_ref[gi] += _tn(pm, dybs[HALO:HALO + tm])
            cat_parts.append(ybpre * scale)
        du = jnp.concatenate(du_parts, axis=-1).astype(BF16)
        du_ref[...] = du
        cat_ref[...] = jnp.concatenate(cat_parts, axis=-1).astype(BF16)
        dy_ref[...] = dye[HALO:HALO + tm, :]
        dh = _nn(du, wi_s[...].reshape(2 * D, D))
        g, sc, sh = vec_ref[0:1], vec_ref[1:2], vec_ref[2:3]
        h, nrm, r = _norm_fwd(x_ref[...], g, sc, sh)
        h_ref[...] = h.astype(BF16)
        dxi_ref[...] = dxo_m + _norm_bwd(dh, nrm, r, g, sc)
        sums_ref[1:2, :] += _rowsum(dh)
        sums_ref[2:3, :] += _rowsum(dh * nrm)

        @pl.when(i == n - 1)
        def _():
            p = sums_ref[2:3, :]
            sums_ref[3:4, :] = p * (1.0 + sc)
            sums_ref[2:3, :] = p * g

    prev_u, nxt_u = _halo_specs(tm, 2 * D, t)
    prev_d, nxt_d = _halo_specs(tm, D, t)
    return _host_call(
        body, comm, name=f"mix_even_bwd_l{layer}", grid=(n,),
        in_specs=[_tile(tm, D), prev_d, nxt_d, _tile(tm, 2 * D), prev_u, nxt_u, _tile(tm, D), _tile(tm, D),
                  _full((8, D)), _full((3, DA)), _full((4, PG, PG)), _full((1, DA)), ANY],
        out_specs=[_tile(tm, D), _tile(tm, 2 * D), _tile(tm, D), _tile(tm, D), _tile(tm, D),
                   _full((16, D)), _full((4, PG, PG))],
        out_shape=[jax.ShapeDtypeStruct((t, D), F32), jax.ShapeDtypeStruct((t, 2 * D), BF16),
                   jax.ShapeDtypeStruct((t, D), BF16), jax.ShapeDtypeStruct((t, D), BF16),
                   jax.ShapeDtypeStruct((t, D), BF16), jax.ShapeDtypeStruct((16, D), F32),
                   jax.ShapeDtypeStruct((4, PG, PG), F32)],
        scratch_shapes=[pltpu.VMEM((NDEV, ROWS_OUT, D), BF16), pltpu.VMEM((NDEV, ROWS_IN, D), BF16),
                        pltpu.VMEM((e, D), BF16), pltpu.VMEM((e, DA), F32), pltpu.VMEM((e, DA), F32),
                        pltpu.VMEM((e, DA), F32), pltpu.VMEM((e, DA), F32), pltpu.VMEM((e, DA), F32),
                        pltpu.SemaphoreType.DMA((2,))],
        operands=(dxo, dxo, dxo, u, u, u, x, y, vec, ca, wp, ps, gw))


def _fill_glu(ze, part_ref, lo, rows, valid):
    a = part_ref[:, 0:D].astype(F32)
    g = part_ref[:, D:2 * D].astype(F32)
    z = a * _sig(g)
    ze[lo:lo + rows, :] = z if valid is None else jnp.where(valid, z, 0.0)


SHIFT_ROWS = 24


def _shifted_copies(dst, src, tm):
    rows = tm + SHIFT_ROWS
    for j in range(8):
        dst[j, :, :] = src[j:j + rows, :]


def _shifted(dst, shift, tm):
    lo = shift // 8 * 8
    return dst[shift % 8, lo:lo + tm, :]


def _layer_norm_parts(z2):
    mu = _lanemean(z2)
    d = z2 - mu
    rstd = lax.rsqrt(_lanemean(d * d) + LN_EPS)
    return d * rstd, rstd


def _mix_odd_fwd(u, x, vec, wdw, sm, gw, layer, tm):
    t = x.shape[0]
    n = t // tm
    e = tm + 2 * HALO

    def body(u_ref, up_ref, un_ref, x_ref, vec_ref, wdw_ref, sm_ref, g_ref, xo_ref, y_ref, z2_ref, w_s, ze, zsh, sems):
        i = pl.program_id(0)

        @pl.when(i == 0)
        def _():
            _load_weights(g_ref, [(OFF_OUT, ROWS_OUT, w_s)], sems)

        _fill_glu(ze, up_ref, 0, HALO, i > 0)
        _fill_glu(ze, u_ref, HALO, tm, None)
        _fill_glu(ze, un_ref, HALO + tm, HALO, i < n - 1)
        _shifted_copies(zsh, ze, tm)
        z2 = sm_ref[0:1] + wdw_ref[0:1] * _shifted(zsh, 1, tm)
        for k in range(1, CONF_K):
            z2 = z2 + wdw_ref[k:k + 1] * _shifted(zsh, 1 + k, tm)
        z2_ref[...] = z2.astype(BF16)
        zn, _ = _layer_norm_parts(z2)
        lo = zn * sm_ref[1:2] + sm_ref[2:3]
        z3 = lo * _sig(lo)
        y = _nn(z3.astype(BF16), w_s[...].reshape(D, D)) + sm_ref[3:4]
        y_ref[...] = y.astype(BF16)
        xo_ref[...] = x_ref[...] + vec_ref[3:4] * y

    prev, nxt = _halo_specs(tm, 2 * D, t)
    return pl.pallas_call(
        body, name=f"mix_odd_fwd_l{layer}", grid=(n,),
        in_specs=[_tile(tm, 2 * D), prev, nxt, _tile(tm, D), _full((8, D)), _full((32, D)), _full((8, D)), ANY],
        out_specs=[_tile(tm, D), _tile(tm, D), _tile(tm, D)],
        out_shape=[jax.ShapeDtypeStruct((t, D), F32), jax.ShapeDtypeStruct((t, D), BF16),
                   jax.ShapeDtypeStruct((t, D), BF16)],
        scratch_shapes=[pltpu.VMEM((NDEV, ROWS_OUT, D), BF16), pltpu.VMEM((e, D), F32),
                        pltpu.VMEM((8, tm + SHIFT_ROWS, D), F32), pltpu.SemaphoreType.DMA((1,))],
        compiler_params=_cparams("arbitrary"),
    )(u, u, u, x, vec, wdw, sm, gw)


def _mix_odd_bwd1(dxo, y, z2, vec, sm, gw, layer, tm):
    t = dxo.shape[0]
    n = t // tm

    def body(dxo_ref, y_ref, z2_ref, vec_ref, sm_ref, g_ref, dy_ref, z3_ref, dz2_ref, sums_ref, w_s, sems):
        i = pl.program_id(0)

        @pl.when(i == 0)
        def _():
            _load_weights(g_ref, [(OFF_OUT, ROWS_OUT, w_s)], sems)
            sums_ref[...] = jnp.zeros_like(sums_ref)

        dxo_m = dxo_ref[...]
        dy = vec_ref[3:4] * dxo_m
        dyb = dy.astype(BF16)
        dy_ref[...] = dyb
        sums_ref[0:1, :] += _rowsum(dxo_m * y_ref[...].astype(F32))
        sums_ref[4:5, :] += _rowsum(dy)
        dz3 = _nt(dyb, w_s[...].reshape(D, D))
        zn, rstd = _layer_norm_parts(z2_ref[...].astype(F32))
        lo = zn * sm_ref[1:2] + sm_ref[2:3]
        sg = _sig(lo)
        z3_ref[...] = (lo * sg).astype(BF16)
        dlo = dz3 * (sg * (1.0 + lo * (1.0 - sg)))
        sums_ref[5:6, :] += _rowsum(dlo * zn)
        sums_ref[6:7, :] += _rowsum(dlo)
        dzn = dlo * sm_ref[1:2]
        dz2 = rstd * (dzn - _lanemean(dzn) - zn * _lanemean(dzn * zn))
        sums_ref[7:8, :] += _rowsum(dz2)
        dz2_ref[...] = dz2.astype(BF16)

    return pl.pallas_call(
        body, name=f"mix_odd_bwd1_l{layer}", grid=(n,),
        in_specs=[_tile(tm, D), _tile(tm, D), _tile(tm, D), _full((8, D)), _full((8, D)), ANY],
        out_specs=[_tile(tm, D), _tile(tm, D), _tile(tm, D), _full((16, D))],
        out_shape=[jax.ShapeDtypeStruct((t, D), BF16)] * 3 + [jax.ShapeDtypeStruct((16, D), F32)],
        scratch_shapes=[pltpu.VMEM((NDEV, ROWS_OUT, D), BF16), pltpu.SemaphoreType.DMA((1,))],
        compiler_params=_cparams("arbitrary"),
    )(dxo, y, z2, vec, sm, gw)


def _mix_odd_bwd2(dz2, u, x, dxo, vec, wdw, gw, layer, tm, comm=None):
    t = x.shape[0]
    n = t // tm
    e = tm + 2 * HALO

    def body(dz_ref, dzp_ref, dzn_ref, u_ref, up_ref, un_ref, x_ref, dxo_ref, vec_ref, wdw_ref, g_ref,
             dxi_ref, du_ref, h_ref, sums_ref, dw_ref, w_s, ze, de, zsh, sems):
        i = pl.program_id(0)

        @pl.when(i == 0)
        def _():
            _load_weights(g_ref, [(OFF_IN, ROWS_IN, w_s)], sems)
            sums_ref[...] = jnp.zeros_like(sums_ref)
            dw_ref[...] = jnp.zeros_like(dw_ref)

        _fill_glu(ze, up_ref, 0, HALO, i > 0)
        _fill_glu(ze, u_ref, HALO, tm, None)
        _fill_glu(ze, un_ref, HALO + tm, HALO, i < n - 1)
        de[0:HALO, :] = jnp.where(i > 0, dzp_ref[...].astype(F32), 0.0)
        dz2_m = dz_ref[...].astype(F32)
        de[HALO:HALO + tm, :] = dz2_m
        de[HALO + tm:e, :] = jnp.where(i < n - 1, dzn_ref[...].astype(F32), 0.0)
        _shifted_copies(zsh, de, tm)
        dz = wdw_ref[0:1] * _shifted(zsh, CONF_K, tm)
        for k in range(1, CONF_K):
            dz = dz + wdw_ref[k:k + 1] * _shifted(zsh, CONF_K - k, tm)
        _shifted_copies(zsh, ze, tm)
        for k in range(CONF_K):
            dw_ref[k:k + 1, :] += _rowsum(dz2_m * _shifted(zsh, 1 + k, tm))
        a = u_ref[:, 0:D].astype(F32)
        gg = u_ref[:, D:2 * D].astype(F32)
        sg = _sig(gg)
        da = dz * sg
        dg = dz * a * (sg * (1.0 - sg))
        sums_ref[8:9, :] += _rowsum(da)
        sums_ref[9:10, :] += _rowsum(dg)
        du = jnp.concatenate([da, dg], axis=-1).astype(BF16)
        du_ref[...] = du
        dh = _nn(du, w_s[...].reshape(2 * D, D))
        g, sc, sh = vec_ref[0:1], vec_ref[1:2], vec_ref[2:3]
        h, nrm, r = _norm_fwd(x_ref[...], g, sc, sh)
        h_ref[...] = h.astype(BF16)
        dxi_ref[...] = dxo_ref[...] + _norm_bwd(dh, nrm, r, g, sc)
        sums_ref[1:2, :] += _rowsum(dh)
        sums_ref[2:3, :] += _rowsum(dh * nrm)

        @pl.when(i == n - 1)
        def _():
            p = sums_ref[2:3, :]
            sums_ref[3:4, :] = p * (1.0 + sc)
            sums_ref[2:3, :] = p * g

    prev_u, nxt_u = _halo_specs(tm, 2 * D, t)
    prev_d, nxt_d = _halo_specs(tm, D, t)
    return _host_call(
        body, comm, name=f"mix_odd_bwd2_l{layer}", grid=(n,),
        in_specs=[_tile(tm, D), prev_d, nxt_d, _tile(tm, 2 * D), prev_u, nxt_u, _tile(tm, D), _tile(tm, D),
                  _full((8, D)), _full((32, D)), ANY],
        out_specs=[_tile(tm, D), _tile(tm, 2 * D), _tile(tm, D), _full((16, D)), _full((32, D))],
        out_shape=[jax.ShapeDtypeStruct((t, D), F32), jax.ShapeDtypeStruct((t, 2 * D), BF16),
                   jax.ShapeDtypeStruct((t, D), BF16), jax.ShapeDtypeStruct((16, D), F32),
                   jax.ShapeDtypeStruct((32, D), F32)],
        scratch_shapes=[pltpu.VMEM((NDEV, ROWS_IN, D), BF16), pltpu.VMEM((e, D), F32), pltpu.VMEM((e, D), F32),
                        pltpu.VMEM((8, tm + SHIFT_ROWS, D), F32),
                        pltpu.SemaphoreType.DMA((1,))],
        operands=(dz2, dz2, dz2, u, u, u, x, dxo, vec, wdw, gw))


FCH = FF // 2


def _ffn_fwd(x, vec, gw, layer, tm, comm=None):
    t = x.shape[0]

    def body(x_ref, vec_ref, g_ref, xo_ref, a_ref, b_ref, y_ref, wg_s, wu_s, wd_s, sems):
        @pl.when(pl.program_id(0) == 0)
        def _():
            _load_weights(g_ref, [(OFF_G, FS, wg_s), (OFF_U, FS, wu_s), (OFF_D, FS, wd_s)], sems)

        xv = x_ref[...]
        h, _, _ = _norm_fwd(xv, vec_ref[0:1], vec_ref[1:2], vec_ref[2:3])
        hb = h.astype(BF16)
        y = jnp.zeros((tm, D), F32)
        for ch in range(2):
            a = _nt(hb, wg_s[4 * ch:4 * ch + 4].reshape(FCH, D))
            b = _nt(hb, wu_s[4 * ch:4 * ch + 4].reshape(FCH, D))
            a_ref[:, ch * FCH:(ch + 1) * FCH] = a.astype(BF16)
            b_ref[:, ch * FCH:(ch + 1) * FCH] = b.astype(BF16)
            s = (a * _sig(a)) * b
            y = y + _nn(s.astype(BF16), wd_s[4 * ch:4 * ch + 4].reshape(FCH, D))
        y_ref[...] = y.astype(BF16)
        xo_ref[...] = xv + vec_ref[3:4] * y

    wsc = pltpu.VMEM((NDEV, FS, D), BF16)
    return _host_call(
        body, comm, name=f"ffn_fwd_l{layer}", grid=(t // tm,),
        in_specs=[_tile(tm, D), _full((8, D)), ANY],
        out_specs=[_tile(tm, D), _tile(tm, FF), _tile(tm, FF), _tile(tm, D)],
        out_shape=[jax.ShapeDtypeStruct((t, D), F32), jax.ShapeDtypeStruct((t, FF), BF16),
                   jax.ShapeDtypeStruct((t, FF), BF16), jax.ShapeDtypeStruct((t, D), BF16)],
        scratch_shapes=[wsc, wsc, wsc, pltpu.SemaphoreType.DMA((3,))],
        operands=(x, vec, gw))


def _ffn_bwd_hidden(dxo, y, a, b, vec, gw, layer, tm, comm=None):
    t = dxo.shape[0]

    def body(dxo_ref, y_ref, a_ref, b_ref, vec_ref, g_ref, dy_ref, da_ref, db_ref, sums_ref, wd_s, sems):
        @pl.when(pl.program_id(0) == 0)
        def _():
            _load_weights(g_ref, [(OFF_D, FS, wd_s)], sems)
            sums_ref[...] = jnp.zeros_like(sums_ref)

        dxo_m = dxo_ref[...]
        sums_ref[0:1, :] += _rowsum(dxo_m * y_ref[...].astype(F32))
        dyb = (vec_ref[3:4] * dxo_m).astype(BF16)
        dy_ref[...] = dyb
        for ch in range(2):
            cols = slice(ch * FCH, (ch + 1) * FCH)
            ds = _nt(dyb, wd_s[4 * ch:4 * ch + 4].reshape(FCH, D))
            av = a_ref[:, cols].astype(F32)
            bv = b_ref[:, cols].astype(F32)
            sg = _sig(av)
            db_ref[:, cols] = (ds * (av * sg)).astype(BF16)
            da_ref[:, cols] = (ds * bv * (sg * (1.0 + av * (1.0 - sg)))).astype(BF16)

    return _host_call(
        body, comm, name=f"ffn_bwd_hidden_l{layer}", grid=(t // tm,),
        in_specs=[_tile(tm, D), _tile(tm, D), _tile(tm, FF), _tile(tm, FF), _full((8, D)), ANY],
        out_specs=[_tile(tm, D), _tile(tm, FF), _tile(tm, FF), _full((8, D))],
        out_shape=[jax.ShapeDtypeStruct((t, D), BF16), jax.ShapeDtypeStruct((t, FF), BF16),
                   jax.ShapeDtypeStruct((t, FF), BF16), jax.ShapeDtypeStruct((8, D), F32)],
        scratch_shapes=[pltpu.VMEM((NDEV, FS, D), BF16), pltpu.SemaphoreType.DMA((1,))],
        operands=(dxo, y, a, b, vec, gw))


def _ffn_bwd_input(da, db, x, dxo, vec, gw, layer, tm, comm=None):
    t = x.shape[0]
    n = t // tm

    def body(da_ref, db_ref, x_ref, dxo_ref, vec_ref, g_ref, dxi_ref, h_ref, sums_ref, wg_s, wu_s, sems):
        i = pl.program_id(0)

        @pl.when(i == 0)
        def _():
            _load_weights(g_ref, [(OFF_G, FS, wg_s), (OFF_U, FS, wu_s)], sems)
            sums_ref[...] = jnp.zeros_like(sums_ref)

        dh = _nn(da_ref[...], wg_s[...].reshape(FF, D)) + _nn(db_ref[...], wu_s[...].reshape(FF, D))
        g, sc, sh = vec_ref[0:1], vec_ref[1:2], vec_ref[2:3]
        h, nrm, r = _norm_fwd(x_ref[...], g, sc, sh)
        h_ref[...] = h.astype(BF16)
        dxi_ref[...] = dxo_ref[...] + _norm_bwd(dh, nrm, r, g, sc)
        sums_ref[1:2, :] += _rowsum(dh)
        sums_ref[2:3, :] += _rowsum(dh * nrm)

        @pl.when(i == n - 1)
        def _():
            p = sums_ref[2:3, :]
            sums_ref[3:4, :] = p * (1.0 + sc)
            sums_ref[2:3, :] = p * g

    wsc = pltpu.VMEM((NDEV, FS, D), BF16)
    return _host_call(
        body, comm, name=f"ffn_bwd_input_l{layer}", grid=(n,),
        in_specs=[_tile(tm, FF), _tile(tm, FF), _tile(tm, D), _tile(tm, D), _full((8, D)), ANY],
        out_specs=[_tile(tm, D), _tile(tm, D), _full((8, D))],
        out_shape=[jax.ShapeDtypeStruct((t, D), F32), jax.ShapeDtypeStruct((t, D), BF16),
                   jax.ShapeDtypeStruct((8, D), F32)],
        scratch_shapes=[wsc, wsc, pltpu.SemaphoreType.DMA((2,))],
        operands=(da, db, x, dxo, vec, gw))


def _wgrad(lhs, rhs, name, tk, lhs2=None):
    t, m = lhs.shape
    n = t // tk

    def body(*refs):
        if lhs2 is None:
            l_ref, r_ref, o_ref, acc = refs
            lv = l_ref[...]
        else:
            l_ref, l2_ref, r_ref, o_ref, acc = refs
            av = l_ref[...].astype(F32)
            lv = ((av * _sig(av)) * l2_ref[...].astype(F32)).astype(BF16)
        i = pl.program_id(0)

        @pl.when(i == 0)
        def _():
            acc[...] = jnp.zeros_like(acc)

        acc[...] += _tn(lv, r_ref[...])

        @pl.when(i == n - 1)
        def _():
            o_ref[...] = acc[...].astype(BF16)

    ins = [lhs] + ([] if lhs2 is None else [lhs2]) + [rhs]
    specs = [_tile(tk, m)] * (len(ins) - 1) + [_tile(tk, D)]
    return pl.pallas_call(
        body, name=name, grid=(n,),
        in_specs=specs, out_specs=_full((m, D)), out_shape=jax.ShapeDtypeStruct((m, D), BF16),
        scratch_shapes=[pltpu.VMEM((m, D), F32)],
        compiler_params=_cparams("arbitrary"),
    )(*ins)


def _final(x, tgt, gf, tm):
    t = x.shape[0]

    def body(x_ref, t_ref, g_ref, dx_ref, sums_ref):
        @pl.when(pl.program_id(0) == 0)
        def _():
            sums_ref[...] = jnp.zeros_like(sums_ref)

        xv = x_ref[...]
        r = lax.rsqrt(_lanemean(xv * xv) + RMS_EPS)
        nrm = xv * r
        g = g_ref[...]
        err = nrm * g - t_ref[...]
        sums_ref[1:2, :] += _rowsum(err * err) * (0.5 / D)
        dout = err * (1.0 / D)
        sums_ref[0:1, :] += _rowsum(dout * nrm)
        dn = dout * g
        dx_ref[...] = r * (dn - nrm * _lanemean(dn * nrm))

    return pl.pallas_call(
        body, name="loss_head", grid=(t // tm,),
        in_specs=[_tile(tm, D), _tile(tm, D), _full((1, D))],
        out_specs=[_tile(tm, D), _full((8, D))],
        out_shape=[jax.ShapeDtypeStruct((t, D), F32), jax.ShapeDtypeStruct((8, D), F32)],
        compiler_params=_cparams("arbitrary"),
    )(x, tgt, gf)


ADAM_ROWS = LROWS // 5


def _adam_big(recv, w, m, v):
    def body(r_ref, w_ref, m_ref, v_ref, g_ref, d_ref, mo_ref, vo_ref):
        g = r_ref[0, 0].astype(F32)
        for s in range(1, NDEV):
            g = g + r_ref[s, 0].astype(F32)
        delta, m2, v2 = _adam(w_ref[0], g, m_ref[0], v_ref[0])
        g_ref[0], d_ref[0], mo_ref[0], vo_ref[0] = g, delta, m2, v2

    blk = pl.BlockSpec((1, ADAM_ROWS, D), lambda l, j: (l, j, 0))
    sds = jax.ShapeDtypeStruct(w.shape, F32)
    return pl.pallas_call(
        body, name="adam_big", grid=(DEPTH, LROWS // ADAM_ROWS),
        in_specs=[pl.BlockSpec((NDEV, 1, ADAM_ROWS, D), lambda l, j: (0, l, j, 0)), blk, blk, blk],
        out_specs=[blk] * 4, out_shape=[sds] * 4,
        compiler_params=_cparams("arbitrary", "arbitrary"),
    )(recv, w, m, v)


def _sum_small(gathered, rows):
    def body(g_ref, o_ref):
        acc = g_ref[0:rows, :]
        for s in range(1, NDEV):
            acc = acc + g_ref[s * rows:(s + 1) * rows, :]
        o_ref[...] = acc

    return pl.pallas_call(
        body, name="sum_small",
        in_specs=[pl.BlockSpec(memory_space=pltpu.VMEM)], out_specs=pl.BlockSpec(memory_space=pltpu.VMEM),
        out_shape=jax.ShapeDtypeStruct((rows, D), F32),
        compiler_params=pltpu.CompilerParams(vmem_limit_bytes=VMEM_LIMIT),
    )(gathered)


def _adam_small(params):
    k = len(params)

    def body(*refs):
        ins, outs = refs[:4 * k], refs[4 * k:]
        for j in range(k):
            w_ref, g_ref, m_ref, v_ref = ins[4 * j:4 * j + 4]
            delta, m2, v2 = _adam(w_ref[...], g_ref[...], m_ref[...], v_ref[...])
            outs[3 * j][...], outs[3 * j + 1][...], outs[3 * j + 2][...] = delta, m2, v2

    flat = [a for p in params for a in p]
    shapes = [jax.ShapeDtypeStruct(p[0].shape, F32) for p in params for _ in range(3)]
    vm = pl.BlockSpec(memory_space=pltpu.VMEM)
    res = pl.pallas_call(
        body, name="adam_small", in_specs=[vm] * len(flat), out_specs=[vm] * len(shapes), out_shape=shapes,
        compiler_params=pltpu.CompilerParams(vmem_limit_bytes=VMEM_LIMIT),
    )(*flat)
    return [tuple(res[3 * j:3 * j + 3]) for j in range(k)]


def _pack(ab_in, ab_out, pw1, pw2, wg, wu, wd):
    ins = jnp.swapaxes(jnp.stack([ab_in[0], pw1[0], ab_in[1], pw1[1]]), 1, 2)
    outs = jnp.stack([ab_out[0], pw2[0], ab_out[1], pw2[1]])
    return jnp.concatenate([ins, outs, jnp.swapaxes(wg, 1, 2), jnp.swapaxes(wu, 1, 2), wd], axis=1)


def _unpack(p):
    ins = jnp.swapaxes(p[:, OFF_IN:OFF_OUT], 1, 2)
    outs = p[:, OFF_OUT:OFF_G]
    return (ins[0::2], outs[0::2], ins[1::2], outs[1::2], jnp.swapaxes(p[:, OFF_G:OFF_U], 1, 2),
            jnp.swapaxes(p[:, OFF_U:OFF_D], 1, 2), p[:, OFF_D:LROWS])


def _unshard(flat, lead, per):
    k = len(lead)
    a = flat.reshape((NDEV,) + tuple(lead) + (per,))
    a = jnp.transpose(a, tuple(range(1, k + 1)) + (0, k + 1))
    return a.reshape(tuple(lead) + (NDEV * per,))


def _rows_of(a):
    f = a.reshape(-1)
    pad = (-f.shape[0]) % D
    if pad:
        f = jnp.concatenate([f, jnp.zeros((pad,), f.dtype)])
    return f.reshape(-1, D)


def _pad_rows(a, rows):
    return jnp.concatenate([a, jnp.zeros((rows - a.shape[0],) + a.shape[1:], a.dtype)], axis=0)


def kernel(x, c, norm_mix_g, norm_ffn_g, w_mod, b_mod, ab_w_in, ab_conv, ab_w_pool, ab_pool_scale, ab_w_out, cf_w_pw1, cf_b_pw1, cf_w_dw, cf_b_dw, cf_ln_g, cf_ln_b, cf_w_pw2, cf_b_pw2, ffn_w_gate, ffn_w_up, ffn_w_down, final_norm_g, loss_target, m_norm_mix_g, m_norm_ffn_g, m_w_mod, m_b_mod, m_ab_w_in, m_ab_conv, m_ab_w_pool, m_ab_pool_scale, m_ab_w_out, m_cf_w_pw1, m_cf_b_pw1, m_cf_w_dw, m_cf_b_dw, m_cf_ln_g, m_cf_ln_b, m_cf_w_pw2, m_cf_b_pw2, m_ffn_w_gate, m_ffn_w_up, m_ffn_w_down, m_final_norm_g, v_norm_mix_g, v_norm_ffn_g, v_w_mod, v_b_mod, v_ab_w_in, v_ab_conv, v_ab_w_pool, v_ab_pool_scale, v_ab_w_out, v_cf_w_pw1, v_cf_b_pw1, v_cf_w_dw, v_cf_b_dw, v_cf_ln_g, v_cf_ln_b, v_cf_w_pw2, v_cf_b_pw2, v_ffn_w_gate, v_ffn_w_up, v_ffn_w_down, v_final_norm_g):
    t = x.shape[1]
    tm = 512 if t % 512 == 0 else t // 2
    tk = 512 if t % 512 == 0 else t // 2
    tmo = tm // 2
    me = 4 * lax.axis_index("x") + 2 * lax.axis_index("y") + lax.axis_index("c")
    xs, tgt = x[0], loss_target[0]

    sharded = [ab_conv, cf_b_pw1, cf_w_dw, cf_b_dw, cf_ln_g, cf_ln_b, cf_b_pw2]
    flat = jnp.concatenate([a.reshape(-1) for a in sharded])
    n_flat = flat.shape[0]
    g1 = _gather_small(jnp.concatenate([_pad_rows(c, 8), _pad_rows(_rows_of(flat), 16)], axis=0), "gather_cond")
    g1 = g1.reshape(NDEV, 24, D)
    c_all = g1[:, 0, :]
    flat_all = g1[:, 8:, :].reshape(NDEV, -1)[:, :n_flat]
    full, o = [], 0
    for a in sharded:
        lead, per = a.shape[:-1], a.shape[-1]
        size = a.size
        full.append(_unshard(flat_all[:, o:o + size], lead, per))
        o += size
    ab_conv_f, b_pw1_f, w_dw_f, b_dw_f, ln_g_f, ln_b_f, b_pw2_f = full

    b_sl = lax.dynamic_slice_in_dim(b_mod, me * MODW, MODW, axis=1).reshape(DEPTH, 1, MODW)
    mod_part, c_act = _mod_fwd(c_all, w_mod, b_sl)
    g2 = _gather_small(mod_part, "gather_mod").reshape(NDEV, NDEV, DEPTH, MODW)
    mod = jnp.transpose(lax.dynamic_index_in_dim(g2, me, axis=1, keepdims=False), (1, 0, 2)).reshape(DEPTH, N_MOD, D)
    zeros4 = jnp.zeros((4, D), F32)

    def vec_of(g, layer, k):
        return jnp.concatenate([g[layer][None], mod[layer, k + 1][None], mod[layer, k][None],
                                mod[layer, k + 2][None], zeros4], axis=0)

    vmix = [vec_of(norm_mix_g, l, 0) for l in range(DEPTH)]
    vffn = [vec_of(norm_ffn_g, l, 3) for l in range(DEPTH)]

    w_pack = _pack(ab_w_in, ab_w_out, cf_w_pw1, cf_w_pw2, ffn_w_gate, ffn_w_up, ffn_w_down)
    p16 = w_pack.astype(BF16)
    gw = [_gather_weights(p16[0])]

    wp16 = ab_w_pool.astype(BF16)
    wdw32 = [_pad_rows(w_dw_f[i], 32) for i in range(2)]
    sm_odd = [jnp.concatenate([b_dw_f[i][None], ln_g_f[i][None], ln_b_f[i][None], b_pw2_f[i][None], zeros4], axis=0)
              for i in range(2)]
    zero_bias = jnp.zeros((1, 2 * D), F32)

    saved = []
    xc = xs
    for l in range(DEPTH):
        i = l // 2
        if l % 2 == 0:
            u = _inproj(xc, vmix[l], zero_bias, gw[l], l, tm)
            x_mid, y_mix = _mix_even_fwd(u, xc, vmix[l], ab_conv_f[i], wp16[i], ab_pool_scale[i][None], gw[l], l, tm)
            z2 = None
        else:
            u = _inproj(xc, vmix[l], b_pw1_f[i][None], gw[l], l, tm)
            x_mid, y_mix, z2 = _mix_odd_fwd(u, xc, vmix[l], wdw32[i], sm_odd[i], gw[l], l, tmo)
        if l + 1 < DEPTH:
            x_out, a, b, y_ffn, g_next = _ffn_fwd(x_mid, vffn[l], gw[l], l, tm, _gather_comm(p16[l + 1]))
            gw.append(g_next)
        else:
            x_out, a, b, y_ffn = _ffn_fwd(x_mid, vffn[l], gw[l], l, tm)
        saved.append((xc, u, y_mix, z2, x_mid, a, b, y_ffn))
        xc = x_out

    dx, fsum = _final(xc, tgt, final_norm_g[None], tm)
    loss = lax.psum(jnp.sum(fsum[1]), ("x", "y", "c"))
    d_final_g = fsum[0]

    recv = _empty_recv()
    io_specs = [(OFF_IN, ROWS_IN), (OFF_OUT, ROWS_OUT)]
    pending = None
    dmod = [None] * DEPTH
    d_mix_g, d_ffn_g = [None] * DEPTH, [None] * DEPTH
    d_conv, d_pool, d_pscale = [None] * 2, [None] * 2, [None] * 2
    d_bpw1, d_wdw, d_bdw, d_lng, d_lnb, d_bpw2 = ([None] * 2 for _ in range(6))
    for l in reversed(range(DEPTH)):
        i = l // 2
        x_in, u, y_mix, z2, x_mid, a, b, y_ffn = saved[l]
        if pending is None:
            dy, da, db, s_h = _ffn_bwd_hidden(dx, y_ffn, a, b, vffn[l], gw[l], l, tm)
        else:
            dy, da, db, s_h, recv = _ffn_bwd_hidden(dx, y_ffn, a, b, vffn[l], gw[l], l, tm,
                                                    _scatter_comm(pending, io_specs, recv, l + 1))
        g_down = _wgrad(a, dy, f"wgrad_down_l{l}", tk, lhs2=b)
        dx_mid, h2, s_i, recv = _ffn_bwd_input(da, db, x_mid, dx, vffn[l], gw[l], l, tm,
                                               _scatter_comm([g_down], [(OFF_D, FS)], recv, l))
        g_gate = _wgrad(da, h2, f"wgrad_gate_l{l}", tk)
        g_up = _wgrad(db, h2, f"wgrad_up_l{l}", tk)
        gu_comm = _scatter_comm([g_gate, g_up], [(OFF_G, FS), (OFF_U, FS)], recv, l)
        d_ffn_g[l] = s_i[3]
        mod_ffn = [s_i[1], s_i[2], s_h[0]]
        if l % 2 == 0:
            dx, du, h, cat, dym, s_m, dwp, recv = _mix_even_bwd(dx_mid, u, x_in, y_mix, vmix[l], ab_conv_f[i], wp16[i],
                                                                ab_pool_scale[i][None], gw[l], l, tm, gu_comm)
            g_out = _wgrad(cat, dym, f"wgrad_out_l{l}", tk)
            d_conv[i], d_pool[i], d_pscale[i] = s_m[4:7, :DA], dwp, s_m[7, :DA]
            mod_mix = [s_m[1], s_m[2], s_m[0]]
            d_mix_g[l] = s_m[3]
        else:
            dym, z3, dz2, s_1 = _mix_odd_bwd1(dx_mid, y_mix, z2, vmix[l], sm_odd[i], gw[l], l, tm)
            dx, du, h, s_2, dwdw, recv = _mix_odd_bwd2(dz2, u, x_in, dx_mid, vmix[l], wdw32[i], gw[l], l, tmo, gu_comm)
            g_out = _wgrad(z3, dym, f"wgrad_out_l{l}", tk)
            d_bpw1[i], d_wdw[i], d_bdw[i] = s_2[8:10].reshape(2 * D), dwdw[:CONF_K], s_1[7]
            d_lng[i], d_lnb[i], d_bpw2[i] = s_1[5], s_1[6], s_1[4]
            mod_mix = [s_2[1], s_2[2], s_1[0]]
            d_mix_g[l] = s_2[3]
        pending = [_wgrad(du, h, f"wgrad_in_l{l}", tk), g_out]
        dmod[l] = jnp.stack(mod_mix + mod_ffn)
    grad_x = dx[None]
    recv = _scatter_grads(pending, io_specs, recv, 0)

    m_pack = _pack(m_ab_w_in, m_ab_w_out, m_cf_w_pw1, m_cf_w_pw2, m_ffn_w_gate, m_ffn_w_up, m_ffn_w_down)
    v_pack = _pack(v_ab_w_in, v_ab_w_out, v_cf_w_pw1, v_cf_w_pw2, v_ffn_w_gate, v_ffn_w_up, v_ffn_w_down)
    big = [_unpack(p) for p in _adam_big(recv, w_pack, m_pack, v_pack)]

    small = [jnp.stack(dmod), jnp.stack(d_mix_g), jnp.stack(d_ffn_g), d_final_g, jnp.stack(d_pool),
             jnp.stack(d_pscale), jnp.stack(d_conv), jnp.stack(d_bpw1), jnp.stack(d_wdw), jnp.stack(d_bdw),
             jnp.stack(d_lng), jnp.stack(d_lnb), jnp.stack(d_bpw2)]
    small_rows = [_rows_of(a) for a in small]
    n_rows = sum(a.shape[0] for a in small_rows)
    pad_rows = -(-n_rows // 8) * 8
    g3 = _gather_small(_pad_rows(jnp.concatenate(small_rows, axis=0), pad_rows), "gather_small_grads")
    summed = _sum_small(g3, pad_rows)
    outs, o = [], 0
    for a, r in zip(small, small_rows):
        outs.append(summed[o:o + r.shape[0]].reshape(-1)[:a.size].reshape(a.shape))
        o += r.shape[0]
    (g_bmod, g_mix_g, g_ffn_g, g_final, g_pool, g_pscale, g_conv, g_bpw1, g_wdw, g_bdw, g_lng, g_lnb, g_bpw2) = outs
    g_bmod = g_bmod.reshape(DEPTH, N_MOD * D)

    def my_shard(a):
        per = a.shape[-1] // NDEV
        return lax.dynamic_slice_in_dim(a, me * per, per, axis=a.ndim - 1)

    g_conv, g_bpw1, g_wdw, g_bdw, g_lng, g_lnb, g_bpw2 = [
        my_shard(a) for a in (g_conv, g_bpw1, g_wdw, g_bdw, g_lng, g_lnb, g_bpw2)]

    dmod_all = g3.reshape(NDEV, pad_rows, D)[:, :DEPTH * N_MOD, :].reshape(NDEV, DEPTH, N_MOD * D)
    dmod_mine = jnp.transpose(lax.dynamic_slice_in_dim(dmod_all, me * MODW, MODW, axis=2), (1, 0, 2))
    g_wmod, d_wmod, nm_wmod, nv_wmod = _mod_bwd_adam(c_act.T, dmod_mine, w_mod, m_w_mod, v_w_mod)

    small_params = [
        (norm_mix_g, g_mix_g, m_norm_mix_g, v_norm_mix_g), (norm_ffn_g, g_ffn_g, m_norm_ffn_g, v_norm_ffn_g),
        (b_mod, g_bmod, m_b_mod, v_b_mod), (ab_conv, g_conv, m_ab_conv, v_ab_conv),
        (ab_w_pool, g_pool, m_ab_w_pool, v_ab_w_pool), (ab_pool_scale, g_pscale, m_ab_pool_scale, v_ab_pool_scale),
        (cf_b_pw1, g_bpw1, m_cf_b_pw1, v_cf_b_pw1), (cf_w_dw, g_wdw, m_cf_w_dw, v_cf_w_dw),
        (cf_b_dw, g_bdw, m_cf_b_dw, v_cf_b_dw), (cf_ln_g, g_lng, m_cf_ln_g, v_cf_ln_g),
        (cf_ln_b, g_lnb, m_cf_ln_b, v_cf_ln_b), (cf_b_pw2, g_bpw2, m_cf_b_pw2, v_cf_b_pw2),
        (final_norm_g, g_final, m_final_norm_g, v_final_norm_g)]

    def two_d(a):
        return a.reshape(-1, a.shape[-1])

    upd = _adam_small([tuple(two_d(a) for a in p) for p in small_params])
    upd = [tuple(r.reshape(p[0].shape) for r in u) for u, p in zip(upd, small_params)]
    (s_mix, s_ffn, s_bmod, s_conv, s_pool, s_pscale, s_bpw1, s_wdw, s_bdw, s_lng, s_lnb, s_bpw2, s_final) = upd
    small_g = [p[1] for p in small_params]
    (q_mix, q_ffn, q_bmod, q_conv, q_pool, q_pscale, q_bpw1, q_wdw, q_bdw, q_lng, q_lnb, q_bpw2, q_final) = small_g

    def ordered(k):
        ab_in, ab_out, pw1, pw2, wg, wu, wd = big[k]
        if k == 0:
            sm = dict(mix=q_mix, ffn=q_ffn, bmod=q_bmod, conv=q_conv, pool=q_pool, pscale=q_pscale, bpw1=q_bpw1,
                      wdw=q_wdw, bdw=q_bdw, lng=q_lng, lnb=q_lnb, bpw2=q_bpw2, final=q_final)
            wmod = g_wmod
        else:
            j = k - 1
            sm = dict(mix=s_mix[j], ffn=s_ffn[j], bmod=s_bmod[j], conv=s_conv[j], pool=s_pool[j], pscale=s_pscale[j],
                      bpw1=s_bpw1[j], wdw=s_wdw[j], bdw=s_bdw[j], lng=s_lng[j], lnb=s_lnb[j], bpw2=s_bpw2[j],
                      final=s_final[j])
            wmod = (d_wmod, nm_wmod, nv_wmod)[j]
        return [sm["mix"], sm["ffn"], wmod, sm["bmod"], ab_in, sm["conv"], sm["pool"], sm["pscale"], ab_out,
                pw1, sm["bpw1"], sm["wdw"], sm["bdw"], sm["lng"], sm["lnb"], pw2, sm["bpw2"], wg, wu, wd, sm["final"]]

    return (loss, grad_x, *ordered(0), *ordered(1), *ordered(2), *ordered(3))
```

```python
import functools

import jax
import jax.numpy as jnp
from jax import lax
from jax.experimental import pallas as pl
from jax.experimental.pallas import tpu as pltpu

F32 = jnp.float32
BF16 = jnp.bfloat16
MESH = pl.DeviceIdType.MESH

NDEV = 8
DEPTH = 4
D = 1024
FF = 2816
FS = FF // NDEV
DA = 512
PG = 128
POOL = ((2, 1, 0), (4, 2, 1), (8, 4, 3), (16, 8, 7))
CONF_K = 31
CONF_L = 15
N_MOD = 6
MODW = N_MOD * D // NDEV
RMS_EPS = 1e-6
LN_EPS = 1e-5

ROWS_IN, ROWS_OUT = 2 * D // NDEV, D // NDEV
OFF_IN, OFF_OUT = 0, ROWS_IN
OFF_G = OFF_OUT + ROWS_OUT
OFF_U = OFF_G + FS
OFF_D = OFF_U + FS
LROWS = OFF_D + FS

HALO = 16
VMEM_LIMIT = 60 * 1024 * 1024

ADAM_LR, ADAM_B1, ADAM_B2, ADAM_EPS, ADAM_WD, ADAM_STEP = 1e-3, 0.9, 0.999, 1e-8, 0.01, 10
ADAM_C1 = 1.0 / (1.0 - ADAM_B1 ** ADAM_STEP)
ADAM_C2 = 1.0 / (1.0 - ADAM_B2 ** ADAM_STEP)


def _nn(a, b):
    return jnp.dot(a, b, preferred_element_type=F32)


def _nt(a, b):
    return lax.dot_general(a, b, (((1,), (1,)), ((), ())), preferred_element_type=F32)


def _tn(a, b):
    return lax.dot_general(a, b, (((0,), (0,)), ((), ())), preferred_element_type=F32)


def _sig(x):
    return 1.0 / (1.0 + jnp.exp(-x))


def _rowsum(x):
    return jnp.sum(x, axis=0, keepdims=True)


def _lanemean(x):
    return jnp.mean(x, axis=-1, keepdims=True)


def _norm_fwd(x, g, sc, sh):
    r = lax.rsqrt(_lanemean(x * x) + RMS_EPS)
    n = x * r
    return n * (g * (1.0 + sc)) + sh, n, r


def _norm_bwd(dh, n, r, g, sc):
    dn = dh * (g * (1.0 + sc))
    return r * (dn - n * _lanemean(dn * n))


def _adam(w, g, m, v):
    m2 = ADAM_B1 * m + (1.0 - ADAM_B1) * g
    v2 = ADAM_B2 * v + (1.0 - ADAM_B2) * (g * g)
    delta = -ADAM_LR * ((m2 * ADAM_C1) / (jnp.sqrt(v2 * ADAM_C2) + ADAM_EPS) + ADAM_WD * w)
    return delta, m2, v2


def _load_weights(g_ref, specs, sems):
    cps = [pltpu.make_async_copy(g_ref.at[:, pl.ds(off, rows), :], dst, sems.at[k])
           for k, (off, rows, dst) in enumerate(specs)]
    for cp in cps:
        cp.start()
    for cp in cps:
        cp.wait()


def _cparams(*sem):
    return pltpu.CompilerParams(dimension_semantics=sem if sem else None, vmem_limit_bytes=VMEM_LIMIT)


def _tile(tm, w):
    return pl.BlockSpec((tm, w), lambda i: (i, 0))


def _full(shape):
    nd = len(shape)
    return pl.BlockSpec(shape, lambda i: (0,) * nd)


def _halo_specs(tm, w, total_rows):
    tb = tm // HALO
    nb = total_rows // HALO
    prev = pl.BlockSpec((HALO, w), lambda i: (jnp.maximum(i * tb - 1, 0), 0))
    nxt = pl.BlockSpec((HALO, w), lambda i: (jnp.minimum((i + 1) * tb, nb - 1), 0))
    return prev, nxt


ANY = pl.BlockSpec(memory_space=pl.ANY)


def _peers():
    x, y, c = lax.axis_index("x"), lax.axis_index("y"), lax.axis_index("c")
    return x, y, c


def _gather_small(v, name):
    m_per, n = v.shape

    def body(x_ref, out_ref, send_sems, recv_sems, local_sem):
        x, y, c = _peers()
        me, sibling = (x, y, c), (x, y, 1 - c)
        chips = [(1 - x, y), (x, 1 - y), (1 - x, 1 - y)]

        def rows(px, py, pc):
            return out_ref.at[pl.ds((4 * px + 2 * py + pc) * m_per, m_per), :]

        def copy(k, block, to, src=None):
            return pltpu.make_async_remote_copy(
                src_ref=rows(*block) if src is None else src, dst_ref=rows(*block),
                send_sem=send_sems.at[k], recv_sem=recv_sems.at[k], device_id=to, device_id_type=MESH)

        mine = pltpu.make_async_copy(x_ref, rows(*me), local_sem)
        mine.start()
        first = [copy(0, me, sibling, src=x_ref)]
        first += [copy(1 + j, me, (*chip, c), src=x_ref) for j, chip in enumerate(chips)]
        for cp in first:
            cp.start()
        passed = [copy(4 + j, (*chip, c), sibling) for j, chip in enumerate(chips)]
        for j, chip in enumerate(chips):
            copy(1 + j, (*chip, c), me).wait_recv()
            passed[j].start()
        copy(0, sibling, me).wait_recv()
        for j, chip in enumerate(chips):
            copy(4 + j, (*chip, 1 - c), me).wait_recv()
        for cp in first + passed:
            cp.wait_send()
        mine.wait()

    return pl.pallas_call(
        body, name=name,
        out_shape=jax.ShapeDtypeStruct((NDEV * m_per, n), v.dtype),
        in_specs=[pl.BlockSpec(memory_space=pltpu.VMEM)],
        out_specs=pl.BlockSpec(memory_space=pltpu.VMEM),
        scratch_shapes=[pltpu.SemaphoreType.DMA((7,)), pltpu.SemaphoreType.DMA((7,)), pltpu.SemaphoreType.DMA],
        compiler_params=pltpu.CompilerParams(vmem_limit_bytes=VMEM_LIMIT),
    )(v)


class _Comm:
    def __init__(self, ins, outs, aliases, bind):
        self.ins, self.outs, self.aliases, self.bind = ins, outs, aliases, bind


COMM_SEMS = [pltpu.SemaphoreType.DMA((7,)), pltpu.SemaphoreType.DMA((7,)), pltpu.SemaphoreType.DMA]


def _gather_hooks(p_ref, out_ref, send_sems, recv_sems, local_sem):
    def parts():
        x, y, c = _peers()
        me, sibling = (x, y, c), (x, y, 1 - c)
        chips = [(1 - x, y), (x, 1 - y), (1 - x, 1 - y)]

        def slab(px, py, pc):
            return out_ref.at[4 * px + 2 * py + pc]

        def copy(k, block, to, src=None):
            return pltpu.make_async_remote_copy(
                src_ref=slab(*block) if src is None else src, dst_ref=slab(*block),
                send_sem=send_sems.at[k], recv_sem=recv_sems.at[k], device_id=to, device_id_type=MESH)

        def mine():
            return pltpu.make_async_copy(p_ref, slab(*me), local_sem)

        def first():
            return [copy(0, me, sibling, src=p_ref)] + [copy(1 + j, me, (*chip, c), src=p_ref)
                                                        for j, chip in enumerate(chips)]

        def passed():
            return [copy(4 + j, (*chip, c), sibling) for j, chip in enumerate(chips)]

        def from_chips():
            return [copy(1 + j, (*chip, c), me) for j, chip in enumerate(chips)]

        def from_sibling():
            return [copy(0, sibling, me)] + [copy(4 + j, (*chip, 1 - c), me) for j, chip in enumerate(chips)]

        return mine, first, passed, from_chips, from_sibling

    def start():
        mine, first, _, _, _ = parts()
        mine().start()
        for cp in first():
            cp.start()

    def middle():
        _, _, passed, from_chips, _ = parts()
        for arrived, onward in zip(from_chips(), passed()):
            arrived.wait_recv()
            onward.start()

    def end():
        mine, first, passed, _, from_sibling = parts()
        for cp in from_sibling():
            cp.wait_recv()
        for cp in first() + passed():
            cp.wait_send()
        mine().wait()

    return start, middle, end


def _gather_comm(p):
    return _Comm([p], [jax.ShapeDtypeStruct((NDEV,) + p.shape, p.dtype)], {},
                 lambda cins, couts, sems: _gather_hooks(cins[0], couts[0], *sems))


def _gather_weights(p):
    def body(p_ref, out_ref, send_sems, recv_sems, local_sem):
        for hook in _gather_hooks(p_ref, out_ref, send_sems, recv_sems, local_sem):
            hook()

    return pl.pallas_call(
        body, name="gather_weights",
        out_shape=jax.ShapeDtypeStruct((NDEV,) + p.shape, p.dtype),
        in_specs=[ANY], out_specs=ANY, scratch_shapes=COMM_SEMS,
    )(p)


def _scatter_hooks(src_refs, specs, r_ref, layer, send_sems, recv_sems, local_sem):
    off0 = specs[0][0]
    total = sum(rows for _, rows in specs)

    def start():
        x, y, c = _peers()
        me = 4 * x + 2 * y + c

        def part(k, dev):
            off, rows = specs[k]
            src = src_refs[k].at[pl.ds(pl.multiple_of(dev * rows, 16), rows), :]
            return src, r_ref.at[me, layer, pl.ds(off, rows), :]

        for k in range(len(specs)):
            src, dst = part(k, me)
            pltpu.make_async_copy(src, dst, local_sem).start()
        for r in range(1, NDEV):
            px = 1 - x if r & 4 else x
            py = 1 - y if r & 2 else y
            pc = 1 - c if r & 1 else c
            for k in range(len(specs)):
                src, dst = part(k, 4 * px + 2 * py + pc)
                pltpu.make_async_remote_copy(
                    src_ref=src, dst_ref=dst, send_sem=send_sems.at[r - 1], recv_sem=recv_sems.at[r - 1],
                    device_id=(px, py, pc), device_id_type=MESH).start()

    def end():
        x, y, c = _peers()
        whole = r_ref.at[0, layer, pl.ds(off0, total), :]
        for r in range(1, NDEV):
            done = pltpu.make_async_remote_copy(
                src_ref=whole, dst_ref=whole, send_sem=send_sems.at[r - 1], recv_sem=recv_sems.at[r - 1],
                device_id=(x, y, c), device_id_type=MESH)
            done.wait_recv()
            done.wait_send()
        pltpu.make_async_copy(whole, whole, local_sem).wait()

    return start, None, end


def _scatter_comm(srcs, specs, recv, layer):
    k = len(srcs)
    return _Comm(list(srcs) + [recv], [jax.ShapeDtypeStruct(recv.shape, recv.dtype)], {k: 0},
                 lambda cins, couts, sems: _scatter_hooks(cins[:k], specs, couts[0], layer, *sems))


def _scatter_grads(srcs, specs, recv, layer):
    k = len(srcs)

    def body(*refs):
        start, _, end = _scatter_hooks(refs[:k], specs, refs[k + 1], layer, *refs[k + 2:])
        start()
        end()

    return pl.pallas_call(
        body, name=f"scatter_grads_l{layer}",
        out_shape=jax.ShapeDtypeStruct(recv.shape, recv.dtype),
        in_specs=[ANY] * (k + 1), out_specs=ANY, scratch_shapes=COMM_SEMS,
        input_output_aliases={k: 0},
    )(*srcs, recv)


def _empty_recv():
    def body(o_ref):
        del o_ref

    return pl.pallas_call(body, name="recv_buffer", out_specs=ANY,
                          out_shape=jax.ShapeDtypeStruct((NDEV, DEPTH, LROWS, D), BF16))()


def _host_call(inner, comm, *, name, grid, in_specs, out_specs, out_shape, scratch_shapes, operands):
    if comm is None:
        return pl.pallas_call(
            inner, name=name, grid=grid, in_specs=in_specs, out_specs=out_specs, out_shape=out_shape,
            scratch_shapes=scratch_shapes, compiler_params=_cparams("arbitrary"))(*operands)
    n_in, n_out, n_s = len(in_specs), len(out_specs), len(scratch_shapes)
    k_in, k_out = len(comm.ins), len(comm.outs)
    steps = grid[0]

    def body(*refs):
        ins, cins = refs[:n_in], refs[n_in:n_in + k_in]
        o0 = n_in + k_in
        outs, couts = refs[o0:o0 + n_out], refs[o0 + n_out:o0 + n_out + k_out]
        s0 = o0 + n_out + k_out
        scr, sems = refs[s0:s0 + n_s], refs[s0 + n_s:]
        start, middle, end = comm.bind(cins, couts, sems)
        i = pl.program_id(0)
        pl.when(i == 0)(start)
        if middle is not None:
            pl.when(i == steps * 3 // 4)(middle)
        inner(*ins, *outs, *scr)
        pl.when(i == steps - 1)(end)

    return pl.pallas_call(
        body, name=name, grid=grid, in_specs=list(in_specs) + [ANY] * k_in,
        out_specs=list(out_specs) + [ANY] * k_out, out_shape=list(out_shape) + list(comm.outs),
        scratch_shapes=list(scratch_shapes) + COMM_SEMS,
        input_output_aliases={n_in + a: n_out + b for a, b in comm.aliases.items()},
        compiler_params=_cparams("arbitrary"))(*operands, *comm.ins)


def _mod_fwd(c_all, w_mod, b_sl):
    def body(c_ref, w_ref, b_ref, o_ref, ca_ref):
        cv = c_ref[...]
        ca = cv * _sig(cv)
        ca_ref[...] = ca
        o_ref[...] = jnp.dot(ca, w_ref[0], preferred_element_type=F32, precision=lax.Precision.HIGHEST) + b_ref[0]

    return pl.pallas_call(
        body, name="mod_fwd", grid=(DEPTH,),
        in_specs=[_full((NDEV, D)), pl.BlockSpec((1, D, MODW), lambda l: (l, 0, 0)),
                  pl.BlockSpec((1, 1, MODW), lambda l: (l, 0, 0))],
        out_specs=[pl.BlockSpec((NDEV, MODW), lambda l: (0, l)), _full((NDEV, D))],
        out_shape=[jax.ShapeDtypeStruct((NDEV, DEPTH * MODW), F32), jax.ShapeDtypeStruct((NDEV, D), F32)],
        compiler_params=_cparams("arbitrary"),
    )(c_all, w_mod, b_sl)


def _mod_bwd_adam(ca_t, dmod, w, m, v, comm=None):
    def body(ct_ref, dm_ref, w_ref, m_ref, v_ref, g_ref, d_ref, mo_ref, vo_ref):
        g = jnp.dot(ct_ref[...], dm_ref[0], preferred_element_type=F32, precision=lax.Precision.HIGHEST)
        delta, m2, v2 = _adam(w_ref[0], g, m_ref[0], v_ref[0])
        g_ref[0], d_ref[0], mo_ref[0], vo_ref[0] = g, delta, m2, v2

    blk = pl.BlockSpec((1, D, MODW), lambda l: (l, 0, 0))
    sds = jax.ShapeDtypeStruct(w.shape, F32)
    return _host_call(
        body, comm, name="mod_bwd_adam", grid=(DEPTH,),
        in_specs=[_full((D, NDEV)), pl.BlockSpec((1, NDEV, MODW), lambda l: (l, 0, 0)), blk, blk, blk],
        out_specs=[blk] * 4, out_shape=[sds] * 4, scratch_shapes=[],
        operands=(ca_t, dmod, w, m, v))


def _inproj(x, vec, bias, gw, layer, tm):
    t = x.shape[0]

    def body(x_ref, vec_ref, b_ref, g_ref, u_ref, w_s, sems):
        @pl.when(pl.program_id(0) == 0)
        def _():
            _load_weights(g_ref, [(OFF_IN, ROWS_IN, w_s)], sems)

        h, _, _ = _norm_fwd(x_ref[...], vec_ref[0:1], vec_ref[1:2], vec_ref[2:3])
        w = w_s[...].reshape(2 * D, D)
        u_ref[...] = (_nt(h.astype(BF16), w) + b_ref[...]).astype(BF16)

    return pl.pallas_call(
        body, name=f"inproj_l{layer}", grid=(t // tm,),
        in_specs=[_tile(tm, D), _full((8, D)), _full((1, 2 * D)), ANY],
        out_specs=_tile(tm, 2 * D), out_shape=jax.ShapeDtypeStruct((t, 2 * D), BF16),
        scratch_shapes=[pltpu.VMEM((NDEV, ROWS_IN, D), BF16), pltpu.SemaphoreType.DMA((1,))],
        compiler_params=_cparams("arbitrary"),
    )(x, vec, bias, gw)


def _fill_even(qe, pe, be, part_ref, lo, rows, valid):
    cg = part_ref[:, DA:2 * DA].astype(F32)
    v = part_ref[:, 2 * DA:3 * DA].astype(F32)
    q = cg * v
    p = part_ref[:, 3 * DA:4 * DA].astype(F32)
    if valid is not None:
        q = jnp.where(valid, q, 0.0)
        p = jnp.where(valid, p, 0.0)
    qe[lo:lo + rows, :] = q
    pe[lo:lo + rows, :] = p
    if be is not None:
        b = part_ref[:, 0:DA].astype(F32)
        be[lo:lo + rows, :] = b if valid is None else jnp.where(valid, b, 0.0)


def _conv3(ca_ref, qe, tm):
    return (ca_ref[0:1] * qe[HALO - 1:HALO - 1 + tm] + ca_ref[1:2] * qe[HALO:HALO + tm]
            + ca_ref[2:3] * qe[HALO + 1:HALO + 1 + tm])


def _pool_counts(t0, rows, first_row, left, right, t):
    tg = t0 + first_row + lax.broadcasted_iota(jnp.int32, (rows, 1), 0)
    cnt = jnp.minimum(tg + right, t - 1) - jnp.maximum(tg - left, 0) + 1
    return jnp.maximum(cnt, 1).astype(F32)


def _pool_minus_id(pe, gi, left, right, inv_cnt, tm):
    c0 = gi * PG
    s = pe[HALO - left:HALO - left + tm, c0:c0 + PG]
    for j in range(-left + 1, right + 1):
        s = s + pe[HALO + j:HALO + j + tm, c0:c0 + PG]
    return s * inv_cnt - pe[HALO:HALO + tm, c0:c0 + PG]


def _mix_even_fwd(u, x, vec, ca, wp, ps, gw, layer, tm):
    t = x.shape[0]
    n = t // tm
    e = tm + 2 * HALO

    def body(u_ref, up_ref, un_ref, x_ref, vec_ref, ca_ref, wp_ref, ps_ref, g_ref, xo_ref, y_ref, w_s, qe, pe, sems):
        i = pl.program_id(0)

        @pl.when(i == 0)
        def _():
            _load_weights(g_ref, [(OFF_OUT, ROWS_OUT, w_s)], sems)

        _fill_even(qe, pe, None, up_ref, 0, HALO, i > 0)
        _fill_even(qe, pe, None, u_ref, HALO, tm, None)
        _fill_even(qe, pe, None, un_ref, HALO + tm, HALO, i < n - 1)
        ya = u_ref[:, 0:DA].astype(F32) * _conv3(ca_ref, qe, tm)
        parts = [ya]
        for gi, (_, left, right) in enumerate(POOL):
            inv = 1.0 / _pool_counts(i * tm, tm, 0, left, right, t)
            pm = _pool_minus_id(pe, gi, left, right, inv, tm)
            parts.append(_nn(pm.astype(BF16), wp_ref[gi]) * ps_ref[0:1, gi * PG:(gi + 1) * PG])
        cat = jnp.concatenate(parts, axis=-1).astype(BF16)
        y = _nn(cat, w_s[...].reshape(D, D))
        y_ref[...] = y.astype(BF16)
        xo_ref[...] = x_ref[...] + vec_ref[3:4] * y

    prev, nxt = _halo_specs(tm, 2 * D, t)
    return pl.pallas_call(
        body, name=f"mix_even_fwd_l{layer}", grid=(n,),
        in_specs=[_tile(tm, 2 * D), prev, nxt, _tile(tm, D), _full((8, D)), _full((3, DA)),
                  _full((4, PG, PG)), _full((1, DA)), ANY],
        out_specs=[_tile(tm, D), _tile(tm, D)],
        out_shape=[jax.ShapeDtypeStruct((t, D), F32), jax.ShapeDtypeStruct((t, D), BF16)],
        scratch_shapes=[pltpu.VMEM((NDEV, ROWS_OUT, D), BF16), pltpu.VMEM((e, DA), F32), pltpu.VMEM((e, DA), F32),
                        pltpu.SemaphoreType.DMA((1,))],
        compiler_params=_cparams("arbitrary"),
    )(u, u, u, x, vec, ca, wp, ps, gw)


def _mix_even_bwd(dxo, u, x, y, vec, ca, wp, ps, gw, layer, tm, comm=None):
    t = x.shape[0]
    n = t // tm
    e = tm + 2 * HALO

    def body(dxo_ref, dp_ref, dn_ref, u_ref, up_ref, un_ref, x_ref, y_ref, vec_ref, ca_ref, wp_ref, ps_ref, g_ref,
             dxi_ref, du_ref, h_ref, cat_ref, dy_ref, sums_ref, dwp_ref,
             wo_s, wi_s, dye, qe, pe, be, dce, epe, sems):
        i = pl.program_id(0)

        @pl.when(i == 0)
        def _():
            _load_weights(g_ref, [(OFF_OUT, ROWS_OUT, wo_s), (OFF_IN, ROWS_IN, wi_s)], sems)
            sums_ref[...] = jnp.zeros_like(sums_ref)
            dwp_ref[...] = jnp.zeros_like(dwp_ref)

        gate = vec_ref[3:4]
        dxo_m = dxo_ref[...]
        dye[0:HALO, :] = jnp.where(i > 0, gate * dp_ref[...], 0.0).astype(BF16)
        dye[HALO:HALO + tm, :] = (gate * dxo_m).astype(BF16)
        dye[HALO + tm:e, :] = jnp.where(i < n - 1, gate * dn_ref[...], 0.0).astype(BF16)
        _fill_even(qe, pe, be, up_ref, 0, HALO, i > 0)
        _fill_even(qe, pe, be, u_ref, HALO, tm, None)
        _fill_even(qe, pe, be, un_ref, HALO + tm, HALO, i < n - 1)
        sums_ref[0:1, :] += _rowsum(dxo_m * y_ref[...].astype(F32))

        dcat = _nt(dye[...], wo_s[...].reshape(D, D))
        dce[...] = dcat[:, 0:DA] * be[...]
        cq = _conv3(ca_ref, qe, tm)
        bg = be[HALO:HALO + tm]
        dc_m = dce[HALO:HALO + tm]
        dbg = dcat[HALO:HALO + tm, 0:DA] * cq
        dq = (ca_ref[0:1] * dce[HALO + 1:HALO + 1 + tm] + ca_ref[1:2] * dc_m
              + ca_ref[2:3] * dce[HALO - 1:HALO - 1 + tm])
        cg = u_ref[:, DA:2 * DA].astype(F32)
        v = u_ref[:, 2 * DA:3 * DA].astype(F32)
        for k in range(3):
            sums_ref[4 + k:5 + k, 0:DA] += _rowsum(dc_m * qe[HALO - 1 + k:HALO - 1 + k + tm])
        du_parts = [dbg, dq * v, dq * cg]
        cat_parts = [bg * cq]
        for gi, (_, left, right) in enumerate(POOL):
            c0 = gi * PG
            scale = ps_ref[0:1, c0:c0 + PG]
            dyb = dcat[:, DA + c0:DA + c0 + PG]
            dybs = (dyb * scale).astype(BF16)
            dpm = _nt(dybs, wp_ref[gi])
            inv_e = 1.0 / _pool_counts(i * tm, e, -HALO, left, right, t)
            epe[:, c0:c0 + PG] = dpm * inv_e
            s_adj = epe[HALO - right:HALO - right + tm, c0:c0 + PG]
            for j in range(-right + 1, left + 1):
                s_adj = s_adj + epe[HALO + j:HALO + j + tm, c0:c0 + PG]
            du_parts.append(s_adj - dpm[HALO:HALO + tm])
            inv_m = 1.0 / _pool_counts(i * tm, tm, 0, left, right, t)
            pm = _pool_minus_id(pe, gi, left, right, inv_m, tm).astype(BF16)
            ybpre = _nn(pm, wp_ref[gi])
            sums_ref[7:8, c0:c0 + PG] += _rowsum(dyb[HALO:HALO + tm] * ybpre)
            dwp_ref[gi] += _tn(pm, dybs[HALO:HALO + tm])
            cat_parts.append(ybpre * scale)
        du = jnp.concatenate(du_parts, axis=-1).astype(BF16)
        du_ref[...] = du
        cat_ref[...] = jnp.concatenate(cat_parts, axis=-1).astype(BF16)
        dy_ref[...] = dye[HALO:HALO + tm, :]
        dh = _nn(du, wi_s[...].reshape(2 * D, D))
        g, sc, sh = vec_ref[0:1], vec_ref[1:2], vec_ref[2:3]
        h, nrm, r = _norm_fwd(x_ref[...], g, sc, sh)
        h_ref[...] = h.astype(BF16)
        dxi_ref[...] = dxo_m + _norm_bwd(dh, nrm, r, g, sc)
        sums_ref[1:2, :] += _rowsum(dh)
        sums_ref[2:3, :] += _rowsum(dh * nrm)

        @pl.when(i == n - 1)
        def _():
            p = sums_ref[2:3, :]
            sums_ref[3:4, :] = p * (1.0 + sc)
            sums_ref[2:3, :] = p * g

    prev_u, nxt_u = _halo_specs(tm, 2 * D, t)
    prev_d, nxt_d = _halo_specs(tm, D, t)
    return _host_call(
        body, comm, name=f"mix_even_bwd_l{layer}", grid=(n,),
        in_specs=[_tile(tm, D), prev_d, nxt_d, _tile(tm, 2 * D), prev_u, nxt_u, _tile(tm, D), _tile(tm, D),
                  _full((8, D)), _full((3, DA)), _full((4, PG, PG)), _full((1, DA)), ANY],
        out_specs=[_tile(tm, D), _tile(tm, 2 * D), _tile(tm, D), _tile(tm, D), _tile(tm, D),
                   _full((16, D)), _full((4, PG, PG))],
        out_shape=[jax.ShapeDtypeStruct((t, D), F32), jax.ShapeDtypeStruct((t, 2 * D), BF16),
                   jax.ShapeDtypeStruct((t, D), BF16), jax.ShapeDtypeStruct((t, D), BF16),
                   jax.ShapeDtypeStruct((t, D), BF16), jax.ShapeDtypeStruct((16, D), F32),
                   jax.ShapeDtypeStruct((4, PG, PG), F32)],
        scratch_shapes=[pltpu.VMEM((NDEV, ROWS_OUT, D), BF16), pltpu.VMEM((NDEV, ROWS_IN, D), BF16),
                        pltpu.VMEM((e, D), BF16), pltpu.VMEM((e, DA), F32), pltpu.VMEM((e, DA), F32),
                        pltpu.VMEM((e, DA), F32), pltpu.VMEM((e, DA), F32), pltpu.VMEM((e, DA), F32),
                        pltpu.SemaphoreType.DMA((2,))],
        operands=(dxo, dxo, dxo, u, u, u, x, y, vec, ca, wp, ps, gw))


def _fill_glu(ze, part_ref, lo, rows, valid):
    a = part_ref[:, 0:D].astype(F32)
    g = part_ref[:, D:2 * D].astype(F32)
    z = a * _sig(g)
    ze[lo:lo + rows, :] = z if valid is None else jnp.where(valid, z, 0.0)


SHIFT_ROWS = 24


def _shifted_copies(dst, src, tm):
    rows = tm + SHIFT_ROWS
    for j in range(8):
        dst[j, :, :] = src[j:j + rows, :]


def _shifted(dst, shift, tm):
    lo = shift // 8 * 8
    return dst[shift % 8, lo:lo + tm, :]


def _layer_norm_parts(z2):
    mu = _lanemean(z2)
    d = z2 - mu
    rstd = lax.rsqrt(_lanemean(d * d) + LN_EPS)
    return d * rstd, rstd


def _mix_odd_fwd(u, x, vec, wdw, sm, gw, layer, tm):
    t = x.shape[0]
    n = t // tm
    e = tm + 2 * HALO

    def body(u_ref, up_ref, un_ref, x_ref, vec_ref, wdw_ref, sm_ref, g_ref, xo_ref, y_ref, z2_ref, w_s, ze, zsh, sems):
        i = pl.program_id(0)

        @pl.when(i == 0)
        def _():
            _load_weights(g_ref, [(OFF_OUT, ROWS_OUT, w_s)], sems)

        _fill_glu(ze, up_ref, 0, HALO, i > 0)
        _fill_glu(ze, u_ref, HALO, tm, None)
        _fill_glu(ze, un_ref, HALO + tm, HALO, i < n - 1)
        _shifted_copies(zsh, ze, tm)
        z2 = sm_ref[0:1] + wdw_ref[0:1] * _shifted(zsh, 1, tm)
        for k in range(1, CONF_K):
            z2 = z2 + wdw_ref[k:k + 1] * _shifted(zsh, 1 + k, tm)
        z2_ref[...] = z2.astype(BF16)
        zn, _ = _layer_norm_parts(z2)
        lo = zn * sm_ref[1:2] + sm_ref[2:3]
        z3 = lo * _sig(lo)
        y = _nn(z3.astype(BF16), w_s[...].reshape(D, D)) + sm_ref[3:4]
        y_ref[...] = y.astype(BF16)
        xo_ref[...] = x_ref[...] + vec_ref[3:4] * y

    prev, nxt = _halo_specs(tm, 2 * D, t)
    return pl.pallas_call(
        body, name=f"mix_odd_fwd_l{layer}", grid=(n,),
        in_specs=[_tile(tm, 2 * D), prev, nxt, _tile(tm, D), _full((8, D)), _full((32, D)), _full((8, D)), ANY],
        out_specs=[_tile(tm, D), _tile(tm, D), _tile(tm, D)],
        out_shape=[jax.ShapeDtypeStruct((t, D), F32), jax.ShapeDtypeStruct((t, D), BF16),
                   jax.ShapeDtypeStruct((t, D), BF16)],
        scratch_shapes=[pltpu.VMEM((NDEV, ROWS_OUT, D), BF16), pltpu.VMEM((e, D), F32),
                        pltpu.VMEM((8, tm + SHIFT_ROWS, D), F32), pltpu.SemaphoreType.DMA((1,))],
        compiler_params=_cparams("arbitrary"),
    )(u, u, u, x, vec, wdw, sm, gw)


def _mix_odd_bwd1(dxo, y, z2, vec, sm, gw, layer, tm):
    t = dxo.shape[0]
    n = t // tm

    def body(dxo_ref, y_ref, z2_ref, vec_ref, sm_ref, g_ref, dy_ref, z3_ref, dz2_ref, sums_ref, w_s, sems):
        i = pl.program_id(0)

        @pl.when(i == 0)
        def _():
            _load_weights(g_ref, [(OFF_OUT, ROWS_OUT, w_s)], sems)
            sums_ref[...] = jnp.zeros_like(sums_ref)

        dxo_m = dxo_ref[...]
        dy = vec_ref[3:4] * dxo_m
        dyb = dy.astype(BF16)
        dy_ref[...] = dyb
        sums_ref[0:1, :] += _rowsum(dxo_m * y_ref[...].astype(F32))
        sums_ref[4:5, :] += _rowsum(dy)
        dz3 = _nt(dyb, w_s[...].reshape(D, D))
        zn, rstd = _layer_norm_parts(z2_ref[...].astype(F32))
        lo = zn * sm_ref[1:2] + sm_ref[2:3]
        sg = _sig(lo)
        z3_ref[...] = (lo * sg).astype(BF16)
        dlo = dz3 * (sg * (1.0 + lo * (1.0 - sg)))
        sums_ref[5:6, :] += _rowsum(dlo * zn)
        sums_ref[6:7, :] += _rowsum(dlo)
        dzn = dlo * sm_ref[1:2]
        dz2 = rstd * (dzn - _lanemean(dzn) - zn * _lanemean(dzn * zn))
        sums_ref[7:8, :] += _rowsum(dz2)
        dz2_ref[...] = dz2.astype(BF16)

    return pl.pallas_call(
        body, name=f"mix_odd_bwd1_l{layer}", grid=(n,),
        in_specs=[_tile(tm, D), _tile(tm, D), _tile(tm, D), _full((8, D)), _full((8, D)), ANY],
        out_specs=[_tile(tm, D), _tile(tm, D), _tile(tm, D), _full((16, D))],
        out_shape=[jax.ShapeDtypeStruct((t, D), BF16)] * 3 + [jax.ShapeDtypeStruct((16, D), F32)],
        scratch_shapes=[pltpu.VMEM((NDEV, ROWS_OUT, D), BF16), pltpu.SemaphoreType.DMA((1,))],
        compiler_params=_cparams("arbitrary"),
    )(dxo, y, z2, vec, sm, gw)


def _mix_odd_bwd2(dz2, u, x, dxo, vec, wdw, gw, layer, tm, comm=None):
    t = x.shape[0]
    n = t // tm
    e = tm + 2 * HALO

    def body(dz_ref, dzp_ref, dzn_ref, u_ref, up_ref, un_ref, x_ref, dxo_ref, vec_ref, wdw_ref, g_ref,
             dxi_ref, du_ref, h_ref, sums_ref, dw_ref, w_s, ze, de, zsh, sems):
        i = pl.program_id(0)

        @pl.when(i == 0)
        def _():
            _load_weights(g_ref, [(OFF_IN, ROWS_IN, w_s)], sems)
            sums_ref[...] = jnp.zeros_like(sums_ref)
            dw_ref[...] = jnp.zeros_like(dw_ref)

        _fill_glu(ze, up_ref, 0, HALO, i > 0)
        _fill_glu(ze, u_ref, HALO, tm, None)
        _fill_glu(ze, un_ref, HALO + tm, HALO, i < n - 1)
        de[0:HALO, :] = jnp.where(i > 0, dzp_ref[...].astype(F32), 0.0)
        dz2_m = dz_ref[...].astype(F32)
        de[HALO:HALO + tm, :] = dz2_m
        de[HALO + tm:e, :] = jnp.where(i < n - 1, dzn_ref[...].astype(F32), 0.0)
        _shifted_copies(zsh, de, tm)
        dz = wdw_ref[0:1] * _shifted(zsh, CONF_K, tm)
        for k in range(1, CONF_K):
            dz = dz + wdw_ref[k:k + 1] * _shifted(zsh, CONF_K - k, tm)
        _shifted_copies(zsh, ze, tm)
        for k in range(CONF_K):
            dw_ref[k:k + 1, :] += _rowsum(dz2_m * _shifted(zsh, 1 + k, tm))
        a = u_ref[:, 0:D].astype(F32)
        gg = u_ref[:, D:2 * D].astype(F32)
        sg = _sig(gg)
        da = dz * sg
        dg = dz * a * (sg * (1.0 - sg))
        sums_ref[8:9, :] += _rowsum(da)
        sums_ref[9:10, :] += _rowsum(dg)
        du = jnp.concatenate([da, dg], axis=-1).astype(BF16)
        du_ref[...] = du
        dh = _nn(du, w_s[...].reshape(2 * D, D))
        g, sc, sh = vec_ref[0:1], vec_ref[1:2], vec_ref[2:3]
        h, nrm, r = _norm_fwd(x_ref[...], g, sc, sh)
        h_ref[...] = h.astype(BF16)
        dxi_ref[...] = dxo_ref[...] + _norm_bwd(dh, nrm, r, g, sc)
        sums_ref[1:2, :] += _rowsum(dh)
        sums_ref[2:3, :] += _rowsum(dh * nrm)

        @pl.when(i == n - 1)
        def _():
            p = sums_ref[2:3, :]
            sums_ref[3:4, :] = p * (1.0 + sc)
            sums_ref[2:3, :] = p * g

    prev_u, nxt_u = _halo_specs(tm, 2 * D, t)
    prev_d, nxt_d = _halo_specs(tm, D, t)
    return _host_call(
        body, comm, name=f"mix_odd_bwd2_l{layer}", grid=(n,),
        in_specs=[_tile(tm, D), prev_d, nxt_d, _tile(tm, 2 * D), prev_u, nxt_u, _tile(tm, D), _tile(tm, D),
                  _full((8, D)), _full((32, D)), ANY],
        out_specs=[_tile(tm, D), _tile(tm, 2 * D), _tile(tm, D), _full((16, D)), _full((32, D))],
        out_shape=[jax.ShapeDtypeStruct((t, D), F32), jax.ShapeDtypeStruct((t, 2 * D), BF16),
                   jax.ShapeDtypeStruct((t, D), BF16), jax.ShapeDtypeStruct((16, D), F32),
                   jax.ShapeDtypeStruct((32, D), F32)],
        scratch_shapes=[pltpu.VMEM((NDEV, ROWS_IN, D), BF16), pltpu.VMEM((e, D), F32), pltpu.VMEM((e, D), F32),
                        pltpu.VMEM((8, tm + SHIFT_ROWS, D), F32),
                        pltpu.SemaphoreType.DMA((1,))],
        operands=(dz2, dz2, dz2, u, u, u, x, dxo, vec, wdw, gw))


FCH = FF // 2


def _ffn_fwd(x, vec, gw, layer, tm, comm=None):
    t = x.shape[0]

    def body(x_ref, vec_ref, g_ref, xo_ref, a_ref, b_ref, y_ref, wg_s, wu_s, wd_s, sems):
        @pl.when(pl.program_id(0) == 0)
        def _():
            _load_weights(g_ref, [(OFF_G, FS, wg_s), (OFF_U, FS, wu_s), (OFF_D, FS, wd_s)], sems)

        xv = x_ref[...]
        h, _, _ = _norm_fwd(xv, vec_ref[0:1], vec_ref[1:2], vec_ref[2:3])
        hb = h.astype(BF16)
        y = jnp.zeros((tm, D), F32)
        for ch in range(2):
            a = _nt(hb, wg_s[4 * ch:4 * ch + 4].reshape(FCH, D))
            b = _nt(hb, wu_s[4 * ch:4 * ch + 4].reshape(FCH, D))
            a_ref[:, ch * FCH:(ch + 1) * FCH] = a.astype(BF16)
            b_ref[:, ch * FCH:(ch + 1) * FCH] = b.astype(BF16)
            s = (a * _sig(a)) * b
            y = y + _nn(s.astype(BF16), wd_s[4 * ch:4 * ch + 4].reshape(FCH, D))
        y_ref[...] = y.astype(BF16)
        xo_ref[...] = xv + vec_ref[3:4] * y

    wsc = pltpu.VMEM((NDEV, FS, D), BF16)
    return _host_call(
        body, comm, name=f"ffn_fwd_l{layer}", grid=(t // tm,),
        in_specs=[_tile(tm, D), _full((8, D)), ANY],
        out_specs=[_tile(tm, D), _tile(tm, FF), _tile(tm, FF), _tile(tm, D)],
        out_shape=[jax.ShapeDtypeStruct((t, D), F32), jax.ShapeDtypeStruct((t, FF), BF16),
                   jax.ShapeDtypeStruct((t, FF), BF16), jax.ShapeDtypeStruct((t, D), BF16)],
        scratch_shapes=[wsc, wsc, wsc, pltpu.SemaphoreType.DMA((3,))],
        operands=(x, vec, gw))


def _ffn_bwd_hidden(dxo, y, a, b, vec, gw, layer, tm, comm=None):
    t = dxo.shape[0]

    def body(dxo_ref, y_ref, a_ref, b_ref, vec_ref, g_ref, dy_ref, da_ref, db_ref, sums_ref, wd_s, sems):
        @pl.when(pl.program_id(0) == 0)
        def _():
            _load_weights(g_ref, [(OFF_D, FS, wd_s)], sems)
            sums_ref[...] = jnp.zeros_like(sums_ref)

        dxo_m = dxo_ref[...]
        sums_ref[0:1, :] += _rowsum(dxo_m * y_ref[...].astype(F32))
        dyb = (vec_ref[3:4] * dxo_m).astype(BF16)
        dy_ref[...] = dyb
        for ch in range(2):
            cols = slice(ch * FCH, (ch + 1) * FCH)
            ds = _nt(dyb, wd_s[4 * ch:4 * ch + 4].reshape(FCH, D)).astype(BF16)
            av = a_ref[:, cols]
            bv = b_ref[:, cols]
            sg = _sig(av)
            db_ref[:, cols] = ds * (av * sg)
            da_ref[:, cols] = (ds * bv) * (sg * (1.0 + av * (1.0 - sg)))

    return _host_call(
        body, comm, name=f"ffn_bwd_hidden_l{layer}", grid=(t // tm,),
        in_specs=[_tile(tm, D), _tile(tm, D), _tile(tm, FF), _tile(tm, FF), _full((8, D)), ANY],
        out_specs=[_tile(tm, D), _tile(tm, FF), _tile(tm, FF), _full((8, D))],
        out_shape=[jax.ShapeDtypeStruct((t, D), BF16), jax.ShapeDtypeStruct((t, FF), BF16),
                   jax.ShapeDtypeStruct((t, FF), BF16), jax.ShapeDtypeStruct((8, D), F32)],
        scratch_shapes=[pltpu.VMEM((NDEV, FS, D), BF16), pltpu.SemaphoreType.DMA((1,))],
        operands=(dxo, y, a, b, vec, gw))


def _ffn_bwd_input(da, db, x, dxo, vec, gw, layer, tm, comm=None):
    t = x.shape[0]
    n = t // tm

    def body(da_ref, db_ref, x_ref, dxo_ref, vec_ref, g_ref, dxi_ref, h_ref, sums_ref, wg_s, wu_s, sems):
        i = pl.program_id(0)

        @pl.when(i == 0)
        def _():
            _load_weights(g_ref, [(OFF_G, FS, wg_s), (OFF_U, FS, wu_s)], sems)
            sums_ref[...] = jnp.zeros_like(sums_ref)

        dh = _nn(da_ref[...], wg_s[...].reshape(FF, D)) + _nn(db_ref[...], wu_s[...].reshape(FF, D))
        g, sc, sh = vec_ref[0:1], vec_ref[1:2], vec_ref[2:3]
        h, nrm, r = _norm_fwd(x_ref[...], g, sc, sh)
        h_ref[...] = h.astype(BF16)
        dxi_ref[...] = dxo_ref[...] + _norm_bwd(dh, nrm, r, g, sc)
        sums_ref[1:2, :] += _rowsum(dh)
        sums_ref[2:3, :] += _rowsum(dh * nrm)

        @pl.when(i == n - 1)
        def _():
            p = sums_ref[2:3, :]
            sums_ref[3:4, :] = p * (1.0 + sc)
            sums_ref[2:3, :] = p * g

    wsc = pltpu.VMEM((NDEV, FS, D), BF16)
    return _host_call(
        body, comm, name=f"ffn_bwd_input_l{layer}", grid=(n,),
        in_specs=[_tile(tm, FF), _tile(tm, FF), _tile(tm, D), _tile(tm, D), _full((8, D)), ANY],
        out_specs=[_tile(tm, D), _tile(tm, D), _full((8, D))],
        out_shape=[jax.ShapeDtypeStruct((t, D), F32), jax.ShapeDtypeStruct((t, D), BF16),
                   jax.ShapeDtypeStruct((8, D), F32)],
        scratch_shapes=[wsc, wsc, pltpu.SemaphoreType.DMA((2,))],
        operands=(da, db, x, dxo, vec, gw))


def _wgrad(lhs, rhs, name, tk, lhs2=None, comm=None):
    t, m = lhs.shape
    n = t // tk

    def body(*refs):
        if lhs2 is None:
            l_ref, r_ref, o_ref, acc = refs
            lv = l_ref[...]
        else:
            l_ref, l2_ref, r_ref, o_ref, acc = refs
            av = l_ref[...]
            lv = (av * _sig(av)) * l2_ref[...]
        i = pl.program_id(0)

        @pl.when(i == 0)
        def _():
            acc[...] = jnp.zeros_like(acc)

        acc[...] += _tn(lv, r_ref[...])

        @pl.when(i == n - 1)
        def _():
            o_ref[...] = acc[...].astype(BF16)

    ins = [lhs] + ([] if lhs2 is None else [lhs2]) + [rhs]
    specs = [_tile(tk, m)] * (len(ins) - 1) + [_tile(tk, D)]
    res = _host_call(
        body, comm, name=name, grid=(n,),
        in_specs=specs, out_specs=[_full((m, D))], out_shape=[jax.ShapeDtypeStruct((m, D), BF16)],
        scratch_shapes=[pltpu.VMEM((m, D), F32)], operands=ins)
    return res[0] if comm is None else res


def _final(x, tgt, gf, tm):
    t = x.shape[0]

    def body(x_ref, t_ref, g_ref, dx_ref, sums_ref):
        @pl.when(pl.program_id(0) == 0)
        def _():
            sums_ref[...] = jnp.zeros_like(sums_ref)

        xv = x_ref[...]
        r = lax.rsqrt(_lanemean(xv * xv) + RMS_EPS)
        nrm = xv * r
        g = g_ref[...]
        err = nrm * g - t_ref[...]
        sums_ref[1:2, :] += _rowsum(err * err) * (0.5 / D)
        dout = err * (1.0 / D)
        sums_ref[0:1, :] += _rowsum(dout * nrm)
        dn = dout * g
        dx_ref[...] = r * (dn - nrm * _lanemean(dn * nrm))

    return pl.pallas_call(
        body, name="loss_head", grid=(t // tm,),
        in_specs=[_tile(tm, D), _tile(tm, D), _full((1, D))],
        out_specs=[_tile(tm, D), _full((8, D))],
        out_shape=[jax.ShapeDtypeStruct((t, D), F32), jax.ShapeDtypeStruct((8, D), F32)],
        compiler_params=_cparams("arbitrary"),
    )(x, tgt, gf)


ADAM_ROWS = LROWS // 5


def _adam_big(recv, w, m, v):
    def body(r_ref, w_ref, m_ref, v_ref, g_ref, d_ref, mo_ref, vo_ref):
        g = r_ref[0, 0].astype(F32)
        for s in range(1, NDEV):
            g = g + r_ref[s, 0].astype(F32)
        delta, m2, v2 = _adam(w_ref[0], g, m_ref[0], v_ref[0])
        g_ref[0], d_ref[0], mo_ref[0], vo_ref[0] = g, delta, m2, v2

    blk = pl.BlockSpec((1, ADAM_ROWS, D), lambda l, j: (l, j, 0))
    sds = jax.ShapeDtypeStruct(w.shape, F32)
    return pl.pallas_call(
        body, name="adam_big", grid=(DEPTH, LROWS // ADAM_ROWS),
        in_specs=[pl.BlockSpec((NDEV, 1, ADAM_ROWS, D), lambda l, j: (0, l, j, 0)), blk, blk, blk],
        out_specs=[blk] * 4, out_shape=[sds] * 4,
        compiler_params=_cparams("arbitrary", "arbitrary"),
    )(recv, w, m, v)


def _sum_small(gathered, rows):
    def body(g_ref, o_ref):
        acc = g_ref[0:rows, :]
        for s in range(1, NDEV):
            acc = acc + g_ref[s * rows:(s + 1) * rows, :]
        o_ref[...] = acc

    return pl.pallas_call(
        body, name="sum_small",
        in_specs=[pl.BlockSpec(memory_space=pltpu.VMEM)], out_specs=pl.BlockSpec(memory_space=pltpu.VMEM),
        out_shape=jax.ShapeDtypeStruct((rows, D), F32),
        compiler_params=pltpu.CompilerParams(vmem_limit_bytes=VMEM_LIMIT),
    )(gathered)


def _adam_small(params):
    k = len(params)

    def body(*refs):
        ins, outs = refs[:4 * k], refs[4 * k:]
        for j in range(k):
            w_ref, g_ref, m_ref, v_ref = ins[4 * j:4 * j + 4]
            delta, m2, v2 = _adam(w_ref[...], g_ref[...], m_ref[...], v_ref[...])
            outs[3 * j][...], outs[3 * j + 1][...], outs[3 * j + 2][...] = delta, m2, v2

    flat = [a for p in params for a in p]
    shapes = [jax.ShapeDtypeStruct(p[0].shape, F32) for p in params for _ in range(3)]
    vm = pl.BlockSpec(memory_space=pltpu.VMEM)
    res = pl.pallas_call(
        body, name="adam_small", in_specs=[vm] * len(flat), out_specs=[vm] * len(shapes), out_shape=shapes,
        compiler_params=pltpu.CompilerParams(vmem_limit_bytes=VMEM_LIMIT),
    )(*flat)
    return [tuple(res[3 * j:3 * j + 3]) for j in range(k)]


def _pack(ab_in, ab_out, pw1, pw2, wg, wu, wd):
    ins = jnp.swapaxes(jnp.stack([ab_in[0], pw1[0], ab_in[1], pw1[1]]), 1, 2)
    outs = jnp.stack([ab_out[0], pw2[0], ab_out[1], pw2[1]])
    return jnp.concatenate([ins, outs, jnp.swapaxes(wg, 1, 2), jnp.swapaxes(wu, 1, 2), wd], axis=1)


def _unpack(p):
    ins = jnp.swapaxes(p[:, OFF_IN:OFF_OUT], 1, 2)
    outs = p[:, OFF_OUT:OFF_G]
    return (ins[0::2], outs[0::2], ins[1::2], outs[1::2], jnp.swapaxes(p[:, OFF_G:OFF_U], 1, 2),
            jnp.swapaxes(p[:, OFF_U:OFF_D], 1, 2), p[:, OFF_D:LROWS])


def _unshard(flat, lead, per):
    k = len(lead)
    a = flat.reshape((NDEV,) + tuple(lead) + (per,))
    a = jnp.transpose(a, tuple(range(1, k + 1)) + (0, k + 1))
    return a.reshape(tuple(lead) + (NDEV * per,))


def _rows_of(a):
    f = a.reshape(-1)
    pad = (-f.shape[0]) % D
    if pad:
        f = jnp.concatenate([f, jnp.zeros((pad,), f.dtype)])
    return f.reshape(-1, D)


def _pad_rows(a, rows):
    return jnp.concatenate([a, jnp.zeros((rows - a.shape[0],) + a.shape[1:], a.dtype)], axis=0)


def kernel(x, c, norm_mix_g, norm_ffn_g, w_mod, b_mod, ab_w_in, ab_conv, ab_w_pool, ab_pool_scale, ab_w_out, cf_w_pw1, cf_b_pw1, cf_w_dw, cf_b_dw, cf_ln_g, cf_ln_b, cf_w_pw2, cf_b_pw2, ffn_w_gate, ffn_w_up, ffn_w_down, final_norm_g, loss_target, m_norm_mix_g, m_norm_ffn_g, m_w_mod, m_b_mod, m_ab_w_in, m_ab_conv, m_ab_w_pool, m_ab_pool_scale, m_ab_w_out, m_cf_w_pw1, m_cf_b_pw1, m_cf_w_dw, m_cf_b_dw, m_cf_ln_g, m_cf_ln_b, m_cf_w_pw2, m_cf_b_pw2, m_ffn_w_gate, m_ffn_w_up, m_ffn_w_down, m_final_norm_g, v_norm_mix_g, v_norm_ffn_g, v_w_mod, v_b_mod, v_ab_w_in, v_ab_conv, v_ab_w_pool, v_ab_pool_scale, v_ab_w_out, v_cf_w_pw1, v_cf_b_pw1, v_cf_w_dw, v_cf_b_dw, v_cf_ln_g, v_cf_ln_b, v_cf_w_pw2, v_cf_b_pw2, v_ffn_w_gate, v_ffn_w_up, v_ffn_w_down, v_final_norm_g):
    t = x.shape[1]
    tm = 512 if t % 512 == 0 else t // 2
    tk = 512 if t % 512 == 0 else t // 2
    tmo = tm // 2
    me = 4 * lax.axis_index("x") + 2 * lax.axis_index("y") + lax.axis_index("c")
    xs, tgt = x[0], loss_target[0]

    sharded = [ab_conv, cf_b_pw1, cf_w_dw, cf_b_dw, cf_ln_g, cf_ln_b, cf_b_pw2]
    flat = jnp.concatenate([a.reshape(-1) for a in sharded])
    n_flat = flat.shape[0]
    g1 = _gather_small(jnp.concatenate([_pad_rows(c, 8), _pad_rows(_rows_of(flat), 16)], axis=0), "gather_cond")
    g1 = g1.reshape(NDEV, 24, D)
    c_all = g1[:, 0, :]
    flat_all = g1[:, 8:, :].reshape(NDEV, -1)[:, :n_flat]
    full, o = [], 0
    for a in sharded:
        lead, per = a.shape[:-1], a.shape[-1]
        size = a.size
        full.append(_unshard(flat_all[:, o:o + size], lead, per))
        o += size
    ab_conv_f, b_pw1_f, w_dw_f, b_dw_f, ln_g_f, ln_b_f, b_pw2_f = full

    b_sl = lax.dynamic_slice_in_dim(b_mod, me * MODW, MODW, axis=1).reshape(DEPTH, 1, MODW)
    mod_part, c_act = _mod_fwd(c_all, w_mod, b_sl)
    g2 = _gather_small(mod_part, "gather_mod").reshape(NDEV, NDEV, DEPTH, MODW)
    mod = jnp.transpose(lax.dynamic_index_in_dim(g2, me, axis=1, keepdims=False), (1, 0, 2)).reshape(DEPTH, N_MOD, D)
    zeros4 = jnp.zeros((4, D), F32)

    def vec_of(g, layer, k):
        return jnp.concatenate([g[layer][None], mod[layer, k + 1][None], mod[layer, k][None],
                                mod[layer, k + 2][None], zeros4], axis=0)

    vmix = [vec_of(norm_mix_g, l, 0) for l in range(DEPTH)]
    vffn = [vec_of(norm_ffn_g, l, 3) for l in range(DEPTH)]

    w_pack = _pack(ab_w_in, ab_w_out, cf_w_pw1, cf_w_pw2, ffn_w_gate, ffn_w_up, ffn_w_down)
    p16 = w_pack.astype(BF16)
    gw = [_gather_weights(p16[0])]

    wp16 = ab_w_pool.astype(BF16)
    wdw32 = [_pad_rows(w_dw_f[i], 32) for i in range(2)]
    sm_odd = [jnp.concatenate([b_dw_f[i][None], ln_g_f[i][None], ln_b_f[i][None], b_pw2_f[i][None], zeros4], axis=0)
              for i in range(2)]
    zero_bias = jnp.zeros((1, 2 * D), F32)

    saved = []
    xc = xs
    for l in range(DEPTH):
        i = l // 2
        if l % 2 == 0:
            u = _inproj(xc, vmix[l], zero_bias, gw[l], l, tm)
            x_mid, y_mix = _mix_even_fwd(u, xc, vmix[l], ab_conv_f[i], wp16[i], ab_pool_scale[i][None], gw[l], l, tm)
            z2 = None
        else:
            u = _inproj(xc, vmix[l], b_pw1_f[i][None], gw[l], l, tm)
            x_mid, y_mix, z2 = _mix_odd_fwd(u, xc, vmix[l], wdw32[i], sm_odd[i], gw[l], l, tmo)
        if l + 1 < DEPTH:
            x_out, a, b, y_ffn, g_next = _ffn_fwd(x_mid, vffn[l], gw[l], l, tm, _gather_comm(p16[l + 1]))
            gw.append(g_next)
        else:
            x_out, a, b, y_ffn = _ffn_fwd(x_mid, vffn[l], gw[l], l, tm)
        saved.append((xc, u, y_mix, z2, x_mid, a, b, y_ffn))
        xc = x_out

    dx, fsum = _final(xc, tgt, final_norm_g[None], tm)
    loss = lax.psum(jnp.sum(fsum[1]), ("x", "y", "c"))
    d_final_g = fsum[0]

    recv = _empty_recv()
    io_specs = [(OFF_IN, ROWS_IN), (OFF_OUT, ROWS_OUT)]
    pending = None
    dmod = [None] * DEPTH
    d_mix_g, d_ffn_g = [None] * DEPTH, [None] * DEPTH
    d_conv, d_pool, d_pscale = [None] * 2, [None] * 2, [None] * 2
    d_bpw1, d_wdw, d_bdw, d_lng, d_lnb, d_bpw2 = ([None] * 2 for _ in range(6))
    for l in reversed(range(DEPTH)):
        i = l // 2
        x_in, u, y_mix, z2, x_mid, a, b, y_ffn = saved[l]
        if pending is None:
            dy, da, db, s_h = _ffn_bwd_hidden(dx, y_ffn, a, b, vffn[l], gw[l], l, tm)
        else:
            dy, da, db, s_h, recv = _ffn_bwd_hidden(dx, y_ffn, a, b, vffn[l], gw[l], l, tm,
                                                    _scatter_comm(pending, io_specs, recv, l + 1))
        g_down = _wgrad(a, dy, f"wgrad_down_l{l}", tk, lhs2=b)
        dx_mid, h2, s_i, recv = _ffn_bwd_input(da, db, x_mid, dx, vffn[l], gw[l], l, tm,
                                               _scatter_comm([g_down], [(OFF_D, FS)], recv, l))
        g_gate = _wgrad(da, h2, f"wgrad_gate_l{l}", tk)
        g_up = _wgrad(db, h2, f"wgrad_up_l{l}", tk)
        gu_comm = _scatter_comm([g_gate, g_up], [(OFF_G, FS), (OFF_U, FS)], recv, l)
        d_ffn_g[l] = s_i[3]
        mod_ffn = [s_i[1], s_i[2], s_h[0]]
        if l % 2 == 0:
            dx, du, h, cat, dym, s_m, dwp, recv = _mix_even_bwd(dx_mid, u, x_in, y_mix, vmix[l], ab_conv_f[i], wp16[i],
                                                                ab_pool_scale[i][None], gw[l], l, tm, gu_comm)
            g_out = _wgrad(cat, dym, f"wgrad_out_l{l}", tk)
            d_conv[i], d_pool[i], d_pscale[i] = s_m[4:7, :DA], dwp, s_m[7, :DA]
            mod_mix = [s_m[1], s_m[2], s_m[0]]
            d_mix_g[l] = s_m[3]
        else:
            dym, z3, dz2, s_1 = _mix_odd_bwd1(dx_mid, y_mix, z2, vmix[l], sm_odd[i], gw[l], l, tm)
            dx, du, h, s_2, dwdw, recv = _mix_odd_bwd2(dz2, u, x_in, dx_mid, vmix[l], wdw32[i], gw[l], l, tmo, gu_comm)
            g_out = _wgrad(z3, dym, f"wgrad_out_l{l}", tk)
            d_bpw1[i], d_wdw[i], d_bdw[i] = s_2[8:10].reshape(2 * D), dwdw[:CONF_K], s_1[7]
            d_lng[i], d_lnb[i], d_bpw2[i] = s_1[5], s_1[6], s_1[4]
            mod_mix = [s_2[1], s_2[2], s_1[0]]
            d_mix_g[l] = s_2[3]
        dmod[l] = jnp.stack(mod_mix + mod_ffn)
        pending = [_wgrad(du, h, f"wgrad_in_l{l}", tk), g_out]
    grad_x = dx[None]
    recv = _scatter_grads(pending, io_specs, recv, 0)

    m_pack = _pack(m_ab_w_in, m_ab_w_out, m_cf_w_pw1, m_cf_w_pw2, m_ffn_w_gate, m_ffn_w_up, m_ffn_w_down)
    v_pack = _pack(v_ab_w_in, v_ab_w_out, v_cf_w_pw1, v_cf_w_pw2, v_ffn_w_gate, v_ffn_w_up, v_ffn_w_down)
    big = [_unpack(p) for p in _adam_big(recv, w_pack, m_pack, v_pack)]

    small = [jnp.stack(dmod), jnp.stack(d_mix_g), jnp.stack(d_ffn_g), d_final_g, jnp.stack(d_pool),
             jnp.stack(d_pscale), jnp.stack(d_conv), jnp.stack(d_bpw1), jnp.stack(d_wdw), jnp.stack(d_bdw),
             jnp.stack(d_lng), jnp.stack(d_lnb), jnp.stack(d_bpw2)]
    small_rows = [_rows_of(a) for a in small]
    n_rows = sum(a.shape[0] for a in small_rows)
    pad_rows = -(-n_rows // 8) * 8
    g3 = _gather_small(_pad_rows(jnp.concatenate(small_rows, axis=0), pad_rows), "gather_small_grads")
    summed = _sum_small(g3, pad_rows)
    outs, o = [], 0
    for a, r in zip(small, small_rows):
        outs.append(summed[o:o + r.shape[0]].reshape(-1)[:a.size].reshape(a.shape))
        o += r.shape[0]
    (g_bmod, g_mix_g, g_ffn_g, g_final, g_pool, g_pscale, g_conv, g_bpw1, g_wdw, g_bdw, g_lng, g_lnb, g_bpw2) = outs
    g_bmod = g_bmod.reshape(DEPTH, N_MOD * D)

    def my_shard(a):
        per = a.shape[-1] // NDEV
        return lax.dynamic_slice_in_dim(a, me * per, per, axis=a.ndim - 1)

    g_conv, g_bpw1, g_wdw, g_bdw, g_lng, g_lnb, g_bpw2 = [
        my_shard(a) for a in (g_conv, g_bpw1, g_wdw, g_bdw, g_lng, g_lnb, g_bpw2)]

    dmod_all = g3.reshape(NDEV, pad_rows, D)[:, :DEPTH * N_MOD, :].reshape(NDEV, DEPTH, N_MOD * D)
    dmod_mine = jnp.transpose(lax.dynamic_slice_in_dim(dmod_all, me * MODW, MODW, axis=2), (1, 0, 2))
    g_wmod, d_wmod, nm_wmod, nv_wmod = _mod_bwd_adam(c_act.T, dmod_mine, w_mod, m_w_mod, v_w_mod)

    small_params = [
        (norm_mix_g, g_mix_g, m_norm_mix_g, v_norm_mix_g), (norm_ffn_g, g_ffn_g, m_norm_ffn_g, v_norm_ffn_g),
        (b_mod, g_bmod, m_b_mod, v_b_mod), (ab_conv, g_conv, m_ab_conv, v_ab_conv),
        (ab_w_pool, g_pool, m_ab_w_pool, v_ab_w_pool), (ab_pool_scale, g_pscale, m_ab_pool_scale, v_ab_pool_scale),
        (cf_b_pw1, g_bpw1, m_cf_b_pw1, v_cf_b_pw1), (cf_w_dw, g_wdw, m_cf_w_dw, v_cf_w_dw),
        (cf_b_dw, g_bdw, m_cf_b_dw, v_cf_b_dw), (cf_ln_g, g_lng, m_cf_ln_g, v_cf_ln_g),
        (cf_ln_b, g_lnb, m_cf_ln_b, v_cf_ln_b), (cf_b_pw2, g_bpw2, m_cf_b_pw2, v_cf_b_pw2),
        (final_norm_g, g_final, m_final_norm_g, v_final_norm_g)]

    def two_d(a):
        return a.reshape(-1, a.shape[-1])

    upd = _adam_small([tuple(two_d(a) for a in p) for p in small_params])
    upd = [tuple(r.reshape(p[0].shape) for r in u) for u, p in zip(upd, small_params)]
    (s_mix, s_ffn, s_bmod, s_conv, s_pool, s_pscale, s_bpw1, s_wdw, s_bdw, s_lng, s_lnb, s_bpw2, s_final) = upd
    small_g = [p[1] for p in small_params]
    (q_mix, q_ffn, q_bmod, q_conv, q_pool, q_pscale, q_bpw1, q_wdw, q_bdw, q_lng, q_lnb, q_bpw2, q_final) = small_g

    def ordered(k):
        ab_in, ab_out, pw1, pw2, wg, wu, wd = big[k]
        if k == 0:
            sm = dict(mix=q_mix, ffn=q_ffn, bmod=q_bmod, conv=q_conv, pool=q_pool, pscale=q_pscale, bpw1=q_bpw1,
                      wdw=q_wdw, bdw=q_bdw, lng=q_lng, lnb=q_lnb, bpw2=q_bpw2, final=q_final)
            wmod = g_wmod
        else:
            j = k - 1
            sm = dict(mix=s_mix[j], ffn=s_ffn[j], bmod=s_bmod[j], conv=s_conv[j], pool=s_pool[j], pscale=s_pscale[j],
                      bpw1=s_bpw1[j], wdw=s_wdw[j], bdw=s_bdw[j], lng=s_lng[j], lnb=s_lnb[j], bpw2=s_bpw2[j],
                      final=s_final[j])
            wmod = (d_wmod, nm_wmod, nv_wmod)[j]
        return [sm["mix"], sm["ffn"], wmod, sm["bmod"], ab_in, sm["conv"], sm["pool"], sm["pscale"], ab_out,
                pw1, sm["bpw1"], sm["wdw"], sm["bdw"], sm["lng"], sm["lnb"], pw2, sm["bpw2"], wg, wu, wd, sm["final"]]

    return (loss, grad_x, *ordered(0), *ordered(1), *ordered(2), *ordered(3))
```

```python
import functools

import jax
import jax.numpy as jnp
from jax import lax
from jax.experimental import pallas as pl
from jax.experimental.pallas import tpu as pltpu

F32 = jnp.float32
BF16 = jnp.bfloat16
MESH = pl.DeviceIdType.MESH

NDEV = 8
DEPTH = 4
D = 1024
FF = 2816
FS = FF // NDEV
DA = 512
PG = 128
POOL = ((2, 1, 0), (4, 2, 1), (8, 4, 3), (16, 8, 7))
CONF_K = 31
CONF_L = 15
N_MOD = 6
MODW = N_MOD * D // NDEV
RMS_EPS = 1e-6
LN_EPS = 1e-5

ROWS_IN, ROWS_OUT = 2 * D // NDEV, D // NDEV
OFF_IN, OFF_OUT = 0, ROWS_IN
OFF_G = OFF_OUT + ROWS_OUT
OFF_U = OFF_G + FS
OFF_D = OFF_U + FS
LROWS = OFF_D + FS

MXU_N = 256
HALO = 16
VMEM_LIMIT = 60 * 1024 * 1024

ADAM_LR, ADAM_B1, ADAM_B2, ADAM_EPS, ADAM_WD, ADAM_STEP = 1e-3, 0.9, 0.999, 1e-8, 0.01, 10
ADAM_C1 = 1.0 / (1.0 - ADAM_B1 ** ADAM_STEP)
ADAM_C2 = 1.0 / (1.0 - ADAM_B2 ** ADAM_STEP)


def _nn(a, b):
    return jnp.dot(a, b, preferred_element_type=F32)


def _nt(a, b):
    return lax.dot_general(a, b, (((1,), (1,)), ((), ())), preferred_element_type=F32)


def _tn(a, b):
    return lax.dot_general(a, b, (((0,), (0,)), ((), ())), preferred_element_type=F32)


def _sig(x):
    return 1.0 / (1.0 + jnp.exp(-x))


def _rowsum(x):
    return jnp.sum(x, axis=0, keepdims=True)


def _lanemean(x):
    return jnp.mean(x, axis=-1, keepdims=True)


def _norm_fwd(x, g, sc, sh):
    r = lax.rsqrt(_lanemean(x * x) + RMS_EPS)
    n = x * r
    return n * (g * (1.0 + sc)) + sh, n, r


def _norm_bwd(dh, n, r, g, sc):
    dn = dh * (g * (1.0 + sc))
    return r * (dn - n * _lanemean(dn * n))


def _adam(w, g, m, v):
    m2 = ADAM_B1 * m + (1.0 - ADAM_B1) * g
    v2 = ADAM_B2 * v + (1.0 - ADAM_B2) * (g * g)
    delta = -ADAM_LR * ((m2 * ADAM_C1) / (jnp.sqrt(v2 * ADAM_C2) + ADAM_EPS) + ADAM_WD * w)
    return delta, m2, v2


def _load_weights(g_ref, specs, sems):
    cps = [pltpu.make_async_copy(g_ref.at[:, pl.ds(off, rows), :], dst, sems.at[k])
           for k, (off, rows, dst) in enumerate(specs)]
    for cp in cps:
        cp.start()
    for cp in cps:
        cp.wait()


def _load_weight_rows(g_ref, specs, sems):
    cps = [pltpu.make_async_copy(g_ref.at[d, pl.ds(off, rows), :], dst.at[pl.ds(d * rows, rows), :],
                                 sems.at[NDEV * k + d])
           for k, (off, rows, dst) in enumerate(specs) for d in range(NDEV)]
    for cp in cps:
        cp.start()
    for cp in cps:
        cp.wait()


def _cparams(*sem):
    return pltpu.CompilerParams(dimension_semantics=sem if sem else None, vmem_limit_bytes=VMEM_LIMIT)


def _tile(tm, w):
    return pl.BlockSpec((tm, w), lambda i: (i, 0))


def _full(shape):
    nd = len(shape)
    return pl.BlockSpec(shape, lambda i: (0,) * nd)


def _halo_specs(tm, w, total_rows):
    tb = tm // HALO
    nb = total_rows // HALO
    prev = pl.BlockSpec((HALO, w), lambda i: (jnp.maximum(i * tb - 1, 0), 0))
    nxt = pl.BlockSpec((HALO, w), lambda i: (jnp.minimum((i + 1) * tb, nb - 1), 0))
    return prev, nxt


ANY = pl.BlockSpec(memory_space=pl.ANY)


def _peers():
    x, y, c = lax.axis_index("x"), lax.axis_index("y"), lax.axis_index("c")
    return x, y, c


def _gather_small(v, name):
    m_per, n = v.shape

    def body(x_ref, out_ref, send_sems, recv_sems, local_sem):
        x, y, c = _peers()
        me, sibling = (x, y, c), (x, y, 1 - c)
        chips = [(1 - x, y), (x, 1 - y), (1 - x, 1 - y)]

        def rows(px, py, pc):
            return out_ref.at[pl.ds((4 * px + 2 * py + pc) * m_per, m_per), :]

        def copy(k, block, to, src=None):
            return pltpu.make_async_remote_copy(
                src_ref=rows(*block) if src is None else src, dst_ref=rows(*block),
                send_sem=send_sems.at[k], recv_sem=recv_sems.at[k], device_id=to, device_id_type=MESH)

        mine = pltpu.make_async_copy(x_ref, rows(*me), local_sem)
        mine.start()
        first = [copy(0, me, sibling, src=x_ref)]
        first += [copy(1 + j, me, (*chip, c), src=x_ref) for j, chip in enumerate(chips)]
        for cp in first:
            cp.start()
        passed = [copy(4 + j, (*chip, c), sibling) for j, chip in enumerate(chips)]
        for j, chip in enumerate(chips):
            copy(1 + j, (*chip, c), me).wait_recv()
            passed[j].start()
        copy(0, sibling, me).wait_recv()
        for j, chip in enumerate(chips):
            copy(4 + j, (*chip, 1 - c), me).wait_recv()
        for cp in first + passed:
            cp.wait_send()
        mine.wait()

    return pl.pallas_call(
        body, name=name,
        out_shape=jax.ShapeDtypeStruct((NDEV * m_per, n), v.dtype),
        in_specs=[pl.BlockSpec(memory_space=pltpu.VMEM)],
        out_specs=pl.BlockSpec(memory_space=pltpu.VMEM),
        scratch_shapes=[pltpu.SemaphoreType.DMA((7,)), pltpu.SemaphoreType.DMA((7,)), pltpu.SemaphoreType.DMA],
        compiler_params=pltpu.CompilerParams(vmem_limit_bytes=VMEM_LIMIT),
    )(v)


class _Comm:
    def __init__(self, ins, outs, aliases, bind):
        self.ins, self.outs, self.aliases, self.bind = ins, outs, aliases, bind


COMM_SEMS = [pltpu.SemaphoreType.DMA((7,)), pltpu.SemaphoreType.DMA((7,)), pltpu.SemaphoreType.DMA]


def _gather_hooks(p_ref, out_ref, send_sems, recv_sems, local_sem):
    def parts():
        x, y, c = _peers()
        me, sibling = (x, y, c), (x, y, 1 - c)
        chips = [(1 - x, y), (x, 1 - y), (1 - x, 1 - y)]

        def slab(px, py, pc):
            return out_ref.at[4 * px + 2 * py + pc]

        def copy(k, block, to, src=None):
            return pltpu.make_async_remote_copy(
                src_ref=slab(*block) if src is None else src, dst_ref=slab(*block),
                send_sem=send_sems.at[k], recv_sem=recv_sems.at[k], device_id=to, device_id_type=MESH)

        def mine():
            return pltpu.make_async_copy(p_ref, slab(*me), local_sem)

        def first():
            return [copy(0, me, sibling, src=p_ref)] + [copy(1 + j, me, (*chip, c), src=p_ref)
                                                        for j, chip in enumerate(chips)]

        def passed():
            return [copy(4 + j, (*chip, c), sibling) for j, chip in enumerate(chips)]

        def from_chips():
            return [copy(1 + j, (*chip, c), me) for j, chip in enumerate(chips)]

        def from_sibling():
            return [copy(0, sibling, me)] + [copy(4 + j, (*chip, 1 - c), me) for j, chip in enumerate(chips)]

        return mine, first, passed, from_chips, from_sibling

    def start():
        mine, first, _, _, _ = parts()
        mine().start()
        for cp in first():
            cp.start()

    def middle():
        _, _, passed, from_chips, _ = parts()
        for arrived, onward in zip(from_chips(), passed()):
            arrived.wait_recv()
            onward.start()

    def end():
        mine, first, passed, _, from_sibling = parts()
        for cp in from_sibling():
            cp.wait_recv()
        for cp in first() + passed():
            cp.wait_send()
        mine().wait()

    return start, middle, end


def _gather_comm(p):
    return _Comm([p], [jax.ShapeDtypeStruct((NDEV,) + p.shape, p.dtype)], {},
                 lambda cins, couts, sems: _gather_hooks(cins[0], couts[0], *sems))


def _gather_weights(p):
    def body(p_ref, out_ref, send_sems, recv_sems, local_sem):
        for hook in _gather_hooks(p_ref, out_ref, send_sems, recv_sems, local_sem):
            hook()

    return pl.pallas_call(
        body, name="gather_weights",
        out_shape=jax.ShapeDtypeStruct((NDEV,) + p.shape, p.dtype),
        in_specs=[ANY], out_specs=ANY, scratch_shapes=COMM_SEMS,
    )(p)


def _scatter_hooks(src_refs, specs, r_ref, layer, send_sems, recv_sems, local_sem):
    off0 = specs[0][0]
    total = sum(rows for _, rows in specs)

    def start():
        x, y, c = _peers()
        me = 4 * x + 2 * y + c

        def part(k, dev):
            off, rows = specs[k]
            src = src_refs[k].at[pl.ds(pl.multiple_of(dev * rows, 16), rows), :]
            return src, r_ref.at[me, layer, pl.ds(off, rows), :]

        for k in range(len(specs)):
            src, dst = part(k, me)
            pltpu.make_async_copy(src, dst, local_sem).start()
        for r in range(1, NDEV):
            px = 1 - x if r & 4 else x
            py = 1 - y if r & 2 else y
            pc = 1 - c if r & 1 else c
            for k in range(len(specs)):
                src, dst = part(k, 4 * px + 2 * py + pc)
                pltpu.make_async_remote_copy(
                    src_ref=src, dst_ref=dst, send_sem=send_sems.at[r - 1], recv_sem=recv_sems.at[r - 1],
                    device_id=(px, py, pc), device_id_type=MESH).start()

    def end():
        x, y, c = _peers()
        whole = r_ref.at[0, layer, pl.ds(off0, total), :]
        for r in range(1, NDEV):
            done = pltpu.make_async_remote_copy(
                src_ref=whole, dst_ref=whole, send_sem=send_sems.at[r - 1], recv_sem=recv_sems.at[r - 1],
                device_id=(x, y, c), device_id_type=MESH)
            done.wait_recv()
            done.wait_send()
        pltpu.make_async_copy(whole, whole, local_sem).wait()

    return start, None, end


def _scatter_comm(srcs, specs, recv, layer):
    k = len(srcs)
    return _Comm(list(srcs) + [recv], [jax.ShapeDtypeStruct(recv.shape, recv.dtype)], {k: 0},
                 lambda cins, couts, sems: _scatter_hooks(cins[:k], specs, couts[0], layer, *sems))


def _scatter_grads(srcs, specs, recv, layer):
    k = len(srcs)

    def body(*refs):
        start, _, end = _scatter_hooks(refs[:k], specs, refs[k + 1], layer, *refs[k + 2:])
        start()
        end()

    return pl.pallas_call(
        body, name=f"scatter_grads_l{layer}",
        out_shape=jax.ShapeDtypeStruct(recv.shape, recv.dtype),
        in_specs=[ANY] * (k + 1), out_specs=ANY, scratch_shapes=COMM_SEMS,
        input_output_aliases={k: 0},
    )(*srcs, recv)


def _empty_recv():
    def body(o_ref):
        del o_ref

    return pl.pallas_call(body, name="recv_buffer", out_specs=ANY,
                          out_shape=jax.ShapeDtypeStruct((NDEV, DEPTH, LROWS, D), BF16))()


def _host_call(inner, comm, *, name, grid, in_specs, out_specs, out_shape, scratch_shapes, operands):
    if comm is None:
        return pl.pallas_call(
            inner, name=name, grid=grid, in_specs=in_specs, out_specs=out_specs, out_shape=out_shape,
            scratch_shapes=scratch_shapes, compiler_params=_cparams("arbitrary"))(*operands)
    n_in, n_out, n_s = len(in_specs), len(out_specs), len(scratch_shapes)
    k_in, k_out = len(comm.ins), len(comm.outs)
    steps = grid[0]

    def body(*refs):
        ins, cins = refs[:n_in], refs[n_in:n_in + k_in]
        o0 = n_in + k_in
        outs, couts = refs[o0:o0 + n_out], refs[o0 + n_out:o0 + n_out + k_out]
        s0 = o0 + n_out + k_out
        scr, sems = refs[s0:s0 + n_s], refs[s0 + n_s:]
        start, middle, end = comm.bind(cins, couts, sems)
        i = pl.program_id(0)
        pl.when(i == 0)(start)
        if middle is not None:
            pl.when(i == steps * 3 // 4)(middle)
        inner(*ins, *outs, *scr)
        pl.when(i == steps - 1)(end)

    return pl.pallas_call(
        body, name=name, grid=grid, in_specs=list(in_specs) + [ANY] * k_in,
        out_specs=list(out_specs) + [ANY] * k_out, out_shape=list(out_shape) + list(comm.outs),
        scratch_shapes=list(scratch_shapes) + COMM_SEMS,
        input_output_aliases={n_in + a: n_out + b for a, b in comm.aliases.items()},
        compiler_params=_cparams("arbitrary"))(*operands, *comm.ins)


def _mod_fwd(c_all, w_mod, b_sl):
    def body(c_ref, w_ref, b_ref, o_ref, ca_ref):
        cv = c_ref[...]
        ca = cv * _sig(cv)
        ca_ref[...] = ca
        o_ref[...] = jnp.dot(ca, w_ref[0], preferred_element_type=F32, precision=lax.Precision.HIGHEST) + b_ref[0]

    return pl.pallas_call(
        body, name="mod_fwd", grid=(DEPTH,),
        in_specs=[_full((NDEV, D)), pl.BlockSpec((1, D, MODW), lambda l: (l, 0, 0)),
                  pl.BlockSpec((1, 1, MODW), lambda l: (l, 0, 0))],
        out_specs=[pl.BlockSpec((NDEV, MODW), lambda l: (0, l)), _full((NDEV, D))],
        out_shape=[jax.ShapeDtypeStruct((NDEV, DEPTH * MODW), F32), jax.ShapeDtypeStruct((NDEV, D), F32)],
        compiler_params=_cparams("arbitrary"),
    )(c_all, w_mod, b_sl)


def _mod_bwd_adam(ca_t, dmod, w, m, v):
    def body(ct_ref, dm_ref, w_ref, m_ref, v_ref, g_ref, d_ref, mo_ref, vo_ref):
        g = jnp.dot(ct_ref[...], dm_ref[0], preferred_element_type=F32, precision=lax.Precision.HIGHEST)
        delta, m2, v2 = _adam(w_ref[0], g, m_ref[0], v_ref[0])
        g_ref[0], d_ref[0], mo_ref[0], vo_ref[0] = g, delta, m2, v2

    blk = pl.BlockSpec((1, D, MODW), lambda l: (l, 0, 0))
    sds = jax.ShapeDtypeStruct(w.shape, F32)
    return pl.pallas_call(
        body, name="mod_bwd_adam", grid=(DEPTH,),
        in_specs=[_full((D, NDEV)), pl.BlockSpec((1, NDEV, MODW), lambda l: (l, 0, 0)), blk, blk, blk],
        out_specs=[blk] * 4, out_shape=[sds] * 4,
        compiler_params=_cparams("arbitrary"),
    )(ca_t, dmod, w, m, v)


def _inproj(x, vec, bias, gw, layer, tm):
    t = x.shape[0]

    def body(x_ref, vec_ref, b_ref, g_ref, u_ref, w_s, sems):
        @pl.when(pl.program_id(0) == 0)
        def _():
            _load_weights(g_ref, [(OFF_IN, ROWS_IN, w_s)], sems)

        h, _, _ = _norm_fwd(x_ref[...], vec_ref[0:1], vec_ref[1:2], vec_ref[2:3])
        w = w_s[...].reshape(2 * D, D)
        u_ref[...] = (_nt(h.astype(BF16), w) + b_ref[...]).astype(BF16)

    return pl.pallas_call(
        body, name=f"inproj_l{layer}", grid=(t // tm,),
        in_specs=[_tile(tm, D), _full((8, D)), _full((1, 2 * D)), ANY],
        out_specs=_tile(tm, 2 * D), out_shape=jax.ShapeDtypeStruct((t, 2 * D), BF16),
        scratch_shapes=[pltpu.VMEM((NDEV, ROWS_IN, D), BF16), pltpu.SemaphoreType.DMA((1,))],
        compiler_params=_cparams("arbitrary"),
    )(x, vec, bias, gw)


def _fill_even(qe, pe, be, part_ref, lo, rows, valid):
    cg = part_ref[:, DA:2 * DA].astype(F32)
    v = part_ref[:, 2 * DA:3 * DA].astype(F32)
    q = cg * v
    p = part_ref[:, 3 * DA:4 * DA].astype(F32)
    if valid is not None:
        q = jnp.where(valid, q, 0.0)
        p = jnp.where(valid, p, 0.0)
    qe[lo:lo + rows, :] = q
    pe[lo:lo + rows, :] = p
    if be is not None:
        b = part_ref[:, 0:DA].astype(F32)
        be[lo:lo + rows, :] = b if valid is None else jnp.where(valid, b, 0.0)


def _conv3(ca_ref, qe, tm):
    return (ca_ref[0:1] * qe[HALO - 1:HALO - 1 + tm] + ca_ref[1:2] * qe[HALO:HALO + tm]
            + ca_ref[2:3] * qe[HALO + 1:HALO + 1 + tm])


def _pool_counts(t0, rows, first_row, left, right, t):
    tg = t0 + first_row + lax.broadcasted_iota(jnp.int32, (rows, 1), 0)
    cnt = jnp.minimum(tg + right, t - 1) - jnp.maximum(tg - left, 0) + 1
    return jnp.maximum(cnt, 1).astype(F32)


def _pool_minus_id(pe, gi, left, right, inv_cnt, tm):
    c0 = gi * PG
    s = pe[HALO - left:HALO - left + tm, c0:c0 + PG]
    for j in range(-left + 1, right + 1):
        s = s + pe[HALO + j:HALO + j + tm, c0:c0 + PG]
    return s * inv_cnt - pe[HALO:HALO + tm, c0:c0 + PG]


def _mix_even_fwd(u, x, vec, ca, wp, ps, gw, layer, tm):
    t = x.shape[0]
    n = t // tm
    e = tm + 2 * HALO

    def body(u_ref, up_ref, un_ref, x_ref, vec_ref, ca_ref, wp_ref, ps_ref, g_ref, xo_ref, y_ref, w_s, qe, pe, sems):
        i = pl.program_id(0)

        @pl.when(i == 0)
        def _():
            _load_weights(g_ref, [(OFF_OUT, ROWS_OUT, w_s)], sems)

        _fill_even(qe, pe, None, up_ref, 0, HALO, i > 0)
        _fill_even(qe, pe, None, u_ref, HALO, tm, None)
        _fill_even(qe, pe, None, un_ref, HALO + tm, HALO, i < n - 1)
        ya = u_ref[:, 0:DA].astype(F32) * _conv3(ca_ref, qe, tm)
        parts = [ya]
        for gi, (_, left, right) in enumerate(POOL):
            inv = 1.0 / _pool_counts(i * tm, tm, 0, left, right, t)
            pm = _pool_minus_id(pe, gi, left, right, inv, tm)
            parts.append(_nn(pm.astype(BF16), wp_ref[gi]) * ps_ref[0:1, gi * PG:(gi + 1) * PG])
        cat = jnp.concatenate(parts, axis=-1).astype(BF16)
        y = _nn(cat, w_s[...].reshape(D, D))
        y_ref[...] = y.astype(BF16)
        xo_ref[...] = x_ref[...] + vec_ref[3:4] * y

    prev, nxt = _halo_specs(tm, 2 * D, t)
    return pl.pallas_call(
        body, name=f"mix_even_fwd_l{layer}", grid=(n,),
        in_specs=[_tile(tm, 2 * D), prev, nxt, _tile(tm, D), _full((8, D)), _full((3, DA)),
                  _full((4, PG, PG)), _full((1, DA)), ANY],
        out_specs=[_tile(tm, D), _tile(tm, D)],
        out_shape=[jax.ShapeDtypeStruct((t, D), F32), jax.ShapeDtypeStruct((t, D), BF16)],
        scratch_shapes=[pltpu.VMEM((NDEV, ROWS_OUT, D), BF16), pltpu.VMEM((e, DA), F32), pltpu.VMEM((e, DA), F32),
                        pltpu.SemaphoreType.DMA((1,))],
        compiler_params=_cparams("arbitrary"),
    )(u, u, u, x, vec, ca, wp, ps, gw)


def _mix_even_bwd(dxo, u, x, y, vec, ca, wp, ps, gw, layer, tm, comm=None):
    t = x.shape[0]
    n = t // tm
    e = tm + 2 * HALO

    def body(dxo_ref, dp_ref, dn_ref, u_ref, up_ref, un_ref, x_ref, y_ref, vec_ref, ca_ref, wp_ref, ps_ref, g_ref,
             dxi_ref, du_ref, h_ref, cat_ref, dy_ref, sums_ref, dwp_ref,
             wo_s, wi_s, dye, qe, pe, be, dce, epe, sems):
        i = pl.program_id(0)

        @pl.when(i == 0)
        def _():
            _load_weights(g_ref, [(OFF_OUT, ROWS_OUT, wo_s), (OFF_IN, ROWS_IN, wi_s)], sems)
            sums_ref[...] = jnp.zeros_like(sums_ref)
            dwp_ref[...] = jnp.zeros_like(dwp_ref)

        gate = vec_ref[3:4]
        dxo_m = dxo_ref[...]
        dye[0:HALO, :] = jnp.where(i > 0, gate * dp_ref[...], 0.0).astype(BF16)
        dye[HALO:HALO + tm, :] = (gate * dxo_m).astype(BF16)
        dye[HALO + tm:e, :] = jnp.where(i < n - 1, gate * dn_ref[...], 0.0).astype(BF16)
        _fill_even(qe, pe, be, up_ref, 0, HALO, i > 0)
        _fill_even(qe, pe, be, u_ref, HALO, tm, None)
        _fill_even(qe, pe, be, un_ref, HALO + tm, HALO, i < n - 1)
        sums_ref[0:1, :] += _rowsum(dxo_m * y_ref[...].astype(F32))

        dcat = _nt(dye[...], wo_s[...].reshape(D, D))
        dce[...] = dcat[:, 0:DA] * be[...]
        cq = _conv3(ca_ref, qe, tm)
        bg = be[HALO:HALO + tm]
        dc_m = dce[HALO:HALO + tm]
        dbg = dcat[HALO:HALO + tm, 0:DA] * cq
        dq = (ca_ref[0:1] * dce[HALO + 1:HALO + 1 + tm] + ca_ref[1:2] * dc_m
              + ca_ref[2:3] * dce[HALO - 1:HALO - 1 + tm])
        cg = u_ref[:, DA:2 * DA].astype(F32)
        v = u_ref[:, 2 * DA:3 * DA].astype(F32)
        for k in range(3):
            sums_ref[4 + k:5 + k, 0:DA] += _rowsum(dc_m * qe[HALO - 1 + k:HALO - 1 + k + tm])
        du_parts = [dbg, dq * v, dq * cg]
        cat_parts = [bg * cq]
        for gi, (_, left, right) in enumerate(POOL):
            c0 = gi * PG
            scale = ps_ref[0:1, c0:c0 + PG]
            dyb = dcat[:, DA + c0:DA + c0 + PG]
            dybs = (dyb * scale).astype(BF16)
            dpm = _nt(dybs, wp_ref[gi])
            inv_e = 1.0 / _pool_counts(i * tm, e, -HALO, left, right, t)
            epe[:, c0:c0 + PG] = dpm * inv_e
            s_adj = epe[HALO - right:HALO - right + tm, c0:c0 + PG]
            for j in range(-right + 1, left + 1):
                s_adj = s_adj + epe[HALO + j:HALO + j + tm, c0:c0 + PG]
            du_parts.append(s_adj - dpm[HALO:HALO + tm])
            inv_m = 1.0 / _pool_counts(i * tm, tm, 0, left, right, t)
            pm = _pool_minus_id(pe, gi, left, right, inv_m, tm).astype(BF16)
            ybpre = _nn(pm, wp_ref[gi])
            sums_ref[7:8, c0:c0 + PG] += _rowsum(dyb[HALO:HALO + tm] * ybpre)
            dwp_ref[gi] += _tn(pm, dybs[HALO:HALO + tm])
            cat_parts.append(ybpre * scale)
        du = jnp.concatenate(du_parts, axis=-1).astype(BF16)
        du_ref[...] = du
        cat_ref[...] = jnp.concatenate(cat_parts, axis=-1).astype(BF16)
        dy_ref[...] = dye[HALO:HALO + tm, :]
        dh = _nn(du, wi_s[...].reshape(2 * D, D))
        g, sc, sh = vec_ref[0:1], vec_ref[1:2], vec_ref[2:3]
        h, nrm, r = _norm_fwd(x_ref[...], g, sc, sh)
        h_ref[...] = h.astype(BF16)
        dxi_ref[...] = dxo_m + _norm_bwd(dh, nrm, r, g, sc)
        sums_ref[1:2, :] += _rowsum(dh)
        sums_ref[2:3, :] += _rowsum(dh * nrm)

        @pl.when(i == n - 1)
        def _():
            p = sums_ref[2:3, :]
            sums_ref[3:4, :] = p * (1.0 + sc)
            sums_ref[2:3, :] = p * g

    prev_u, nxt_u = _halo_specs(tm, 2 * D, t)
    prev_d, nxt_d = _halo_specs(tm, D, t)
    return _host_call(
        body, comm, name=f"mix_even_bwd_l{layer}", grid=(n,),
        in_specs=[_tile(tm, D), prev_d, nxt_d, _tile(tm, 2 * D), prev_u, nxt_u, _tile(tm, D), _tile(tm, D),
                  _full((8, D)), _full((3, DA)), _full((4, PG, PG)), _full((1, DA)), ANY],
        out_specs=[_tile(tm, D), _tile(tm, 2 * D), _tile(tm, D), _tile(tm, D), _tile(tm, D),
                   _full((16, D)), _full((4, PG, PG))],
        out_shape=[jax.ShapeDtypeStruct((t, D), F32), jax.ShapeDtypeStruct((t, 2 * D), BF16),
                   jax.ShapeDtypeStruct((t, D), BF16), jax.ShapeDtypeStruct((t, D), BF16),
                   jax.ShapeDtypeStruct((t, D), BF16), jax.ShapeDtypeStruct((16, D), F32),
                   jax.ShapeDtypeStruct((4, PG, PG), F32)],
        scratch_shapes=[pltpu.VMEM((NDEV, ROWS_OUT, D), BF16), pltpu.VMEM((NDEV, ROWS_IN, D), BF16),
                        pltpu.VMEM((e, D), BF16), pltpu.VMEM((e, DA), F32), pltpu.VMEM((e, DA), F32),
                        pltpu.VMEM((e, DA), F32), pltpu.VMEM((e, DA), F32), pltpu.VMEM((e, DA), F32),
                        pltpu.SemaphoreType.DMA((2,))],
        operands=(dxo, dxo, dxo, u, u, u, x, y, vec, ca, wp, ps, gw))


def _fill_glu(ze, part_ref, lo, rows, valid):
    a = part_ref[:, 0:D].astype(F32)
    g = part_ref[:, D:2 * D].astype(F32)
    z = a * _sig(g)
    ze[lo:lo + rows, :] = z if valid is None else jnp.where(valid, z, 0.0)


SHIFT_ROWS = 24


def _shifted_copies(dst, src, tm):
    rows = tm + SHIFT_ROWS
    for j in range(8):
        dst[j, :, :] = src[j:j + rows, :]


def _shifted(dst, shift, tm):
    lo = shift // 8 * 8
    return dst[shift % 8, lo:lo + tm, :]


def _layer_norm_parts(z2):
    mu = _lanemean(z2)
    d = z2 - mu
    rstd = lax.rsqrt(_lanemean(d * d) + LN_EPS)
    return d * rstd, rstd


def _mix_odd_fwd(u, x, vec, wdw, sm, gw, layer, tm):
    t = x.shape[0]
    n = t // tm
    e = tm + 2 * HALO

    def body(u_ref, up_ref, un_ref, x_ref, vec_ref, wdw_ref, sm_ref, g_ref, xo_ref, y_ref, z2_ref, w_s, ze, zsh, sems):
        i = pl.program_id(0)

        @pl.when(i == 0)
        def _():
            _load_weights(g_ref, [(OFF_OUT, ROWS_OUT, w_s)], sems)

        _fill_glu(ze, up_ref, 0, HALO, i > 0)
        _fill_glu(ze, u_ref, HALO, tm, None)
        _fill_glu(ze, un_ref, HALO + tm, HALO, i < n - 1)
        _shifted_copies(zsh, ze, tm)
        z2 = sm_ref[0:1] + wdw_ref[0:1] * _shifted(zsh, 1, tm)
        for k in range(1, CONF_K):
            z2 = z2 + wdw_ref[k:k + 1] * _shifted(zsh, 1 + k, tm)
        z2_ref[...] = z2.astype(BF16)
        zn, _ = _layer_norm_parts(z2)
        lo = zn * sm_ref[1:2] + sm_ref[2:3]
        z3 = lo * _sig(lo)
        y = _nn(z3.astype(BF16), w_s[...].reshape(D, D)) + sm_ref[3:4]
        y_ref[...] = y.astype(BF16)
        xo_ref[...] = x_ref[...] + vec_ref[3:4] * y

    prev, nxt = _halo_specs(tm, 2 * D, t)
    return pl.pallas_call(
        body, name=f"mix_odd_fwd_l{layer}", grid=(n,),
        in_specs=[_tile(tm, 2 * D), prev, nxt, _tile(tm, D), _full((8, D)), _full((32, D)), _full((8, D)), ANY],
        out_specs=[_tile(tm, D), _tile(tm, D), _tile(tm, D)],
        out_shape=[jax.ShapeDtypeStruct((t, D), F32), jax.ShapeDtypeStruct((t, D), BF16),
                   jax.ShapeDtypeStruct((t, D), BF16)],
        scratch_shapes=[pltpu.VMEM((NDEV, ROWS_OUT, D), BF16), pltpu.VMEM((e, D), F32),
                        pltpu.VMEM((8, tm + SHIFT_ROWS, D), F32), pltpu.SemaphoreType.DMA((1,))],
        compiler_params=_cparams("arbitrary"),
    )(u, u, u, x, vec, wdw, sm, gw)


def _mix_odd_bwd1(dxo, y, z2, vec, sm, gw, layer, tm):
    t = dxo.shape[0]
    n = t // tm

    def body(dxo_ref, y_ref, z2_ref, vec_ref, sm_ref, g_ref, dy_ref, z3_ref, dz2_ref, sums_ref, w_s, sems):
        i = pl.program_id(0)

        @pl.when(i == 0)
        def _():
            _load_weights(g_ref, [(OFF_OUT, ROWS_OUT, w_s)], sems)
            sums_ref[...] = jnp.zeros_like(sums_ref)

        dxo_m = dxo_ref[...]
        dy = vec_ref[3:4] * dxo_m
        dyb = dy.astype(BF16)
        dy_ref[...] = dyb
        sums_ref[0:1, :] += _rowsum(dxo_m * y_ref[...].astype(F32))
        sums_ref[4:5, :] += _rowsum(dy)
        dz3 = _nt(dyb, w_s[...].reshape(D, D))
        zn, rstd = _layer_norm_parts(z2_ref[...].astype(F32))
        lo = zn * sm_ref[1:2] + sm_ref[2:3]
        sg = _sig(lo)
        z3_ref[...] = (lo * sg).astype(BF16)
        dlo = dz3 * (sg * (1.0 + lo * (1.0 - sg)))
        sums_ref[5:6, :] += _rowsum(dlo * zn)
        sums_ref[6:7, :] += _rowsum(dlo)
        dzn = dlo * sm_ref[1:2]
        dz2 = rstd * (dzn - _lanemean(dzn) - zn * _lanemean(dzn * zn))
        sums_ref[7:8, :] += _rowsum(dz2)
        dz2_ref[...] = dz2.astype(BF16)

    return pl.pallas_call(
        body, name=f"mix_odd_bwd1_l{layer}", grid=(n,),
        in_specs=[_tile(tm, D), _tile(tm, D), _tile(tm, D), _full((8, D)), _full((8, D)), ANY],
        out_specs=[_tile(tm, D), _tile(tm, D), _tile(tm, D), _full((16, D))],
        out_shape=[jax.ShapeDtypeStruct((t, D), BF16)] * 3 + [jax.ShapeDtypeStruct((16, D), F32)],
        scratch_shapes=[pltpu.VMEM((NDEV, ROWS_OUT, D), BF16), pltpu.SemaphoreType.DMA((1,))],
        compiler_params=_cparams("arbitrary"),
    )(dxo, y, z2, vec, sm, gw)


def _mix_odd_bwd2(dz2, u, x, dxo, vec, wdw, gw, layer, tm, comm=None):
    t = x.shape[0]
    n = t // tm
    e = tm + 2 * HALO

    def body(dz_ref, dzp_ref, dzn_ref, u_ref, x_ref, dxo_ref, vec_ref, wdw_ref, g_ref,
             dxi_ref, du_ref, h_ref, sums_ref, dw_ref, w_s, de, zsh, sems):
        i = pl.program_id(0)

        @pl.when(i == 0)
        def _():
            _load_weights(g_ref, [(OFF_IN, ROWS_IN, w_s)], sems)
            sums_ref[...] = jnp.zeros_like(sums_ref)
            dw_ref[...] = jnp.zeros_like(dw_ref)

        de[0:HALO, :] = jnp.where(i > 0, dzp_ref[...].astype(F32), 0.0)
        de[HALO:HALO + tm, :] = dz_ref[...].astype(F32)
        de[HALO + tm:e, :] = jnp.where(i < n - 1, dzn_ref[...].astype(F32), 0.0)
        a = u_ref[:, 0:D].astype(F32)
        gg = u_ref[:, D:2 * D].astype(F32)
        sg = _sig(gg)
        z = a * sg
        _shifted_copies(zsh, de, tm)
        dz = None
        for k in range(CONF_K):
            shifted = _shifted(zsh, CONF_K - k, tm)
            term = wdw_ref[k:k + 1] * shifted
            dz = term if dz is None else dz + term
            dw_ref[k:k + 1, :] += _rowsum(z * shifted)
        da = dz * sg
        dg = dz * a * (sg * (1.0 - sg))
        sums_ref[8:9, :] += _rowsum(da)
        sums_ref[9:10, :] += _rowsum(dg)
        du = jnp.concatenate([da, dg], axis=-1).astype(BF16)
        du_ref[...] = du
        dh = _nn(du, w_s[...].reshape(2 * D, D))
        g, sc, sh = vec_ref[0:1], vec_ref[1:2], vec_ref[2:3]
        h, nrm, r = _norm_fwd(x_ref[...], g, sc, sh)
        h_ref[...] = h.astype(BF16)
        dxi_ref[...] = dxo_ref[...] + _norm_bwd(dh, nrm, r, g, sc)
        sums_ref[1:2, :] += _rowsum(dh)
        sums_ref[2:3, :] += _rowsum(dh * nrm)

        @pl.when(i == n - 1)
        def _():
            p = sums_ref[2:3, :]
            sums_ref[3:4, :] = p * (1.0 + sc)
            sums_ref[2:3, :] = p * g

    prev_d, nxt_d = _halo_specs(tm, D, t)
    return _host_call(
        body, comm, name=f"mix_odd_bwd2_l{layer}", grid=(n,),
        in_specs=[_tile(tm, D), prev_d, nxt_d, _tile(tm, 2 * D), _tile(tm, D), _tile(tm, D),
                  _full((8, D)), _full((32, D)), ANY],
        out_specs=[_tile(tm, D), _tile(tm, 2 * D), _tile(tm, D), _full((16, D)), _full((32, D))],
        out_shape=[jax.ShapeDtypeStruct((t, D), F32), jax.ShapeDtypeStruct((t, 2 * D), BF16),
                   jax.ShapeDtypeStruct((t, D), BF16), jax.ShapeDtypeStruct((16, D), F32),
                   jax.ShapeDtypeStruct((32, D), F32)],
        scratch_shapes=[pltpu.VMEM((NDEV, ROWS_IN, D), BF16), pltpu.VMEM((e, D), F32),
                        pltpu.VMEM((8, tm + SHIFT_ROWS, D), F32),
                        pltpu.SemaphoreType.DMA((1,))],
        operands=(dz2, dz2, dz2, u, x, dxo, vec, wdw, gw))


FCH = FF // 2


def _ffn_fwd(x, vec, gw, layer, tm, comm=None):
    t = x.shape[0]

    def body(x_ref, vec_ref, g_ref, xo_ref, a_ref, b_ref, y_ref, wg_s, wu_s, wd_s, sems):
        @pl.when(pl.program_id(0) == 0)
        def _():
            _load_weights(g_ref, [(OFF_G, FS, wg_s), (OFF_U, FS, wu_s), (OFF_D, FS, wd_s)], sems)

        xv = x_ref[...]
        h, _, _ = _norm_fwd(xv, vec_ref[0:1], vec_ref[1:2], vec_ref[2:3])
        hb = h.astype(BF16)
        y = jnp.zeros((tm, D), F32)
        for ch in range(2):
            a = _nt(hb, wg_s[4 * ch:4 * ch + 4].reshape(FCH, D))
            b = _nt(hb, wu_s[4 * ch:4 * ch + 4].reshape(FCH, D))
            a_ref[:, ch * FCH:(ch + 1) * FCH] = a.astype(BF16)
            b_ref[:, ch * FCH:(ch + 1) * FCH] = b.astype(BF16)
            s = (a * _sig(a)) * b
            y = y + _nn(s.astype(BF16), wd_s[4 * ch:4 * ch + 4].reshape(FCH, D))
        y_ref[...] = y.astype(BF16)
        xo_ref[...] = xv + vec_ref[3:4] * y

    wsc = pltpu.VMEM((NDEV, FS, D), BF16)
    return _host_call(
        body, comm, name=f"ffn_fwd_l{layer}", grid=(t // tm,),
        in_specs=[_tile(tm, D), _full((8, D)), ANY],
        out_specs=[_tile(tm, D), _tile(tm, FF), _tile(tm, FF), _tile(tm, D)],
        out_shape=[jax.ShapeDtypeStruct((t, D), F32), jax.ShapeDtypeStruct((t, FF), BF16),
                   jax.ShapeDtypeStruct((t, FF), BF16), jax.ShapeDtypeStruct((t, D), BF16)],
        scratch_shapes=[wsc, wsc, wsc, pltpu.SemaphoreType.DMA((3,))],
        operands=(x, vec, gw))


def _ffn_bwd_hidden(dxo, y, a, b, vec, gw, layer, tm, comm=None):
    t = dxo.shape[0]

    def body(dxo_ref, y_ref, a_ref, b_ref, vec_ref, g_ref, dy_ref, s_ref, da_ref, db_ref, sums_ref, wd_s, sems):
        @pl.when(pl.program_id(0) == 0)
        def _():
            _load_weight_rows(g_ref, [(OFF_D, FS, wd_s)], sems)
            sums_ref[...] = jnp.zeros_like(sums_ref)

        dxo_m = dxo_ref[...]
        sums_ref[0:1, :] += _rowsum(dxo_m * y_ref[...].astype(F32))
        dyb = (vec_ref[3:4] * dxo_m).astype(BF16)
        dy_ref[...] = dyb
        for ch in range(FF // MXU_N):
            cols = slice(ch * MXU_N, (ch + 1) * MXU_N)
            ds = _nt(dyb, wd_s[cols, :]).astype(BF16)
            av = a_ref[:, cols]
            bv = b_ref[:, cols]
            sg = _sig(av)
            sl = av * sg
            s_ref[:, cols] = sl * bv
            db_ref[:, cols] = ds * sl
            da_ref[:, cols] = (ds * bv) * (sg * (1.0 + av * (1.0 - sg)))

    return _host_call(
        body, comm, name=f"ffn_bwd_hidden_l{layer}", grid=(t // tm,),
        in_specs=[_tile(tm, D), _tile(tm, D), _tile(tm, FF), _tile(tm, FF), _full((8, D)), ANY],
        out_specs=[_tile(tm, D), _tile(tm, FF), _tile(tm, FF), _tile(tm, FF), _full((8, D))],
        out_shape=[jax.ShapeDtypeStruct((t, D), BF16), jax.ShapeDtypeStruct((t, FF), BF16),
                   jax.ShapeDtypeStruct((t, FF), BF16), jax.ShapeDtypeStruct((t, FF), BF16),
                   jax.ShapeDtypeStruct((8, D), F32)],
        scratch_shapes=[pltpu.VMEM((FF, D), BF16), pltpu.SemaphoreType.DMA((NDEV,))],
        operands=(dxo, y, a, b, vec, gw))


def _ffn_bwd_input(da, db, x, dxo, vec, gw, layer, tm, comm=None):
    t = x.shape[0]
    n = t // tm

    def body(da_ref, db_ref, x_ref, dxo_ref, vec_ref, g_ref, dxi_ref, h_ref, sums_ref, wg_s, wu_s, sems):
        i = pl.program_id(0)

        @pl.when(i == 0)
        def _():
            _load_weights(g_ref, [(OFF_G, FS, wg_s), (OFF_U, FS, wu_s)], sems)
            sums_ref[...] = jnp.zeros_like(sums_ref)

        dh = _nn(da_ref[...], wg_s[...].reshape(FF, D)) + _nn(db_ref[...], wu_s[...].reshape(FF, D))
        g, sc, sh = vec_ref[0:1], vec_ref[1:2], vec_ref[2:3]
        h, nrm, r = _norm_fwd(x_ref[...], g, sc, sh)
        h_ref[...] = h.astype(BF16)
        dxi_ref[...] = dxo_ref[...] + _norm_bwd(dh, nrm, r, g, sc)
        sums_ref[1:2, :] += _rowsum(dh)
        sums_ref[2:3, :] += _rowsum(dh * nrm)

        @pl.when(i == n - 1)
        def _():
            p = sums_ref[2:3, :]
            sums_ref[3:4, :] = p * (1.0 + sc)
            sums_ref[2:3, :] = p * g

    wsc = pltpu.VMEM((NDEV, FS, D), BF16)
    return _host_call(
        body, comm, name=f"ffn_bwd_input_l{layer}", grid=(n,),
        in_specs=[_tile(tm, FF), _tile(tm, FF), _tile(tm, D), _tile(tm, D), _full((8, D)), ANY],
        out_specs=[_tile(tm, D), _tile(tm, D), _full((8, D))],
        out_shape=[jax.ShapeDtypeStruct((t, D), F32), jax.ShapeDtypeStruct((t, D), BF16),
                   jax.ShapeDtypeStruct((8, D), F32)],
        scratch_shapes=[wsc, wsc, pltpu.SemaphoreType.DMA((2,))],
        operands=(da, db, x, dxo, vec, gw))


def _wgrad(lhs, rhs, name, tk):
    t, m = lhs.shape
    n = t // tk

    def body(l_ref, r_ref, o_ref, acc):
        i = pl.program_id(0)

        @pl.when(i == 0)
        def _():
            acc[...] = jnp.zeros_like(acc)

        acc[...] += _tn(l_ref[...], r_ref[...])

        @pl.when(i == n - 1)
        def _():
            o_ref[...] = acc[...].astype(BF16)

    return pl.pallas_call(
        body, name=name, grid=(n,),
        in_specs=[_tile(tk, m), _tile(tk, D)], out_specs=_full((m, D)),
        out_shape=jax.ShapeDtypeStruct((m, D), BF16),
        scratch_shapes=[pltpu.VMEM((m, D), F32)],
        compiler_params=_cparams("arbitrary"),
    )(lhs, rhs)


def _final(x, tgt, gf, tm):
    t = x.shape[0]

    def body(x_ref, t_ref, g_ref, dx_ref, sums_ref):
        @pl.when(pl.program_id(0) == 0)
        def _():
            sums_ref[...] = jnp.zeros_like(sums_ref)

        xv = x_ref[...]
        r = lax.rsqrt(_lanemean(xv * xv) + RMS_EPS)
        nrm = xv * r
        g = g_ref[...]
        err = nrm * g - t_ref[...]
        sums_ref[1:2, :] += _rowsum(err * err) * (0.5 / D)
        dout = err * (1.0 / D)
        sums_ref[0:1, :] += _rowsum(dout * nrm)
        dn = dout * g
        dx_ref[...] = r * (dn - nrm * _lanemean(dn * nrm))

    return pl.pallas_call(
        body, name="loss_head", grid=(t // tm,),
        in_specs=[_tile(tm, D), _tile(tm, D), _full((1, D))],
        out_specs=[_tile(tm, D), _full((8, D))],
        out_shape=[jax.ShapeDtypeStruct((t, D), F32), jax.ShapeDtypeStruct((8, D), F32)],
        compiler_params=_cparams("arbitrary"),
    )(x, tgt, gf)


ADAM_ROWS = LROWS // 5


def _adam_big(recv, w, m, v):
    def body(r_ref, w_ref, m_ref, v_ref, g_ref, d_ref, mo_ref, vo_ref):
        g = r_ref[0, 0].astype(F32)
        for s in range(1, NDEV):
            g = g + r_ref[s, 0].astype(F32)
        delta, m2, v2 = _adam(w_ref[0], g, m_ref[0], v_ref[0])
        g_ref[0], d_ref[0], mo_ref[0], vo_ref[0] = g, delta, m2, v2

    blk = pl.BlockSpec((1, ADAM_ROWS, D), lambda l, j: (l, j, 0))
    sds = jax.ShapeDtypeStruct(w.shape, F32)
    return pl.pallas_call(
        body, name="adam_big", grid=(DEPTH, LROWS // ADAM_ROWS),
        in_specs=[pl.BlockSpec((NDEV, 1, ADAM_ROWS, D), lambda l, j: (0, l, j, 0)), blk, blk, blk],
        out_specs=[blk] * 4, out_shape=[sds] * 4,
        compiler_params=_cparams("arbitrary", "arbitrary"),
    )(recv, w, m, v)


def _sum_small(gathered, rows):
    def body(g_ref, o_ref):
        acc = g_ref[0:rows, :]
        for s in range(1, NDEV):
            acc = acc + g_ref[s * rows:(s + 1) * rows, :]
        o_ref[...] = acc

    return pl.pallas_call(
        body, name="sum_small",
        in_specs=[pl.BlockSpec(memory_space=pltpu.VMEM)], out_specs=pl.BlockSpec(memory_space=pltpu.VMEM),
        out_shape=jax.ShapeDtypeStruct((rows, D), F32),
        compiler_params=pltpu.CompilerParams(vmem_limit_bytes=VMEM_LIMIT),
    )(gathered)


def _adam_small(params):
    k = len(params)

    def body(*refs):
        ins, outs = refs[:4 * k], refs[4 * k:]
        for j in range(k):
            w_ref, g_ref, m_ref, v_ref = ins[4 * j:4 * j + 4]
            delta, m2, v2 = _adam(w_ref[...], g_ref[...], m_ref[...], v_ref[...])
            outs[3 * j][...], outs[3 * j + 1][...], outs[3 * j + 2][...] = delta, m2, v2

    flat = [a for p in params for a in p]
    shapes = [jax.ShapeDtypeStruct(p[0].shape, F32) for p in params for _ in range(3)]
    vm = pl.BlockSpec(memory_space=pltpu.VMEM)
    res = pl.pallas_call(
        body, name="adam_small", in_specs=[vm] * len(flat), out_specs=[vm] * len(shapes), out_shape=shapes,
        compiler_params=pltpu.CompilerParams(vmem_limit_bytes=VMEM_LIMIT),
    )(*flat)
    return [tuple(res[3 * j:3 * j + 3]) for j in range(k)]


def _pack(ab_in, ab_out, pw1, pw2, wg, wu, wd):
    ins = jnp.swapaxes(jnp.stack([ab_in[0], pw1[0], ab_in[1], pw1[1]]), 1, 2)
    outs = jnp.stack([ab_out[0], pw2[0], ab_out[1], pw2[1]])
    return jnp.concatenate([ins, outs, jnp.swapaxes(wg, 1, 2), jnp.swapaxes(wu, 1, 2), wd], axis=1)


def _unpack(p):
    ins = jnp.swapaxes(p[:, OFF_IN:OFF_OUT], 1, 2)
    outs = p[:, OFF_OUT:OFF_G]
    return (ins[0::2], outs[0::2], ins[1::2], outs[1::2], jnp.swapaxes(p[:, OFF_G:OFF_U], 1, 2),
            jnp.swapaxes(p[:, OFF_U:OFF_D], 1, 2), p[:, OFF_D:LROWS])


def _unshard(flat, lead, per):
    k = len(lead)
    a = flat.reshape((NDEV,) + tuple(lead) + (per,))
    a = jnp.transpose(a, tuple(range(1, k + 1)) + (0, k + 1))
    return a.reshape(tuple(lead) + (NDEV * per,))


def _rows_of(a):
    f = a.reshape(-1)
    pad = (-f.shape[0]) % D
    if pad:
        f = jnp.concatenate([f, jnp.zeros((pad,), f.dtype)])
    return f.reshape(-1, D)


def _pad_rows(a, rows):
    return jnp.concatenate([a, jnp.zeros((rows - a.shape[0],) + a.shape[1:], a.dtype)], axis=0)


def kernel(x, c, norm_mix_g, norm_ffn_g, w_mod, b_mod, ab_w_in, ab_conv, ab_w_pool, ab_pool_scale, ab_w_out, cf_w_pw1, cf_b_pw1, cf_w_dw, cf_b_dw, cf_ln_g, cf_ln_b, cf_w_pw2, cf_b_pw2, ffn_w_gate, ffn_w_up, ffn_w_down, final_norm_g, loss_target, m_norm_mix_g, m_norm_ffn_g, m_w_mod, m_b_mod, m_ab_w_in, m_ab_conv, m_ab_w_pool, m_ab_pool_scale, m_ab_w_out, m_cf_w_pw1, m_cf_b_pw1, m_cf_w_dw, m_cf_b_dw, m_cf_ln_g, m_cf_ln_b, m_cf_w_pw2, m_cf_b_pw2, m_ffn_w_gate, m_ffn_w_up, m_ffn_w_down, m_final_norm_g, v_norm_mix_g, v_norm_ffn_g, v_w_mod, v_b_mod, v_ab_w_in, v_ab_conv, v_ab_w_pool, v_ab_pool_scale, v_ab_w_out, v_cf_w_pw1, v_cf_b_pw1, v_cf_w_dw, v_cf_b_dw, v_cf_ln_g, v_cf_ln_b, v_cf_w_pw2, v_cf_b_pw2, v_ffn_w_gate, v_ffn_w_up, v_ffn_w_down, v_final_norm_g):
    t = x.shape[1]
    tm = 512 if t % 512 == 0 else t // 2
    tk = 512 if t % 512 == 0 else t // 2
    tmo = tm // 2
    me = 4 * lax.axis_index("x") + 2 * lax.axis_index("y") + lax.axis_index("c")
    xs, tgt = x[0], loss_target[0]

    sharded = [ab_conv, cf_b_pw1, cf_w_dw, cf_b_dw, cf_ln_g, cf_ln_b, cf_b_pw2]
    flat = jnp.concatenate([a.reshape(-1) for a in sharded])
    n_flat = flat.shape[0]
    g1 = _gather_small(jnp.concatenate([_pad_rows(c, 8), _pad_rows(_rows_of(flat), 16)], axis=0), "gather_cond")
    g1 = g1.reshape(NDEV, 24, D)
    c_all = g1[:, 0, :]
    flat_all = g1[:, 8:, :].reshape(NDEV, -1)[:, :n_flat]
    full, o = [], 0
    for a in sharded:
        lead, per = a.shape[:-1], a.shape[-1]
        size = a.size
        full.append(_unshard(flat_all[:, o:o + size], lead, per))
        o += size
    ab_conv_f, b_pw1_f, w_dw_f, b_dw_f, ln_g_f, ln_b_f, b_pw2_f = full

    b_sl = lax.dynamic_slice_in_dim(b_mod, me * MODW, MODW, axis=1).reshape(DEPTH, 1, MODW)
    mod_part, c_act = _mod_fwd(c_all, w_mod, b_sl)
    g2 = _gather_small(mod_part, "gather_mod").reshape(NDEV, NDEV, DEPTH, MODW)
    mod = jnp.transpose(lax.dynamic_index_in_dim(g2, me, axis=1, keepdims=False), (1, 0, 2)).reshape(DEPTH, N_MOD, D)
    zeros4 = jnp.zeros((4, D), F32)

    def vec_of(g, layer, k):
        return jnp.concatenate([g[layer][None], mod[layer, k + 1][None], mod[layer, k][None],
                                mod[layer, k + 2][None], zeros4], axis=0)

    vmix = [vec_of(norm_mix_g, l, 0) for l in range(DEPTH)]
    vffn = [vec_of(norm_ffn_g, l, 3) for l in range(DEPTH)]

    w_pack = _pack(ab_w_in, ab_w_out, cf_w_pw1, cf_w_pw2, ffn_w_gate, ffn_w_up, ffn_w_down)
    p16 = w_pack.astype(BF16)
    gw = [_gather_weights(p16[0])]

    wp16 = ab_w_pool.astype(BF16)
    wdw32 = [_pad_rows(w_dw_f[i], 32) for i in range(2)]
    sm_odd = [jnp.concatenate([b_dw_f[i][None], ln_g_f[i][None], ln_b_f[i][None], b_pw2_f[i][None], zeros4], axis=0)
              for i in range(2)]
    zero_bias = jnp.zeros((1, 2 * D), F32)

    saved = []
    xc = xs
    for l in range(DEPTH):
        i = l // 2
        if l % 2 == 0:
            u = _inproj(xc, vmix[l], zero_bias, gw[l], l, tm)
            x_mid, y_mix = _mix_even_fwd(u, xc, vmix[l], ab_conv_f[i], wp16[i], ab_pool_scale[i][None], gw[l], l, tm)
            z2 = None
        else:
            u = _inproj(xc, vmix[l], b_pw1_f[i][None], gw[l], l, tm)
            x_mid, y_mix, z2 = _mix_odd_fwd(u, xc, vmix[l], wdw32[i], sm_odd[i], gw[l], l, tmo)
        if l + 1 < DEPTH:
            x_out, a, b, y_ffn, g_next = _ffn_fwd(x_mid, vffn[l], gw[l], l, tm, _gather_comm(p16[l + 1]))
            gw.append(g_next)
        else:
            x_out, a, b, y_ffn = _ffn_fwd(x_mid, vffn[l], gw[l], l, tm)
        saved.append((xc, u, y_mix, z2, x_mid, a, b, y_ffn))
        xc = x_out

    dx, fsum = _final(xc, tgt, final_norm_g[None], tm)
    loss = lax.psum(jnp.sum(fsum[1]), ("x", "y", "c"))
    d_final_g = fsum[0]

    recv = _empty_recv()
    io_specs = [(OFF_IN, ROWS_IN), (OFF_OUT, ROWS_OUT)]
    pending = None
    dmod = [None] * DEPTH
    d_mix_g, d_ffn_g = [None] * DEPTH, [None] * DEPTH
    d_conv, d_pool, d_pscale = [None] * 2, [None] * 2, [None] * 2
    d_bpw1, d_wdw, d_bdw, d_lng, d_lnb, d_bpw2 = ([None] * 2 for _ in range(6))
    for l in reversed(range(DEPTH)):
        i = l // 2
        x_in, u, y_mix, z2, x_mid, a, b, y_ffn = saved[l]
        if pending is None:
            dy, s, da, db, s_h = _ffn_bwd_hidden(dx, y_ffn, a, b, vffn[l], gw[l], l, tm)
        else:
            dy, s, da, db, s_h, recv = _ffn_bwd_hidden(dx, y_ffn, a, b, vffn[l], gw[l], l, tm,
                                                       _scatter_comm(pending, io_specs, recv, l + 1))
        g_down = _wgrad(s, dy, f"wgrad_down_l{l}", tk)
        dx_mid, h2, s_i, recv = _ffn_bwd_input(da, db, x_mid, dx, vffn[l], gw[l], l, tm,
                                               _scatter_comm([g_down], [(OFF_D, FS)], recv, l))
        g_gate = _wgrad(da, h2, f"wgrad_gate_l{l}", tk)
        g_up = _wgrad(db, h2, f"wgrad_up_l{l}", tk)
        gu_comm = _scatter_comm([g_gate, g_up], [(OFF_G, FS), (OFF_U, FS)], recv, l)
        d_ffn_g[l] = s_i[3]
        mod_ffn = [s_i[1], s_i[2], s_h[0]]
        if l % 2 == 0:
            dx, du, h, cat, dym, s_m, dwp, recv = _mix_even_bwd(dx_mid, u, x_in, y_mix, vmix[l], ab_conv_f[i], wp16[i],
                                                                ab_pool_scale[i][None], gw[l], l, tm, gu_comm)
            g_out = _wgrad(cat, dym, f"wgrad_out_l{l}", tk)
            d_conv[i], d_pool[i], d_pscale[i] = s_m[4:7, :DA], dwp, s_m[7, :DA]
            mod_mix = [s_m[1], s_m[2], s_m[0]]
            d_mix_g[l] = s_m[3]
        else:
            dym, z3, dz2, s_1 = _mix_odd_bwd1(dx_mid, y_mix, z2, vmix[l], sm_odd[i], gw[l], l, tm)
            dx, du, h, s_2, dwdw, recv = _mix_odd_bwd2(dz2, u, x_in, dx_mid, vmix[l], wdw32[i], gw[l], l, tmo, gu_comm)
            g_out = _wgrad(z3, dym, f"wgrad_out_l{l}", tk)
            d_bpw1[i], d_wdw[i], d_bdw[i] = s_2[8:10].reshape(2 * D), dwdw[:CONF_K], s_1[7]
            d_lng[i], d_lnb[i], d_bpw2[i] = s_1[5], s_1[6], s_1[4]
            mod_mix = [s_2[1], s_2[2], s_1[0]]
            d_mix_g[l] = s_2[3]
        dmod[l] = jnp.stack(mod_mix + mod_ffn)
        pending = [_wgrad(du, h, f"wgrad_in_l{l}", tk), g_out]
    grad_x = dx[None]
    recv = _scatter_grads(pending, io_specs, recv, 0)

    m_pack = _pack(m_ab_w_in, m_ab_w_out, m_cf_w_pw1, m_cf_w_pw2, m_ffn_w_gate, m_ffn_w_up, m_ffn_w_down)
    v_pack = _pack(v_ab_w_in, v_ab_w_out, v_cf_w_pw1, v_cf_w_pw2, v_ffn_w_gate, v_ffn_w_up, v_ffn_w_down)
    big = [_unpack(p) for p in _adam_big(recv, w_pack, m_pack, v_pack)]

    small = [jnp.stack(dmod), jnp.stack(d_mix_g), jnp.stack(d_ffn_g), d_final_g, jnp.stack(d_pool),
             jnp.stack(d_pscale), jnp.stack(d_conv), jnp.stack(d_bpw1), jnp.stack(d_wdw), jnp.stack(d_bdw),
             jnp.stack(d_lng), jnp.stack(d_lnb), jnp.stack(d_bpw2)]
    small_rows = [_rows_of(a) for a in small]
    n_rows = sum(a.shape[0] for a in small_rows)
    pad_rows = -(-n_rows // 8) * 8
    g3 = _gather_small(_pad_rows(jnp.concatenate(small_rows, axis=0), pad_rows), "gather_small_grads")
    summed = _sum_small(g3, pad_rows)
    outs, o = [], 0
    for a, r in zip(small, small_rows):
        outs.append(summed[o:o + r.shape[0]].reshape(-1)[:a.size].reshape(a.shape))
        o += r.shape[0]
    (g_bmod, g_mix_g, g_ffn_g, g_final, g_pool, g_pscale, g_conv, g_bpw1, g_wdw, g_bdw, g_lng, g_lnb, g_bpw2) = outs
    g_bmod = g_bmod.reshape(DEPTH, N_MOD * D)

    def my_shard(a):
        per = a.shape[-1] // NDEV
        return lax.dynamic_slice_in_dim(a, me * per, per, axis=a.ndim - 1)

    g_conv, g_bpw1, g_wdw, g_bdw, g_lng, g_lnb, g_bpw2 = [
        my_shard(a) for a in (g_conv, g_bpw1, g_wdw, g_bdw, g_lng, g_lnb, g_bpw2)]

    dmod_all = g3.reshape(NDEV, pad_rows, D)[:, :DEPTH * N_MOD, :].reshape(NDEV, DEPTH, N_MOD * D)
    dmod_mine = jnp.transpose(lax.dynamic_slice_in_dim(dmod_all, me * MODW, MODW, axis=2), (1, 0, 2))
    g_wmod, d_wmod, nm_wmod, nv_wmod = _mod_bwd_adam(c_act.T, dmod_mine, w_mod, m_w_mod, v_w_mod)

    small_params = [
        (norm_mix_g, g_mix_g, m_norm_mix_g, v_norm_mix_g), (norm_ffn_g, g_ffn_g, m_norm_ffn_g, v_norm_ffn_g),
        (b_mod, g_bmod, m_b_mod, v_b_mod), (ab_conv, g_conv, m_ab_conv, v_ab_conv),
        (ab_w_pool, g_pool, m_ab_w_pool, v_ab_w_pool), (ab_pool_scale, g_pscale, m_ab_pool_scale, v_ab_pool_scale),
        (cf_b_pw1, g_bpw1, m_cf_b_pw1, v_cf_b_pw1), (cf_w_dw, g_wdw, m_cf_w_dw, v_cf_w_dw),
        (cf_b_dw, g_bdw, m_cf_b_dw, v_cf_b_dw), (cf_ln_g, g_lng, m_cf_ln_g, v_cf_ln_g),
        (cf_ln_b, g_lnb, m_cf_ln_b, v_cf_ln_b), (cf_b_pw2, g_bpw2, m_cf_b_pw2, v_cf_b_pw2),
        (final_norm_g, g_final, m_final_norm_g, v_final_norm_g)]

    def two_d(a):
        return a.reshape(-1, a.shape[-1])

    upd = _adam_small([tuple(two_d(a) for a in p) for p in small_params])
    upd = [tuple(r.reshape(p[0].shape) for r in u) for u, p in zip(upd, small_params)]
    (s_mix, s_ffn, s_bmod, s_conv, s_pool, s_pscale, s_bpw1, s_wdw, s_bdw, s_lng, s_lnb, s_bpw2, s_final) = upd
    small_g = [p[1] for p in small_params]
    (q_mix, q_ffn, q_bmod, q_conv, q_pool, q_pscale, q_bpw1, q_wdw, q_bdw, q_lng, q_lnb, q_bpw2, q_final) = small_g

    def ordered(k):
        ab_in, ab_out, pw1, pw2, wg, wu, wd = big[k]
        if k == 0:
            sm = dict(mix=q_mix, ffn=q_ffn, bmod=q_bmod, conv=q_conv, pool=q_pool, pscale=q_pscale, bpw1=q_bpw1,
                      wdw=q_wdw, bdw=q_bdw, lng=q_lng, lnb=q_lnb, bpw2=q_bpw2, final=q_final)
            wmod = g_wmod
        else:
            j = k - 1
            sm = dict(mix=s_mix[j], ffn=s_ffn[j], bmod=s_bmod[j], conv=s_conv[j], pool=s_pool[j], pscale=s_pscale[j],
                      bpw1=s_bpw1[j], wdw=s_wdw[j], bdw=s_bdw[j], lng=s_lng[j], lnb=s_lnb[j], bpw2=s_bpw2[j],
                      final=s_final[j])
            wmod = (d_wmod, nm_wmod, nv_wmod)[j]
        return [sm["mix"], sm["ffn"], wmod, sm["bmod"], ab_in, sm["conv"], sm["pool"], sm["pscale"], ab_out,
                pw1, sm["bpw1"], sm["wdw"], sm["bdw"], sm["lng"], sm["lnb"], pw2, sm["bpw2"], wg, wu, wd, sm["final"]]

    return (loss, grad_x, *ordered(0), *ordered(1), *ordered(2), *ordered(3))
```

```python
import functools

import jax
import jax.numpy as jnp
from jax import lax
from jax.experimental import pallas as pl
from jax.experimental.pallas import tpu as pltpu

F32 = jnp.float32
BF16 = jnp.bfloat16
MESH = pl.DeviceIdType.MESH

NDEV = 8
DEPTH = 4
D = 1024
FF = 2816
FS = FF // NDEV
DA = 512
PG = 128
POOL = ((2, 1, 0), (4, 2, 1), (8, 4, 3), (16, 8, 7))
CONF_K = 31
CONF_L = 15
N_MOD = 6
MODW = N_MOD * D // NDEV
RMS_EPS = 1e-6
LN_EPS = 1e-5

ROWS_IN, ROWS_OUT = 2 * D // NDEV, D // NDEV
OFF_IN, OFF_OUT = 0, ROWS_IN
OFF_G = OFF_OUT + ROWS_OUT
OFF_U = OFF_G + FS
OFF_D = OFF_U + FS
LROWS = OFF_D + FS

MXU_N = 256
HALO = 16
VMEM_LIMIT = 60 * 1024 * 1024

ADAM_LR, ADAM_B1, ADAM_B2, ADAM_EPS, ADAM_WD, ADAM_STEP = 1e-3, 0.9, 0.999, 1e-8, 0.01, 10
ADAM_C1 = 1.0 / (1.0 - ADAM_B1 ** ADAM_STEP)
ADAM_C2 = 1.0 / (1.0 - ADAM_B2 ** ADAM_STEP)


def _nn(a, b):
    return jnp.dot(a, b, preferred_element_type=F32)


def _nt(a, b):
    return lax.dot_general(a, b, (((1,), (1,)), ((), ())), preferred_element_type=F32)


def _tn(a, b):
    return lax.dot_general(a, b, (((0,), (0,)), ((), ())), preferred_element_type=F32)


def _sig(x):
    return 1.0 / (1.0 + jnp.exp(-x))


def _rowsum(x):
    return jnp.sum(x, axis=0, keepdims=True)


def _lanemean(x):
    return jnp.mean(x, axis=-1, keepdims=True)


def _norm_fwd(x, g, sc, sh):
    r = lax.rsqrt(_lanemean(x * x) + RMS_EPS)
    n = x * r
    return n * (g * (1.0 + sc)) + sh, n, r


def _norm_bwd(dh, n, r, g, sc):
    dn = dh * (g * (1.0 + sc))
    return r * (dn - n * _lanemean(dn * n))


def _adam(w, g, m, v):
    m2 = ADAM_B1 * m + (1.0 - ADAM_B1) * g
    v2 = ADAM_B2 * v + (1.0 - ADAM_B2) * (g * g)
    delta = -ADAM_LR * ((m2 * ADAM_C1) / (jnp.sqrt(v2 * ADAM_C2) + ADAM_EPS) + ADAM_WD * w)
    return delta, m2, v2


def _load_weights(g_ref, specs, sems):
    cps = [pltpu.make_async_copy(g_ref.at[:, pl.ds(off, rows), :], dst, sems.at[k])
           for k, (off, rows, dst) in enumerate(specs)]
    for cp in cps:
        cp.start()
    for cp in cps:
        cp.wait()


def _load_weight_rows(g_ref, specs, sems):
    cps = [pltpu.make_async_copy(g_ref.at[d, pl.ds(off, rows), :], dst.at[pl.ds(d * rows, rows), :],
                                 sems.at[NDEV * k + d])
           for k, (off, rows, dst) in enumerate(specs) for d in range(NDEV)]
    for cp in cps:
        cp.start()
    for cp in cps:
        cp.wait()


def _cparams(*sem):
    return pltpu.CompilerParams(dimension_semantics=sem if sem else None, vmem_limit_bytes=VMEM_LIMIT)


def _tile(tm, w):
    return pl.BlockSpec((tm, w), lambda i: (i, 0))


def _full(shape):
    nd = len(shape)
    return pl.BlockSpec(shape, lambda i: (0,) * nd)


def _halo_specs(tm, w, total_rows):
    tb = tm // HALO
    nb = total_rows // HALO
    prev = pl.BlockSpec((HALO, w), lambda i: (jnp.maximum(i * tb - 1, 0), 0))
    nxt = pl.BlockSpec((HALO, w), lambda i: (jnp.minimum((i + 1) * tb, nb - 1), 0))
    return prev, nxt


ANY = pl.BlockSpec(memory_space=pl.ANY)


def _peers():
    x, y, c = lax.axis_index("x"), lax.axis_index("y"), lax.axis_index("c")
    return x, y, c


def _gather_small(v, name):
    m_per, n = v.shape

    def body(x_ref, out_ref, send_sems, recv_sems, local_sem):
        x, y, c = _peers()
        me, sibling = (x, y, c), (x, y, 1 - c)
        chips = [(1 - x, y), (x, 1 - y), (1 - x, 1 - y)]

        def rows(px, py, pc):
            return out_ref.at[pl.ds((4 * px + 2 * py + pc) * m_per, m_per), :]

        def copy(k, block, to, src=None):
            return pltpu.make_async_remote_copy(
                src_ref=rows(*block) if src is None else src, dst_ref=rows(*block),
                send_sem=send_sems.at[k], recv_sem=recv_sems.at[k], device_id=to, device_id_type=MESH)

        mine = pltpu.make_async_copy(x_ref, rows(*me), local_sem)
        mine.start()
        first = [copy(0, me, sibling, src=x_ref)]
        first += [copy(1 + j, me, (*chip, c), src=x_ref) for j, chip in enumerate(chips)]
        for cp in first:
            cp.start()
        passed = [copy(4 + j, (*chip, c), sibling) for j, chip in enumerate(chips)]
        for j, chip in enumerate(chips):
            copy(1 + j, (*chip, c), me).wait_recv()
            passed[j].start()
        copy(0, sibling, me).wait_recv()
        for j, chip in enumerate(chips):
            copy(4 + j, (*chip, 1 - c), me).wait_recv()
        for cp in first + passed:
            cp.wait_send()
        mine.wait()

    return pl.pallas_call(
        body, name=name,
        out_shape=jax.ShapeDtypeStruct((NDEV * m_per, n), v.dtype),
        in_specs=[pl.BlockSpec(memory_space=pltpu.VMEM)],
        out_specs=pl.BlockSpec(memory_space=pltpu.VMEM),
        scratch_shapes=[pltpu.SemaphoreType.DMA((7,)), pltpu.SemaphoreType.DMA((7,)), pltpu.SemaphoreType.DMA],
        compiler_params=pltpu.CompilerParams(vmem_limit_bytes=VMEM_LIMIT),
    )(v)


class _Comm:
    def __init__(self, ins, outs, aliases, bind):
        self.ins, self.outs, self.aliases, self.bind = ins, outs, aliases, bind


COMM_SEMS = [pltpu.SemaphoreType.DMA((7,)), pltpu.SemaphoreType.DMA((7,)), pltpu.SemaphoreType.DMA]


def _gather_hooks(p_ref, out_ref, send_sems, recv_sems, local_sem):
    def parts():
        x, y, c = _peers()
        me, sibling = (x, y, c), (x, y, 1 - c)
        chips = [(1 - x, y), (x, 1 - y), (1 - x, 1 - y)]

        def slab(px, py, pc):
            return out_ref.at[4 * px + 2 * py + pc]

        def copy(k, block, to, src=None):
            return pltpu.make_async_remote_copy(
                src_ref=slab(*block) if src is None else src, dst_ref=slab(*block),
                send_sem=send_sems.at[k], recv_sem=recv_sems.at[k], device_id=to, device_id_type=MESH)

        def mine():
            return pltpu.make_async_copy(p_ref, slab(*me), local_sem)

        def first():
            return [copy(0, me, sibling, src=p_ref)] + [copy(1 + j, me, (*chip, c), src=p_ref)
                                                        for j, chip in enumerate(chips)]

        def passed():
            return [copy(4 + j, (*chip, c), sibling) for j, chip in enumerate(chips)]

        def from_chips():
            return [copy(1 + j, (*chip, c), me) for j, chip in enumerate(chips)]

        def from_sibling():
            return [copy(0, sibling, me)] + [copy(4 + j, (*chip, 1 - c), me) for j, chip in enumerate(chips)]

        return mine, first, passed, from_chips, from_sibling

    def start():
        mine, first, _, _, _ = parts()
        mine().start()
        for cp in first():
            cp.start()

    def middle():
        _, _, passed, from_chips, _ = parts()
        for arrived, onward in zip(from_chips(), passed()):
            arrived.wait_recv()
            onward.start()

    def end():
        mine, first, passed, _, from_sibling = parts()
        for cp in from_sibling():
            cp.wait_recv()
        for cp in first() + passed():
            cp.wait_send()
        mine().wait()

    return start, middle, end


def _gather_comm(p):
    return _Comm([p], [jax.ShapeDtypeStruct((NDEV,) + p.shape, p.dtype)], {},
                 lambda cins, couts, sems: _gather_hooks(cins[0], couts[0], *sems))


def _gather_weights(p):
    def body(p_ref, out_ref, send_sems, recv_sems, local_sem):
        for hook in _gather_hooks(p_ref, out_ref, send_sems, recv_sems, local_sem):
            hook()

    return pl.pallas_call(
        body, name="gather_weights",
        out_shape=jax.ShapeDtypeStruct((NDEV,) + p.shape, p.dtype),
        in_specs=[ANY], out_specs=ANY, scratch_shapes=COMM_SEMS,
    )(p)


def _scatter_hooks(src_refs, specs, r_ref, layer, send_sems, recv_sems, local_sem):
    total = sum(rows for _, rows in specs)

    def start():
        x, y, c = _peers()
        me = 4 * x + 2 * y + c

        def part(k, dev):
            off, rows = specs[k]
            src = src_refs[k].at[pl.ds(pl.multiple_of(dev * rows, 16), rows), :]
            return src, r_ref.at[me, layer, pl.ds(off, rows), :]

        for k in range(len(specs)):
            src, dst = part(k, me)
            pltpu.make_async_copy(src, dst, local_sem).start()
        for r in range(1, NDEV):
            px = 1 - x if r & 4 else x
            py = 1 - y if r & 2 else y
            pc = 1 - c if r & 1 else c
            for k in range(len(specs)):
                src, dst = part(k, 4 * px + 2 * py + pc)
                pltpu.make_async_remote_copy(
                    src_ref=src, dst_ref=dst, send_sem=send_sems.at[r - 1], recv_sem=recv_sems.at[r - 1],
                    device_id=(px, py, pc), device_id_type=MESH).start()

    def end():
        x, y, c = _peers()
        whole = r_ref.at[0, layer, pl.ds(0, total), :]
        for r in range(1, NDEV):
            done = pltpu.make_async_remote_copy(
                src_ref=whole, dst_ref=whole, send_sem=send_sems.at[r - 1], recv_sem=recv_sems.at[r - 1],
                device_id=(x, y, c), device_id_type=MESH)
            done.wait_recv()
            done.wait_send()
        pltpu.make_async_copy(whole, whole, local_sem).wait()

    return start, None, end


def _scatter_comm(srcs, specs, recv, layer):
    k = len(srcs)
    return _Comm(list(srcs) + [recv], [jax.ShapeDtypeStruct(recv.shape, recv.dtype)], {k: 0},
                 lambda cins, couts, sems: _scatter_hooks(cins[:k], specs, couts[0], layer, *sems))


def _scatter_grads(srcs, specs, recv, layer):
    k = len(srcs)

    def body(*refs):
        start, _, end = _scatter_hooks(refs[:k], specs, refs[k + 1], layer, *refs[k + 2:])
        start()
        end()

    return pl.pallas_call(
        body, name=f"scatter_grads_l{layer}",
        out_shape=jax.ShapeDtypeStruct(recv.shape, recv.dtype),
        in_specs=[ANY] * (k + 1), out_specs=ANY, scratch_shapes=COMM_SEMS,
        input_output_aliases={k: 0},
    )(*srcs, recv)


def _empty_recv():
    def body(o_ref):
        del o_ref

    return pl.pallas_call(body, name="recv_buffer", out_specs=ANY,
                          out_shape=jax.ShapeDtypeStruct((NDEV, DEPTH, LROWS, D), BF16))()


def _host_call(inner, comm, *, name, grid, in_specs, out_specs, out_shape, scratch_shapes, operands):
    if comm is None:
        return pl.pallas_call(
            inner, name=name, grid=grid, in_specs=in_specs, out_specs=out_specs, out_shape=out_shape,
            scratch_shapes=scratch_shapes, compiler_params=_cparams("arbitrary"))(*operands)
    n_in, n_out, n_s = len(in_specs), len(out_specs), len(scratch_shapes)
    k_in, k_out = len(comm.ins), len(comm.outs)
    steps = grid[0]

    def body(*refs):
        ins, cins = refs[:n_in], refs[n_in:n_in + k_in]
        o0 = n_in + k_in
        outs, couts = refs[o0:o0 + n_out], refs[o0 + n_out:o0 + n_out + k_out]
        s0 = o0 + n_out + k_out
        scr, sems = refs[s0:s0 + n_s], refs[s0 + n_s:]
        start, middle, end = comm.bind(cins, couts, sems)
        i = pl.program_id(0)
        pl.when(i == 0)(start)
        if middle is not None:
            pl.when(i == steps * 3 // 4)(middle)
        inner(*ins, *outs, *scr)
        pl.when(i == steps - 1)(end)

    return pl.pallas_call(
        body, name=name, grid=grid, in_specs=list(in_specs) + [ANY] * k_in,
        out_specs=list(out_specs) + [ANY] * k_out, out_shape=list(out_shape) + list(comm.outs),
        scratch_shapes=list(scratch_shapes) + COMM_SEMS,
        input_output_aliases={n_in + a: n_out + b for a, b in comm.aliases.items()},
        compiler_params=_cparams("arbitrary"))(*operands, *comm.ins)


def _mod_fwd(c_all, w_mod, b_sl):
    def body(c_ref, w_ref, b_ref, o_ref, ca_ref):
        cv = c_ref[...]
        ca = cv * _sig(cv)
        ca_ref[...] = ca
        o_ref[...] = jnp.dot(ca, w_ref[0], preferred_element_type=F32, precision=lax.Precision.HIGHEST) + b_ref[0]

    return pl.pallas_call(
        body, name="mod_fwd", grid=(DEPTH,),
        in_specs=[_full((NDEV, D)), pl.BlockSpec((1, D, MODW), lambda l: (l, 0, 0)),
                  pl.BlockSpec((1, 1, MODW), lambda l: (l, 0, 0))],
        out_specs=[pl.BlockSpec((NDEV, MODW), lambda l: (0, l)), _full((NDEV, D))],
        out_shape=[jax.ShapeDtypeStruct((NDEV, DEPTH * MODW), F32), jax.ShapeDtypeStruct((NDEV, D), F32)],
        compiler_params=_cparams("arbitrary"),
    )(c_all, w_mod, b_sl)


def _mod_bwd_adam(ca_t, dmod, w, m, v):
    def body(ct_ref, dm_ref, w_ref, m_ref, v_ref, g_ref, d_ref, mo_ref, vo_ref):
        g = jnp.dot(ct_ref[...], dm_ref[0], preferred_element_type=F32, precision=lax.Precision.HIGHEST)
        delta, m2, v2 = _adam(w_ref[0], g, m_ref[0], v_ref[0])
        g_ref[0], d_ref[0], mo_ref[0], vo_ref[0] = g, delta, m2, v2

    blk = pl.BlockSpec((1, D, MODW), lambda l: (l, 0, 0))
    sds = jax.ShapeDtypeStruct(w.shape, F32)
    return pl.pallas_call(
        body, name="mod_bwd_adam", grid=(DEPTH,),
        in_specs=[_full((D, NDEV)), pl.BlockSpec((1, NDEV, MODW), lambda l: (l, 0, 0)), blk, blk, blk],
        out_specs=[blk] * 4, out_shape=[sds] * 4,
        compiler_params=_cparams("arbitrary"),
    )(ca_t, dmod, w, m, v)


def _inproj(x, vec, bias, gw, layer, tm):
    t = x.shape[0]

    def body(x_ref, vec_ref, b_ref, g_ref, u_ref, w_s, sems):
        @pl.when(pl.program_id(0) == 0)
        def _():
            _load_weights(g_ref, [(OFF_IN, ROWS_IN, w_s)], sems)

        h, _, _ = _norm_fwd(x_ref[...], vec_ref[0:1], vec_ref[1:2], vec_ref[2:3])
        w = w_s[...].reshape(2 * D, D)
        u_ref[...] = (_nt(h.astype(BF16), w) + b_ref[...]).astype(BF16)

    return pl.pallas_call(
        body, name=f"inproj_l{layer}", grid=(t // tm,),
        in_specs=[_tile(tm, D), _full((8, D)), _full((1, 2 * D)), ANY],
        out_specs=_tile(tm, 2 * D), out_shape=jax.ShapeDtypeStruct((t, 2 * D), BF16),
        scratch_shapes=[pltpu.VMEM((NDEV, ROWS_IN, D), BF16), pltpu.SemaphoreType.DMA((1,))],
        compiler_params=_cparams("arbitrary"),
    )(x, vec, bias, gw)


def _fill_even(qe, pe, be, part_ref, lo, rows, valid):
    cg = part_ref[:, DA:2 * DA].astype(F32)
    v = part_ref[:, 2 * DA:3 * DA].astype(F32)
    q = cg * v
    p = part_ref[:, 3 * DA:4 * DA].astype(F32)
    if valid is not None:
        q = jnp.where(valid, q, 0.0)
        p = jnp.where(valid, p, 0.0)
    qe[lo:lo + rows, :] = q
    pe[lo:lo + rows, :] = p
    if be is not None:
        b = part_ref[:, 0:DA].astype(F32)
        be[lo:lo + rows, :] = b if valid is None else jnp.where(valid, b, 0.0)


def _conv3(ca_ref, qe, tm):
    return (ca_ref[0:1] * qe[HALO - 1:HALO - 1 + tm] + ca_ref[1:2] * qe[HALO:HALO + tm]
            + ca_ref[2:3] * qe[HALO + 1:HALO + 1 + tm])


def _pool_counts(t0, rows, first_row, left, right, t):
    tg = t0 + first_row + lax.broadcasted_iota(jnp.int32, (rows, 1), 0)
    cnt = jnp.minimum(tg + right, t - 1) - jnp.maximum(tg - left, 0) + 1
    return jnp.maximum(cnt, 1).astype(F32)


def _pool_minus_id(pe, gi, left, right, inv_cnt, tm):
    c0 = gi * PG
    s = pe[HALO - left:HALO - left + tm, c0:c0 + PG]
    for j in range(-left + 1, right + 1):
        s = s + pe[HALO + j:HALO + j + tm, c0:c0 + PG]
    return s * inv_cnt - pe[HALO:HALO + tm, c0:c0 + PG]


def _mix_even_fwd(u, x, vec, ca, wp, ps, gw, layer, tm):
    t = x.shape[0]
    n = t // tm
    e = tm + 2 * HALO

    def body(u_ref, up_ref, un_ref, x_ref, vec_ref, ca_ref, wp_ref, ps_ref, g_ref, xo_ref, y_ref, w_s, qe, pe, sems):
        i = pl.program_id(0)

        @pl.when(i == 0)
        def _():
            _load_weights(g_ref, [(OFF_OUT, ROWS_OUT, w_s)], sems)

        _fill_even(qe, pe, None, up_ref, 0, HALO, i > 0)
        _fill_even(qe, pe, None, u_ref, HALO, tm, None)
        _fill_even(qe, pe, None, un_ref, HALO + tm, HALO, i < n - 1)
        ya = u_ref[:, 0:DA].astype(F32) * _conv3(ca_ref, qe, tm)
        parts = [ya]
        for gi, (_, left, right) in enumerate(POOL):
            inv = 1.0 / _pool_counts(i * tm, tm, 0, left, right, t)
            pm = _pool_minus_id(pe, gi, left, right, inv, tm)
            parts.append(_nn(pm.astype(BF16), wp_ref[gi]) * ps_ref[0:1, gi * PG:(gi + 1) * PG])
        cat = jnp.concatenate(parts, axis=-1).astype(BF16)
        y = _nn(cat, w_s[...].reshape(D, D))
        y_ref[...] = y.astype(BF16)
        xo_ref[...] = x_ref[...] + vec_ref[3:4] * y

    prev, nxt = _halo_specs(tm, 2 * D, t)
    return pl.pallas_call(
        body, name=f"mix_even_fwd_l{layer}", grid=(n,),
        in_specs=[_tile(tm, 2 * D), prev, nxt, _tile(tm, D), _full((8, D)), _full((3, DA)),
                  _full((4, PG, PG)), _full((1, DA)), ANY],
        out_specs=[_tile(tm, D), _tile(tm, D)],
        out_shape=[jax.ShapeDtypeStruct((t, D), F32), jax.ShapeDtypeStruct((t, D), BF16)],
        scratch_shapes=[pltpu.VMEM((NDEV, ROWS_OUT, D), BF16), pltpu.VMEM((e, DA), F32), pltpu.VMEM((e, DA), F32),
                        pltpu.SemaphoreType.DMA((1,))],
        compiler_params=_cparams("arbitrary"),
    )(u, u, u, x, vec, ca, wp, ps, gw)


def _mix_even_bwd(dxo, u, x, y, vec, ca, wp, ps, gw, layer, tm, comm=None):
    t = x.shape[0]
    n = t // tm
    e = tm + 2 * HALO

    def body(dxo_ref, dp_ref, dn_ref, u_ref, up_ref, un_ref, x_ref, y_ref, vec_ref, ca_ref, wp_ref, ps_ref, g_ref,
             dxi_ref, du_ref, h_ref, cat_ref, dy_ref, sums_ref, dwp_ref,
             wo_s, wi_s, dye, qe, pe, be, dce, epe, sems):
        i = pl.program_id(0)

        @pl.when(i == 0)
        def _():
            _load_weights(g_ref, [(OFF_OUT, ROWS_OUT, wo_s), (OFF_IN, ROWS_IN, wi_s)], sems)
            sums_ref[...] = jnp.zeros_like(sums_ref)
            dwp_ref[...] = jnp.zeros_like(dwp_ref)

        gate = vec_ref[3:4]
        dxo_m = dxo_ref[...]
        dye[0:HALO, :] = jnp.where(i > 0, gate * dp_ref[...], 0.0).astype(BF16)
        dye[HALO:HALO + tm, :] = (gate * dxo_m).astype(BF16)
        dye[HALO + tm:e, :] = jnp.where(i < n - 1, gate * dn_ref[...], 0.0).astype(BF16)
        _fill_even(qe, pe, be, up_ref, 0, HALO, i > 0)
        _fill_even(qe, pe, be, u_ref, HALO, tm, None)
        _fill_even(qe, pe, be, un_ref, HALO + tm, HALO, i < n - 1)
        sums_ref[0:1, :] += _rowsum(dxo_m * y_ref[...].astype(F32))

        dcat = _nt(dye[...], wo_s[...].reshape(D, D))
        dce[...] = dcat[:, 0:DA] * be[...]
        cq = _conv3(ca_ref, qe, tm)
        bg = be[HALO:HALO + tm]
        dc_m = dce[HALO:HALO + tm]
        dbg = dcat[HALO:HALO + tm, 0:DA] * cq
        dq = (ca_ref[0:1] * dce[HALO + 1:HALO + 1 + tm] + ca_ref[1:2] * dc_m
              + ca_ref[2:3] * dce[HALO - 1:HALO - 1 + tm])
        cg = u_ref[:, DA:2 * DA].astype(F32)
        v = u_ref[:, 2 * DA:3 * DA].astype(F32)
        for k in range(3):
            sums_ref[4 + k:5 + k, 0:DA] += _rowsum(dc_m * qe[HALO - 1 + k:HALO - 1 + k + tm])
        du_parts = [dbg, dq * v, dq * cg]
        cat_parts = [bg * cq]
        for gi, (_, left, right) in enumerate(POOL):
            c0 = gi * PG
            scale = ps_ref[0:1, c0:c0 + PG]
            dyb = dcat[:, DA + c0:DA + c0 + PG]
            dybs = (dyb * scale).astype(BF16)
            dpm = _nt(dybs, wp_ref[gi])
            inv_e = 1.0 / _pool_counts(i * tm, e, -HALO, left, right, t)
            epe[:, c0:c0 + PG] = dpm * inv_e
            s_adj = epe[HALO - right:HALO - right + tm, c0:c0 + PG]
            for j in range(-right + 1, left + 1):
                s_adj = s_adj + epe[HALO + j:HALO + j + tm, c0:c0 + PG]
            du_parts.append(s_adj - dpm[HALO:HALO + tm])
            inv_m = 1.0 / _pool_counts(i * tm, tm, 0, left, right, t)
            pm = _pool_minus_id(pe, gi, left, right, inv_m, tm).astype(BF16)
            ybpre = _nn(pm, wp_ref[gi])
            sums_ref[7:8, c0:c0 + PG] += _rowsum(dyb[HALO:HALO + tm] * ybpre)
            dwp_ref[gi] += _tn(pm, dybs[HALO:HALO + tm])
            cat_parts.append(ybpre * scale)
        du = jnp.concatenate(du_parts, axis=-1).astype(BF16)
        du_ref[...] = du
        cat_ref[...] = jnp.concatenate(cat_parts, axis=-1).astype(BF16)
        dy_ref[...] = dye[HALO:HALO + tm, :]
        dh = _nn(du, wi_s[...].reshape(2 * D, D))
        g, sc, sh = vec_ref[0:1], vec_ref[1:2], vec_ref[2:3]
        h, nrm, r = _norm_fwd(x_ref[...], g, sc, sh)
        h_ref[...] = h.astype(BF16)
        dxi_ref[...] = dxo_m + _norm_bwd(dh, nrm, r, g, sc)
        sums_ref[1:2, :] += _rowsum(dh)
        sums_ref[2:3, :] += _rowsum(dh * nrm)

        @pl.when(i == n - 1)
        def _():
            p = sums_ref[2:3, :]
            sums_ref[3:4, :] = p * (1.0 + sc)
            sums_ref[2:3, :] = p * g

    prev_u, nxt_u = _halo_specs(tm, 2 * D, t)
    prev_d, nxt_d = _halo_specs(tm, D, t)
    return _host_call(
        body, comm, name=f"mix_even_bwd_l{layer}", grid=(n,),
        in_specs=[_tile(tm, D), prev_d, nxt_d, _tile(tm, 2 * D), prev_u, nxt_u, _tile(tm, D), _tile(tm, D),
                  _full((8, D)), _full((3, DA)), _full((4, PG, PG)), _full((1, DA)), ANY],
        out_specs=[_tile(tm, D), _tile(tm, 2 * D), _tile(tm, D), _tile(tm, D), _tile(tm, D),
                   _full((16, D)), _full((4, PG, PG))],
        out_shape=[jax.ShapeDtypeStruct((t, D), F32), jax.ShapeDtypeStruct((t, 2 * D), BF16),
                   jax.ShapeDtypeStruct((t, D), BF16), jax.ShapeDtypeStruct((t, D), BF16),
                   jax.ShapeDtypeStruct((t, D), BF16), jax.ShapeDtypeStruct((16, D), F32),
                   jax.ShapeDtypeStruct((4, PG, PG), F32)],
        scratch_shapes=[pltpu.VMEM((NDEV, ROWS_OUT, D), BF16), pltpu.VMEM((NDEV, ROWS_IN, D), BF16),
                        pltpu.VMEM((e, D), BF16), pltpu.VMEM((e, DA), F32), pltpu.VMEM((e, DA), F32),
                        pltpu.VMEM((e, DA), F32), pltpu.VMEM((e, DA), F32), pltpu.VMEM((e, DA), F32),
                        pltpu.SemaphoreType.DMA((2,))],
        operands=(dxo, dxo, dxo, u, u, u, x, y, vec, ca, wp, ps, gw))


def _fill_glu(ze, part_ref, lo, rows, valid):
    a = part_ref[:, 0:D].astype(F32)
    g = part_ref[:, D:2 * D].astype(F32)
    z = a * _sig(g)
    ze[lo:lo + rows, :] = z if valid is None else jnp.where(valid, z, 0.0)


SHIFT_ROWS = 24


def _shifted_copies(dst, src, tm):
    rows = tm + SHIFT_ROWS
    for j in range(8):
        dst[j, :, :] = src[j:j + rows, :]


def _shifted(dst, shift, tm):
    lo = shift // 8 * 8
    return dst[shift % 8, lo:lo + tm, :]


def _layer_norm_parts(z2):
    mu = _lanemean(z2)
    d = z2 - mu
    rstd = lax.rsqrt(_lanemean(d * d) + LN_EPS)
    return d * rstd, rstd


def _mix_odd_fwd(u, x, vec, wdw, sm, gw, layer, tm):
    t = x.shape[0]
    n = t // tm
    e = tm + 2 * HALO

    def body(u_ref, up_ref, un_ref, x_ref, vec_ref, wdw_ref, sm_ref, g_ref, xo_ref, y_ref, z2_ref, w_s, ze, zsh, sems):
        i = pl.program_id(0)

        @pl.when(i == 0)
        def _():
            _load_weights(g_ref, [(OFF_OUT, ROWS_OUT, w_s)], sems)

        _fill_glu(ze, up_ref, 0, HALO, i > 0)
        _fill_glu(ze, u_ref, HALO, tm, None)
        _fill_glu(ze, un_ref, HALO + tm, HALO, i < n - 1)
        _shifted_copies(zsh, ze, tm)
        z2 = sm_ref[0:1] + wdw_ref[0:1] * _shifted(zsh, 1, tm)
        for k in range(1, CONF_K):
            z2 = z2 + wdw_ref[k:k + 1] * _shifted(zsh, 1 + k, tm)
        z2_ref[...] = z2.astype(BF16)
        zn, _ = _layer_norm_parts(z2)
        lo = zn * sm_ref[1:2] + sm_ref[2:3]
        z3 = lo * _sig(lo)
        y = _nn(z3.astype(BF16), w_s[...].reshape(D, D)) + sm_ref[3:4]
        y_ref[...] = y.astype(BF16)
        xo_ref[...] = x_ref[...] + vec_ref[3:4] * y

    prev, nxt = _halo_specs(tm, 2 * D, t)
    return pl.pallas_call(
        body, name=f"mix_odd_fwd_l{layer}", grid=(n,),
        in_specs=[_tile(tm, 2 * D), prev, nxt, _tile(tm, D), _full((8, D)), _full((32, D)), _full((8, D)), ANY],
        out_specs=[_tile(tm, D), _tile(tm, D), _tile(tm, D)],
        out_shape=[jax.ShapeDtypeStruct((t, D), F32), jax.ShapeDtypeStruct((t, D), BF16),
                   jax.ShapeDtypeStruct((t, D), BF16)],
        scratch_shapes=[pltpu.VMEM((NDEV, ROWS_OUT, D), BF16), pltpu.VMEM((e, D), F32),
                        pltpu.VMEM((8, tm + SHIFT_ROWS, D), F32), pltpu.SemaphoreType.DMA((1,))],
        compiler_params=_cparams("arbitrary"),
    )(u, u, u, x, vec, wdw, sm, gw)


def _mix_odd_bwd1(dxo, y, z2, vec, sm, gw, layer, tm):
    t = dxo.shape[0]
    n = t // tm

    def body(dxo_ref, y_ref, z2_ref, vec_ref, sm_ref, g_ref, dy_ref, z3_ref, dz2_ref, sums_ref, w_s, sems):
        i = pl.program_id(0)

        @pl.when(i == 0)
        def _():
            _load_weights(g_ref, [(OFF_OUT, ROWS_OUT, w_s)], sems)
            sums_ref[...] = jnp.zeros_like(sums_ref)

        dxo_m = dxo_ref[...]
        dy = vec_ref[3:4] * dxo_m
        dyb = dy.astype(BF16)
        dy_ref[...] = dyb
        sums_ref[0:1, :] += _rowsum(dxo_m * y_ref[...].astype(F32))
        sums_ref[4:5, :] += _rowsum(dy)
        dz3 = _nt(dyb, w_s[...].reshape(D, D))
        zn, rstd = _layer_norm_parts(z2_ref[...].astype(F32))
        lo = zn * sm_ref[1:2] + sm_ref[2:3]
        sg = _sig(lo)
        z3_ref[...] = (lo * sg).astype(BF16)
        dlo = dz3 * (sg * (1.0 + lo * (1.0 - sg)))
        sums_ref[5:6, :] += _rowsum(dlo * zn)
        sums_ref[6:7, :] += _rowsum(dlo)
        dzn = dlo * sm_ref[1:2]
        dz2 = rstd * (dzn - _lanemean(dzn) - zn * _lanemean(dzn * zn))
        sums_ref[7:8, :] += _rowsum(dz2)
        dz2_ref[...] = dz2.astype(BF16)

    return pl.pallas_call(
        body, name=f"mix_odd_bwd1_l{layer}", grid=(n,),
        in_specs=[_tile(tm, D), _tile(tm, D), _tile(tm, D), _full((8, D)), _full((8, D)), ANY],
        out_specs=[_tile(tm, D), _tile(tm, D), _tile(tm, D), _full((16, D))],
        out_shape=[jax.ShapeDtypeStruct((t, D), BF16)] * 3 + [jax.ShapeDtypeStruct((16, D), F32)],
        scratch_shapes=[pltpu.VMEM((NDEV, ROWS_OUT, D), BF16), pltpu.SemaphoreType.DMA((1,))],
        compiler_params=_cparams("arbitrary"),
    )(dxo, y, z2, vec, sm, gw)


def _mix_odd_bwd2(dz2, u, x, dxo, vec, wdw, gw, layer, tm, comm=None):
    t = x.shape[0]
    n = t // tm
    e = tm + 2 * HALO

    def body(dz_ref, dzp_ref, dzn_ref, u_ref, x_ref, dxo_ref, vec_ref, wdw_ref, g_ref,
             dxi_ref, du_ref, h_ref, sums_ref, dw_ref, w_s, de, zsh, sems):
        i = pl.program_id(0)

        @pl.when(i == 0)
        def _():
            _load_weights(g_ref, [(OFF_IN, ROWS_IN, w_s)], sems)
            sums_ref[...] = jnp.zeros_like(sums_ref)
            dw_ref[...] = jnp.zeros_like(dw_ref)

        de[0:HALO, :] = jnp.where(i > 0, dzp_ref[...].astype(F32), 0.0)
        de[HALO:HALO + tm, :] = dz_ref[...].astype(F32)
        de[HALO + tm:e, :] = jnp.where(i < n - 1, dzn_ref[...].astype(F32), 0.0)
        a = u_ref[:, 0:D].astype(F32)
        gg = u_ref[:, D:2 * D].astype(F32)
        sg = _sig(gg)
        z = a * sg
        _shifted_copies(zsh, de, tm)
        dz = None
        for k in range(CONF_K):
            shifted = _shifted(zsh, CONF_K - k, tm)
            term = wdw_ref[k:k + 1] * shifted
            dz = term if dz is None else dz + term
            dw_ref[k:k + 1, :] += _rowsum(z * shifted)
        da = dz * sg
        dg = dz * a * (sg * (1.0 - sg))
        sums_ref[8:9, :] += _rowsum(da)
        sums_ref[9:10, :] += _rowsum(dg)
        du = jnp.concatenate([da, dg], axis=-1).astype(BF16)
        du_ref[...] = du
        dh = _nn(du, w_s[...].reshape(2 * D, D))
        g, sc, sh = vec_ref[0:1], vec_ref[1:2], vec_ref[2:3]
        h, nrm, r = _norm_fwd(x_ref[...], g, sc, sh)
        h_ref[...] = h.astype(BF16)
        dxi_ref[...] = dxo_ref[...] + _norm_bwd(dh, nrm, r, g, sc)
        sums_ref[1:2, :] += _rowsum(dh)
        sums_ref[2:3, :] += _rowsum(dh * nrm)

        @pl.when(i == n - 1)
        def _():
            p = sums_ref[2:3, :]
            sums_ref[3:4, :] = p * (1.0 + sc)
            sums_ref[2:3, :] = p * g

    prev_d, nxt_d = _halo_specs(tm, D, t)
    return _host_call(
        body, comm, name=f"mix_odd_bwd2_l{layer}", grid=(n,),
        in_specs=[_tile(tm, D), prev_d, nxt_d, _tile(tm, 2 * D), _tile(tm, D), _tile(tm, D),
                  _full((8, D)), _full((32, D)), ANY],
        out_specs=[_tile(tm, D), _tile(tm, 2 * D), _tile(tm, D), _full((16, D)), _full((32, D))],
        out_shape=[jax.ShapeDtypeStruct((t, D), F32), jax.ShapeDtypeStruct((t, 2 * D), BF16),
                   jax.ShapeDtypeStruct((t, D), BF16), jax.ShapeDtypeStruct((16, D), F32),
                   jax.ShapeDtypeStruct((32, D), F32)],
        scratch_shapes=[pltpu.VMEM((NDEV, ROWS_IN, D), BF16), pltpu.VMEM((e, D), F32),
                        pltpu.VMEM((8, tm + SHIFT_ROWS, D), F32),
                        pltpu.SemaphoreType.DMA((1,))],
        operands=(dz2, dz2, dz2, u, x, dxo, vec, wdw, gw))


FCH = FF // 2


def _ffn_fwd(x, vec, gw, layer, tm, comm=None):
    t = x.shape[0]

    def body(x_ref, vec_ref, g_ref, xo_ref, a_ref, b_ref, y_ref, wg_s, wu_s, wd_s, sems):
        @pl.when(pl.program_id(0) == 0)
        def _():
            _load_weights(g_ref, [(OFF_G, FS, wg_s), (OFF_U, FS, wu_s), (OFF_D, FS, wd_s)], sems)

        xv = x_ref[...]
        h, _, _ = _norm_fwd(xv, vec_ref[0:1], vec_ref[1:2], vec_ref[2:3])
        hb = h.astype(BF16)
        y = jnp.zeros((tm, D), F32)
        for ch in range(2):
            a = _nt(hb, wg_s[4 * ch:4 * ch + 4].reshape(FCH, D))
            b = _nt(hb, wu_s[4 * ch:4 * ch + 4].reshape(FCH, D))
            a_ref[:, ch * FCH:(ch + 1) * FCH] = a.astype(BF16)
            b_ref[:, ch * FCH:(ch + 1) * FCH] = b.astype(BF16)
            s = (a * _sig(a)) * b
            y = y + _nn(s.astype(BF16), wd_s[4 * ch:4 * ch + 4].reshape(FCH, D))
        y_ref[...] = y.astype(BF16)
        xo_ref[...] = xv + vec_ref[3:4] * y

    wsc = pltpu.VMEM((NDEV, FS, D), BF16)
    return _host_call(
        body, comm, name=f"ffn_fwd_l{layer}", grid=(t // tm,),
        in_specs=[_tile(tm, D), _full((8, D)), ANY],
        out_specs=[_tile(tm, D), _tile(tm, FF), _tile(tm, FF), _tile(tm, D)],
        out_shape=[jax.ShapeDtypeStruct((t, D), F32), jax.ShapeDtypeStruct((t, FF), BF16),
                   jax.ShapeDtypeStruct((t, FF), BF16), jax.ShapeDtypeStruct((t, D), BF16)],
        scratch_shapes=[wsc, wsc, wsc, pltpu.SemaphoreType.DMA((3,))],
        operands=(x, vec, gw))


def _ffn_bwd(dxo, x, y, a, b, vec, gw, layer, tm, comm=None):
    t = x.shape[0]
    n = t // tm

    def body(dxo_ref, x_ref, y_ref, a_ref, b_ref, vec_ref, g_ref,
             dxi_ref, h_ref, dy_ref, s_ref, da_ref, db_ref, sums_ref, wg_s, wu_s, wd_s, sems):
        i = pl.program_id(0)

        @pl.when(i == 0)
        def _():
            _load_weight_rows(g_ref, [(OFF_G, FS, wg_s), (OFF_U, FS, wu_s), (OFF_D, FS, wd_s)], sems)
            sums_ref[...] = jnp.zeros_like(sums_ref)

        dxo_m = dxo_ref[...]
        sums_ref[0:1, :] += _rowsum(dxo_m * y_ref[...].astype(F32))
        dyb = (vec_ref[3:4] * dxo_m).astype(BF16)
        dy_ref[...] = dyb
        for ch in range(FF // MXU_N):
            cols = slice(ch * MXU_N, (ch + 1) * MXU_N)
            ds = _nt(dyb, wd_s[cols, :]).astype(BF16)
            av = a_ref[:, cols]
            bv = b_ref[:, cols]
            sg = _sig(av)
            sl = av * sg
            s_ref[:, cols] = sl * bv
            db_ref[:, cols] = ds * sl
            da_ref[:, cols] = (ds * bv) * (sg * (1.0 + av * (1.0 - sg)))
        dh = _nn(da_ref[...], wg_s[...]) + _nn(db_ref[...], wu_s[...])
        g, sc, sh = vec_ref[0:1], vec_ref[1:2], vec_ref[2:3]
        h, nrm, r = _norm_fwd(x_ref[...], g, sc, sh)
        h_ref[...] = h.astype(BF16)
        dxi_ref[...] = dxo_m + _norm_bwd(dh, nrm, r, g, sc)
        sums_ref[1:2, :] += _rowsum(dh)
        sums_ref[2:3, :] += _rowsum(dh * nrm)

        @pl.when(i == n - 1)
        def _():
            p = sums_ref[2:3, :]
            sums_ref[3:4, :] = p * (1.0 + sc)
            sums_ref[2:3, :] = p * g

    wsc = pltpu.VMEM((FF, D), BF16)
    big, small = jax.ShapeDtypeStruct((t, FF), BF16), jax.ShapeDtypeStruct((t, D), BF16)
    return _host_call(
        body, comm, name=f"ffn_bwd_l{layer}", grid=(n,),
        in_specs=[_tile(tm, D), _tile(tm, D), _tile(tm, D), _tile(tm, FF), _tile(tm, FF), _full((8, D)), ANY],
        out_specs=[_tile(tm, D), _tile(tm, D), _tile(tm, D), _tile(tm, FF), _tile(tm, FF), _tile(tm, FF),
                   _full((8, D))],
        out_shape=[jax.ShapeDtypeStruct((t, D), F32), small, small, big, big, big, jax.ShapeDtypeStruct((8, D), F32)],
        scratch_shapes=[wsc, wsc, wsc, pltpu.SemaphoreType.DMA((3 * NDEV,))],
        operands=(dxo, x, y, a, b, vec, gw))


def _wgrad(lhs, rhs, name, tk):
    t, m = lhs.shape
    n = t // tk

    def body(l_ref, r_ref, o_ref, acc):
        i = pl.program_id(0)

        @pl.when(i == 0)
        def _():
            acc[...] = jnp.zeros_like(acc)

        acc[...] += _tn(l_ref[...], r_ref[...])

        @pl.when(i == n - 1)
        def _():
            o_ref[...] = acc[...].astype(BF16)

    return pl.pallas_call(
        body, name=name, grid=(n,),
        in_specs=[_tile(tk, m), _tile(tk, D)], out_specs=_full((m, D)),
        out_shape=jax.ShapeDtypeStruct((m, D), BF16),
        scratch_shapes=[pltpu.VMEM((m, D), F32)],
        compiler_params=_cparams("arbitrary"),
    )(lhs, rhs)


def _final(x, tgt, gf, tm):
    t = x.shape[0]

    def body(x_ref, t_ref, g_ref, dx_ref, sums_ref):
        @pl.when(pl.program_id(0) == 0)
        def _():
            sums_ref[...] = jnp.zeros_like(sums_ref)

        xv = x_ref[...]
        r = lax.rsqrt(_lanemean(xv * xv) + RMS_EPS)
        nrm = xv * r
        g = g_ref[...]
        err = nrm * g - t_ref[...]
        sums_ref[1:2, :] += _rowsum(err * err) * (0.5 / D)
        dout = err * (1.0 / D)
        sums_ref[0:1, :] += _rowsum(dout * nrm)
        dn = dout * g
        dx_ref[...] = r * (dn - nrm * _lanemean(dn * nrm))

    return pl.pallas_call(
        body, name="loss_head", grid=(t // tm,),
        in_specs=[_tile(tm, D), _tile(tm, D), _full((1, D))],
        out_specs=[_tile(tm, D), _full((8, D))],
        out_shape=[jax.ShapeDtypeStruct((t, D), F32), jax.ShapeDtypeStruct((8, D), F32)],
        compiler_params=_cparams("arbitrary"),
    )(x, tgt, gf)


ADAM_ROWS = LROWS // 5


def _adam_big(recv, w, m, v):
    def body(r_ref, w_ref, m_ref, v_ref, g_ref, d_ref, mo_ref, vo_ref):
        g = r_ref[0, 0].astype(F32)
        for s in range(1, NDEV):
            g = g + r_ref[s, 0].astype(F32)
        delta, m2, v2 = _adam(w_ref[0], g, m_ref[0], v_ref[0])
        g_ref[0], d_ref[0], mo_ref[0], vo_ref[0] = g, delta, m2, v2

    blk = pl.BlockSpec((1, ADAM_ROWS, D), lambda l, j: (l, j, 0))
    sds = jax.ShapeDtypeStruct(w.shape, F32)
    return pl.pallas_call(
        body, name="adam_big", grid=(DEPTH, LROWS // ADAM_ROWS),
        in_specs=[pl.BlockSpec((NDEV, 1, ADAM_ROWS, D), lambda l, j: (0, l, j, 0)), blk, blk, blk],
        out_specs=[blk] * 4, out_shape=[sds] * 4,
        compiler_params=_cparams("arbitrary", "arbitrary"),
    )(recv, w, m, v)


def _sum_small(gathered, rows):
    def body(g_ref, o_ref):
        acc = g_ref[0:rows, :]
        for s in range(1, NDEV):
            acc = acc + g_ref[s * rows:(s + 1) * rows, :]
        o_ref[...] = acc

    return pl.pallas_call(
        body, name="sum_small",
        in_specs=[pl.BlockSpec(memory_space=pltpu.VMEM)], out_specs=pl.BlockSpec(memory_space=pltpu.VMEM),
        out_shape=jax.ShapeDtypeStruct((rows, D), F32),
        compiler_params=pltpu.CompilerParams(vmem_limit_bytes=VMEM_LIMIT),
    )(gathered)


def _adam_small(params):
    k = len(params)

    def body(*refs):
        ins, outs = refs[:4 * k], refs[4 * k:]
        for j in range(k):
            w_ref, g_ref, m_ref, v_ref = ins[4 * j:4 * j + 4]
            delta, m2, v2 = _adam(w_ref[...], g_ref[...], m_ref[...], v_ref[...])
            outs[3 * j][...], outs[3 * j + 1][...], outs[3 * j + 2][...] = delta, m2, v2

    flat = [a for p in params for a in p]
    shapes = [jax.ShapeDtypeStruct(p[0].shape, F32) for p in params for _ in range(3)]
    vm = pl.BlockSpec(memory_space=pltpu.VMEM)
    res = pl.pallas_call(
        body, name="adam_small", in_specs=[vm] * len(flat), out_specs=[vm] * len(shapes), out_shape=shapes,
        compiler_params=pltpu.CompilerParams(vmem_limit_bytes=VMEM_LIMIT),
    )(*flat)
    return [tuple(res[3 * j:3 * j + 3]) for j in range(k)]


def _pack(ab_in, ab_out, pw1, pw2, wg, wu, wd):
    ins = jnp.swapaxes(jnp.stack([ab_in[0], pw1[0], ab_in[1], pw1[1]]), 1, 2)
    outs = jnp.stack([ab_out[0], pw2[0], ab_out[1], pw2[1]])
    return jnp.concatenate([ins, outs, jnp.swapaxes(wg, 1, 2), jnp.swapaxes(wu, 1, 2), wd], axis=1)


def _unpack(p):
    ins = jnp.swapaxes(p[:, OFF_IN:OFF_OUT], 1, 2)
    outs = p[:, OFF_OUT:OFF_G]
    return (ins[0::2], outs[0::2], ins[1::2], outs[1::2], jnp.swapaxes(p[:, OFF_G:OFF_U], 1, 2),
            jnp.swapaxes(p[:, OFF_U:OFF_D], 1, 2), p[:, OFF_D:LROWS])


def _unshard(flat, lead, per):
    k = len(lead)
    a = flat.reshape((NDEV,) + tuple(lead) + (per,))
    a = jnp.transpose(a, tuple(range(1, k + 1)) + (0, k + 1))
    return a.reshape(tuple(lead) + (NDEV * per,))


def _rows_of(a):
    f = a.reshape(-1)
    pad = (-f.shape[0]) % D
    if pad:
        f = jnp.concatenate([f, jnp.zeros((pad,), f.dtype)])
    return f.reshape(-1, D)


def _pad_rows(a, rows):
    return jnp.concatenate([a, jnp.zeros((rows - a.shape[0],) + a.shape[1:], a.dtype)], axis=0)


def kernel(x, c, norm_mix_g, norm_ffn_g, w_mod, b_mod, ab_w_in, ab_conv, ab_w_pool, ab_pool_scale, ab_w_out, cf_w_pw1, cf_b_pw1, cf_w_dw, cf_b_dw, cf_ln_g, cf_ln_b, cf_w_pw2, cf_b_pw2, ffn_w_gate, ffn_w_up, ffn_w_down, final_norm_g, loss_target, m_norm_mix_g, m_norm_ffn_g, m_w_mod, m_b_mod, m_ab_w_in, m_ab_conv, m_ab_w_pool, m_ab_pool_scale, m_ab_w_out, m_cf_w_pw1, m_cf_b_pw1, m_cf_w_dw, m_cf_b_dw, m_cf_ln_g, m_cf_ln_b, m_cf_w_pw2, m_cf_b_pw2, m_ffn_w_gate, m_ffn_w_up, m_ffn_w_down, m_final_norm_g, v_norm_mix_g, v_norm_ffn_g, v_w_mod, v_b_mod, v_ab_w_in, v_ab_conv, v_ab_w_pool, v_ab_pool_scale, v_ab_w_out, v_cf_w_pw1, v_cf_b_pw1, v_cf_w_dw, v_cf_b_dw, v_cf_ln_g, v_cf_ln_b, v_cf_w_pw2, v_cf_b_pw2, v_ffn_w_gate, v_ffn_w_up, v_ffn_w_down, v_final_norm_g):
    t = x.shape[1]
    tm = 512 if t % 512 == 0 else t // 2
    tk = 512 if t % 512 == 0 else t // 2
    tmo = tm // 2
    me = 4 * lax.axis_index("x") + 2 * lax.axis_index("y") + lax.axis_index("c")
    xs, tgt = x[0], loss_target[0]

    sharded = [ab_conv, cf_b_pw1, cf_w_dw, cf_b_dw, cf_ln_g, cf_ln_b, cf_b_pw2]
    flat = jnp.concatenate([a.reshape(-1) for a in sharded])
    n_flat = flat.shape[0]
    g1 = _gather_small(jnp.concatenate([_pad_rows(c, 8), _pad_rows(_rows_of(flat), 16)], axis=0), "gather_cond")
    g1 = g1.reshape(NDEV, 24, D)
    c_all = g1[:, 0, :]
    flat_all = g1[:, 8:, :].reshape(NDEV, -1)[:, :n_flat]
    full, o = [], 0
    for a in sharded:
        lead, per = a.shape[:-1], a.shape[-1]
        size = a.size
        full.append(_unshard(flat_all[:, o:o + size], lead, per))
        o += size
    ab_conv_f, b_pw1_f, w_dw_f, b_dw_f, ln_g_f, ln_b_f, b_pw2_f = full

    b_sl = lax.dynamic_slice_in_dim(b_mod, me * MODW, MODW, axis=1).reshape(DEPTH, 1, MODW)
    mod_part, c_act = _mod_fwd(c_all, w_mod, b_sl)
    g2 = _gather_small(mod_part, "gather_mod").reshape(NDEV, NDEV, DEPTH, MODW)
    mod = jnp.transpose(lax.dynamic_index_in_dim(g2, me, axis=1, keepdims=False), (1, 0, 2)).reshape(DEPTH, N_MOD, D)
    zeros4 = jnp.zeros((4, D), F32)

    def vec_of(g, layer, k):
        return jnp.concatenate([g[layer][None], mod[layer, k + 1][None], mod[layer, k][None],
                                mod[layer, k + 2][None], zeros4], axis=0)

    vmix = [vec_of(norm_mix_g, l, 0) for l in range(DEPTH)]
    vffn = [vec_of(norm_ffn_g, l, 3) for l in range(DEPTH)]

    w_pack = _pack(ab_w_in, ab_w_out, cf_w_pw1, cf_w_pw2, ffn_w_gate, ffn_w_up, ffn_w_down)
    p16 = w_pack.astype(BF16)
    gw = [_gather_weights(p16[0])]

    wp16 = ab_w_pool.astype(BF16)
    wdw32 = [_pad_rows(w_dw_f[i], 32) for i in range(2)]
    sm_odd = [jnp.concatenate([b_dw_f[i][None], ln_g_f[i][None], ln_b_f[i][None], b_pw2_f[i][None], zeros4], axis=0)
              for i in range(2)]
    zero_bias = jnp.zeros((1, 2 * D), F32)

    saved = []
    xc = xs
    for l in range(DEPTH):
        i = l // 2
        if l % 2 == 0:
            u = _inproj(xc, vmix[l], zero_bias, gw[l], l, tm)
            x_mid, y_mix = _mix_even_fwd(u, xc, vmix[l], ab_conv_f[i], wp16[i], ab_pool_scale[i][None], gw[l], l, tm)
            z2 = None
        else:
            u = _inproj(xc, vmix[l], b_pw1_f[i][None], gw[l], l, tm)
            x_mid, y_mix, z2 = _mix_odd_fwd(u, xc, vmix[l], wdw32[i], sm_odd[i], gw[l], l, tmo)
        if l + 1 < DEPTH:
            x_out, a, b, y_ffn, g_next = _ffn_fwd(x_mid, vffn[l], gw[l], l, tm, _gather_comm(p16[l + 1]))
            gw.append(g_next)
        else:
            x_out, a, b, y_ffn = _ffn_fwd(x_mid, vffn[l], gw[l], l, tm)
        saved.append((xc, u, y_mix, z2, x_mid, a, b, y_ffn))
        xc = x_out

    dx, fsum = _final(xc, tgt, final_norm_g[None], tm)
    loss = lax.psum(jnp.sum(fsum[1]), ("x", "y", "c"))
    d_final_g = fsum[0]

    recv = _empty_recv()
    late_specs = [(OFF_U, FS), (OFF_IN, ROWS_IN), (OFF_OUT, ROWS_OUT)]
    pending = None
    dmod = [None] * DEPTH
    d_mix_g, d_ffn_g = [None] * DEPTH, [None] * DEPTH
    d_conv, d_pool, d_pscale = [None] * 2, [None] * 2, [None] * 2
    d_bpw1, d_wdw, d_bdw, d_lng, d_lnb, d_bpw2 = ([None] * 2 for _ in range(6))
    for l in reversed(range(DEPTH)):
        i = l // 2
        x_in, u, y_mix, z2, x_mid, a, b, y_ffn = saved[l]
        if pending is None:
            dx_mid, h2, dy, s, da, db, s_f = _ffn_bwd(dx, x_mid, y_ffn, a, b, vffn[l], gw[l], l, tmo)
        else:
            dx_mid, h2, dy, s, da, db, s_f, recv = _ffn_bwd(dx, x_mid, y_ffn, a, b, vffn[l], gw[l], l, tmo,
                                                            _scatter_comm(pending, late_specs, recv, l + 1))
        g_down = _wgrad(s, dy, f"wgrad_down_l{l}", tk)
        g_gate = _wgrad(da, h2, f"wgrad_gate_l{l}", tk)
        gu_comm = _scatter_comm([g_down, g_gate], [(OFF_D, FS), (OFF_G, FS)], recv, l)
        g_up = _wgrad(db, h2, f"wgrad_up_l{l}", tk)
        d_ffn_g[l] = s_f[3]
        mod_ffn = [s_f[1], s_f[2], s_f[0]]
        if l % 2 == 0:
            dx, du, h, cat, dym, s_m, dwp, recv = _mix_even_bwd(dx_mid, u, x_in, y_mix, vmix[l], ab_conv_f[i], wp16[i],
                                                                ab_pool_scale[i][None], gw[l], l, tm, gu_comm)
            g_out = _wgrad(cat, dym, f"wgrad_out_l{l}", tk)
            d_conv[i], d_pool[i], d_pscale[i] = s_m[4:7, :DA], dwp, s_m[7, :DA]
            mod_mix = [s_m[1], s_m[2], s_m[0]]
            d_mix_g[l] = s_m[3]
        else:
            dym, z3, dz2, s_1 = _mix_odd_bwd1(dx_mid, y_mix, z2, vmix[l], sm_odd[i], gw[l], l, tm)
            dx, du, h, s_2, dwdw, recv = _mix_odd_bwd2(dz2, u, x_in, dx_mid, vmix[l], wdw32[i], gw[l], l, tmo, gu_comm)
            g_out = _wgrad(z3, dym, f"wgrad_out_l{l}", tk)
            d_bpw1[i], d_wdw[i], d_bdw[i] = s_2[8:10].reshape(2 * D), dwdw[:CONF_K], s_1[7]
            d_lng[i], d_lnb[i], d_bpw2[i] = s_1[5], s_1[6], s_1[4]
            mod_mix = [s_2[1], s_2[2], s_1[0]]
            d_mix_g[l] = s_2[3]
        dmod[l] = jnp.stack(mod_mix + mod_ffn)
        pending = [g_up, _wgrad(du, h, f"wgrad_in_l{l}", tk), g_out]
    grad_x = dx[None]
    recv = _scatter_grads(pending, late_specs, recv, 0)

    m_pack = _pack(m_ab_w_in, m_ab_w_out, m_cf_w_pw1, m_cf_w_pw2, m_ffn_w_gate, m_ffn_w_up, m_ffn_w_down)
    v_pack = _pack(v_ab_w_in, v_ab_w_out, v_cf_w_pw1, v_cf_w_pw2, v_ffn_w_gate, v_ffn_w_up, v_ffn_w_down)
    big = [_unpack(p) for p in _adam_big(recv, w_pack, m_pack, v_pack)]

    small = [jnp.stack(dmod), jnp.stack(d_mix_g), jnp.stack(d_ffn_g), d_final_g, jnp.stack(d_pool),
             jnp.stack(d_pscale), jnp.stack(d_conv), jnp.stack(d_bpw1), jnp.stack(d_wdw), jnp.stack(d_bdw),
             jnp.stack(d_lng), jnp.stack(d_lnb), jnp.stack(d_bpw2)]
    small_rows = [_rows_of(a) for a in small]
    n_rows = sum(a.shape[0] for a in small_rows)
    pad_rows = -(-n_rows // 8) * 8
    g3 = _gather_small(_pad_rows(jnp.concatenate(small_rows, axis=0), pad_rows), "gather_small_grads")
    summed = _sum_small(g3, pad_rows)
    outs, o = [], 0
    for a, r in zip(small, small_rows):
        outs.append(summed[o:o + r.shape[0]].reshape(-1)[:a.size].reshape(a.shape))
        o += r.shape[0]
    (g_bmod, g_mix_g, g_ffn_g, g_final, g_pool, g_pscale, g_conv, g_bpw1, g_wdw, g_bdw, g_lng, g_lnb, g_bpw2) = outs
    g_bmod = g_bmod.reshape(DEPTH, N_MOD * D)

    def my_shard(a):
        per = a.shape[-1] // NDEV
        return lax.dynamic_slice_in_dim(a, me * per, per, axis=a.ndim - 1)

    g_conv, g_bpw1, g_wdw, g_bdw, g_lng, g_lnb, g_bpw2 = [
        my_shard(a) for a in (g_conv, g_bpw1, g_wdw, g_bdw, g_lng, g_lnb, g_bpw2)]

    dmod_all = g3.reshape(NDEV, pad_rows, D)[:, :DEPTH * N_MOD, :].reshape(NDEV, DEPTH, N_MOD * D)
    dmod_mine = jnp.transpose(lax.dynamic_slice_in_dim(dmod_all, me * MODW, MODW, axis=2), (1, 0, 2))
    g_wmod, d_wmod, nm_wmod, nv_wmod = _mod_bwd_adam(c_act.T, dmod_mine, w_mod, m_w_mod, v_w_mod)

    small_params = [
        (norm_mix_g, g_mix_g, m_norm_mix_g, v_norm_mix_g), (norm_ffn_g, g_ffn_g, m_norm_ffn_g, v_norm_ffn_g),
        (b_mod, g_bmod, m_b_mod, v_b_mod), (ab_conv, g_conv, m_ab_conv, v_ab_conv),
        (ab_w_pool, g_pool, m_ab_w_pool, v_ab_w_pool), (ab_pool_scale, g_pscale, m_ab_pool_scale, v_ab_pool_scale),
        (cf_b_pw1, g_bpw1, m_cf_b_pw1, v_cf_b_pw1), (cf_w_dw, g_wdw, m_cf_w_dw, v_cf_w_dw),
        (cf_b_dw, g_bdw, m_cf_b_dw, v_cf_b_dw), (cf_ln_g, g_lng, m_cf_ln_g, v_cf_ln_g),
        (cf_ln_b, g_lnb, m_cf_ln_b, v_cf_ln_b), (cf_b_pw2, g_bpw2, m_cf_b_pw2, v_cf_b_pw2),
        (final_norm_g, g_final, m_final_norm_g, v_final_norm_g)]

    def two_d(a):
        return a.reshape(-1, a.shape[-1])

    upd = _adam_small([tuple(two_d(a) for a in p) for p in small_params])
    upd = [tuple(r.reshape(p[0].shape) for r in u) for u, p in zip(upd, small_params)]
    (s_mix, s_ffn, s_bmod, s_conv, s_pool, s_pscale, s_bpw1, s_wdw, s_bdw, s_lng, s_lnb, s_bpw2, s_final) = upd
    small_g = [p[1] for p in small_params]
    (q_mix, q_ffn, q_bmod, q_conv, q_pool, q_pscale, q_bpw1, q_wdw, q_bdw, q_lng, q_lnb, q_bpw2, q_final) = small_g

    def ordered(k):
        ab_in, ab_out, pw1, pw2, wg, wu, wd = big[k]
        if k == 0:
            sm = dict(mix=q_mix, ffn=q_ffn, bmod=q_bmod, conv=q_conv, pool=q_pool, pscale=q_pscale, bpw1=q_bpw1,
                      wdw=q_wdw, bdw=q_bdw, lng=q_lng, lnb=q_lnb, bpw2=q_bpw2, final=q_final)
            wmod = g_wmod
        else:
            j = k - 1
            sm = dict(mix=s_mix[j], ffn=s_ffn[j], bmod=s_bmod[j], conv=s_conv[j], pool=s_pool[j], pscale=s_pscale[j],
                      bpw1=s_bpw1[j], wdw=s_wdw[j], bdw=s_bdw[j], lng=s_lng[j], lnb=s_lnb[j], bpw2=s_bpw2[j],
                      final=s_final[j])
            wmod = (d_wmod, nm_wmod, nv_wmod)[j]
        return [sm["mix"], sm["ffn"], wmod, sm["bmod"], ab_in, sm["conv"], sm["pool"], sm["pscale"], ab_out,
                pw1, sm["bpw1"], sm["wdw"], sm["bdw"], sm["lng"], sm["lnb"], pw2, sm["bpw2"], wg, wu, wd, sm["final"]]

    return (loss, grad_x, *ordered(0), *ordered(1), *ordered(2), *ordered(3))
```

```python
import functools

import jax
import jax.numpy as jnp
from jax import lax
from jax.experimental import pallas as pl
from jax.experimental.pallas import tpu as pltpu

F32 = jnp.float32
BF16 = jnp.bfloat16
MESH = pl.DeviceIdType.MESH

NDEV = 8
DEPTH = 4
D = 1024
FF = 2816
FS = FF // NDEV
DA = 512
PG = 128
POOL = ((2, 1, 0), (4, 2, 1), (8, 4, 3), (16, 8, 7))
CONF_K = 31
CONF_L = 15
N_MOD = 6
MODW = N_MOD * D // NDEV
RMS_EPS = 1e-6
LN_EPS = 1e-5

ROWS_IN, ROWS_OUT = 2 * D // NDEV, D // NDEV
OFF_IN, OFF_OUT = 0, ROWS_IN
OFF_G = OFF_OUT + ROWS_OUT
OFF_U = OFF_G + FS
OFF_D = OFF_U + FS
LROWS = OFF_D + FS

MXU_N = 256
HALO = 16
VMEM_LIMIT = 60 * 1024 * 1024

ADAM_LR, ADAM_B1, ADAM_B2, ADAM_EPS, ADAM_WD, ADAM_STEP = 1e-3, 0.9, 0.999, 1e-8, 0.01, 10
ADAM_C1 = 1.0 / (1.0 - ADAM_B1 ** ADAM_STEP)
ADAM_C2 = 1.0 / (1.0 - ADAM_B2 ** ADAM_STEP)


def _nn(a, b):
    return jnp.dot(a, b, preferred_element_type=F32)


def _nt(a, b):
    return lax.dot_general(a, b, (((1,), (1,)), ((), ())), preferred_element_type=F32)


def _tn(a, b):
    return lax.dot_general(a, b, (((0,), (0,)), ((), ())), preferred_element_type=F32)


def _sig(x):
    return 1.0 / (1.0 + jnp.exp(-x))


def _rowsum(x):
    return jnp.sum(x, axis=0, keepdims=True)


def _lanemean(x):
    return jnp.mean(x, axis=-1, keepdims=True)


def _norm_fwd(x, g, sc, sh):
    r = lax.rsqrt(_lanemean(x * x) + RMS_EPS)
    n = x * r
    return n * (g * (1.0 + sc)) + sh, n, r


def _norm_bwd(dh, n, r, g, sc):
    dn = dh * (g * (1.0 + sc))
    return r * (dn - n * _lanemean(dn * n))


def _adam(w, g, m, v):
    m2 = ADAM_B1 * m + (1.0 - ADAM_B1) * g
    v2 = ADAM_B2 * v + (1.0 - ADAM_B2) * (g * g)
    delta = -ADAM_LR * ((m2 * ADAM_C1) / (jnp.sqrt(v2 * ADAM_C2) + ADAM_EPS) + ADAM_WD * w)
    return delta, m2, v2


def _load_weights(g_ref, specs, sems):
    cps = [pltpu.make_async_copy(g_ref.at[:, pl.ds(off, rows), :], dst, sems.at[k])
           for k, (off, rows, dst) in enumerate(specs)]
    for cp in cps:
        cp.start()
    for cp in cps:
        cp.wait()


def _load_weight_rows(g_ref, specs, sems):
    cps = [pltpu.make_async_copy(g_ref.at[d, pl.ds(off, rows), :], dst.at[pl.ds(d * rows, rows), :],
                                 sems.at[NDEV * k + d])
           for k, (off, rows, dst) in enumerate(specs) for d in range(NDEV)]
    for cp in cps:
        cp.start()
    for cp in cps:
        cp.wait()


def _cparams(*sem):
    return pltpu.CompilerParams(dimension_semantics=sem if sem else None, vmem_limit_bytes=VMEM_LIMIT)


def _tile(tm, w):
    return pl.BlockSpec((tm, w), lambda i: (i, 0))


def _full(shape):
    nd = len(shape)
    return pl.BlockSpec(shape, lambda i: (0,) * nd)


def _halo_specs(tm, w, total_rows):
    tb = tm // HALO
    nb = total_rows // HALO
    prev = pl.BlockSpec((HALO, w), lambda i: (jnp.maximum(i * tb - 1, 0), 0))
    nxt = pl.BlockSpec((HALO, w), lambda i: (jnp.minimum((i + 1) * tb, nb - 1), 0))
    return prev, nxt


ANY = pl.BlockSpec(memory_space=pl.ANY)


def _peers():
    x, y, c = lax.axis_index("x"), lax.axis_index("y"), lax.axis_index("c")
    return x, y, c


def _gather_small(v, name):
    m_per, n = v.shape

    def body(x_ref, out_ref, send_sems, recv_sems, local_sem):
        x, y, c = _peers()
        me, sibling = (x, y, c), (x, y, 1 - c)
        chips = [(1 - x, y), (x, 1 - y), (1 - x, 1 - y)]

        def rows(px, py, pc):
            return out_ref.at[pl.ds((4 * px + 2 * py + pc) * m_per, m_per), :]

        def copy(k, block, to, src=None):
            return pltpu.make_async_remote_copy(
                src_ref=rows(*block) if src is None else src, dst_ref=rows(*block),
                send_sem=send_sems.at[k], recv_sem=recv_sems.at[k], device_id=to, device_id_type=MESH)

        mine = pltpu.make_async_copy(x_ref, rows(*me), local_sem)
        mine.start()
        first = [copy(0, me, sibling, src=x_ref)]
        first += [copy(1 + j, me, (*chip, c), src=x_ref) for j, chip in enumerate(chips)]
        for cp in first:
            cp.start()
        passed = [copy(4 + j, (*chip, c), sibling) for j, chip in enumerate(chips)]
        for j, chip in enumerate(chips):
            copy(1 + j, (*chip, c), me).wait_recv()
            passed[j].start()
        copy(0, sibling, me).wait_recv()
        for j, chip in enumerate(chips):
            copy(4 + j, (*chip, 1 - c), me).wait_recv()
        for cp in first + passed:
            cp.wait_send()
        mine.wait()

    return pl.pallas_call(
        body, name=name,
        out_shape=jax.ShapeDtypeStruct((NDEV * m_per, n), v.dtype),
        in_specs=[pl.BlockSpec(memory_space=pltpu.VMEM)],
        out_specs=pl.BlockSpec(memory_space=pltpu.VMEM),
        scratch_shapes=[pltpu.SemaphoreType.DMA((7,)), pltpu.SemaphoreType.DMA((7,)), pltpu.SemaphoreType.DMA],
        compiler_params=pltpu.CompilerParams(vmem_limit_bytes=VMEM_LIMIT),
    )(v)


class _Comm:
    def __init__(self, ins, outs, aliases, bind):
        self.ins, self.outs, self.aliases, self.bind = ins, outs, aliases, bind


COMM_SEMS = [pltpu.SemaphoreType.DMA((7,)), pltpu.SemaphoreType.DMA((7,)), pltpu.SemaphoreType.DMA]


def _gather_hooks(p_ref, out_ref, send_sems, recv_sems, local_sem):
    def parts():
        x, y, c = _peers()
        me, sibling = (x, y, c), (x, y, 1 - c)
        chips = [(1 - x, y), (x, 1 - y), (1 - x, 1 - y)]

        def slab(px, py, pc):
            return out_ref.at[4 * px + 2 * py + pc]

        def copy(k, block, to, src=None):
            return pltpu.make_async_remote_copy(
                src_ref=slab(*block) if src is None else src, dst_ref=slab(*block),
                send_sem=send_sems.at[k], recv_sem=recv_sems.at[k], device_id=to, device_id_type=MESH)

        def mine():
            return pltpu.make_async_copy(p_ref, slab(*me), local_sem)

        def first():
            return [copy(0, me, sibling, src=p_ref)] + [copy(1 + j, me, (*chip, c), src=p_ref)
                                                        for j, chip in enumerate(chips)]

        def passed():
            return [copy(4 + j, (*chip, c), sibling) for j, chip in enumerate(chips)]

        def from_chips():
            return [copy(1 + j, (*chip, c), me) for j, chip in enumerate(chips)]

        def from_sibling():
            return [copy(0, sibling, me)] + [copy(4 + j, (*chip, 1 - c), me) for j, chip in enumerate(chips)]

        return mine, first, passed, from_chips, from_sibling

    def start():
        mine, first, _, _, _ = parts()
        mine().start()
        for cp in first():
            cp.start()

    def middle():
        _, _, passed, from_chips, _ = parts()
        for arrived, onward in zip(from_chips(), passed()):
            arrived.wait_recv()
            onward.start()

    def end():
        mine, first, passed, _, from_sibling = parts()
        for cp in from_sibling():
            cp.wait_recv()
        for cp in first() + passed():
            cp.wait_send()
        mine().wait()

    return start, middle, end


def _gather_comm(p):
    return _Comm([p], [jax.ShapeDtypeStruct((NDEV,) + p.shape, p.dtype)], {},
                 lambda cins, couts, sems: _gather_hooks(cins[0], couts[0], *sems))


def _gather_weights(p):
    def body(p_ref, out_ref, send_sems, recv_sems, local_sem):
        for hook in _gather_hooks(p_ref, out_ref, send_sems, recv_sems, local_sem):
            hook()

    return pl.pallas_call(
        body, name="gather_weights",
        out_shape=jax.ShapeDtypeStruct((NDEV,) + p.shape, p.dtype),
        in_specs=[ANY], out_specs=ANY, scratch_shapes=COMM_SEMS,
    )(p)


def _scatter_hooks(src_refs, specs, r_ref, layer, send_sems, recv_sems, local_sem):
    total = sum(rows for _, rows in specs)

    def start():
        x, y, c = _peers()
        me = 4 * x + 2 * y + c

        def part(k, dev):
            off, rows = specs[k]
            src = src_refs[k].at[pl.ds(pl.multiple_of(dev * rows, 16), rows), :]
            return src, r_ref.at[me, layer, pl.ds(off, rows), :]

        for k in range(len(specs)):
            src, dst = part(k, me)
            pltpu.make_async_copy(src, dst, local_sem).start()
        for r in range(1, NDEV):
            px = 1 - x if r & 4 else x
            py = 1 - y if r & 2 else y
            pc = 1 - c if r & 1 else c
            for k in range(len(specs)):
                src, dst = part(k, 4 * px + 2 * py + pc)
                pltpu.make_async_remote_copy(
                    src_ref=src, dst_ref=dst, send_sem=send_sems.at[r - 1], recv_sem=recv_sems.at[r - 1],
                    device_id=(px, py, pc), device_id_type=MESH).start()

    def end():
        x, y, c = _peers()
        whole = r_ref.at[0, layer, pl.ds(0, total), :]
        for r in range(1, NDEV):
            done = pltpu.make_async_remote_copy(
                src_ref=whole, dst_ref=whole, send_sem=send_sems.at[r - 1], recv_sem=recv_sems.at[r - 1],
                device_id=(x, y, c), device_id_type=MESH)
            done.wait_recv()
            done.wait_send()
        pltpu.make_async_copy(whole, whole, local_sem).wait()

    return start, None, end


def _scatter_comm(srcs, specs, recv, layer):
    k = len(srcs)
    return _Comm(list(srcs) + [recv], [jax.ShapeDtypeStruct(recv.shape, recv.dtype)], {k: 0},
                 lambda cins, couts, sems: _scatter_hooks(cins[:k], specs, couts[0], layer, *sems))


def _tail_comm(srcs, specs, recv, layer, small):
    k = len(srcs)

    def body(*refs):
        src_refs, small_ref, r_ref, g_ref, sems = refs[:k], refs[k + 1], refs[k + 2], refs[k + 3], refs[k + 4:]
        s_start, _, s_end = _scatter_hooks(src_refs, specs, r_ref, layer, *sems[:3])
        g_start, g_middle, g_end = _gather_hooks(small_ref, g_ref, *sems[3:])
        s_start()
        g_start()
        g_middle()
        g_end()
        s_end()

    return pl.pallas_call(
        body, name="tail_comm",
        out_shape=[jax.ShapeDtypeStruct(recv.shape, recv.dtype),
                   jax.ShapeDtypeStruct((NDEV,) + small.shape, small.dtype)],
        in_specs=[ANY] * (k + 2), out_specs=[ANY, ANY], scratch_shapes=COMM_SEMS + COMM_SEMS,
        input_output_aliases={k: 0},
    )(*srcs, recv, small)


def _empty_recv():
    def body(o_ref):
        del o_ref

    return pl.pallas_call(body, name="recv_buffer", out_specs=ANY,
                          out_shape=jax.ShapeDtypeStruct((NDEV, DEPTH, LROWS, D), BF16))()


def _host_call(inner, comm, *, name, grid, in_specs, out_specs, out_shape, scratch_shapes, operands):
    if comm is None:
        return pl.pallas_call(
            inner, name=name, grid=grid, in_specs=in_specs, out_specs=out_specs, out_shape=out_shape,
            scratch_shapes=scratch_shapes, compiler_params=_cparams("arbitrary"))(*operands)
    n_in, n_out, n_s = len(in_specs), len(out_specs), len(scratch_shapes)
    k_in, k_out = len(comm.ins), len(comm.outs)
    steps = grid[0]

    def body(*refs):
        ins, cins = refs[:n_in], refs[n_in:n_in + k_in]
        o0 = n_in + k_in
        outs, couts = refs[o0:o0 + n_out], refs[o0 + n_out:o0 + n_out + k_out]
        s0 = o0 + n_out + k_out
        scr, sems = refs[s0:s0 + n_s], refs[s0 + n_s:]
        start, middle, end = comm.bind(cins, couts, sems)
        i = pl.program_id(0)
        pl.when(i == 0)(start)
        if middle is not None:
            pl.when(i == steps * 3 // 4)(middle)
        inner(*ins, *outs, *scr)
        pl.when(i == steps - 1)(end)

    return pl.pallas_call(
        body, name=name, grid=grid, in_specs=list(in_specs) + [ANY] * k_in,
        out_specs=list(out_specs) + [ANY] * k_out, out_shape=list(out_shape) + list(comm.outs),
        scratch_shapes=list(scratch_shapes) + COMM_SEMS,
        input_output_aliases={n_in + a: n_out + b for a, b in comm.aliases.items()},
        compiler_params=_cparams("arbitrary"))(*operands, *comm.ins)


def _mod_fwd(c_all, w_mod, b_sl):
    def body(c_ref, w_ref, b_ref, o_ref, ca_ref):
        cv = c_ref[...]
        ca = cv * _sig(cv)
        ca_ref[...] = ca
        o_ref[...] = jnp.dot(ca, w_ref[0], preferred_element_type=F32, precision=lax.Precision.HIGHEST) + b_ref[0]

    return pl.pallas_call(
        body, name="mod_fwd", grid=(DEPTH,),
        in_specs=[_full((NDEV, D)), pl.BlockSpec((1, D, MODW), lambda l: (l, 0, 0)),
                  pl.BlockSpec((1, 1, MODW), lambda l: (l, 0, 0))],
        out_specs=[pl.BlockSpec((NDEV, MODW), lambda l: (0, l)), _full((NDEV, D))],
        out_shape=[jax.ShapeDtypeStruct((NDEV, DEPTH * MODW), F32), jax.ShapeDtypeStruct((NDEV, D), F32)],
        compiler_params=_cparams("arbitrary"),
    )(c_all, w_mod, b_sl)


def _mod_bwd_adam(ca_t, dmod, w, m, v):
    def body(ct_ref, dm_ref, w_ref, m_ref, v_ref, g_ref, d_ref, mo_ref, vo_ref):
        g = jnp.dot(ct_ref[...], dm_ref[0], preferred_element_type=F32, precision=lax.Precision.HIGHEST)
        delta, m2, v2 = _adam(w_ref[0], g, m_ref[0], v_ref[0])
        g_ref[0], d_ref[0], mo_ref[0], vo_ref[0] = g, delta, m2, v2

    blk = pl.BlockSpec((1, D, MODW), lambda l: (l, 0, 0))
    sds = jax.ShapeDtypeStruct(w.shape, F32)
    return pl.pallas_call(
        body, name="mod_bwd_adam", grid=(DEPTH,),
        in_specs=[_full((D, NDEV)), pl.BlockSpec((1, NDEV, MODW), lambda l: (l, 0, 0)), blk, blk, blk],
        out_specs=[blk] * 4, out_shape=[sds] * 4,
        compiler_params=_cparams("arbitrary"),
    )(ca_t, dmod, w, m, v)


def _inproj(x, vec, bias, gw, layer, tm):
    t = x.shape[0]

    def body(x_ref, vec_ref, b_ref, g_ref, u_ref, w_s, sems):
        @pl.when(pl.program_id(0) == 0)
        def _():
            _load_weights(g_ref, [(OFF_IN, ROWS_IN, w_s)], sems)

        h, _, _ = _norm_fwd(x_ref[...], vec_ref[0:1], vec_ref[1:2], vec_ref[2:3])
        w = w_s[...].reshape(2 * D, D)
        u_ref[...] = (_nt(h.astype(BF16), w) + b_ref[...]).astype(BF16)

    return pl.pallas_call(
        body, name=f"inproj_l{layer}", grid=(t // tm,),
        in_specs=[_tile(tm, D), _full((8, D)), _full((1, 2 * D)), ANY],
        out_specs=_tile(tm, 2 * D), out_shape=jax.ShapeDtypeStruct((t, 2 * D), BF16),
        scratch_shapes=[pltpu.VMEM((NDEV, ROWS_IN, D), BF16), pltpu.SemaphoreType.DMA((1,))],
        compiler_params=_cparams("arbitrary"),
    )(x, vec, bias, gw)


def _fill_even(qe, pe, be, part_ref, lo, rows, valid):
    cg = part_ref[:, DA:2 * DA].astype(F32)
    v = part_ref[:, 2 * DA:3 * DA].astype(F32)
    q = cg * v
    p = part_ref[:, 3 * DA:4 * DA].astype(F32)
    if valid is not None:
        q = jnp.where(valid, q, 0.0)
        p = jnp.where(valid, p, 0.0)
    qe[lo:lo + rows, :] = q
    pe[lo:lo + rows, :] = p
    if be is not None:
        b = part_ref[:, 0:DA].astype(F32)
        be[lo:lo + rows, :] = b if valid is None else jnp.where(valid, b, 0.0)


def _conv3(ca_ref, qe, tm):
    return (ca_ref[0:1] * qe[HALO - 1:HALO - 1 + tm] + ca_ref[1:2] * qe[HALO:HALO + tm]
            + ca_ref[2:3] * qe[HALO + 1:HALO + 1 + tm])


def _pool_counts(t0, rows, first_row, left, right, t):
    tg = t0 + first_row + lax.broadcasted_iota(jnp.int32, (rows, 1), 0)
    cnt = jnp.minimum(tg + right, t - 1) - jnp.maximum(tg - left, 0) + 1
    return jnp.maximum(cnt, 1).astype(F32)


def _pool_minus_id(pe, gi, left, right, inv_cnt, tm):
    c0 = gi * PG
    s = pe[HALO - left:HALO - left + tm, c0:c0 + PG]
    for j in range(-left + 1, right + 1):
        s = s + pe[HALO + j:HALO + j + tm, c0:c0 + PG]
    return s * inv_cnt - pe[HALO:HALO + tm, c0:c0 + PG]


def _mix_even_fwd(u, x, vec, ca, wp, ps, gw, layer, tm):
    t = x.shape[0]
    n = t // tm
    e = tm + 2 * HALO

    def body(u_ref, up_ref, un_ref, x_ref, vec_ref, ca_ref, wp_ref, ps_ref, g_ref, xo_ref, y_ref, w_s, qe, pe, sems):
        i = pl.program_id(0)

        @pl.when(i == 0)
        def _():
            _load_weights(g_ref, [(OFF_OUT, ROWS_OUT, w_s)], sems)

        _fill_even(qe, pe, None, up_ref, 0, HALO, i > 0)
        _fill_even(qe, pe, None, u_ref, HALO, tm, None)
        _fill_even(qe, pe, None, un_ref, HALO + tm, HALO, i < n - 1)
        ya = u_ref[:, 0:DA].astype(F32) * _conv3(ca_ref, qe, tm)
        parts = [ya]
        for gi, (_, left, right) in enumerate(POOL):
            inv = 1.0 / _pool_counts(i * tm, tm, 0, left, right, t)
            pm = _pool_minus_id(pe, gi, left, right, inv, tm)
            parts.append(_nn(pm.astype(BF16), wp_ref[gi]) * ps_ref[0:1, gi * PG:(gi + 1) * PG])
        cat = jnp.concatenate(parts, axis=-1).astype(BF16)
        y = _nn(cat, w_s[...].reshape(D, D))
        y_ref[...] = y.astype(BF16)
        xo_ref[...] = x_ref[...] + vec_ref[3:4] * y

    prev, nxt = _halo_specs(tm, 2 * D, t)
    return pl.pallas_call(
        body, name=f"mix_even_fwd_l{layer}", grid=(n,),
        in_specs=[_tile(tm, 2 * D), prev, nxt, _tile(tm, D), _full((8, D)), _full((3, DA)),
                  _full((4, PG, PG)), _full((1, DA)), ANY],
        out_specs=[_tile(tm, D), _tile(tm, D)],
        out_shape=[jax.ShapeDtypeStruct((t, D), F32), jax.ShapeDtypeStruct((t, D), BF16)],
        scratch_shapes=[pltpu.VMEM((NDEV, ROWS_OUT, D), BF16), pltpu.VMEM((e, DA), F32), pltpu.VMEM((e, DA), F32),
                        pltpu.SemaphoreType.DMA((1,))],
        compiler_params=_cparams("arbitrary"),
    )(u, u, u, x, vec, ca, wp, ps, gw)


def _mix_even_bwd(dxo, u, x, y, vec, ca, wp, ps, gw, layer, tm, comm=None):
    t = x.shape[0]
    n = t // tm
    e = tm + 2 * HALO

    def body(dxo_ref, dp_ref, dn_ref, u_ref, up_ref, un_ref, x_ref, y_ref, vec_ref, ca_ref, wp_ref, ps_ref, g_ref,
             dxi_ref, du_ref, h_ref, cat_ref, dy_ref, sums_ref, dwp_ref,
             wo_s, wi_s, dye, qe, pe, be, dce, epe, sems):
        i = pl.program_id(0)

        @pl.when(i == 0)
        def _():
            _load_weights(g_ref, [(OFF_OUT, ROWS_OUT, wo_s), (OFF_IN, ROWS_IN, wi_s)], sems)
            sums_ref[...] = jnp.zeros_like(sums_ref)
            dwp_ref[...] = jnp.zeros_like(dwp_ref)

        gate = vec_ref[3:4]
        dxo_m = dxo_ref[...]
        dye[0:HALO, :] = jnp.where(i > 0, gate * dp_ref[...], 0.0).astype(BF16)
        dye[HALO:HALO + tm, :] = (gate * dxo_m).astype(BF16)
        dye[HALO + tm:e, :] = jnp.where(i < n - 1, gate * dn_ref[...], 0.0).astype(BF16)
        _fill_even(qe, pe, be, up_ref, 0, HALO, i > 0)
        _fill_even(qe, pe, be, u_ref, HALO, tm, None)
        _fill_even(qe, pe, be, un_ref, HALO + tm, HALO, i < n - 1)
        sums_ref[0:1, :] += _rowsum(dxo_m * y_ref[...].astype(F32))

        dcat = _nt(dye[...], wo_s[...].reshape(D, D))
        dce[...] = dcat[:, 0:DA] * be[...]
        cq = _conv3(ca_ref, qe, tm)
        bg = be[HALO:HALO + tm]
        dc_m = dce[HALO:HALO + tm]
        dbg = dcat[HALO:HALO + tm, 0:DA] * cq
        dq = (ca_ref[0:1] * dce[HALO + 1:HALO + 1 + tm] + ca_ref[1:2] * dc_m
              + ca_ref[2:3] * dce[HALO - 1:HALO - 1 + tm])
        cg = u_ref[:, DA:2 * DA].astype(F32)
        v = u_ref[:, 2 * DA:3 * DA].astype(F32)
        for k in range(3):
            sums_ref[4 + k:5 + k, 0:DA] += _rowsum(dc_m * qe[HALO - 1 + k:HALO - 1 + k + tm])
        du_parts = [dbg, dq * v, dq * cg]
        cat_parts = [bg * cq]
        for gi, (_, left, right) in enumerate(POOL):
            c0 = gi * PG
            scale = ps_ref[0:1, c0:c0 + PG]
            dyb = dcat[:, DA + c0:DA + c0 + PG]
            dybs = (dyb * scale).astype(BF16)
            dpm = _nt(dybs, wp_ref[gi])
            inv_e = 1.0 / _pool_counts(i * tm, e, -HALO, left, right, t)
            epe[:, c0:c0 + PG] = dpm * inv_e
            s_adj = epe[HALO - right:HALO - right + tm, c0:c0 + PG]
            for j in range(-right + 1, left + 1):
                s_adj = s_adj + epe[HALO + j:HALO + j + tm, c0:c0 + PG]
            du_parts.append(s_adj - dpm[HALO:HALO + tm])
            inv_m = 1.0 / _pool_counts(i * tm, tm, 0, left, right, t)
            pm = _pool_minus_id(pe, gi, left, right, inv_m, tm).astype(BF16)
            ybpre = _nn(pm, wp_ref[gi])
            sums_ref[7:8, c0:c0 + PG] += _rowsum(dyb[HALO:HALO + tm] * ybpre)
            dwp_ref[gi] += _tn(pm, dybs[HALO:HALO + tm])
            cat_parts.append(ybpre * scale)
        du = jnp.concatenate(du_parts, axis=-1).astype(BF16)
        du_ref[...] = du
        cat_ref[...] = jnp.concatenate(cat_parts, axis=-1).astype(BF16)
        dy_ref[...] = dye[HALO:HALO + tm, :]
        dh = _nn(du, wi_s[...].reshape(2 * D, D))
        g, sc, sh = vec_ref[0:1], vec_ref[1:2], vec_ref[2:3]
        h, nrm, r = _norm_fwd(x_ref[...], g, sc, sh)
        h_ref[...] = h.astype(BF16)
        dxi_ref[...] = dxo_m + _norm_bwd(dh, nrm, r, g, sc)
        sums_ref[1:2, :] += _rowsum(dh)
        sums_ref[2:3, :] += _rowsum(dh * nrm)

        @pl.when(i == n - 1)
        def _():
            p = sums_ref[2:3, :]
            sums_ref[3:4, :] = p * (1.0 + sc)
            sums_ref[2:3, :] = p * g

    prev_u, nxt_u = _halo_specs(tm, 2 * D, t)
    prev_d, nxt_d = _halo_specs(tm, D, t)
    return _host_call(
        body, comm, name=f"mix_even_bwd_l{layer}", grid=(n,),
        in_specs=[_tile(tm, D), prev_d, nxt_d, _tile(tm, 2 * D), prev_u, nxt_u, _tile(tm, D), _tile(tm, D),
                  _full((8, D)), _full((3, DA)), _full((4, PG, PG)), _full((1, DA)), ANY],
        out_specs=[_tile(tm, D), _tile(tm, 2 * D), _tile(tm, D), _tile(tm, D), _tile(tm, D),
                   _full((16, D)), _full((4, PG, PG))],
        out_shape=[jax.ShapeDtypeStruct((t, D), F32), jax.ShapeDtypeStruct((t, 2 * D), BF16),
                   jax.ShapeDtypeStruct((t, D), BF16), jax.ShapeDtypeStruct((t, D), BF16),
                   jax.ShapeDtypeStruct((t, D), BF16), jax.ShapeDtypeStruct((16, D), F32),
                   jax.ShapeDtypeStruct((4, PG, PG), F32)],
        scratch_shapes=[pltpu.VMEM((NDEV, ROWS_OUT, D), BF16), pltpu.VMEM((NDEV, ROWS_IN, D), BF16),
                        pltpu.VMEM((e, D), BF16), pltpu.VMEM((e, DA), F32), pltpu.VMEM((e, DA), F32),
                        pltpu.VMEM((e, DA), F32), pltpu.VMEM((e, DA), F32), pltpu.VMEM((e, DA), F32),
                        pltpu.SemaphoreType.DMA((2,))],
        operands=(dxo, dxo, dxo, u, u, u, x, y, vec, ca, wp, ps, gw))


def _fill_glu(ze, part_ref, lo, rows, valid):
    a = part_ref[:, 0:D].astype(F32)
    g = part_ref[:, D:2 * D].astype(F32)
    z = a * _sig(g)
    ze[lo:lo + rows, :] = z if valid is None else jnp.where(valid, z, 0.0)


SHIFT_ROWS = 24


def _shifted_copies(dst, src, tm):
    rows = tm + SHIFT_ROWS
    for j in range(8):
        dst[j, :, :] = src[j:j + rows, :]


def _shifted(dst, shift, tm):
    lo = shift // 8 * 8
    return dst[shift % 8, lo:lo + tm, :]


def _layer_norm_parts(z2):
    mu = _lanemean(z2)
    d = z2 - mu
    rstd = lax.rsqrt(_lanemean(d * d) + LN_EPS)
    return d * rstd, rstd


def _mix_odd_fwd(u, x, vec, wdw, sm, gw, layer, tm):
    t = x.shape[0]
    n = t // tm
    e = tm + 2 * HALO

    def body(u_ref, up_ref, un_ref, x_ref, vec_ref, wdw_ref, sm_ref, g_ref, xo_ref, y_ref, z2_ref, w_s, ze, zsh, sems):
        i = pl.program_id(0)

        @pl.when(i == 0)
        def _():
            _load_weights(g_ref, [(OFF_OUT, ROWS_OUT, w_s)], sems)

        _fill_glu(ze, up_ref, 0, HALO, i > 0)
        _fill_glu(ze, u_ref, HALO, tm, None)
        _fill_glu(ze, un_ref, HALO + tm, HALO, i < n - 1)
        _shifted_copies(zsh, ze, tm)
        z2 = sm_ref[0:1] + wdw_ref[0:1] * _shifted(zsh, 1, tm)
        for k in range(1, CONF_K):
            z2 = z2 + wdw_ref[k:k + 1] * _shifted(zsh, 1 + k, tm)
        z2_ref[...] = z2.astype(BF16)
        zn, _ = _layer_norm_parts(z2)
        lo = zn * sm_ref[1:2] + sm_ref[2:3]
        z3 = lo * _sig(lo)
        y = _nn(z3.astype(BF16), w_s[...].reshape(D, D)) + sm_ref[3:4]
        y_ref[...] = y.astype(BF16)
        xo_ref[...] = x_ref[...] + vec_ref[3:4] * y

    prev, nxt = _halo_specs(tm, 2 * D, t)
    return pl.pallas_call(
        body, name=f"mix_odd_fwd_l{layer}", grid=(n,),
        in_specs=[_tile(tm, 2 * D), prev, nxt, _tile(tm, D), _full((8, D)), _full((32, D)), _full((8, D)), ANY],
        out_specs=[_tile(tm, D), _tile(tm, D), _tile(tm, D)],
        out_shape=[jax.ShapeDtypeStruct((t, D), F32), jax.ShapeDtypeStruct((t, D), BF16),
                   jax.ShapeDtypeStruct((t, D), BF16)],
        scratch_shapes=[pltpu.VMEM((NDEV, ROWS_OUT, D), BF16), pltpu.VMEM((e, D), F32),
                        pltpu.VMEM((8, tm + SHIFT_ROWS, D), F32), pltpu.SemaphoreType.DMA((1,))],
        compiler_params=_cparams("arbitrary"),
    )(u, u, u, x, vec, wdw, sm, gw)


def _mix_odd_bwd1(dxo, y, z2, vec, sm, gw, layer, tm):
    t = dxo.shape[0]
    n = t // tm

    def body(dxo_ref, y_ref, z2_ref, vec_ref, sm_ref, g_ref, dy_ref, z3_ref, dz2_ref, sums_ref, w_s, sems):
        i = pl.program_id(0)

        @pl.when(i == 0)
        def _():
            _load_weights(g_ref, [(OFF_OUT, ROWS_OUT, w_s)], sems)
            sums_ref[...] = jnp.zeros_like(sums_ref)

        dxo_m = dxo_ref[...]
        dy = vec_ref[3:4] * dxo_m
        dyb = dy.astype(BF16)
        dy_ref[...] = dyb
        sums_ref[0:1, :] += _rowsum(dxo_m * y_ref[...].astype(F32))
        sums_ref[4:5, :] += _rowsum(dy)
        dz3 = _nt(dyb, w_s[...].reshape(D, D))
        zn, rstd = _layer_norm_parts(z2_ref[...].astype(F32))
        lo = zn * sm_ref[1:2] + sm_ref[2:3]
        sg = _sig(lo)
        z3_ref[...] = (lo * sg).astype(BF16)
        dlo = dz3 * (sg * (1.0 + lo * (1.0 - sg)))
        sums_ref[5:6, :] += _rowsum(dlo * zn)
        sums_ref[6:7, :] += _rowsum(dlo)
        dzn = dlo * sm_ref[1:2]
        dz2 = rstd * (dzn - _lanemean(dzn) - zn * _lanemean(dzn * zn))
        sums_ref[7:8, :] += _rowsum(dz2)
        dz2_ref[...] = dz2.astype(BF16)

    return pl.pallas_call(
        body, name=f"mix_odd_bwd1_l{layer}", grid=(n,),
        in_specs=[_tile(tm, D), _tile(tm, D), _tile(tm, D), _full((8, D)), _full((8, D)), ANY],
        out_specs=[_tile(tm, D), _tile(tm, D), _tile(tm, D), _full((16, D))],
        out_shape=[jax.ShapeDtypeStruct((t, D), BF16)] * 3 + [jax.ShapeDtypeStruct((16, D), F32)],
        scratch_shapes=[pltpu.VMEM((NDEV, ROWS_OUT, D), BF16), pltpu.SemaphoreType.DMA((1,))],
        compiler_params=_cparams("arbitrary"),
    )(dxo, y, z2, vec, sm, gw)


def _mix_odd_bwd2(dz2, u, x, dxo, vec, wdw, gw, layer, tm, comm=None):
    t = x.shape[0]
    n = t // tm
    e = tm + 2 * HALO

    def body(dz_ref, dzp_ref, dzn_ref, u_ref, x_ref, dxo_ref, vec_ref, wdw_ref, g_ref,
             dxi_ref, du_ref, h_ref, sums_ref, dw_ref, w_s, de, zsh, sems):
        i = pl.program_id(0)

        @pl.when(i == 0)
        def _():
            _load_weights(g_ref, [(OFF_IN, ROWS_IN, w_s)], sems)
            sums_ref[...] = jnp.zeros_like(sums_ref)
            dw_ref[...] = jnp.zeros_like(dw_ref)

        de[0:HALO, :] = jnp.where(i > 0, dzp_ref[...].astype(F32), 0.0)
        de[HALO:HALO + tm, :] = dz_ref[...].astype(F32)
        de[HALO + tm:e, :] = jnp.where(i < n - 1, dzn_ref[...].astype(F32), 0.0)
        a = u_ref[:, 0:D].astype(F32)
        gg = u_ref[:, D:2 * D].astype(F32)
        sg = _sig(gg)
        z = a * sg
        _shifted_copies(zsh, de, tm)
        dz = None
        for k in range(CONF_K):
            shifted = _shifted(zsh, CONF_K - k, tm)
            term = wdw_ref[k:k + 1] * shifted
            dz = term if dz is None else dz + term
            dw_ref[k:k + 1, :] += _rowsum(z * shifted)
        da = dz * sg
        dg = dz * a * (sg * (1.0 - sg))
        sums_ref[8:9, :] += _rowsum(da)
        sums_ref[9:10, :] += _rowsum(dg)
        du = jnp.concatenate([da, dg], axis=-1).astype(BF16)
        du_ref[...] = du
        dh = _nn(du, w_s[...].reshape(2 * D, D))
        g, sc, sh = vec_ref[0:1], vec_ref[1:2], vec_ref[2:3]
        h, nrm, r = _norm_fwd(x_ref[...], g, sc, sh)
        h_ref[...] = h.astype(BF16)
        dxi_ref[...] = dxo_ref[...] + _norm_bwd(dh, nrm, r, g, sc)
        sums_ref[1:2, :] += _rowsum(dh)
        sums_ref[2:3, :] += _rowsum(dh * nrm)

        @pl.when(i == n - 1)
        def _():
            p = sums_ref[2:3, :]
            sums_ref[3:4, :] = p * (1.0 + sc)
            sums_ref[2:3, :] = p * g

    prev_d, nxt_d = _halo_specs(tm, D, t)
    return _host_call(
        body, comm, name=f"mix_odd_bwd2_l{layer}", grid=(n,),
        in_specs=[_tile(tm, D), prev_d, nxt_d, _tile(tm, 2 * D), _tile(tm, D), _tile(tm, D),
                  _full((8, D)), _full((32, D)), ANY],
        out_specs=[_tile(tm, D), _tile(tm, 2 * D), _tile(tm, D), _full((16, D)), _full((32, D))],
        out_shape=[jax.ShapeDtypeStruct((t, D), F32), jax.ShapeDtypeStruct((t, 2 * D), BF16),
                   jax.ShapeDtypeStruct((t, D), BF16), jax.ShapeDtypeStruct((16, D), F32),
                   jax.ShapeDtypeStruct((32, D), F32)],
        scratch_shapes=[pltpu.VMEM((NDEV, ROWS_IN, D), BF16), pltpu.VMEM((e, D), F32),
                        pltpu.VMEM((8, tm + SHIFT_ROWS, D), F32),
                        pltpu.SemaphoreType.DMA((1,))],
        operands=(dz2, dz2, dz2, u, x, dxo, vec, wdw, gw))


FCH = FF // 2


def _ffn_fwd(x, vec, gw, layer, tm, comm=None):
    t = x.shape[0]

    def body(x_ref, vec_ref, g_ref, xo_ref, a_ref, b_ref, y_ref, wg_s, wu_s, wd_s, sems):
        @pl.when(pl.program_id(0) == 0)
        def _():
            _load_weights(g_ref, [(OFF_G, FS, wg_s), (OFF_U, FS, wu_s), (OFF_D, FS, wd_s)], sems)

        xv = x_ref[...]
        h, _, _ = _norm_fwd(xv, vec_ref[0:1], vec_ref[1:2], vec_ref[2:3])
        hb = h.astype(BF16)
        y = jnp.zeros((tm, D), F32)
        for ch in range(2):
            a = _nt(hb, wg_s[4 * ch:4 * ch + 4].reshape(FCH, D))
            b = _nt(hb, wu_s[4 * ch:4 * ch + 4].reshape(FCH, D))
            a_ref[:, ch * FCH:(ch + 1) * FCH] = a.astype(BF16)
            b_ref[:, ch * FCH:(ch + 1) * FCH] = b.astype(BF16)
            s = (a * _sig(a)) * b
            y = y + _nn(s.astype(BF16), wd_s[4 * ch:4 * ch + 4].reshape(FCH, D))
        y_ref[...] = y.astype(BF16)
        xo_ref[...] = xv + vec_ref[3:4] * y

    wsc = pltpu.VMEM((NDEV, FS, D), BF16)
    return _host_call(
        body, comm, name=f"ffn_fwd_l{layer}", grid=(t // tm,),
        in_specs=[_tile(tm, D), _full((8, D)), ANY],
        out_specs=[_tile(tm, D), _tile(tm, FF), _tile(tm, FF), _tile(tm, D)],
        out_shape=[jax.ShapeDtypeStruct((t, D), F32), jax.ShapeDtypeStruct((t, FF), BF16),
                   jax.ShapeDtypeStruct((t, FF), BF16), jax.ShapeDtypeStruct((t, D), BF16)],
        scratch_shapes=[wsc, wsc, wsc, pltpu.SemaphoreType.DMA((3,))],
        operands=(x, vec, gw))


def _ffn_bwd(dxo, x, y, a, b, vec, gw, layer, tm, comm=None):
    t = x.shape[0]
    n = t // tm

    def body(dxo_ref, x_ref, y_ref, a_ref, b_ref, vec_ref, g_ref,
             dxi_ref, h_ref, dy_ref, s_ref, da_ref, db_ref, sums_ref, wg_s, wu_s, wd_s, sems):
        i = pl.program_id(0)

        @pl.when(i == 0)
        def _():
            _load_weight_rows(g_ref, [(OFF_G, FS, wg_s), (OFF_U, FS, wu_s), (OFF_D, FS, wd_s)], sems)
            sums_ref[...] = jnp.zeros_like(sums_ref)

        dxo_m = dxo_ref[...]
        sums_ref[0:1, :] += _rowsum(dxo_m * y_ref[...].astype(F32))
        dyb = (vec_ref[3:4] * dxo_m).astype(BF16)
        dy_ref[...] = dyb
        for ch in range(FF // MXU_N):
            cols = slice(ch * MXU_N, (ch + 1) * MXU_N)
            ds = _nt(dyb, wd_s[cols, :]).astype(BF16)
            av = a_ref[:, cols]
            bv = b_ref[:, cols]
            sg = _sig(av)
            sl = av * sg
            s_ref[:, cols] = sl * bv
            db_ref[:, cols] = ds * sl
            da_ref[:, cols] = (ds * bv) * (sg * (1.0 + av * (1.0 - sg)))
        dh = _nn(da_ref[...], wg_s[...]) + _nn(db_ref[...], wu_s[...])
        g, sc, sh = vec_ref[0:1], vec_ref[1:2], vec_ref[2:3]
        h, nrm, r = _norm_fwd(x_ref[...], g, sc, sh)
        h_ref[...] = h.astype(BF16)
        dxi_ref[...] = dxo_m + _norm_bwd(dh, nrm, r, g, sc)
        sums_ref[1:2, :] += _rowsum(dh)
        sums_ref[2:3, :] += _rowsum(dh * nrm)

        @pl.when(i == n - 1)
        def _():
            p = sums_ref[2:3, :]
            sums_ref[3:4, :] = p * (1.0 + sc)
            sums_ref[2:3, :] = p * g

    wsc = pltpu.VMEM((FF, D), BF16)
    big, small = jax.ShapeDtypeStruct((t, FF), BF16), jax.ShapeDtypeStruct((t, D), BF16)
    return _host_call(
        body, comm, name=f"ffn_bwd_l{layer}", grid=(n,),
        in_specs=[_tile(tm, D), _tile(tm, D), _tile(tm, D), _tile(tm, FF), _tile(tm, FF), _full((8, D)), ANY],
        out_specs=[_tile(tm, D), _tile(tm, D), _tile(tm, D), _tile(tm, FF), _tile(tm, FF), _tile(tm, FF),
                   _full((8, D))],
        out_shape=[jax.ShapeDtypeStruct((t, D), F32), small, small, big, big, big, jax.ShapeDtypeStruct((8, D), F32)],
        scratch_shapes=[wsc, wsc, wsc, pltpu.SemaphoreType.DMA((3 * NDEV,))],
        operands=(dxo, x, y, a, b, vec, gw))


def _wgrad(lhs, rhs, name, tk):
    t, m = lhs.shape
    n = t // tk

    def body(l_ref, r_ref, o_ref, acc):
        i = pl.program_id(0)

        @pl.when(i == 0)
        def _():
            acc[...] = jnp.zeros_like(acc)

        acc[...] += _tn(l_ref[...], r_ref[...])

        @pl.when(i == n - 1)
        def _():
            o_ref[...] = acc[...].astype(BF16)

    return pl.pallas_call(
        body, name=name, grid=(n,),
        in_specs=[_tile(tk, m), _tile(tk, D)], out_specs=_full((m, D)),
        out_shape=jax.ShapeDtypeStruct((m, D), BF16),
        scratch_shapes=[pltpu.VMEM((m, D), F32)],
        compiler_params=_cparams("arbitrary"),
    )(lhs, rhs)


def _final(x, tgt, gf, tm):
    t = x.shape[0]

    def body(x_ref, t_ref, g_ref, dx_ref, sums_ref):
        @pl.when(pl.program_id(0) == 0)
        def _():
            sums_ref[...] = jnp.zeros_like(sums_ref)

        xv = x_ref[...]
        r = lax.rsqrt(_lanemean(xv * xv) + RMS_EPS)
        nrm = xv * r
        g = g_ref[...]
        err = nrm * g - t_ref[...]
        sums_ref[1:2, :] += _rowsum(err * err) * (0.5 / D)
        dout = err * (1.0 / D)
        sums_ref[0:1, :] += _rowsum(dout * nrm)
        dn = dout * g
        dx_ref[...] = r * (dn - nrm * _lanemean(dn * nrm))

    return pl.pallas_call(
        body, name="loss_head", grid=(t // tm,),
        in_specs=[_tile(tm, D), _tile(tm, D), _full((1, D))],
        out_specs=[_tile(tm, D), _full((8, D))],
        out_shape=[jax.ShapeDtypeStruct((t, D), F32), jax.ShapeDtypeStruct((8, D), F32)],
        compiler_params=_cparams("arbitrary"),
    )(x, tgt, gf)


ADAM_ROWS = LROWS // 5


def _adam_big(recv, w, m, v):
    def body(r_ref, w_ref, m_ref, v_ref, g_ref, d_ref, mo_ref, vo_ref):
        g = r_ref[0, 0].astype(F32)
        for s in range(1, NDEV):
            g = g + r_ref[s, 0].astype(F32)
        delta, m2, v2 = _adam(w_ref[0], g, m_ref[0], v_ref[0])
        g_ref[0], d_ref[0], mo_ref[0], vo_ref[0] = g, delta, m2, v2

    blk = pl.BlockSpec((1, ADAM_ROWS, D), lambda l, j: (l, j, 0))
    sds = jax.ShapeDtypeStruct(w.shape, F32)
    return pl.pallas_call(
        body, name="adam_big", grid=(DEPTH, LROWS // ADAM_ROWS),
        in_specs=[pl.BlockSpec((NDEV, 1, ADAM_ROWS, D), lambda l, j: (0, l, j, 0)), blk, blk, blk],
        out_specs=[blk] * 4, out_shape=[sds] * 4,
        compiler_params=_cparams("arbitrary", "arbitrary"),
    )(recv, w, m, v)


def _sum_small(gathered, rows):
    def body(g_ref, o_ref):
        acc = g_ref[0:rows, :]
        for s in range(1, NDEV):
            acc = acc + g_ref[s * rows:(s + 1) * rows, :]
        o_ref[...] = acc

    return pl.pallas_call(
        body, name="sum_small",
        in_specs=[pl.BlockSpec(memory_space=pltpu.VMEM)], out_specs=pl.BlockSpec(memory_space=pltpu.VMEM),
        out_shape=jax.ShapeDtypeStruct((rows, D), F32),
        compiler_params=pltpu.CompilerParams(vmem_limit_bytes=VMEM_LIMIT),
    )(gathered)


def _adam_small(params):
    k = len(params)

    def body(*refs):
        ins, outs = refs[:4 * k], refs[4 * k:]
        for j in range(k):
            w_ref, g_ref, m_ref, v_ref = ins[4 * j:4 * j + 4]
            delta, m2, v2 = _adam(w_ref[...], g_ref[...], m_ref[...], v_ref[...])
            outs[3 * j][...], outs[3 * j + 1][...], outs[3 * j + 2][...] = delta, m2, v2

    flat = [a for p in params for a in p]
    shapes = [jax.ShapeDtypeStruct(p[0].shape, F32) for p in params for _ in range(3)]
    vm = pl.BlockSpec(memory_space=pltpu.VMEM)
    res = pl.pallas_call(
        body, name="adam_small", in_specs=[vm] * len(flat), out_specs=[vm] * len(shapes), out_shape=shapes,
        compiler_params=pltpu.CompilerParams(vmem_limit_bytes=VMEM_LIMIT),
    )(*flat)
    return [tuple(res[3 * j:3 * j + 3]) for j in range(k)]


def _pack(ab_in, ab_out, pw1, pw2, wg, wu, wd):
    ins = jnp.swapaxes(jnp.stack([ab_in[0], pw1[0], ab_in[1], pw1[1]]), 1, 2)
    outs = jnp.stack([ab_out[0], pw2[0], ab_out[1], pw2[1]])
    return jnp.concatenate([ins, outs, jnp.swapaxes(wg, 1, 2), jnp.swapaxes(wu, 1, 2), wd], axis=1)


def _unpack(p):
    ins = jnp.swapaxes(p[:, OFF_IN:OFF_OUT], 1, 2)
    outs = p[:, OFF_OUT:OFF_G]
    return (ins[0::2], outs[0::2], ins[1::2], outs[1::2], jnp.swapaxes(p[:, OFF_G:OFF_U], 1, 2),
            jnp.swapaxes(p[:, OFF_U:OFF_D], 1, 2), p[:, OFF_D:LROWS])


def _unshard(flat, lead, per):
    k = len(lead)
    a = flat.reshape((NDEV,) + tuple(lead) + (per,))
    a = jnp.transpose(a, tuple(range(1, k + 1)) + (0, k + 1))
    return a.reshape(tuple(lead) + (NDEV * per,))


def _rows_of(a):
    f = a.reshape(-1)
    pad = (-f.shape[0]) % D
    if pad:
        f = jnp.concatenate([f, jnp.zeros((pad,), f.dtype)])
    return f.reshape(-1, D)


def _pad_rows(a, rows):
    return jnp.concatenate([a, jnp.zeros((rows - a.shape[0],) + a.shape[1:], a.dtype)], axis=0)


def kernel(x, c, norm_mix_g, norm_ffn_g, w_mod, b_mod, ab_w_in, ab_conv, ab_w_pool, ab_pool_scale, ab_w_out, cf_w_pw1, cf_b_pw1, cf_w_dw, cf_b_dw, cf_ln_g, cf_ln_b, cf_w_pw2, cf_b_pw2, ffn_w_gate, ffn_w_up, ffn_w_down, final_norm_g, loss_target, m_norm_mix_g, m_norm_ffn_g, m_w_mod, m_b_mod, m_ab_w_in, m_ab_conv, m_ab_w_pool, m_ab_pool_scale, m_ab_w_out, m_cf_w_pw1, m_cf_b_pw1, m_cf_w_dw, m_cf_b_dw, m_cf_ln_g, m_cf_ln_b, m_cf_w_pw2, m_cf_b_pw2, m_ffn_w_gate, m_ffn_w_up, m_ffn_w_down, m_final_norm_g, v_norm_mix_g, v_norm_ffn_g, v_w_mod, v_b_mod, v_ab_w_in, v_ab_conv, v_ab_w_pool, v_ab_pool_scale, v_ab_w_out, v_cf_w_pw1, v_cf_b_pw1, v_cf_w_dw, v_cf_b_dw, v_cf_ln_g, v_cf_ln_b, v_cf_w_pw2, v_cf_b_pw2, v_ffn_w_gate, v_ffn_w_up, v_ffn_w_down, v_final_norm_g):
    t = x.shape[1]
    tm = 512 if t % 512 == 0 else t // 2
    tk = 512 if t % 512 == 0 else t // 2
    tmo = tm
    tmb = tm // 2
    me = 4 * lax.axis_index("x") + 2 * lax.axis_index("y") + lax.axis_index("c")
    xs, tgt = x[0], loss_target[0]

    sharded = [ab_conv, cf_b_pw1, cf_w_dw, cf_b_dw, cf_ln_g, cf_ln_b, cf_b_pw2]
    flat = jnp.concatenate([a.reshape(-1) for a in sharded])
    n_flat = flat.shape[0]
    g1 = _gather_small(jnp.concatenate([_pad_rows(c, 8), _pad_rows(_rows_of(flat), 16)], axis=0), "gather_cond")
    g1 = g1.reshape(NDEV, 24, D)
    c_all = g1[:, 0, :]
    flat_all = g1[:, 8:, :].reshape(NDEV, -1)[:, :n_flat]
    full, o = [], 0
    for a in sharded:
        lead, per = a.shape[:-1], a.shape[-1]
        size = a.size
        full.append(_unshard(flat_all[:, o:o + size], lead, per))
        o += size
    ab_conv_f, b_pw1_f, w_dw_f, b_dw_f, ln_g_f, ln_b_f, b_pw2_f = full

    b_sl = lax.dynamic_slice_in_dim(b_mod, me * MODW, MODW, axis=1).reshape(DEPTH, 1, MODW)
    mod_part, c_act = _mod_fwd(c_all, w_mod, b_sl)
    g2 = _gather_small(mod_part, "gather_mod").reshape(NDEV, NDEV, DEPTH, MODW)
    mod = jnp.transpose(lax.dynamic_index_in_dim(g2, me, axis=1, keepdims=False), (1, 0, 2)).reshape(DEPTH, N_MOD, D)
    zeros4 = jnp.zeros((4, D), F32)

    def vec_of(g, layer, k):
        return jnp.concatenate([g[layer][None], mod[layer, k + 1][None], mod[layer, k][None],
                                mod[layer, k + 2][None], zeros4], axis=0)

    vmix = [vec_of(norm_mix_g, l, 0) for l in range(DEPTH)]
    vffn = [vec_of(norm_ffn_g, l, 3) for l in range(DEPTH)]

    w_pack = _pack(ab_w_in, ab_w_out, cf_w_pw1, cf_w_pw2, ffn_w_gate, ffn_w_up, ffn_w_down)
    p16 = w_pack.astype(BF16)
    gw = [_gather_weights(p16[0])]

    wp16 = ab_w_pool.astype(BF16)
    wdw32 = [_pad_rows(w_dw_f[i], 32) for i in range(2)]
    sm_odd = [jnp.concatenate([b_dw_f[i][None], ln_g_f[i][None], ln_b_f[i][None], b_pw2_f[i][None], zeros4], axis=0)
              for i in range(2)]
    zero_bias = jnp.zeros((1, 2 * D), F32)

    saved = []
    xc = xs
    for l in range(DEPTH):
        i = l // 2
        if l % 2 == 0:
            u = _inproj(xc, vmix[l], zero_bias, gw[l], l, tm)
            x_mid, y_mix = _mix_even_fwd(u, xc, vmix[l], ab_conv_f[i], wp16[i], ab_pool_scale[i][None], gw[l], l, tm)
            z2 = None
        else:
            u = _inproj(xc, vmix[l], b_pw1_f[i][None], gw[l], l, tm)
            x_mid, y_mix, z2 = _mix_odd_fwd(u, xc, vmix[l], wdw32[i], sm_odd[i], gw[l], l, tmo)
        if l + 1 < DEPTH:
            x_out, a, b, y_ffn, g_next = _ffn_fwd(x_mid, vffn[l], gw[l], l, tm, _gather_comm(p16[l + 1]))
            gw.append(g_next)
        else:
            x_out, a, b, y_ffn = _ffn_fwd(x_mid, vffn[l], gw[l], l, tm)
        saved.append((xc, u, y_mix, z2, x_mid, a, b, y_ffn))
        xc = x_out

    dx, fsum = _final(xc, tgt, final_norm_g[None], tm)
    loss = lax.psum(jnp.sum(fsum[1]), ("x", "y", "c"))
    d_final_g = fsum[0]

    recv = _empty_recv()
    late_specs = [(OFF_U, FS), (OFF_IN, ROWS_IN), (OFF_OUT, ROWS_OUT)]
    pending = None
    dmod = [None] * DEPTH
    d_mix_g, d_ffn_g = [None] * DEPTH, [None] * DEPTH
    d_conv, d_pool, d_pscale = [None] * 2, [None] * 2, [None] * 2
    d_bpw1, d_wdw, d_bdw, d_lng, d_lnb, d_bpw2 = ([None] * 2 for _ in range(6))
    for l in reversed(range(DEPTH)):
        i = l // 2
        x_in, u, y_mix, z2, x_mid, a, b, y_ffn = saved[l]
        if pending is None:
            dx_mid, h2, dy, s, da, db, s_f = _ffn_bwd(dx, x_mid, y_ffn, a, b, vffn[l], gw[l], l, tmb)
        else:
            dx_mid, h2, dy, s, da, db, s_f, recv = _ffn_bwd(dx, x_mid, y_ffn, a, b, vffn[l], gw[l], l, tmb,
                                                            _scatter_comm(pending, late_specs, recv, l + 1))
        g_down = _wgrad(s, dy, f"wgrad_down_l{l}", tk)
        g_gate = _wgrad(da, h2, f"wgrad_gate_l{l}", tk)
        gu_comm = _scatter_comm([g_down, g_gate], [(OFF_D, FS), (OFF_G, FS)], recv, l)
        g_up = _wgrad(db, h2, f"wgrad_up_l{l}", tk)
        d_ffn_g[l] = s_f[3]
        mod_ffn = [s_f[1], s_f[2], s_f[0]]
        if l % 2 == 0:
            dx, du, h, cat, dym, s_m, dwp, recv = _mix_even_bwd(dx_mid, u, x_in, y_mix, vmix[l], ab_conv_f[i], wp16[i],
                                                                ab_pool_scale[i][None], gw[l], l, tm, gu_comm)
            g_out = _wgrad(cat, dym, f"wgrad_out_l{l}", tk)
            d_conv[i], d_pool[i], d_pscale[i] = s_m[4:7, :DA], dwp, s_m[7, :DA]
            mod_mix = [s_m[1], s_m[2], s_m[0]]
            d_mix_g[l] = s_m[3]
        else:
            dym, z3, dz2, s_1 = _mix_odd_bwd1(dx_mid, y_mix, z2, vmix[l], sm_odd[i], gw[l], l, tm)
            dx, du, h, s_2, dwdw, recv = _mix_odd_bwd2(dz2, u, x_in, dx_mid, vmix[l], wdw32[i], gw[l], l, tmo, gu_comm)
            g_out = _wgrad(z3, dym, f"wgrad_out_l{l}", tk)
            d_bpw1[i], d_wdw[i], d_bdw[i] = s_2[8:10].reshape(2 * D), dwdw[:CONF_K], s_1[7]
            d_lng[i], d_lnb[i], d_bpw2[i] = s_1[5], s_1[6], s_1[4]
            mod_mix = [s_2[1], s_2[2], s_1[0]]
            d_mix_g[l] = s_2[3]
        dmod[l] = jnp.stack(mod_mix + mod_ffn)
        pending = [g_up, _wgrad(du, h, f"wgrad_in_l{l}", tk), g_out]
    grad_x = dx[None]

    small = [jnp.stack(dmod), jnp.stack(d_mix_g), jnp.stack(d_ffn_g), d_final_g, jnp.stack(d_pool),
             jnp.stack(d_pscale), jnp.stack(d_conv), jnp.stack(d_bpw1), jnp.stack(d_wdw), jnp.stack(d_bdw),
             jnp.stack(d_lng), jnp.stack(d_lnb), jnp.stack(d_bpw2)]
    small_rows = [_rows_of(a) for a in small]
    n_rows = sum(a.shape[0] for a in small_rows)
    pad_rows = -(-n_rows // 8) * 8
    recv, g3 = _tail_comm(pending, late_specs, recv, 0, _pad_rows(jnp.concatenate(small_rows, axis=0), pad_rows))
    g3 = g3.reshape(NDEV * pad_rows, D)
    summed = _sum_small(g3, pad_rows)

    m_pack = _pack(m_ab_w_in, m_ab_w_out, m_cf_w_pw1, m_cf_w_pw2, m_ffn_w_gate, m_ffn_w_up, m_ffn_w_down)
    v_pack = _pack(v_ab_w_in, v_ab_w_out, v_cf_w_pw1, v_cf_w_pw2, v_ffn_w_gate, v_ffn_w_up, v_ffn_w_down)
    big = [_unpack(p) for p in _adam_big(recv, w_pack, m_pack, v_pack)]

    outs, o = [], 0
    for a, r in zip(small, small_rows):
        outs.append(summed[o:o + r.shape[0]].reshape(-1)[:a.size].reshape(a.shape))
        o += r.shape[0]
    (g_bmod, g_mix_g, g_ffn_g, g_final, g_pool, g_pscale, g_conv, g_bpw1, g_wdw, g_bdw, g_lng, g_lnb, g_bpw2) = outs
    g_bmod = g_bmod.reshape(DEPTH, N_MOD * D)

    def my_shard(a):
        per = a.shape[-1] // NDEV
        return lax.dynamic_slice_in_dim(a, me * per, per, axis=a.ndim - 1)

    g_conv, g_bpw1, g_wdw, g_bdw, g_lng, g_lnb, g_bpw2 = [
        my_shard(a) for a in (g_conv, g_bpw1, g_wdw, g_bdw, g_lng, g_lnb, g_bpw2)]

    dmod_all = g3.reshape(NDEV, pad_rows, D)[:, :DEPTH * N_MOD, :].reshape(NDEV, DEPTH, N_MOD * D)
    dmod_mine = jnp.transpose(lax.dynamic_slice_in_dim(dmod_all, me * MODW, MODW, axis=2), (1, 0, 2))
    g_wmod, d_wmod, nm_wmod, nv_wmod = _mod_bwd_adam(c_act.T, dmod_mine, w_mod, m_w_mod, v_w_mod)

    small_params = [
        (norm_mix_g, g_mix_g, m_norm_mix_g, v_norm_mix_g), (norm_ffn_g, g_ffn_g, m_norm_ffn_g, v_norm_ffn_g),
        (b_mod, g_bmod, m_b_mod, v_b_mod), (ab_conv, g_conv, m_ab_conv, v_ab_conv),
        (ab_w_pool, g_pool, m_ab_w_pool, v_ab_w_pool), (ab_pool_scale, g_pscale, m_ab_pool_scale, v_ab_pool_scale),
        (cf_b_pw1, g_bpw1, m_cf_b_pw1, v_cf_b_pw1), (cf_w_dw, g_wdw, m_cf_w_dw, v_cf_w_dw),
        (cf_b_dw, g_bdw, m_cf_b_dw, v_cf_b_dw), (cf_ln_g, g_lng, m_cf_ln_g, v_cf_ln_g),
        (cf_ln_b, g_lnb, m_cf_ln_b, v_cf_ln_b), (cf_b_pw2, g_bpw2, m_cf_b_pw2, v_cf_b_pw2),
        (final_norm_g, g_final, m_final_norm_g, v_final_norm_g)]

    def two_d(a):
        return a.reshape(-1, a.shape[-1])

    upd = _adam_small([tuple(two_d(a) for a in p) for p in small_params])
    upd = [tuple(r.reshape(p[0].shape) for r in u) for u, p in zip(upd, small_params)]
    (s_mix, s_ffn, s_bmod, s_conv, s_pool, s_pscale, s_bpw1, s_wdw, s_bdw, s_lng, s_lnb, s_bpw2, s_final) = upd
    small_g = [p[1] for p in small_params]
    (q_mix, q_ffn, q_bmod, q_conv, q_pool, q_pscale, q_bpw1, q_wdw, q_bdw, q_lng, q_lnb, q_bpw2, q_final) = small_g

    def ordered(k):
        ab_in, ab_out, pw1, pw2, wg, wu, wd = big[k]
        if k == 0:
            sm = dict(mix=q_mix, ffn=q_ffn, bmod=q_bmod, conv=q_conv, pool=q_pool, pscale=q_pscale, bpw1=q_bpw1,
                      wdw=q_wdw, bdw=q_bdw, lng=q_lng, lnb=q_lnb, bpw2=q_bpw2, final=q_final)
            wmod = g_wmod
        else:
            j = k - 1
            sm = dict(mix=s_mix[j], ffn=s_ffn[j], bmod=s_bmod[j], conv=s_conv[j], pool=s_pool[j], pscale=s_pscale[j],
                      bpw1=s_bpw1[j], wdw=s_wdw[j], bdw=s_bdw[j], lng=s_lng[j], lnb=s_lnb[j], bpw2=s_bpw2[j],
                      final=s_final[j])
            wmod = (d_wmod, nm_wmod, nv_wmod)[j]
        return [sm["mix"], sm["ffn"], wmod, sm["bmod"], ab_in, sm["conv"], sm["pool"], sm["pscale"], ab_out,
                pw1, sm["bpw1"], sm["wdw"], sm["bdw"], sm["lng"], sm["lnb"], pw2, sm["bpw2"], wg, wu, wd, sm["final"]]

    return (loss, grad_x, *ordered(0), *ordered(1), *ordered(2), *ordered(3))
```

```python
import functools

import jax
import jax.numpy as jnp
from jax import lax
from jax.experimental import pallas as pl
from jax.experimental.pallas import tpu as pltpu

F32 = jnp.float32
BF16 = jnp.bfloat16
MESH = pl.DeviceIdType.MESH

NDEV = 8
DEPTH = 4
D = 1024
FF = 2816
FS = FF // NDEV
DA = 512
PG = 128
POOL = ((2, 1, 0), (4, 2, 1), (8, 4, 3), (16, 8, 7))
CONF_K = 31
CONF_L = 15
N_MOD = 6
MODW = N_MOD * D // NDEV
RMS_EPS = 1e-6
LN_EPS = 1e-5

ROWS_IN, ROWS_OUT = 2 * D // NDEV, D // NDEV
OFF_IN, OFF_OUT = 0, ROWS_IN
OFF_G = OFF_OUT + ROWS_OUT
OFF_U = OFF_G + FS
OFF_D = OFF_U + FS
LROWS = OFF_D + FS

MXU_N = 256
HALO = 16
VMEM_LIMIT = 60 * 1024 * 1024

ADAM_LR, ADAM_B1, ADAM_B2, ADAM_EPS, ADAM_WD, ADAM_STEP = 1e-3, 0.9, 0.999, 1e-8, 0.01, 10
ADAM_C1 = 1.0 / (1.0 - ADAM_B1 ** ADAM_STEP)
ADAM_C2 = 1.0 / (1.0 - ADAM_B2 ** ADAM_STEP)


def _nn(a, b):
    return jnp.dot(a, b, preferred_element_type=F32)


def _nt(a, b):
    return lax.dot_general(a, b, (((1,), (1,)), ((), ())), preferred_element_type=F32)


def _tn(a, b):
    return lax.dot_general(a, b, (((0,), (0,)), ((), ())), preferred_element_type=F32)


def _sig(x):
    return 1.0 / (1.0 + jnp.exp(-x))


def _rowsum(x):
    return jnp.sum(x, axis=0, keepdims=True)


def _lanemean(x):
    return jnp.mean(x, axis=-1, keepdims=True)


def _norm_fwd(x, g, sc, sh):
    r = lax.rsqrt(_lanemean(x * x) + RMS_EPS)
    n = x * r
    return n * (g * (1.0 + sc)) + sh, n, r


def _norm_bwd(dh, n, r, g, sc):
    dn = dh * (g * (1.0 + sc))
    return r * (dn - n * _lanemean(dn * n))


def _adam(w, g, m, v):
    m2 = ADAM_B1 * m + (1.0 - ADAM_B1) * g
    v2 = ADAM_B2 * v + (1.0 - ADAM_B2) * (g * g)
    delta = -ADAM_LR * ((m2 * ADAM_C1) / (jnp.sqrt(v2 * ADAM_C2) + ADAM_EPS) + ADAM_WD * w)
    return delta, m2, v2


def _load_weights(g_ref, specs, sems):
    cps = [pltpu.make_async_copy(g_ref.at[:, pl.ds(off, rows), :], dst, sems.at[k])
           for k, (off, rows, dst) in enumerate(specs)]
    for cp in cps:
        cp.start()
    for cp in cps:
        cp.wait()


def _load_weight_rows(specs, sems):
    cps = [pltpu.make_async_copy(g_ref.at[d, pl.ds(off, rows), :], dst.at[pl.ds(d * rows, rows), :],
                                 sems.at[NDEV * k + d])
           for k, (g_ref, off, rows, dst) in enumerate(specs) for d in range(NDEV)]
    for cp in cps:
        cp.start()
    for cp in cps:
        cp.wait()


def _cparams(*sem):
    return pltpu.CompilerParams(dimension_semantics=sem if sem else None, vmem_limit_bytes=VMEM_LIMIT)


def _tile(tm, w):
    return pl.BlockSpec((tm, w), lambda i: (i, 0))


def _full(shape):
    nd = len(shape)
    return pl.BlockSpec(shape, lambda i: (0,) * nd)


def _halo_specs(tm, w, total_rows):
    tb = tm // HALO
    nb = total_rows // HALO
    prev = pl.BlockSpec((HALO, w), lambda i: (jnp.maximum(i * tb - 1, 0), 0))
    nxt = pl.BlockSpec((HALO, w), lambda i: (jnp.minimum((i + 1) * tb, nb - 1), 0))
    return prev, nxt


ANY = pl.BlockSpec(memory_space=pl.ANY)


def _peers():
    x, y, c = lax.axis_index("x"), lax.axis_index("y"), lax.axis_index("c")
    return x, y, c


def _gather_small(v, name):
    m_per, n = v.shape

    def body(x_ref, out_ref, send_sems, recv_sems, local_sem):
        x, y, c = _peers()
        me, sibling = (x, y, c), (x, y, 1 - c)
        chips = [(1 - x, y), (x, 1 - y), (1 - x, 1 - y)]

        def rows(px, py, pc):
            return out_ref.at[pl.ds((4 * px + 2 * py + pc) * m_per, m_per), :]

        def copy(k, block, to, src=None):
            return pltpu.make_async_remote_copy(
                src_ref=rows(*block) if src is None else src, dst_ref=rows(*block),
                send_sem=send_sems.at[k], recv_sem=recv_sems.at[k], device_id=to, device_id_type=MESH)

        mine = pltpu.make_async_copy(x_ref, rows(*me), local_sem)
        mine.start()
        first = [copy(0, me, sibling, src=x_ref)]
        first += [copy(1 + j, me, (*chip, c), src=x_ref) for j, chip in enumerate(chips)]
        for cp in first:
            cp.start()
        passed = [copy(4 + j, (*chip, c), sibling) for j, chip in enumerate(chips)]
        for j, chip in enumerate(chips):
            copy(1 + j, (*chip, c), me).wait_recv()
            passed[j].start()
        copy(0, sibling, me).wait_recv()
        for j, chip in enumerate(chips):
            copy(4 + j, (*chip, 1 - c), me).wait_recv()
        for cp in first + passed:
            cp.wait_send()
        mine.wait()

    return pl.pallas_call(
        body, name=name,
        out_shape=jax.ShapeDtypeStruct((NDEV * m_per, n), v.dtype),
        in_specs=[pl.BlockSpec(memory_space=pltpu.VMEM)],
        out_specs=pl.BlockSpec(memory_space=pltpu.VMEM),
        scratch_shapes=[pltpu.SemaphoreType.DMA((7,)), pltpu.SemaphoreType.DMA((7,)), pltpu.SemaphoreType.DMA],
        compiler_params=pltpu.CompilerParams(vmem_limit_bytes=VMEM_LIMIT),
    )(v)


class _Comm:
    def __init__(self, ins, outs, aliases, bind):
        self.ins, self.outs, self.aliases, self.bind = ins, outs, aliases, bind


COMM_SEMS = [pltpu.SemaphoreType.DMA((7,)), pltpu.SemaphoreType.DMA((7,)), pltpu.SemaphoreType.DMA]


def _gather_hooks(p_ref, out_ref, send_sems, recv_sems, local_sem):
    def parts():
        x, y, c = _peers()
        me, sibling = (x, y, c), (x, y, 1 - c)
        chips = [(1 - x, y), (x, 1 - y), (1 - x, 1 - y)]

        def slab(px, py, pc):
            return out_ref.at[4 * px + 2 * py + pc]

        def copy(k, block, to, src=None):
            return pltpu.make_async_remote_copy(
                src_ref=slab(*block) if src is None else src, dst_ref=slab(*block),
                send_sem=send_sems.at[k], recv_sem=recv_sems.at[k], device_id=to, device_id_type=MESH)

        def mine():
            return pltpu.make_async_copy(p_ref, slab(*me), local_sem)

        def first():
            return [copy(0, me, sibling, src=p_ref)] + [copy(1 + j, me, (*chip, c), src=p_ref)
                                                        for j, chip in enumerate(chips)]

        def passed():
            return [copy(4 + j, (*chip, c), sibling) for j, chip in enumerate(chips)]

        def from_chips():
            return [copy(1 + j, (*chip, c), me) for j, chip in enumerate(chips)]

        def from_sibling():
            return [copy(0, sibling, me)] + [copy(4 + j, (*chip, 1 - c), me) for j, chip in enumerate(chips)]

        return mine, first, passed, from_chips, from_sibling

    def start():
        mine, first, _, _, _ = parts()
        mine().start()
        for cp in first():
            cp.start()

    def middle():
        _, _, passed, from_chips, _ = parts()
        for arrived, onward in zip(from_chips(), passed()):
            arrived.wait_recv()
            onward.start()

    def end():
        mine, first, passed, _, from_sibling = parts()
        for cp in from_sibling():
            cp.wait_recv()
        for cp in first() + passed():
            cp.wait_send()
        mine().wait()

    return start, middle, end


def _gather_comm(p):
    return _Comm([p], [jax.ShapeDtypeStruct((NDEV,) + p.shape, p.dtype)], {},
                 lambda cins, couts, sems: _gather_hooks(cins[0], couts[0], *sems))


def _head_comm(cond, p):
    def body(c_ref, p_ref, co_ref, po_ref, *sems):
        c_hooks = _gather_hooks(c_ref, co_ref, *sems[:3])
        p_hooks = _gather_hooks(p_ref, po_ref, *sems[3:])
        for step in range(3):
            c_hooks[step]()
            p_hooks[step]()

    return pl.pallas_call(
        body, name="head_comm",
        out_shape=[jax.ShapeDtypeStruct((NDEV,) + cond.shape, cond.dtype),
                   jax.ShapeDtypeStruct((NDEV,) + p.shape, p.dtype)],
        in_specs=[ANY, ANY], out_specs=[ANY, ANY], scratch_shapes=COMM_SEMS + COMM_SEMS,
    )(cond, p)


def _scatter_hooks(src_refs, specs, r_ref, layer, send_sems, recv_sems, local_sem):
    total = sum(rows for _, rows in specs)

    def start():
        x, y, c = _peers()
        me = 4 * x + 2 * y + c

        def part(k, dev):
            off, rows = specs[k]
            src = src_refs[k].at[pl.ds(pl.multiple_of(dev * rows, 16), rows), :]
            return src, r_ref.at[me, layer, pl.ds(off, rows), :]

        for k in range(len(specs)):
            src, dst = part(k, me)
            pltpu.make_async_copy(src, dst, local_sem).start()
        for r in range(1, NDEV):
            px = 1 - x if r & 4 else x
            py = 1 - y if r & 2 else y
            pc = 1 - c if r & 1 else c
            for k in range(len(specs)):
                src, dst = part(k, 4 * px + 2 * py + pc)
                pltpu.make_async_remote_copy(
                    src_ref=src, dst_ref=dst, send_sem=send_sems.at[r - 1], recv_sem=recv_sems.at[r - 1],
                    device_id=(px, py, pc), device_id_type=MESH).start()

    def end():
        x, y, c = _peers()
        whole = r_ref.at[0, layer, pl.ds(0, total), :]
        for r in range(1, NDEV):
            done = pltpu.make_async_remote_copy(
                src_ref=whole, dst_ref=whole, send_sem=send_sems.at[r - 1], recv_sem=recv_sems.at[r - 1],
                device_id=(x, y, c), device_id_type=MESH)
            done.wait_recv()
            done.wait_send()
        pltpu.make_async_copy(whole, whole, local_sem).wait()

    return start, None, end


def _scatter_comm(srcs, specs, recv, layer):
    k = len(srcs)
    return _Comm(list(srcs) + [recv], [jax.ShapeDtypeStruct(recv.shape, recv.dtype)], {k: 0},
                 lambda cins, couts, sems: _scatter_hooks(cins[:k], specs, couts[0], layer, *sems))


def _tail_comm(srcs, specs, recv, layer, small):
    k = len(srcs)

    def body(*refs):
        src_refs, small_ref, r_ref, g_ref, sems = refs[:k], refs[k + 1], refs[k + 2], refs[k + 3], refs[k + 4:]
        s_start, _, s_end = _scatter_hooks(src_refs, specs, r_ref, layer, *sems[:3])
        g_start, g_middle, g_end = _gather_hooks(small_ref, g_ref, *sems[3:])
        s_start()
        g_start()
        g_middle()
        g_end()
        s_end()

    return pl.pallas_call(
        body, name="tail_comm",
        out_shape=[jax.ShapeDtypeStruct(recv.shape, recv.dtype),
                   jax.ShapeDtypeStruct((NDEV,) + small.shape, small.dtype)],
        in_specs=[ANY] * (k + 2), out_specs=[ANY, ANY], scratch_shapes=COMM_SEMS + COMM_SEMS,
        input_output_aliases={k: 0},
    )(*srcs, recv, small)


def _empty_recv():
    def body(o_ref):
        del o_ref

    return pl.pallas_call(body, name="recv_buffer", out_specs=ANY,
                          out_shape=jax.ShapeDtypeStruct((NDEV, DEPTH, LROWS, D), BF16))()


def _host_call(inner, comm, *, name, grid, in_specs, out_specs, out_shape, scratch_shapes, operands):
    if comm is None:
        return pl.pallas_call(
            inner, name=name, grid=grid, in_specs=in_specs, out_specs=out_specs, out_shape=out_shape,
            scratch_shapes=scratch_shapes, compiler_params=_cparams("arbitrary"))(*operands)
    n_in, n_out, n_s = len(in_specs), len(out_specs), len(scratch_shapes)
    k_in, k_out = len(comm.ins), len(comm.outs)
    steps = grid[0]

    def body(*refs):
        ins, cins = refs[:n_in], refs[n_in:n_in + k_in]
        o0 = n_in + k_in
        outs, couts = refs[o0:o0 + n_out], refs[o0 + n_out:o0 + n_out + k_out]
        s0 = o0 + n_out + k_out
        scr, sems = refs[s0:s0 + n_s], refs[s0 + n_s:]
        start, middle, end = comm.bind(cins, couts, sems)
        i = pl.program_id(0)
        pl.when(i == 0)(start)
        if middle is not None:
            pl.when(i == steps * 3 // 4)(middle)
        inner(*ins, *outs, *scr)
        pl.when(i == steps - 1)(end)

    return pl.pallas_call(
        body, name=name, grid=grid, in_specs=list(in_specs) + [ANY] * k_in,
        out_specs=list(out_specs) + [ANY] * k_out, out_shape=list(out_shape) + list(comm.outs),
        scratch_shapes=list(scratch_shapes) + COMM_SEMS,
        input_output_aliases={n_in + a: n_out + b for a, b in comm.aliases.items()},
        compiler_params=_cparams("arbitrary"))(*operands, *comm.ins)


def _mod_fwd(c_all, w_mod, b_sl):
    def body(c_ref, w_ref, b_ref, o_ref, ca_ref):
        cv = c_ref[...]
        ca = cv * _sig(cv)
        ca_ref[...] = ca
        o_ref[...] = jnp.dot(ca, w_ref[0], preferred_element_type=F32, precision=lax.Precision.HIGHEST) + b_ref[0]

    return pl.pallas_call(
        body, name="mod_fwd", grid=(DEPTH,),
        in_specs=[_full((NDEV, D)), pl.BlockSpec((1, D, MODW), lambda l: (l, 0, 0)),
                  pl.BlockSpec((1, 1, MODW), lambda l: (l, 0, 0))],
        out_specs=[pl.BlockSpec((NDEV, MODW), lambda l: (0, l)), _full((NDEV, D))],
        out_shape=[jax.ShapeDtypeStruct((NDEV, DEPTH * MODW), F32), jax.ShapeDtypeStruct((NDEV, D), F32)],
        compiler_params=_cparams("arbitrary"),
    )(c_all, w_mod, b_sl)


def _mod_bwd_adam(ca_t, dmod, w, m, v):
    def body(ct_ref, dm_ref, w_ref, m_ref, v_ref, g_ref, d_ref, mo_ref, vo_ref):
        g = jnp.dot(ct_ref[...], dm_ref[0], preferred_element_type=F32, precision=lax.Precision.HIGHEST)
        delta, m2, v2 = _adam(w_ref[0], g, m_ref[0], v_ref[0])
        g_ref[0], d_ref[0], mo_ref[0], vo_ref[0] = g, delta, m2, v2

    blk = pl.BlockSpec((1, D, MODW), lambda l: (l, 0, 0))
    sds = jax.ShapeDtypeStruct(w.shape, F32)
    return pl.pallas_call(
        body, name="mod_bwd_adam", grid=(DEPTH,),
        in_specs=[_full((D, NDEV)), pl.BlockSpec((1, NDEV, MODW), lambda l: (l, 0, 0)), blk, blk, blk],
        out_specs=[blk] * 4, out_shape=[sds] * 4,
        compiler_params=_cparams("arbitrary"),
    )(ca_t, dmod, w, m, v)


def _inproj(x, vec, bias, gw, layer, tm, comm=None):
    t = x.shape[0]

    def body(x_ref, vec_ref, b_ref, g_ref, u_ref, w_s, sems):
        @pl.when(pl.program_id(0) == 0)
        def _():
            _load_weights(g_ref, [(OFF_IN, ROWS_IN, w_s)], sems)

        h, _, _ = _norm_fwd(x_ref[...], vec_ref[0:1], vec_ref[1:2], vec_ref[2:3])
        w = w_s[...].reshape(2 * D, D)
        u_ref[...] = (_nt(h.astype(BF16), w) + b_ref[...]).astype(BF16)

    res = _host_call(
        body, comm, name=f"inproj_l{layer}", grid=(t // tm,),
        in_specs=[_tile(tm, D), _full((8, D)), _full((1, 2 * D)), ANY],
        out_specs=[_tile(tm, 2 * D)], out_shape=[jax.ShapeDtypeStruct((t, 2 * D), BF16)],
        scratch_shapes=[pltpu.VMEM((NDEV, ROWS_IN, D), BF16), pltpu.SemaphoreType.DMA((1,))],
        operands=(x, vec, bias, gw))
    return res[0] if comm is None else res


def _fill_even(qe, pe, be, part_ref, lo, rows, valid):
    cg = part_ref[:, DA:2 * DA].astype(F32)
    v = part_ref[:, 2 * DA:3 * DA].astype(F32)
    q = cg * v
    p = part_ref[:, 3 * DA:4 * DA].astype(F32)
    if valid is not None:
        q = jnp.where(valid, q, 0.0)
        p = jnp.where(valid, p, 0.0)
    qe[lo:lo + rows, :] = q
    pe[lo:lo + rows, :] = p
    if be is not None:
        b = part_ref[:, 0:DA].astype(F32)
        be[lo:lo + rows, :] = b if valid is None else jnp.where(valid, b, 0.0)


def _conv3(ca_ref, qe, tm):
    return (ca_ref[0:1] * qe[HALO - 1:HALO - 1 + tm] + ca_ref[1:2] * qe[HALO:HALO + tm]
            + ca_ref[2:3] * qe[HALO + 1:HALO + 1 + tm])


def _pool_counts(t0, rows, first_row, left, right, t):
    tg = t0 + first_row + lax.broadcasted_iota(jnp.int32, (rows, 1), 0)
    cnt = jnp.minimum(tg + right, t - 1) - jnp.maximum(tg - left, 0) + 1
    return jnp.maximum(cnt, 1).astype(F32)


def _pool_minus_id(pe, gi, left, right, inv_cnt, tm):
    c0 = gi * PG
    s = pe[HALO - left:HALO - left + tm, c0:c0 + PG]
    for j in range(-left + 1, right + 1):
        s = s + pe[HALO + j:HALO + j + tm, c0:c0 + PG]
    return s * inv_cnt - pe[HALO:HALO + tm, c0:c0 + PG]


def _mix_even_fwd(u, x, vec, ca, wp, ps, gw, layer, tm, comm=None):
    t = x.shape[0]
    n = t // tm
    e = tm + 2 * HALO

    def body(u_ref, up_ref, un_ref, x_ref, vec_ref, ca_ref, wp_ref, ps_ref, g_ref, xo_ref, y_ref, w_s, qe, pe, sems):
        i = pl.program_id(0)

        @pl.when(i == 0)
        def _():
            _load_weights(g_ref, [(OFF_OUT, ROWS_OUT, w_s)], sems)

        _fill_even(qe, pe, None, up_ref, 0, HALO, i > 0)
        _fill_even(qe, pe, None, u_ref, HALO, tm, None)
        _fill_even(qe, pe, None, un_ref, HALO + tm, HALO, i < n - 1)
        ya = u_ref[:, 0:DA].astype(F32) * _conv3(ca_ref, qe, tm)
        parts = [ya]
        for gi, (_, left, right) in enumerate(POOL):
            inv = 1.0 / _pool_counts(i * tm, tm, 0, left, right, t)
            pm = _pool_minus_id(pe, gi, left, right, inv, tm)
            parts.append(_nn(pm.astype(BF16), wp_ref[gi]) * ps_ref[0:1, gi * PG:(gi + 1) * PG])
        cat = jnp.concatenate(parts, axis=-1).astype(BF16)
        y = _nn(cat, w_s[...].reshape(D, D))
        y_ref[...] = y.astype(BF16)
        xo_ref[...] = x_ref[...] + vec_ref[3:4] * y

    prev, nxt = _halo_specs(tm, 2 * D, t)
    return _host_call(
        body, comm, name=f"mix_even_fwd_l{layer}", grid=(n,),
        in_specs=[_tile(tm, 2 * D), prev, nxt, _tile(tm, D), _full((8, D)), _full((3, DA)),
                  _full((4, PG, PG)), _full((1, DA)), ANY],
        out_specs=[_tile(tm, D), _tile(tm, D)],
        out_shape=[jax.ShapeDtypeStruct((t, D), F32), jax.ShapeDtypeStruct((t, D), BF16)],
        scratch_shapes=[pltpu.VMEM((NDEV, ROWS_OUT, D), BF16), pltpu.VMEM((e, DA), F32), pltpu.VMEM((e, DA), F32),
                        pltpu.SemaphoreType.DMA((1,))],
        operands=(u, u, u, x, vec, ca, wp, ps, gw))


def _mix_even_bwd(dxo, u, x, y, vec, ca, wp, ps, gw, layer, tm, comm=None):
    t = x.shape[0]
    n = t // tm
    e = tm + 2 * HALO

    def body(dxo_ref, dp_ref, dn_ref, u_ref, up_ref, un_ref, x_ref, y_ref, vec_ref, ca_ref, wp_ref, ps_ref, g_ref,
             dxi_ref, du_ref, h_ref, cat_ref, dy_ref, sums_ref, dwp_ref,
             wo_s, wi_s, dye, qe, pe, be, dce, epe, sems):
        i = pl.program_id(0)

        @pl.when(i == 0)
        def _():
            _load_weights(g_ref, [(OFF_OUT, ROWS_OUT, wo_s), (OFF_IN, ROWS_IN, wi_s)], sems)
            sums_ref[...] = jnp.zeros_like(sums_ref)
            dwp_ref[...] = jnp.zeros_like(dwp_ref)

        gate = vec_ref[3:4]
        dxo_m = dxo_ref[...]
        dye[0:HALO, :] = jnp.where(i > 0, gate * dp_ref[...], 0.0).astype(BF16)
        dye[HALO:HALO + tm, :] = (gate * dxo_m).astype(BF16)
        dye[HALO + tm:e, :] = jnp.where(i < n - 1, gate * dn_ref[...], 0.0).astype(BF16)
        _fill_even(qe, pe, be, up_ref, 0, HALO, i > 0)
        _fill_even(qe, pe, be, u_ref, HALO, tm, None)
        _fill_even(qe, pe, be, un_ref, HALO + tm, HALO, i < n - 1)
        sums_ref[0:1, :] += _rowsum(dxo_m * y_ref[...].astype(F32))

        dcat = _nt(dye[...], wo_s[...].reshape(D, D))
        dce[...] = dcat[:, 0:DA] * be[...]
        cq = _conv3(ca_ref, qe, tm)
        bg = be[HALO:HALO + tm]
        dc_m = dce[HALO:HALO + tm]
        dbg = dcat[HALO:HALO + tm, 0:DA] * cq
        dq = (ca_ref[0:1] * dce[HALO + 1:HALO + 1 + tm] + ca_ref[1:2] * dc_m
              + ca_ref[2:3] * dce[HALO - 1:HALO - 1 + tm])
        cg = u_ref[:, DA:2 * DA].astype(F32)
        v = u_ref[:, 2 * DA:3 * DA].astype(F32)
        for k in range(3):
            sums_ref[4 + k:5 + k, 0:DA] += _rowsum(dc_m * qe[HALO - 1 + k:HALO - 1 + k + tm])
        du_parts = [dbg, dq * v, dq * cg]
        cat_parts = [bg * cq]
        for gi, (_, left, right) in enumerate(POOL):
            c0 = gi * PG
            scale = ps_ref[0:1, c0:c0 + PG]
            dyb = dcat[:, DA + c0:DA + c0 + PG]
            dybs = (dyb * scale).astype(BF16)
            dpm = _nt(dybs, wp_ref[gi])
            inv_e = 1.0 / _pool_counts(i * tm, e, -HALO, left, right, t)
            epe[:, c0:c0 + PG] = dpm * inv_e
            s_adj = epe[HALO - right:HALO - right + tm, c0:c0 + PG]
            for j in range(-right + 1, left + 1):
                s_adj = s_adj + epe[HALO + j:HALO + j + tm, c0:c0 + PG]
            du_parts.append(s_adj - dpm[HALO:HALO + tm])
            inv_m = 1.0 / _pool_counts(i * tm, tm, 0, left, right, t)
            pm = _pool_minus_id(pe, gi, left, right, inv_m, tm).astype(BF16)
            ybpre = _nn(pm, wp_ref[gi])
            sums_ref[7:8, c0:c0 + PG] += _rowsum(dyb[HALO:HALO + tm] * ybpre)
            dwp_ref[gi] += _tn(pm, dybs[HALO:HALO + tm])
            cat_parts.append(ybpre * scale)
        du = jnp.concatenate(du_parts, axis=-1).astype(BF16)
        du_ref[...] = du
        cat_ref[...] = jnp.concatenate(cat_parts, axis=-1).astype(BF16)
        dy_ref[...] = dye[HALO:HALO + tm, :]
        dh = _nn(du, wi_s[...].reshape(2 * D, D))
        g, sc, sh = vec_ref[0:1], vec_ref[1:2], vec_ref[2:3]
        h, nrm, r = _norm_fwd(x_ref[...], g, sc, sh)
        h_ref[...] = h.astype(BF16)
        dxi_ref[...] = dxo_m + _norm_bwd(dh, nrm, r, g, sc)
        sums_ref[1:2, :] += _rowsum(dh)
        sums_ref[2:3, :] += _rowsum(dh * nrm)

        @pl.when(i == n - 1)
        def _():
            p = sums_ref[2:3, :]
            sums_ref[3:4, :] = p * (1.0 + sc)
            sums_ref[2:3, :] = p * g

    prev_u, nxt_u = _halo_specs(tm, 2 * D, t)
    prev_d, nxt_d = _halo_specs(tm, D, t)
    return _host_call(
        body, comm, name=f"mix_even_bwd_l{layer}", grid=(n,),
        in_specs=[_tile(tm, D), prev_d, nxt_d, _tile(tm, 2 * D), prev_u, nxt_u, _tile(tm, D), _tile(tm, D),
                  _full((8, D)), _full((3, DA)), _full((4, PG, PG)), _full((1, DA)), ANY],
        out_specs=[_tile(tm, D), _tile(tm, 2 * D), _tile(tm, D), _tile(tm, D), _tile(tm, D),
                   _full((16, D)), _full((4, PG, PG))],
        out_shape=[jax.ShapeDtypeStruct((t, D), F32), jax.ShapeDtypeStruct((t, 2 * D), BF16),
                   jax.ShapeDtypeStruct((t, D), BF16), jax.ShapeDtypeStruct((t, D), BF16),
                   jax.ShapeDtypeStruct((t, D), BF16), jax.ShapeDtypeStruct((16, D), F32),
                   jax.ShapeDtypeStruct((4, PG, PG), F32)],
        scratch_shapes=[pltpu.VMEM((NDEV, ROWS_OUT, D), BF16), pltpu.VMEM((NDEV, ROWS_IN, D), BF16),
                        pltpu.VMEM((e, D), BF16), pltpu.VMEM((e, DA), F32), pltpu.VMEM((e, DA), F32),
                        pltpu.VMEM((e, DA), F32), pltpu.VMEM((e, DA), F32), pltpu.VMEM((e, DA), F32),
                        pltpu.SemaphoreType.DMA((2,))],
        operands=(dxo, dxo, dxo, u, u, u, x, y, vec, ca, wp, ps, gw))


def _fill_glu(ze, part_ref, lo, rows, valid):
    a = part_ref[:, 0:D].astype(F32)
    g = part_ref[:, D:2 * D].astype(F32)
    z = a * _sig(g)
    ze[lo:lo + rows, :] = z if valid is None else jnp.where(valid, z, 0.0)


SHIFT_ROWS = 24


def _shifted_copies(dst, src, tm):
    rows = tm + SHIFT_ROWS
    for j in range(8):
        dst[j, :, :] = src[j:j + rows, :]


def _shifted(dst, shift, tm):
    lo = shift // 8 * 8
    return dst[shift % 8, lo:lo + tm, :]


def _layer_norm_parts(z2):
    mu = _lanemean(z2)
    d = z2 - mu
    rstd = lax.rsqrt(_lanemean(d * d) + LN_EPS)
    return d * rstd, rstd


def _mix_odd_fwd(u, x, vec, wdw, sm, gw, layer, tm):
    t = x.shape[0]
    n = t // tm
    e = tm + 2 * HALO

    def body(u_ref, up_ref, un_ref, x_ref, vec_ref, wdw_ref, sm_ref, g_ref, xo_ref, y_ref, z2_ref, w_s, ze, zsh, sems):
        i = pl.program_id(0)

        @pl.when(i == 0)
        def _():
            _load_weights(g_ref, [(OFF_OUT, ROWS_OUT, w_s)], sems)

        _fill_glu(ze, up_ref, 0, HALO, i > 0)
        _fill_glu(ze, u_ref, HALO, tm, None)
        _fill_glu(ze, un_ref, HALO + tm, HALO, i < n - 1)
        _shifted_copies(zsh, ze, tm)
        z2 = sm_ref[0:1] + wdw_ref[0:1] * _shifted(zsh, 1, tm)
        for k in range(1, CONF_K):
            z2 = z2 + wdw_ref[k:k + 1] * _shifted(zsh, 1 + k, tm)
        z2_ref[...] = z2.astype(BF16)
        zn, _ = _layer_norm_parts(z2)
        lo = zn * sm_ref[1:2] + sm_ref[2:3]
        z3 = lo * _sig(lo)
        y = _nn(z3.astype(BF16), w_s[...].reshape(D, D)) + sm_ref[3:4]
        y_ref[...] = y.astype(BF16)
        xo_ref[...] = x_ref[...] + vec_ref[3:4] * y

    prev, nxt = _halo_specs(tm, 2 * D, t)
    return pl.pallas_call(
        body, name=f"mix_odd_fwd_l{layer}", grid=(n,),
        in_specs=[_tile(tm, 2 * D), prev, nxt, _tile(tm, D), _full((8, D)), _full((32, D)), _full((8, D)), ANY],
        out_specs=[_tile(tm, D), _tile(tm, D), _tile(tm, D)],
        out_shape=[jax.ShapeDtypeStruct((t, D), F32), jax.ShapeDtypeStruct((t, D), BF16),
                   jax.ShapeDtypeStruct((t, D), BF16)],
        scratch_shapes=[pltpu.VMEM((NDEV, ROWS_OUT, D), BF16), pltpu.VMEM((e, D), F32),
                        pltpu.VMEM((8, tm + SHIFT_ROWS, D), F32), pltpu.SemaphoreType.DMA((1,))],
        compiler_params=_cparams("arbitrary"),
    )(u, u, u, x, vec, wdw, sm, gw)


def _mix_odd_bwd1(dxo, y, z2, vec, sm, gw, layer, tm):
    t = dxo.shape[0]
    n = t // tm

    def body(dxo_ref, y_ref, z2_ref, vec_ref, sm_ref, g_ref, dy_ref, z3_ref, dz2_ref, sums_ref, w_s, sems):
        i = pl.program_id(0)

        @pl.when(i == 0)
        def _():
            _load_weights(g_ref, [(OFF_OUT, ROWS_OUT, w_s)], sems)
            sums_ref[...] = jnp.zeros_like(sums_ref)

        dxo_m = dxo_ref[...]
        dy = vec_ref[3:4] * dxo_m
        dyb = dy.astype(BF16)
        dy_ref[...] = dyb
        sums_ref[0:1, :] += _rowsum(dxo_m * y_ref[...].astype(F32))
        sums_ref[4:5, :] += _rowsum(dy)
        dz3 = _nt(dyb, w_s[...].reshape(D, D))
        zn, rstd = _layer_norm_parts(z2_ref[...].astype(F32))
        lo = zn * sm_ref[1:2] + sm_ref[2:3]
        sg = _sig(lo)
        z3_ref[...] = (lo * sg).astype(BF16)
        dlo = dz3 * (sg * (1.0 + lo * (1.0 - sg)))
        sums_ref[5:6, :] += _rowsum(dlo * zn)
        sums_ref[6:7, :] += _rowsum(dlo)
        dzn = dlo * sm_ref[1:2]
        dz2 = rstd * (dzn - _lanemean(dzn) - zn * _lanemean(dzn * zn))
        sums_ref[7:8, :] += _rowsum(dz2)
        dz2_ref[...] = dz2.astype(BF16)

    return pl.pallas_call(
        body, name=f"mix_odd_bwd1_l{layer}", grid=(n,),
        in_specs=[_tile(tm, D), _tile(tm, D), _tile(tm, D), _full((8, D)), _full((8, D)), ANY],
        out_specs=[_tile(tm, D), _tile(tm, D), _tile(tm, D), _full((16, D))],
        out_shape=[jax.ShapeDtypeStruct((t, D), BF16)] * 3 + [jax.ShapeDtypeStruct((16, D), F32)],
        scratch_shapes=[pltpu.VMEM((NDEV, ROWS_OUT, D), BF16), pltpu.SemaphoreType.DMA((1,))],
        compiler_params=_cparams("arbitrary"),
    )(dxo, y, z2, vec, sm, gw)


def _mix_odd_bwd2(dz2, u, x, dxo, vec, wdw, gw, layer, tm, comm=None):
    t = x.shape[0]
    n = t // tm
    e = tm + 2 * HALO

    def body(dz_ref, dzp_ref, dzn_ref, u_ref, x_ref, dxo_ref, vec_ref, wdw_ref, g_ref,
             dxi_ref, du_ref, h_ref, sums_ref, dw_ref, w_s, de, zsh, sems):
        i = pl.program_id(0)

        @pl.when(i == 0)
        def _():
            _load_weights(g_ref, [(OFF_IN, ROWS_IN, w_s)], sems)
            sums_ref[...] = jnp.zeros_like(sums_ref)
            dw_ref[...] = jnp.zeros_like(dw_ref)

        de[0:HALO, :] = jnp.where(i > 0, dzp_ref[...].astype(F32), 0.0)
        de[HALO:HALO + tm, :] = dz_ref[...].astype(F32)
        de[HALO + tm:e, :] = jnp.where(i < n - 1, dzn_ref[...].astype(F32), 0.0)
        a = u_ref[:, 0:D].astype(F32)
        gg = u_ref[:, D:2 * D].astype(F32)
        sg = _sig(gg)
        z = a * sg
        _shifted_copies(zsh, de, tm)
        dz = None
        for k in range(CONF_K):
            shifted = _shifted(zsh, CONF_K - k, tm)
            term = wdw_ref[k:k + 1] * shifted
            dz = term if dz is None else dz + term
            dw_ref[k:k + 1, :] += _rowsum(z * shifted)
        da = dz * sg
        dg = dz * a * (sg * (1.0 - sg))
        sums_ref[8:9, :] += _rowsum(da)
        sums_ref[9:10, :] += _rowsum(dg)
        du = jnp.concatenate([da, dg], axis=-1).astype(BF16)
        du_ref[...] = du
        dh = _nn(du, w_s[...].reshape(2 * D, D))
        g, sc, sh = vec_ref[0:1], vec_ref[1:2], vec_ref[2:3]
        h, nrm, r = _norm_fwd(x_ref[...], g, sc, sh)
        h_ref[...] = h.astype(BF16)
        dxi_ref[...] = dxo_ref[...] + _norm_bwd(dh, nrm, r, g, sc)
        sums_ref[1:2, :] += _rowsum(dh)
        sums_ref[2:3, :] += _rowsum(dh * nrm)

        @pl.when(i == n - 1)
        def _():
            p = sums_ref[2:3, :]
            sums_ref[3:4, :] = p * (1.0 + sc)
            sums_ref[2:3, :] = p * g

    prev_d, nxt_d = _halo_specs(tm, D, t)
    return _host_call(
        body, comm, name=f"mix_odd_bwd2_l{layer}", grid=(n,),
        in_specs=[_tile(tm, D), prev_d, nxt_d, _tile(tm, 2 * D), _tile(tm, D), _tile(tm, D),
                  _full((8, D)), _full((32, D)), ANY],
        out_specs=[_tile(tm, D), _tile(tm, 2 * D), _tile(tm, D), _full((16, D)), _full((32, D))],
        out_shape=[jax.ShapeDtypeStruct((t, D), F32), jax.ShapeDtypeStruct((t, 2 * D), BF16),
                   jax.ShapeDtypeStruct((t, D), BF16), jax.ShapeDtypeStruct((16, D), F32),
                   jax.ShapeDtypeStruct((32, D), F32)],
        scratch_shapes=[pltpu.VMEM((NDEV, ROWS_IN, D), BF16), pltpu.VMEM((e, D), F32),
                        pltpu.VMEM((8, tm + SHIFT_ROWS, D), F32),
                        pltpu.SemaphoreType.DMA((1,))],
        operands=(dz2, dz2, dz2, u, x, dxo, vec, wdw, gw))


FCH = FF // 2


def _loss_head(x, tgt, g):
    r = lax.rsqrt(_lanemean(x * x) + RMS_EPS)
    nrm = x * r
    err = nrm * g - tgt
    dout = err * (1.0 / D)
    dn = dout * g
    return r * (dn - nrm * _lanemean(dn * nrm)), _rowsum(err * err) * (0.5 / D), _rowsum(dout * nrm)


def _ffn_fwd(x, vec, wsrc, layer, tm, comm=None, head=None):
    t = x.shape[0]
    g_gu, off_g, off_u, g_d, off_d = wsrc

    def body(*refs):
        if head is None:
            x_ref, vec_ref, ggu_ref, gd_ref, xo_ref, a_ref, b_ref, y_ref, wg_s, wu_s, wd_s, sems = refs
        else:
            (x_ref, vec_ref, ggu_ref, gd_ref, t_ref, gf_ref, xo_ref, a_ref, b_ref, y_ref, fsum_ref,
             wg_s, wu_s, wd_s, sems) = refs

        @pl.when(pl.program_id(0) == 0)
        def _():
            _load_weight_rows([(ggu_ref, off_g, FS, wg_s), (ggu_ref, off_u, FS, wu_s), (gd_ref, off_d, FS, wd_s)], sems)
            if head is not None:
                fsum_ref[...] = jnp.zeros_like(fsum_ref)

        xv = x_ref[...]
        h, _, _ = _norm_fwd(xv, vec_ref[0:1], vec_ref[1:2], vec_ref[2:3])
        hb = h.astype(BF16)
        y = jnp.zeros((tm, D), F32)
        for ch in range(2):
            rows = slice(ch * FCH, (ch + 1) * FCH)
            a = _nt(hb, wg_s[rows, :])
            b = _nt(hb, wu_s[rows, :])
            a_ref[:, rows] = a.astype(BF16)
            b_ref[:, rows] = b.astype(BF16)
            s = (a * _sig(a)) * b
            y = y + _nn(s.astype(BF16), wd_s[rows, :])
        y_ref[...] = y.astype(BF16)
        x_out = xv + vec_ref[3:4] * y
        if head is None:
            xo_ref[...] = x_out
        else:
            dx, loss_row, dg_row = _loss_head(x_out, t_ref[...], gf_ref[...])
            xo_ref[...] = dx
            fsum_ref[0:1, :] += dg_row
            fsum_ref[1:2, :] += loss_row

    wsc = pltpu.VMEM((FF, D), BF16)
    in_specs = [_tile(tm, D), _full((8, D)), ANY, ANY]
    out_specs = [_tile(tm, D), _tile(tm, FF), _tile(tm, FF), _tile(tm, D)]
    out_shape = [jax.ShapeDtypeStruct((t, D), F32), jax.ShapeDtypeStruct((t, FF), BF16),
                 jax.ShapeDtypeStruct((t, FF), BF16), jax.ShapeDtypeStruct((t, D), BF16)]
    operands = (x, vec, g_gu, g_d)
    if head is not None:
        in_specs += [_tile(tm, D), _full((1, D))]
        out_specs.append(_full((8, D)))
        out_shape.append(jax.ShapeDtypeStruct((8, D), F32))
        operands += tuple(head)
    return _host_call(
        body, comm, name=f"ffn_fwd_l{layer}", grid=(t // tm,),
        in_specs=in_specs, out_specs=out_specs, out_shape=out_shape,
        scratch_shapes=[wsc, wsc, wsc, pltpu.SemaphoreType.DMA((3 * NDEV,))],
        operands=operands)


def _ffn_bwd(dxo, x, y, a, b, vec, wsrc, layer, tm, comm=None):
    t = x.shape[0]
    n = t // tm
    g_gu, off_g, off_u, g_d, off_d = wsrc

    def body(dxo_ref, x_ref, y_ref, a_ref, b_ref, vec_ref, ggu_ref, gd_ref,
             dxi_ref, h_ref, dy_ref, s_ref, da_ref, db_ref, sums_ref, wg_s, wu_s, wd_s, sems):
        i = pl.program_id(0)

        @pl.when(i == 0)
        def _():
            _load_weight_rows([(ggu_ref, off_g, FS, wg_s), (ggu_ref, off_u, FS, wu_s), (gd_ref, off_d, FS, wd_s)], sems)
            sums_ref[...] = jnp.zeros_like(sums_ref)

        dxo_m = dxo_ref[...]
        sums_ref[0:1, :] += _rowsum(dxo_m * y_ref[...].astype(F32))
        dyb = (vec_ref[3:4] * dxo_m).astype(BF16)
        dy_ref[...] = dyb
        for ch in range(FF // MXU_N):
            cols = slice(ch * MXU_N, (ch + 1) * MXU_N)
            ds = _nt(dyb, wd_s[cols, :]).astype(BF16)
            av = a_ref[:, cols]
            bv = b_ref[:, cols]
            sg = _sig(av)
            sl = av * sg
            s_ref[:, cols] = sl * bv
            db_ref[:, cols] = ds * sl
            da_ref[:, cols] = (ds * bv) * (sg * (1.0 + av * (1.0 - sg)))
        dh = _nn(da_ref[...], wg_s[...]) + _nn(db_ref[...], wu_s[...])
        g, sc, sh = vec_ref[0:1], vec_ref[1:2], vec_ref[2:3]
        h, nrm, r = _norm_fwd(x_ref[...], g, sc, sh)
        h_ref[...] = h.astype(BF16)
        dxi_ref[...] = dxo_m + _norm_bwd(dh, nrm, r, g, sc)
        sums_ref[1:2, :] += _rowsum(dh)
        sums_ref[2:3, :] += _rowsum(dh * nrm)

        @pl.when(i == n - 1)
        def _():
            p = sums_ref[2:3, :]
            sums_ref[3:4, :] = p * (1.0 + sc)
            sums_ref[2:3, :] = p * g

    wsc = pltpu.VMEM((FF, D), BF16)
    big, small = jax.ShapeDtypeStruct((t, FF), BF16), jax.ShapeDtypeStruct((t, D), BF16)
    return _host_call(
        body, comm, name=f"ffn_bwd_l{layer}", grid=(n,),
        in_specs=[_tile(tm, D), _tile(tm, D), _tile(tm, D), _tile(tm, FF), _tile(tm, FF), _full((8, D)), ANY, ANY],
        out_specs=[_tile(tm, D), _tile(tm, D), _tile(tm, D), _tile(tm, FF), _tile(tm, FF), _tile(tm, FF),
                   _full((8, D))],
        out_shape=[jax.ShapeDtypeStruct((t, D), F32), small, small, big, big, big, jax.ShapeDtypeStruct((8, D), F32)],
        scratch_shapes=[wsc, wsc, wsc, pltpu.SemaphoreType.DMA((3 * NDEV,))],
        operands=(dxo, x, y, a, b, vec, g_gu, g_d))


def _wgrad(lhs, rhs, name, tk):
    t, m = lhs.shape
    n = t // tk

    def body(l_ref, r_ref, o_ref, acc):
        i = pl.program_id(0)

        @pl.when(i == 0)
        def _():
            acc[...] = jnp.zeros_like(acc)

        acc[...] += _tn(l_ref[...], r_ref[...])

        @pl.when(i == n - 1)
        def _():
            o_ref[...] = acc[...].astype(BF16)

    return pl.pallas_call(
        body, name=name, grid=(n,),
        in_specs=[_tile(tk, m), _tile(tk, D)], out_specs=_full((m, D)),
        out_shape=jax.ShapeDtypeStruct((m, D), BF16),
        scratch_shapes=[pltpu.VMEM((m, D), F32)],
        compiler_params=_cparams("arbitrary"),
    )(lhs, rhs)


ADAM_ROWS = LROWS // 5


def _adam_big(recv, w, m, v):
    def body(r_ref, w_ref, m_ref, v_ref, g_ref, d_ref, mo_ref, vo_ref):
        g = r_ref[0, 0].astype(F32)
        for s in range(1, NDEV):
            g = g + r_ref[s, 0].astype(F32)
        delta, m2, v2 = _adam(w_ref[0], g, m_ref[0], v_ref[0])
        g_ref[0], d_ref[0], mo_ref[0], vo_ref[0] = g, delta, m2, v2

    blk = pl.BlockSpec((1, ADAM_ROWS, D), lambda l, j: (l, j, 0))
    sds = jax.ShapeDtypeStruct(w.shape, F32)
    return pl.pallas_call(
        body, name="adam_big", grid=(DEPTH, LROWS // ADAM_ROWS),
        in_specs=[pl.BlockSpec((NDEV, 1, ADAM_ROWS, D), lambda l, j: (0, l, j, 0)), blk, blk, blk],
        out_specs=[blk] * 4, out_shape=[sds] * 4,
        compiler_params=_cparams("arbitrary", "arbitrary"),
    )(recv, w, m, v)


def _sum_small(gathered, rows):
    def body(g_ref, o_ref):
        acc = g_ref[0:rows, :]
        for s in range(1, NDEV):
            acc = acc + g_ref[s * rows:(s + 1) * rows, :]
        o_ref[...] = acc

    return pl.pallas_call(
        body, name="sum_small",
        in_specs=[pl.BlockSpec(memory_space=pltpu.VMEM)], out_specs=pl.BlockSpec(memory_space=pltpu.VMEM),
        out_shape=jax.ShapeDtypeStruct((rows, D), F32),
        compiler_params=pltpu.CompilerParams(vmem_limit_bytes=VMEM_LIMIT),
    )(gathered)


def _adam_small(params):
    k = len(params)

    def body(*refs):
        ins, outs = refs[:4 * k], refs[4 * k:]
        for j in range(k):
            w_ref, g_ref, m_ref, v_ref = ins[4 * j:4 * j + 4]
            delta, m2, v2 = _adam(w_ref[...], g_ref[...], m_ref[...], v_ref[...])
            outs[3 * j][...], outs[3 * j + 1][...], outs[3 * j + 2][...] = delta, m2, v2

    flat = [a for p in params for a in p]
    shapes = [jax.ShapeDtypeStruct(p[0].shape, F32) for p in params for _ in range(3)]
    vm = pl.BlockSpec(memory_space=pltpu.VMEM)
    res = pl.pallas_call(
        body, name="adam_small", in_specs=[vm] * len(flat), out_specs=[vm] * len(shapes), out_shape=shapes,
        compiler_params=pltpu.CompilerParams(vmem_limit_bytes=VMEM_LIMIT),
    )(*flat)
    return [tuple(res[3 * j:3 * j + 3]) for j in range(k)]


def _pack(ab_in, ab_out, pw1, pw2, wg, wu, wd):
    ins = jnp.swapaxes(jnp.stack([ab_in[0], pw1[0], ab_in[1], pw1[1]]), 1, 2)
    outs = jnp.stack([ab_out[0], pw2[0], ab_out[1], pw2[1]])
    return jnp.concatenate([ins, outs, jnp.swapaxes(wg, 1, 2), jnp.swapaxes(wu, 1, 2), wd], axis=1)


def _unpack(p):
    ins = jnp.swapaxes(p[:, OFF_IN:OFF_OUT], 1, 2)
    outs = p[:, OFF_OUT:OFF_G]
    return (ins[0::2], outs[0::2], ins[1::2], outs[1::2], jnp.swapaxes(p[:, OFF_G:OFF_U], 1, 2),
            jnp.swapaxes(p[:, OFF_U:OFF_D], 1, 2), p[:, OFF_D:LROWS])


def _unshard(flat, lead, per):
    k = len(lead)
    a = flat.reshape((NDEV,) + tuple(lead) + (per,))
    a = jnp.transpose(a, tuple(range(1, k + 1)) + (0, k + 1))
    return a.reshape(tuple(lead) + (NDEV * per,))


def _rows_of(a):
    f = a.reshape(-1)
    pad = (-f.shape[0]) % D
    if pad:
        f = jnp.concatenate([f, jnp.zeros((pad,), f.dtype)])
    return f.reshape(-1, D)


def _pad_rows(a, rows):
    return jnp.concatenate([a, jnp.zeros((rows - a.shape[0],) + a.shape[1:], a.dtype)], axis=0)


def kernel(x, c, norm_mix_g, norm_ffn_g, w_mod, b_mod, ab_w_in, ab_conv, ab_w_pool, ab_pool_scale, ab_w_out, cf_w_pw1, cf_b_pw1, cf_w_dw, cf_b_dw, cf_ln_g, cf_ln_b, cf_w_pw2, cf_b_pw2, ffn_w_gate, ffn_w_up, ffn_w_down, final_norm_g, loss_target, m_norm_mix_g, m_norm_ffn_g, m_w_mod, m_b_mod, m_ab_w_in, m_ab_conv, m_ab_w_pool, m_ab_pool_scale, m_ab_w_out, m_cf_w_pw1, m_cf_b_pw1, m_cf_w_dw, m_cf_b_dw, m_cf_ln_g, m_cf_ln_b, m_cf_w_pw2, m_cf_b_pw2, m_ffn_w_gate, m_ffn_w_up, m_ffn_w_down, m_final_norm_g, v_norm_mix_g, v_norm_ffn_g, v_w_mod, v_b_mod, v_ab_w_in, v_ab_conv, v_ab_w_pool, v_ab_pool_scale, v_ab_w_out, v_cf_w_pw1, v_cf_b_pw1, v_cf_w_dw, v_cf_b_dw, v_cf_ln_g, v_cf_ln_b, v_cf_w_pw2, v_cf_b_pw2, v_ffn_w_gate, v_ffn_w_up, v_ffn_w_down, v_final_norm_g):
    t = x.shape[1]
    tm = 512 if t % 512 == 0 else t // 2
    tk = 512 if t % 512 == 0 else t // 2
    tmo = tm
    tmb = tm // 2
    me = 4 * lax.axis_index("x") + 2 * lax.axis_index("y") + lax.axis_index("c")
    xs, tgt = x[0], loss_target[0]

    w_pack = _pack(ab_w_in, ab_w_out, cf_w_pw1, cf_w_pw2, ffn_w_gate, ffn_w_up, ffn_w_down)
    p16 = w_pack.astype(BF16)

    sharded = [ab_conv, cf_b_pw1, cf_w_dw, cf_b_dw, cf_ln_g, cf_ln_b, cf_b_pw2]
    flat = jnp.concatenate([a.reshape(-1) for a in sharded])
    n_flat = flat.shape[0]
    g1, g_io0 = _head_comm(jnp.concatenate([_pad_rows(c, 8), _pad_rows(_rows_of(flat), 16)], axis=0),
                           p16[0, OFF_IN:OFF_G])
    c_all = g1[:, 0, :]
    flat_all = g1[:, 8:, :].reshape(NDEV, -1)[:, :n_flat]
    full, o = [], 0
    for a in sharded:
        lead, per = a.shape[:-1], a.shape[-1]
        size = a.size
        full.append(_unshard(flat_all[:, o:o + size], lead, per))
        o += size
    ab_conv_f, b_pw1_f, w_dw_f, b_dw_f, ln_g_f, ln_b_f, b_pw2_f = full

    b_sl = lax.dynamic_slice_in_dim(b_mod, me * MODW, MODW, axis=1).reshape(DEPTH, 1, MODW)
    mod_part, c_act = _mod_fwd(c_all, w_mod, b_sl)
    g2 = _gather_small(mod_part, "gather_mod").reshape(NDEV, NDEV, DEPTH, MODW)
    mod = jnp.transpose(lax.dynamic_index_in_dim(g2, me, axis=1, keepdims=False), (1, 0, 2)).reshape(DEPTH, N_MOD, D)
    zeros4 = jnp.zeros((4, D), F32)

    def vec_of(g, layer, k):
        return jnp.concatenate([g[layer][None], mod[layer, k + 1][None], mod[layer, k][None],
                                mod[layer, k + 2][None], zeros4], axis=0)

    vmix = [vec_of(norm_mix_g, l, 0) for l in range(DEPTH)]
    vffn = [vec_of(norm_ffn_g, l, 3) for l in range(DEPTH)]

    wp16 = ab_w_pool.astype(BF16)
    wdw32 = [_pad_rows(w_dw_f[i], 32) for i in range(2)]
    sm_odd = [jnp.concatenate([b_dw_f[i][None], ln_g_f[i][None], ln_b_f[i][None], b_pw2_f[i][None], zeros4], axis=0)
              for i in range(2)]
    zero_bias = jnp.zeros((1, 2 * D), F32)

    saved = []
    xc = xs
    gw = [g_io0]
    wsrc = []
    for l in range(DEPTH):
        i = l // 2
        if l == 0:
            u, g_gu0 = _inproj(xc, vmix[l], zero_bias, gw[l], l, tm, _gather_comm(p16[0, OFF_G:OFF_D]))
            x_mid, y_mix, g_d0 = _mix_even_fwd(u, xc, vmix[l], ab_conv_f[i], wp16[i], ab_pool_scale[i][None], gw[l],
                                               l, tm, _gather_comm(p16[0, OFF_D:LROWS]))
            wsrc.append((g_gu0, 0, FS, g_d0, 0))
            z2 = None
        elif l % 2 == 0:
            u = _inproj(xc, vmix[l], zero_bias, gw[l], l, tm)
            x_mid, y_mix = _mix_even_fwd(u, xc, vmix[l], ab_conv_f[i], wp16[i], ab_pool_scale[i][None], gw[l], l, tm)
            z2 = None
        else:
            u = _inproj(xc, vmix[l], b_pw1_f[i][None], gw[l], l, tm)
            x_mid, y_mix, z2 = _mix_odd_fwd(u, xc, vmix[l], wdw32[i], sm_odd[i], gw[l], l, tmo)
        if l + 1 < DEPTH:
            x_out, a, b, y_ffn, g_next = _ffn_fwd(x_mid, vffn[l], wsrc[l], l, tm, _gather_comm(p16[l + 1]))
            gw.append(g_next)
            wsrc.append((g_next, OFF_G, OFF_U, g_next, OFF_D))
        else:
            x_out = None
            dx, a, b, y_ffn, fsum = _ffn_fwd(x_mid, vffn[l], wsrc[l], l, tmb, head=(tgt, final_norm_g[None]))
        saved.append((xc, u, y_mix, z2, x_mid, a, b, y_ffn))
        xc = x_out

    loss = lax.psum(jnp.sum(fsum[1]), ("x", "y", "c"))
    d_final_g = fsum[0]

    recv = _empty_recv()
    late_specs = [(OFF_U, FS), (OFF_IN, ROWS_IN), (OFF_OUT, ROWS_OUT)]
    pending = None
    dmod = [None] * DEPTH
    d_mix_g, d_ffn_g = [None] * DEPTH, [None] * DEPTH
    d_conv, d_pool, d_pscale = [None] * 2, [None] * 2, [None] * 2
    d_bpw1, d_wdw, d_bdw, d_lng, d_lnb, d_bpw2 = ([None] * 2 for _ in range(6))
    for l in reversed(range(DEPTH)):
        i = l // 2
        x_in, u, y_mix, z2, x_mid, a, b, y_ffn = saved[l]
        if pending is None:
            dx_mid, h2, dy, s, da, db, s_f = _ffn_bwd(dx, x_mid, y_ffn, a, b, vffn[l], wsrc[l], l, tmb)
        else:
            dx_mid, h2, dy, s, da, db, s_f, recv = _ffn_bwd(dx, x_mid, y_ffn, a, b, vffn[l], wsrc[l], l, tmb,
                                                            _scatter_comm(pending, late_specs, recv, l + 1))
        g_down = _wgrad(s, dy, f"wgrad_down_l{l}", tk)
        g_gate = _wgrad(da, h2, f"wgrad_gate_l{l}", tk)
        gu_comm = _scatter_comm([g_down, g_gate], [(OFF_D, FS), (OFF_G, FS)], recv, l)
        g_up = _wgrad(db, h2, f"wgrad_up_l{l}", tk)
        d_ffn_g[l] = s_f[3]
        mod_ffn = [s_f[1], s_f[2], s_f[0]]
        if l % 2 == 0:
            dx, du, h, cat, dym, s_m, dwp, recv = _mix_even_bwd(dx_mid, u, x_in, y_mix, vmix[l], ab_conv_f[i], wp16[i],
                                                                ab_pool_scale[i][None], gw[l], l, tm, gu_comm)
            g_out = _wgrad(cat, dym, f"wgrad_out_l{l}", tk)
            d_conv[i], d_pool[i], d_pscale[i] = s_m[4:7, :DA], dwp, s_m[7, :DA]
            mod_mix = [s_m[1], s_m[2], s_m[0]]
            d_mix_g[l] = s_m[3]
        else:
            dym, z3, dz2, s_1 = _mix_odd_bwd1(dx_mid, y_mix, z2, vmix[l], sm_odd[i], gw[l], l, tm)
            dx, du, h, s_2, dwdw, recv = _mix_odd_bwd2(dz2, u, x_in, dx_mid, vmix[l], wdw32[i], gw[l], l, tmo, gu_comm)
            g_out = _wgrad(z3, dym, f"wgrad_out_l{l}", tk)
            d_bpw1[i], d_wdw[i], d_bdw[i] = s_2[8:10].reshape(2 * D), dwdw[:CONF_K], s_1[7]
            d_lng[i], d_lnb[i], d_bpw2[i] = s_1[5], s_1[6], s_1[4]
            mod_mix = [s_2[1], s_2[2], s_1[0]]
            d_mix_g[l] = s_2[3]
        dmod[l] = jnp.stack(mod_mix + mod_ffn)
        pending = [g_up, _wgrad(du, h, f"wgrad_in_l{l}", tk), g_out]
    grad_x = dx[None]

    small = [jnp.stack(dmod), jnp.stack(d_mix_g), jnp.stack(d_ffn_g), d_final_g, jnp.stack(d_pool),
             jnp.stack(d_pscale), jnp.stack(d_conv), jnp.stack(d_bpw1), jnp.stack(d_wdw), jnp.stack(d_bdw),
             jnp.stack(d_lng), jnp.stack(d_lnb), jnp.stack(d_bpw2)]
    small_rows = [_rows_of(a) for a in small]
    n_rows = sum(a.shape[0] for a in small_rows)
    pad_rows = -(-n_rows // 8) * 8
    recv, g3 = _tail_comm(pending, late_specs, recv, 0, _pad_rows(jnp.concatenate(small_rows, axis=0), pad_rows))
    g3 = g3.reshape(NDEV * pad_rows, D)
    summed = _sum_small(g3, pad_rows)

    m_pack = _pack(m_ab_w_in, m_ab_w_out, m_cf_w_pw1, m_cf_w_pw2, m_ffn_w_gate, m_ffn_w_up, m_ffn_w_down)
    v_pack = _pack(v_ab_w_in, v_ab_w_out, v_cf_w_pw1, v_cf_w_pw2, v_ffn_w_gate, v_ffn_w_up, v_ffn_w_down)
    big = [_unpack(p) for p in _adam_big(recv, w_pack, m_pack, v_pack)]

    outs, o = [], 0
    for a, r in zip(small, small_rows):
        outs.append(summed[o:o + r.shape[0]].reshape(-1)[:a.size].reshape(a.shape))
        o += r.shape[0]
    (g_bmod, g_mix_g, g_ffn_g, g_final, g_pool, g_pscale, g_conv, g_bpw1, g_wdw, g_bdw, g_lng, g_lnb, g_bpw2) = outs
    g_bmod = g_bmod.reshape(DEPTH, N_MOD * D)

    def my_shard(a):
        per = a.shape[-1] // NDEV
        return lax.dynamic_slice_in_dim(a, me * per, per, axis=a.ndim - 1)

    g_conv, g_bpw1, g_wdw, g_bdw, g_lng, g_lnb, g_bpw2 = [
        my_shard(a) for a in (g_conv, g_bpw1, g_wdw, g_bdw, g_lng, g_lnb, g_bpw2)]

    dmod_all = g3.reshape(NDEV, pad_rows, D)[:, :DEPTH * N_MOD, :].reshape(NDEV, DEPTH, N_MOD * D)
    dmod_mine = jnp.transpose(lax.dynamic_slice_in_dim(dmod_all, me * MODW, MODW, axis=2), (1, 0, 2))
    g_wmod, d_wmod, nm_wmod, nv_wmod = _mod_bwd_adam(c_act.T, dmod_mine, w_mod, m_w_mod, v_w_mod)

    small_params = [
        (norm_mix_g, g_mix_g, m_norm_mix_g, v_norm_mix_g), (norm_ffn_g, g_ffn_g, m_norm_ffn_g, v_norm_ffn_g),
        (b_mod, g_bmod, m_b_mod, v_b_mod), (ab_conv, g_conv, m_ab_conv, v_ab_conv),
        (ab_w_pool, g_pool, m_ab_w_pool, v_ab_w_pool), (ab_pool_scale, g_pscale, m_ab_pool_scale, v_ab_pool_scale),
        (cf_b_pw1, g_bpw1, m_cf_b_pw1, v_cf_b_pw1), (cf_w_dw, g_wdw, m_cf_w_dw, v_cf_w_dw),
        (cf_b_dw, g_bdw, m_cf_b_dw, v_cf_b_dw), (cf_ln_g, g_lng, m_cf_ln_g, v_cf_ln_g),
        (cf_ln_b, g_lnb, m_cf_ln_b, v_cf_ln_b), (cf_b_pw2, g_bpw2, m_cf_b_pw2, v_cf_b_pw2),
        (final_norm_g, g_final, m_final_norm_g, v_final_norm_g)]

    def two_d(a):
        return a.reshape(-1, a.shape[-1])

    upd = _adam_small([tuple(two_d(a) for a in p) for p in small_params])
    upd = [tuple(r.reshape(p[0].shape) for r in u) for u, p in zip(upd, small_params)]
    (s_mix, s_ffn, s_bmod, s_conv, s_pool, s_pscale, s_bpw1, s_wdw, s_bdw, s_lng, s_lnb, s_bpw2, s_final) = upd
    small_g = [p[1] for p in small_params]
    (q_mix, q_ffn, q_bmod, q_conv, q_pool, q_pscale, q_bpw1, q_wdw, q_bdw, q_lng, q_lnb, q_bpw2, q_final) = small_g

    def ordered(k):
        ab_in, ab_out, pw1, pw2, wg, wu, wd = big[k]
        if k == 0:
            sm = dict(mix=q_mix, ffn=q_ffn, bmod=q_bmod, conv=q_conv, pool=q_pool, pscale=q_pscale, bpw1=q_bpw1,
                      wdw=q_wdw, bdw=q_bdw, lng=q_lng, lnb=q_lnb, bpw2=q_bpw2, final=q_final)
            wmod = g_wmod
        else:
            j = k - 1
            sm = dict(mix=s_mix[j], ffn=s_ffn[j], bmod=s_bmod[j], conv=s_conv[j], pool=s_pool[j], pscale=s_pscale[j],
                      bpw1=s_bpw1[j], wdw=s_wdw[j], bdw=s_bdw[j], lng=s_lng[j], lnb=s_lnb[j], bpw2=s_bpw2[j],
                      final=s_final[j])
            wmod = (d_wmod, nm_wmod, nv_wmod)[j]
        return [sm["mix"], sm["ffn"], wmod, sm["bmod"], ab_in, sm["conv"], sm["pool"], sm["pscale"], ab_out,
                pw1, sm["bpw1"], sm["wdw"], sm["bdw"], sm["lng"], sm["lnb"], pw2, sm["bpw2"], wg, wu, wd, sm["final"]]

    return (loss, grad_x, *ordered(0), *ordered(1), *ordered(2), *ordered(3))
```

```python
import functools

import jax
import jax.numpy as jnp
from jax import lax
from jax.experimental import pallas as pl
from jax.experimental.pallas import tpu as pltpu

F32 = jnp.float32
BF16 = jnp.bfloat16
MESH = pl.DeviceIdType.MESH

NDEV = 8
DEPTH = 4
D = 1024
FF = 2816
FS = FF // NDEV
DA = 512
PG = 128
POOL = ((2, 1, 0), (4, 2, 1), (8, 4, 3), (16, 8, 7))
CONF_K = 31
CONF_L = 15
N_MOD = 6
MODW = N_MOD * D // NDEV
RMS_EPS = 1e-6
LN_EPS = 1e-5

ROWS_IN, ROWS_OUT = 2 * D // NDEV, D // NDEV
OFF_IN, OFF_OUT = 0, ROWS_IN
OFF_G = OFF_OUT + ROWS_OUT
OFF_U = OFF_G + FS
OFF_D = OFF_U + FS
LROWS = OFF_D + FS

MXU_N = 256
HALO = 16
VMEM_LIMIT = 60 * 1024 * 1024

ADAM_LR, ADAM_B1, ADAM_B2, ADAM_EPS, ADAM_WD, ADAM_STEP = 1e-3, 0.9, 0.999, 1e-8, 0.01, 10
ADAM_C1 = 1.0 / (1.0 - ADAM_B1 ** ADAM_STEP)
ADAM_C2 = 1.0 / (1.0 - ADAM_B2 ** ADAM_STEP)


def _nn(a, b):
    return jnp.dot(a, b, preferred_element_type=F32)


def _nt(a, b):
    return lax.dot_general(a, b, (((1,), (1,)), ((), ())), preferred_element_type=F32)


def _tn(a, b):
    return lax.dot_general(a, b, (((0,), (0,)), ((), ())), preferred_element_type=F32)


def _sig(x):
    return 1.0 / (1.0 + jnp.exp(-x))


def _rowsum(x):
    return jnp.sum(x, axis=0, keepdims=True)


def _lanemean(x):
    return jnp.mean(x, axis=-1, keepdims=True)


def _norm_fwd(x, g, sc, sh):
    r = lax.rsqrt(_lanemean(x * x) + RMS_EPS)
    n = x * r
    return n * (g * (1.0 + sc)) + sh, n, r


def _norm_bwd(dh, n, r, g, sc):
    dn = dh * (g * (1.0 + sc))
    return r * (dn - n * _lanemean(dn * n))


def _adam(w, g, m, v):
    m2 = ADAM_B1 * m + (1.0 - ADAM_B1) * g
    v2 = ADAM_B2 * v + (1.0 - ADAM_B2) * (g * g)
    delta = -ADAM_LR * ((m2 * ADAM_C1) / (jnp.sqrt(v2 * ADAM_C2) + ADAM_EPS) + ADAM_WD * w)
    return delta, m2, v2


def _load_weights(g_ref, specs, sems):
    cps = [pltpu.make_async_copy(g_ref.at[:, pl.ds(off, rows), :], dst, sems.at[k])
           for k, (off, rows, dst) in enumerate(specs)]
    for cp in cps:
        cp.start()
    for cp in cps:
        cp.wait()


def _load_weight_rows(specs, sems):
    cps = [pltpu.make_async_copy(g_ref.at[d, pl.ds(off, rows), :], dst.at[pl.ds(d * rows, rows), :],
                                 sems.at[NDEV * k + d])
           for k, (g_ref, off, rows, dst) in enumerate(specs) for d in range(NDEV)]
    for cp in cps:
        cp.start()
    for cp in cps:
        cp.wait()


def _cparams(*sem):
    return pltpu.CompilerParams(dimension_semantics=sem if sem else None, vmem_limit_bytes=VMEM_LIMIT)


def _tile(tm, w):
    return pl.BlockSpec((tm, w), lambda i: (i, 0))


def _full(shape):
    nd = len(shape)
    return pl.BlockSpec(shape, lambda i: (0,) * nd)


def _halo_specs(tm, w, total_rows):
    tb = tm // HALO
    nb = total_rows // HALO
    prev = pl.BlockSpec((HALO, w), lambda i: (jnp.maximum(i * tb - 1, 0), 0))
    nxt = pl.BlockSpec((HALO, w), lambda i: (jnp.minimum((i + 1) * tb, nb - 1), 0))
    return prev, nxt


ANY = pl.BlockSpec(memory_space=pl.ANY)


def _peers():
    x, y, c = lax.axis_index("x"), lax.axis_index("y"), lax.axis_index("c")
    return x, y, c


def _gather_small(v, name):
    m_per, n = v.shape

    def body(x_ref, out_ref, send_sems, recv_sems, local_sem):
        x, y, c = _peers()
        me, sibling = (x, y, c), (x, y, 1 - c)
        chips = [(1 - x, y), (x, 1 - y), (1 - x, 1 - y)]

        def rows(px, py, pc):
            return out_ref.at[pl.ds((4 * px + 2 * py + pc) * m_per, m_per), :]

        def copy(k, block, to, src=None):
            return pltpu.make_async_remote_copy(
                src_ref=rows(*block) if src is None else src, dst_ref=rows(*block),
                send_sem=send_sems.at[k], recv_sem=recv_sems.at[k], device_id=to, device_id_type=MESH)

        mine = pltpu.make_async_copy(x_ref, rows(*me), local_sem)
        mine.start()
        first = [copy(0, me, sibling, src=x_ref)]
        first += [copy(1 + j, me, (*chip, c), src=x_ref) for j, chip in enumerate(chips)]
        for cp in first:
            cp.start()
        passed = [copy(4 + j, (*chip, c), sibling) for j, chip in enumerate(chips)]
        for j, chip in enumerate(chips):
            copy(1 + j, (*chip, c), me).wait_recv()
            passed[j].start()
        copy(0, sibling, me).wait_recv()
        for j, chip in enumerate(chips):
            copy(4 + j, (*chip, 1 - c), me).wait_recv()
        for cp in first + passed:
            cp.wait_send()
        mine.wait()

    return pl.pallas_call(
        body, name=name,
        out_shape=jax.ShapeDtypeStruct((NDEV * m_per, n), v.dtype),
        in_specs=[pl.BlockSpec(memory_space=pltpu.VMEM)],
        out_specs=pl.BlockSpec(memory_space=pltpu.VMEM),
        scratch_shapes=[pltpu.SemaphoreType.DMA((7,)), pltpu.SemaphoreType.DMA((7,)), pltpu.SemaphoreType.DMA],
        compiler_params=pltpu.CompilerParams(vmem_limit_bytes=VMEM_LIMIT),
    )(v)


class _Comm:
    def __init__(self, ins, outs, aliases, bind):
        self.ins, self.outs, self.aliases, self.bind = ins, outs, aliases, bind


COMM_SEMS = [pltpu.SemaphoreType.DMA((7,)), pltpu.SemaphoreType.DMA((7,)), pltpu.SemaphoreType.DMA]


def _gather_hooks(p_ref, out_ref, send_sems, recv_sems, local_sem):
    def parts():
        x, y, c = _peers()
        me, sibling = (x, y, c), (x, y, 1 - c)
        chips = [(1 - x, y), (x, 1 - y), (1 - x, 1 - y)]

        def slab(px, py, pc):
            return out_ref.at[4 * px + 2 * py + pc]

        def copy(k, block, to, src=None):
            return pltpu.make_async_remote_copy(
                src_ref=slab(*block) if src is None else src, dst_ref=slab(*block),
                send_sem=send_sems.at[k], recv_sem=recv_sems.at[k], device_id=to, device_id_type=MESH)

        def mine():
            return pltpu.make_async_copy(p_ref, slab(*me), local_sem)

        def first():
            return [copy(0, me, sibling, src=p_ref)] + [copy(1 + j, me, (*chip, c), src=p_ref)
                                                        for j, chip in enumerate(chips)]

        def passed():
            return [copy(4 + j, (*chip, c), sibling) for j, chip in enumerate(chips)]

        def from_chips():
            return [copy(1 + j, (*chip, c), me) for j, chip in enumerate(chips)]

        def from_sibling():
            return [copy(0, sibling, me)] + [copy(4 + j, (*chip, 1 - c), me) for j, chip in enumerate(chips)]

        return mine, first, passed, from_chips, from_sibling

    def start():
        mine, first, _, _, _ = parts()
        mine().start()
        for cp in first():
            cp.start()

    def middle():
        _, _, passed, from_chips, _ = parts()
        for arrived, onward in zip(from_chips(), passed()):
            arrived.wait_recv()
            onward.start()

    def end():
        mine, first, passed, _, from_sibling = parts()
        for cp in from_sibling():
            cp.wait_recv()
        for cp in first() + passed():
            cp.wait_send()
        mine().wait()

    return start, middle, end


def _gather_comm(p):
    return _Comm([p], [jax.ShapeDtypeStruct((NDEV,) + p.shape, p.dtype)], {},
                 lambda cins, couts, sems: _gather_hooks(cins[0], couts[0], *sems))


def _head_comm(cond, p):
    def body(c_ref, p_ref, co_ref, po_ref, *sems):
        c_hooks = _gather_hooks(c_ref, co_ref, *sems[:3])
        p_hooks = _gather_hooks(p_ref, po_ref, *sems[3:])
        for step in range(3):
            c_hooks[step]()
            p_hooks[step]()

    return pl.pallas_call(
        body, name="head_comm",
        out_shape=[jax.ShapeDtypeStruct((NDEV,) + cond.shape, cond.dtype),
                   jax.ShapeDtypeStruct((NDEV,) + p.shape, p.dtype)],
        in_specs=[ANY, ANY], out_specs=[ANY, ANY], scratch_shapes=COMM_SEMS + COMM_SEMS,
    )(cond, p)


def _scatter_hooks(src_refs, specs, r_ref, layer, send_sems, recv_sems, local_sem):
    total = sum(rows for _, rows in specs)

    def start():
        x, y, c = _peers()
        me = 4 * x + 2 * y + c

        def part(k, dev):
            off, rows = specs[k]
            src = src_refs[k].at[pl.ds(pl.multiple_of(dev * rows, 16), rows), :]
            return src, r_ref.at[me, layer, pl.ds(off, rows), :]

        for k in range(len(specs)):
            src, dst = part(k, me)
            pltpu.make_async_copy(src, dst, local_sem).start()
        for r in range(1, NDEV):
            px = 1 - x if r & 4 else x
            py = 1 - y if r & 2 else y
            pc = 1 - c if r & 1 else c
            for k in range(len(specs)):
                src, dst = part(k, 4 * px + 2 * py + pc)
                pltpu.make_async_remote_copy(
                    src_ref=src, dst_ref=dst, send_sem=send_sems.at[r - 1], recv_sem=recv_sems.at[r - 1],
                    device_id=(px, py, pc), device_id_type=MESH).start()

    def end():
        x, y, c = _peers()
        whole = r_ref.at[0, layer, pl.ds(0, total), :]
        for r in range(1, NDEV):
            done = pltpu.make_async_remote_copy(
                src_ref=whole, dst_ref=whole, send_sem=send_sems.at[r - 1], recv_sem=recv_sems.at[r - 1],
                device_id=(x, y, c), device_id_type=MESH)
            done.wait_recv()
            done.wait_send()
        pltpu.make_async_copy(whole, whole, local_sem).wait()

    return start, None, end


def _scatter_comm(srcs, specs, recv, layer):
    k = len(srcs)
    return _Comm(list(srcs) + [recv], [jax.ShapeDtypeStruct(recv.shape, recv.dtype)], {k: 0},
                 lambda cins, couts, sems: _scatter_hooks(cins[:k], specs, couts[0], layer, *sems))


def _tail_comm(srcs, specs, recv, layer, small):
    k = len(srcs)

    def body(*refs):
        src_refs, small_ref, r_ref, g_ref, sems = refs[:k], refs[k + 1], refs[k + 2], refs[k + 3], refs[k + 4:]
        s_start, _, s_end = _scatter_hooks(src_refs, specs, r_ref, layer, *sems[:3])
        g_start, g_middle, g_end = _gather_hooks(small_ref, g_ref, *sems[3:])
        s_start()
        g_start()
        g_middle()
        g_end()
        s_end()

    return pl.pallas_call(
        body, name="tail_comm",
        out_shape=[jax.ShapeDtypeStruct(recv.shape, recv.dtype),
                   jax.ShapeDtypeStruct((NDEV,) + small.shape, small.dtype)],
        in_specs=[ANY] * (k + 2), out_specs=[ANY, ANY], scratch_shapes=COMM_SEMS + COMM_SEMS,
        input_output_aliases={k: 0},
    )(*srcs, recv, small)


def _empty_recv():
    def body(o_ref):
        del o_ref

    return pl.pallas_call(body, name="recv_buffer", out_specs=ANY,
                          out_shape=jax.ShapeDtypeStruct((NDEV, DEPTH, LROWS, D), BF16))()


def _host_call(inner, comm, *, name, grid, in_specs, out_specs, out_shape, scratch_shapes, operands):
    if comm is None:
        return pl.pallas_call(
            inner, name=name, grid=grid, in_specs=in_specs, out_specs=out_specs, out_shape=out_shape,
            scratch_shapes=scratch_shapes, compiler_params=_cparams("arbitrary"))(*operands)
    n_in, n_out, n_s = len(in_specs), len(out_specs), len(scratch_shapes)
    k_in, k_out = len(comm.ins), len(comm.outs)
    steps = grid[0]

    def body(*refs):
        ins, cins = refs[:n_in], refs[n_in:n_in + k_in]
        o0 = n_in + k_in
        outs, couts = refs[o0:o0 + n_out], refs[o0 + n_out:o0 + n_out + k_out]
        s0 = o0 + n_out + k_out
        scr, sems = refs[s0:s0 + n_s], refs[s0 + n_s:]
        start, middle, end = comm.bind(cins, couts, sems)
        i = pl.program_id(0)
        pl.when(i == 0)(start)
        if middle is not None:
            pl.when(i == steps * 3 // 4)(middle)
        inner(*ins, *outs, *scr)
        pl.when(i == steps - 1)(end)

    return pl.pallas_call(
        body, name=name, grid=grid, in_specs=list(in_specs) + [ANY] * k_in,
        out_specs=list(out_specs) + [ANY] * k_out, out_shape=list(out_shape) + list(comm.outs),
        scratch_shapes=list(scratch_shapes) + COMM_SEMS,
        input_output_aliases={n_in + a: n_out + b for a, b in comm.aliases.items()},
        compiler_params=_cparams("arbitrary"))(*operands, *comm.ins)


def _mod_fwd(c_all, w_mod, b_sl):
    def body(c_ref, w_ref, b_ref, o_ref, ca_ref):
        cv = c_ref[...]
        ca = cv * _sig(cv)
        ca_ref[...] = ca
        o_ref[...] = jnp.dot(ca, w_ref[0], preferred_element_type=F32, precision=lax.Precision.HIGHEST) + b_ref[0]

    return pl.pallas_call(
        body, name="mod_fwd", grid=(DEPTH,),
        in_specs=[_full((NDEV, D)), pl.BlockSpec((1, D, MODW), lambda l: (l, 0, 0)),
                  pl.BlockSpec((1, 1, MODW), lambda l: (l, 0, 0))],
        out_specs=[pl.BlockSpec((NDEV, MODW), lambda l: (0, l)), _full((NDEV, D))],
        out_shape=[jax.ShapeDtypeStruct((NDEV, DEPTH * MODW), F32), jax.ShapeDtypeStruct((NDEV, D), F32)],
        compiler_params=_cparams("arbitrary"),
    )(c_all, w_mod, b_sl)


def _mod_bwd_adam(ca_t, dmod, w, m, v):
    def body(ct_ref, dm_ref, w_ref, m_ref, v_ref, g_ref, d_ref, mo_ref, vo_ref):
        g = jnp.dot(ct_ref[...], dm_ref[0], preferred_element_type=F32, precision=lax.Precision.HIGHEST)
        delta, m2, v2 = _adam(w_ref[0], g, m_ref[0], v_ref[0])
        g_ref[0], d_ref[0], mo_ref[0], vo_ref[0] = g, delta, m2, v2

    blk = pl.BlockSpec((1, D, MODW), lambda l: (l, 0, 0))
    sds = jax.ShapeDtypeStruct(w.shape, F32)
    return pl.pallas_call(
        body, name="mod_bwd_adam", grid=(DEPTH,),
        in_specs=[_full((D, NDEV)), pl.BlockSpec((1, NDEV, MODW), lambda l: (l, 0, 0)), blk, blk, blk],
        out_specs=[blk] * 4, out_shape=[sds] * 4,
        compiler_params=_cparams("arbitrary"),
    )(ca_t, dmod, w, m, v)


def _inproj(x, vec, bias, gw, layer, tm, comm=None):
    t = x.shape[0]

    def body(x_ref, vec_ref, b_ref, g_ref, u_ref, w_s, sems):
        @pl.when(pl.program_id(0) == 0)
        def _():
            _load_weights(g_ref, [(OFF_IN, ROWS_IN, w_s)], sems)

        h, _, _ = _norm_fwd(x_ref[...], vec_ref[0:1], vec_ref[1:2], vec_ref[2:3])
        w = w_s[...].reshape(2 * D, D)
        u_ref[...] = (_nt(h.astype(BF16), w) + b_ref[...]).astype(BF16)

    res = _host_call(
        body, comm, name=f"inproj_l{layer}", grid=(t // tm,),
        in_specs=[_tile(tm, D), _full((8, D)), _full((1, 2 * D)), ANY],
        out_specs=[_tile(tm, 2 * D)], out_shape=[jax.ShapeDtypeStruct((t, 2 * D), BF16)],
        scratch_shapes=[pltpu.VMEM((NDEV, ROWS_IN, D), BF16), pltpu.SemaphoreType.DMA((1,))],
        operands=(x, vec, bias, gw))
    return res[0] if comm is None else res


def _fill_even(qe, pe, be, part_ref, lo, rows, valid):
    cg = part_ref[:, DA:2 * DA].astype(F32)
    v = part_ref[:, 2 * DA:3 * DA].astype(F32)
    q = cg * v
    p = part_ref[:, 3 * DA:4 * DA].astype(F32)
    if valid is not None:
        q = jnp.where(valid, q, 0.0)
        p = jnp.where(valid, p, 0.0)
    qe[lo:lo + rows, :] = q
    pe[lo:lo + rows, :] = p
    if be is not None:
        b = part_ref[:, 0:DA].astype(F32)
        be[lo:lo + rows, :] = b if valid is None else jnp.where(valid, b, 0.0)


def _conv3(ca_ref, qe, tm):
    return (ca_ref[0:1] * qe[HALO - 1:HALO - 1 + tm] + ca_ref[1:2] * qe[HALO:HALO + tm]
            + ca_ref[2:3] * qe[HALO + 1:HALO + 1 + tm])


def _pool_counts(t0, rows, first_row, left, right, t):
    tg = t0 + first_row + lax.broadcasted_iota(jnp.int32, (rows, 1), 0)
    cnt = jnp.minimum(tg + right, t - 1) - jnp.maximum(tg - left, 0) + 1
    return jnp.maximum(cnt, 1).astype(F32)


def _pool_minus_id(pe, gi, left, right, inv_cnt, tm):
    c0 = gi * PG
    s = pe[HALO - left:HALO - left + tm, c0:c0 + PG]
    for j in range(-left + 1, right + 1):
        s = s + pe[HALO + j:HALO + j + tm, c0:c0 + PG]
    return s * inv_cnt - pe[HALO:HALO + tm, c0:c0 + PG]


def _mix_even_fwd(u, x, vec, ca, wp, ps, gw, layer, tm, comm=None):
    t = x.shape[0]
    n = t // tm
    e = tm + 2 * HALO

    def body(u_ref, up_ref, un_ref, x_ref, vec_ref, ca_ref, wp_ref, ps_ref, g_ref, xo_ref, y_ref, w_s, qe, pe, sems):
        i = pl.program_id(0)

        @pl.when(i == 0)
        def _():
            _load_weights(g_ref, [(OFF_OUT, ROWS_OUT, w_s)], sems)

        _fill_even(qe, pe, None, up_ref, 0, HALO, i > 0)
        _fill_even(qe, pe, None, u_ref, HALO, tm, None)
        _fill_even(qe, pe, None, un_ref, HALO + tm, HALO, i < n - 1)
        ya = u_ref[:, 0:DA].astype(F32) * _conv3(ca_ref, qe, tm)
        parts = [ya]
        for gi, (_, left, right) in enumerate(POOL):
            inv = 1.0 / _pool_counts(i * tm, tm, 0, left, right, t)
            pm = _pool_minus_id(pe, gi, left, right, inv, tm)
            parts.append(_nn(pm.astype(BF16), wp_ref[gi]) * ps_ref[0:1, gi * PG:(gi + 1) * PG])
        cat = jnp.concatenate(parts, axis=-1).astype(BF16)
        y = _nn(cat, w_s[...].reshape(D, D))
        y_ref[...] = y.astype(BF16)
        xo_ref[...] = x_ref[...] + vec_ref[3:4] * y

    prev, nxt = _halo_specs(tm, 2 * D, t)
    return _host_call(
        body, comm, name=f"mix_even_fwd_l{layer}", grid=(n,),
        in_specs=[_tile(tm, 2 * D), prev, nxt, _tile(tm, D), _full((8, D)), _full((3, DA)),
                  _full((4, PG, PG)), _full((1, DA)), ANY],
        out_specs=[_tile(tm, D), _tile(tm, D)],
        out_shape=[jax.ShapeDtypeStruct((t, D), F32), jax.ShapeDtypeStruct((t, D), BF16)],
        scratch_shapes=[pltpu.VMEM((NDEV, ROWS_OUT, D), BF16), pltpu.VMEM((e, DA), F32), pltpu.VMEM((e, DA), F32),
                        pltpu.SemaphoreType.DMA((1,))],
        operands=(u, u, u, x, vec, ca, wp, ps, gw))


def _mix_even_bwd(dxo, u, x, y, vec, ca, wp, ps, gw, layer, tm, comm=None):
    t = x.shape[0]
    n = t // tm
    e = tm + 2 * HALO

    def body(dxo_ref, dp_ref, dn_ref, u_ref, up_ref, un_ref, x_ref, y_ref, vec_ref, ca_ref, wp_ref, ps_ref, g_ref,
             dxi_ref, du_ref, h_ref, cat_ref, dy_ref, sums_ref, dwp_ref,
             wo_s, wi_s, dye, qe, pe, be, dce, epe, sems):
        i = pl.program_id(0)

        @pl.when(i == 0)
        def _():
            _load_weights(g_ref, [(OFF_OUT, ROWS_OUT, wo_s), (OFF_IN, ROWS_IN, wi_s)], sems)
            sums_ref[...] = jnp.zeros_like(sums_ref)
            dwp_ref[...] = jnp.zeros_like(dwp_ref)

        gate = vec_ref[3:4]
        dxo_m = dxo_ref[...]
        dye[0:HALO, :] = jnp.where(i > 0, gate * dp_ref[...], 0.0).astype(BF16)
        dye[HALO:HALO + tm, :] = (gate * dxo_m).astype(BF16)
        dye[HALO + tm:e, :] = jnp.where(i < n - 1, gate * dn_ref[...], 0.0).astype(BF16)
        _fill_even(qe, pe, be, up_ref, 0, HALO, i > 0)
        _fill_even(qe, pe, be, u_ref, HALO, tm, None)
        _fill_even(qe, pe, be, un_ref, HALO + tm, HALO, i < n - 1)
        sums_ref[0:1, :] += _rowsum(dxo_m * y_ref[...].astype(F32))

        dcat = _nt(dye[...], wo_s[...].reshape(D, D))
        dce[...] = dcat[:, 0:DA] * be[...]
        cq = _conv3(ca_ref, qe, tm)
        bg = be[HALO:HALO + tm]
        dc_m = dce[HALO:HALO + tm]
        dbg = dcat[HALO:HALO + tm, 0:DA] * cq
        dq = (ca_ref[0:1] * dce[HALO + 1:HALO + 1 + tm] + ca_ref[1:2] * dc_m
              + ca_ref[2:3] * dce[HALO - 1:HALO - 1 + tm])
        cg = u_ref[:, DA:2 * DA].astype(F32)
        v = u_ref[:, 2 * DA:3 * DA].astype(F32)
        for k in range(3):
            sums_ref[4 + k:5 + k, 0:DA] += _rowsum(dc_m * qe[HALO - 1 + k:HALO - 1 + k + tm])
        du_parts = [dbg, dq * v, dq * cg]
        cat_parts = [bg * cq]
        for gi, (_, left, right) in enumerate(POOL):
            c0 = gi * PG
            scale = ps_ref[0:1, c0:c0 + PG]
            dyb = dcat[:, DA + c0:DA + c0 + PG]
            dybs = (dyb * scale).astype(BF16)
            dpm = _nt(dybs, wp_ref[gi])
            inv_e = 1.0 / _pool_counts(i * tm, e, -HALO, left, right, t)
            epe[:, c0:c0 + PG] = dpm * inv_e
            s_adj = epe[HALO - right:HALO - right + tm, c0:c0 + PG]
            for j in range(-right + 1, left + 1):
                s_adj = s_adj + epe[HALO + j:HALO + j + tm, c0:c0 + PG]
            du_parts.append(s_adj - dpm[HALO:HALO + tm])
            inv_m = 1.0 / _pool_counts(i * tm, tm, 0, left, right, t)
            pm = _pool_minus_id(pe, gi, left, right, inv_m, tm).astype(BF16)
            ybpre = _nn(pm, wp_ref[gi])
            sums_ref[7:8, c0:c0 + PG] += _rowsum(dyb[HALO:HALO + tm] * ybpre)
            dwp_ref[gi] += _tn(pm, dybs[HALO:HALO + tm])
            cat_parts.append(ybpre * scale)
        du = jnp.concatenate(du_parts, axis=-1).astype(BF16)
        du_ref[...] = du
        cat_ref[...] = jnp.concatenate(cat_parts, axis=-1).astype(BF16)
        dy_ref[...] = dye[HALO:HALO + tm, :]
        dh = _nn(du, wi_s[...].reshape(2 * D, D))
        g, sc, sh = vec_ref[0:1], vec_ref[1:2], vec_ref[2:3]
        h, nrm, r = _norm_fwd(x_ref[...], g, sc, sh)
        h_ref[...] = h.astype(BF16)
        dxi_ref[...] = dxo_m + _norm_bwd(dh, nrm, r, g, sc)
        sums_ref[1:2, :] += _rowsum(dh)
        sums_ref[2:3, :] += _rowsum(dh * nrm)

        @pl.when(i == n - 1)
        def _():
            p = sums_ref[2:3, :]
            sums_ref[3:4, :] = p * (1.0 + sc)
            sums_ref[2:3, :] = p * g

    prev_u, nxt_u = _halo_specs(tm, 2 * D, t)
    prev_d, nxt_d = _halo_specs(tm, D, t)
    return _host_call(
        body, comm, name=f"mix_even_bwd_l{layer}", grid=(n,),
        in_specs=[_tile(tm, D), prev_d, nxt_d, _tile(tm, 2 * D), prev_u, nxt_u, _tile(tm, D), _tile(tm, D),
                  _full((8, D)), _full((3, DA)), _full((4, PG, PG)), _full((1, DA)), ANY],
        out_specs=[_tile(tm, D), _tile(tm, 2 * D), _tile(tm, D), _tile(tm, D), _tile(tm, D),
                   _full((16, D)), _full((4, PG, PG))],
        out_shape=[jax.ShapeDtypeStruct((t, D), F32), jax.ShapeDtypeStruct((t, 2 * D), BF16),
                   jax.ShapeDtypeStruct((t, D), BF16), jax.ShapeDtypeStruct((t, D), BF16),
                   jax.ShapeDtypeStruct((t, D), BF16), jax.ShapeDtypeStruct((16, D), F32),
                   jax.ShapeDtypeStruct((4, PG, PG), F32)],
        scratch_shapes=[pltpu.VMEM((NDEV, ROWS_OUT, D), BF16), pltpu.VMEM((NDEV, ROWS_IN, D), BF16),
                        pltpu.VMEM((e, D), BF16), pltpu.VMEM((e, DA), F32), pltpu.VMEM((e, DA), F32),
                        pltpu.VMEM((e, DA), F32), pltpu.VMEM((e, DA), F32), pltpu.VMEM((e, DA), F32),
                        pltpu.SemaphoreType.DMA((2,))],
        operands=(dxo, dxo, dxo, u, u, u, x, y, vec, ca, wp, ps, gw))


def _fill_glu(ze, part_ref, lo, rows, valid):
    a = part_ref[:, 0:D].astype(F32)
    g = part_ref[:, D:2 * D].astype(F32)
    z = a * _sig(g)
    ze[lo:lo + rows, :] = z if valid is None else jnp.where(valid, z, 0.0)


SHIFT_ROWS = 24


def _shifted_copies(dst, src, tm):
    rows = tm + SHIFT_ROWS
    for j in range(8):
        dst[j, :, :] = src[j:j + rows, :]


def _shifted(dst, shift, tm):
    lo = shift // 8 * 8
    return dst[shift % 8, lo:lo + tm, :]


def _layer_norm_parts(z2):
    mu = _lanemean(z2)
    d = z2 - mu
    rstd = lax.rsqrt(_lanemean(d * d) + LN_EPS)
    return d * rstd, rstd


def _mix_odd_fwd(u, x, vec, wdw, sm, gw, layer, tm):
    t = x.shape[0]
    n = t // tm
    e = tm + 2 * HALO

    def body(u_ref, up_ref, un_ref, x_ref, vec_ref, wdw_ref, sm_ref, g_ref, xo_ref, y_ref, z2_ref, w_s, ze, zsh, sems):
        i = pl.program_id(0)

        @pl.when(i == 0)
        def _():
            _load_weights(g_ref, [(OFF_OUT, ROWS_OUT, w_s)], sems)

        _fill_glu(ze, up_ref, 0, HALO, i > 0)
        _fill_glu(ze, u_ref, HALO, tm, None)
        _fill_glu(ze, un_ref, HALO + tm, HALO, i < n - 1)
        _shifted_copies(zsh, ze, tm)
        z2 = sm_ref[0:1] + wdw_ref[0:1] * _shifted(zsh, 1, tm)
        for k in range(1, CONF_K):
            z2 = z2 + wdw_ref[k:k + 1] * _shifted(zsh, 1 + k, tm)
        z2_ref[...] = z2.astype(BF16)
        zn, _ = _layer_norm_parts(z2)
        lo = zn * sm_ref[1:2] + sm_ref[2:3]
        z3 = lo * _sig(lo)
        y = _nn(z3.astype(BF16), w_s[...].reshape(D, D)) + sm_ref[3:4]
        y_ref[...] = y.astype(BF16)
        xo_ref[...] = x_ref[...] + vec_ref[3:4] * y

    prev, nxt = _halo_specs(tm, 2 * D, t)
    return pl.pallas_call(
        body, name=f"mix_odd_fwd_l{layer}", grid=(n,),
        in_specs=[_tile(tm, 2 * D), prev, nxt, _tile(tm, D), _full((8, D)), _full((32, D)), _full((8, D)), ANY],
        out_specs=[_tile(tm, D), _tile(tm, D), _tile(tm, D)],
        out_shape=[jax.ShapeDtypeStruct((t, D), F32), jax.ShapeDtypeStruct((t, D), BF16),
                   jax.ShapeDtypeStruct((t, D), BF16)],
        scratch_shapes=[pltpu.VMEM((NDEV, ROWS_OUT, D), BF16), pltpu.VMEM((e, D), F32),
                        pltpu.VMEM((8, tm + SHIFT_ROWS, D), F32), pltpu.SemaphoreType.DMA((1,))],
        compiler_params=_cparams("arbitrary"),
    )(u, u, u, x, vec, wdw, sm, gw)


def _mix_odd_bwd1(dxo, y, z2, vec, sm, gw, layer, tm):
    t = dxo.shape[0]
    n = t // tm

    def body(dxo_ref, y_ref, z2_ref, vec_ref, sm_ref, g_ref, dy_ref, z3_ref, dz2_ref, sums_ref, w_s, sems):
        i = pl.program_id(0)

        @pl.when(i == 0)
        def _():
            _load_weights(g_ref, [(OFF_OUT, ROWS_OUT, w_s)], sems)
            sums_ref[...] = jnp.zeros_like(sums_ref)

        dxo_m = dxo_ref[...]
        dy = vec_ref[3:4] * dxo_m
        dyb = dy.astype(BF16)
        dy_ref[...] = dyb
        sums_ref[0:1, :] += _rowsum(dxo_m * y_ref[...].astype(F32))
        sums_ref[4:5, :] += _rowsum(dy)
        dz3 = _nt(dyb, w_s[...].reshape(D, D))
        zn, rstd = _layer_norm_parts(z2_ref[...].astype(F32))
        lo = zn * sm_ref[1:2] + sm_ref[2:3]
        sg = _sig(lo)
        z3_ref[...] = (lo * sg).astype(BF16)
        dlo = dz3 * (sg * (1.0 + lo * (1.0 - sg)))
        sums_ref[5:6, :] += _rowsum(dlo * zn)
        sums_ref[6:7, :] += _rowsum(dlo)
        dzn = dlo * sm_ref[1:2]
        dz2 = rstd * (dzn - _lanemean(dzn) - zn * _lanemean(dzn * zn))
        sums_ref[7:8, :] += _rowsum(dz2)
        dz2_ref[...] = dz2.astype(BF16)

    return pl.pallas_call(
        body, name=f"mix_odd_bwd1_l{layer}", grid=(n,),
        in_specs=[_tile(tm, D), _tile(tm, D), _tile(tm, D), _full((8, D)), _full((8, D)), ANY],
        out_specs=[_tile(tm, D), _tile(tm, D), _tile(tm, D), _full((16, D))],
        out_shape=[jax.ShapeDtypeStruct((t, D), BF16)] * 3 + [jax.ShapeDtypeStruct((16, D), F32)],
        scratch_shapes=[pltpu.VMEM((NDEV, ROWS_OUT, D), BF16), pltpu.SemaphoreType.DMA((1,))],
        compiler_params=_cparams("arbitrary"),
    )(dxo, y, z2, vec, sm, gw)


def _mix_odd_bwd2(dz2, u, x, dxo, vec, wdw, gw, layer, tm, comm=None):
    t = x.shape[0]
    n = t // tm
    e = tm + 2 * HALO

    def body(dz_ref, dzp_ref, dzn_ref, u_ref, x_ref, dxo_ref, vec_ref, wdw_ref, g_ref,
             dxi_ref, du_ref, h_ref, sums_ref, dw_ref, w_s, de, zsh, sems):
        i = pl.program_id(0)

        @pl.when(i == 0)
        def _():
            _load_weights(g_ref, [(OFF_IN, ROWS_IN, w_s)], sems)
            sums_ref[...] = jnp.zeros_like(sums_ref)
            dw_ref[...] = jnp.zeros_like(dw_ref)

        de[0:HALO, :] = jnp.where(i > 0, dzp_ref[...].astype(F32), 0.0)
        de[HALO:HALO + tm, :] = dz_ref[...].astype(F32)
        de[HALO + tm:e, :] = jnp.where(i < n - 1, dzn_ref[...].astype(F32), 0.0)
        a = u_ref[:, 0:D].astype(F32)
        gg = u_ref[:, D:2 * D].astype(F32)
        sg = _sig(gg)
        z = a * sg
        _shifted_copies(zsh, de, tm)
        dz = None
        for k in range(CONF_K):
            shifted = _shifted(zsh, CONF_K - k, tm)
            term = wdw_ref[k:k + 1] * shifted
            dz = term if dz is None else dz + term
            dw_ref[k:k + 1, :] += _rowsum(z * shifted)
        da = dz * sg
        dg = dz * a * (sg * (1.0 - sg))
        sums_ref[8:9, :] += _rowsum(da)
        sums_ref[9:10, :] += _rowsum(dg)
        du = jnp.concatenate([da, dg], axis=-1).astype(BF16)
        du_ref[...] = du
        dh = _nn(du, w_s[...].reshape(2 * D, D))
        g, sc, sh = vec_ref[0:1], vec_ref[1:2], vec_ref[2:3]
        h, nrm, r = _norm_fwd(x_ref[...], g, sc, sh)
        h_ref[...] = h.astype(BF16)
        dxi_ref[...] = dxo_ref[...] + _norm_bwd(dh, nrm, r, g, sc)
        sums_ref[1:2, :] += _rowsum(dh)
        sums_ref[2:3, :] += _rowsum(dh * nrm)

        @pl.when(i == n - 1)
        def _():
            p = sums_ref[2:3, :]
            sums_ref[3:4, :] = p * (1.0 + sc)
            sums_ref[2:3, :] = p * g

    prev_d, nxt_d = _halo_specs(tm, D, t)
    return _host_call(
        body, comm, name=f"mix_odd_bwd2_l{layer}", grid=(n,),
        in_specs=[_tile(tm, D), prev_d, nxt_d, _tile(tm, 2 * D), _tile(tm, D), _tile(tm, D),
                  _full((8, D)), _full((32, D)), ANY],
        out_specs=[_tile(tm, D), _tile(tm, 2 * D), _tile(tm, D), _full((16, D)), _full((32, D))],
        out_shape=[jax.ShapeDtypeStruct((t, D), F32), jax.ShapeDtypeStruct((t, 2 * D), BF16),
                   jax.ShapeDtypeStruct((t, D), BF16), jax.ShapeDtypeStruct((16, D), F32),
                   jax.ShapeDtypeStruct((32, D), F32)],
        scratch_shapes=[pltpu.VMEM((NDEV, ROWS_IN, D), BF16), pltpu.VMEM((e, D), F32),
                        pltpu.VMEM((8, tm + SHIFT_ROWS, D), F32),
                        pltpu.SemaphoreType.DMA((1,))],
        operands=(dz2, dz2, dz2, u, x, dxo, vec, wdw, gw))


FCH = FF // 2


def _loss_head(x, tgt, g):
    r = lax.rsqrt(_lanemean(x * x) + RMS_EPS)
    nrm = x * r
    err = nrm * g - tgt
    dout = err * (1.0 / D)
    dn = dout * g
    return r * (dn - nrm * _lanemean(dn * nrm)), _rowsum(err * err) * (0.5 / D), _rowsum(dout * nrm)


def _ffn_fwd(x, vec, wsrc, layer, tm, comm=None, head=None):
    t = x.shape[0]
    g_gu, off_g, off_u, g_d, off_d = wsrc

    def body(*refs):
        if head is None:
            x_ref, vec_ref, ggu_ref, gd_ref, xo_ref, a_ref, b_ref, y_ref, wg_s, wu_s, wd_s, sems = refs
        else:
            (x_ref, vec_ref, ggu_ref, gd_ref, t_ref, gf_ref, xo_ref, a_ref, b_ref, y_ref, fsum_ref,
             wg_s, wu_s, wd_s, sems) = refs

        @pl.when(pl.program_id(0) == 0)
        def _():
            _load_weight_rows([(ggu_ref, off_g, FS, wg_s), (ggu_ref, off_u, FS, wu_s), (gd_ref, off_d, FS, wd_s)], sems)
            if head is not None:
                fsum_ref[...] = jnp.zeros_like(fsum_ref)

        xv = x_ref[...]
        h, _, _ = _norm_fwd(xv, vec_ref[0:1], vec_ref[1:2], vec_ref[2:3])
        hb = h.astype(BF16)
        y = jnp.zeros((tm, D), F32)
        for ch in range(2):
            rows = slice(ch * FCH, (ch + 1) * FCH)
            a = _nt(hb, wg_s[rows, :])
            b = _nt(hb, wu_s[rows, :])
            a_ref[:, rows] = a.astype(BF16)
            b_ref[:, rows] = b.astype(BF16)
            s = (a * _sig(a)) * b
            y = y + _nn(s.astype(BF16), wd_s[rows, :])
        y_ref[...] = y.astype(BF16)
        x_out = xv + vec_ref[3:4] * y
        if head is None:
            xo_ref[...] = x_out
        else:
            dx, loss_row, dg_row = _loss_head(x_out, t_ref[...], gf_ref[...])
            xo_ref[...] = dx
            fsum_ref[0:1, :] += dg_row
            fsum_ref[1:2, :] += loss_row

    wsc = pltpu.VMEM((FF, D), BF16)
    in_specs = [_tile(tm, D), _full((8, D)), ANY, ANY]
    out_specs = [_tile(tm, D), _tile(tm, FF), _tile(tm, FF), _tile(tm, D)]
    out_shape = [jax.ShapeDtypeStruct((t, D), F32), jax.ShapeDtypeStruct((t, FF), BF16),
                 jax.ShapeDtypeStruct((t, FF), BF16), jax.ShapeDtypeStruct((t, D), BF16)]
    operands = (x, vec, g_gu, g_d)
    if head is not None:
        in_specs += [_tile(tm, D), _full((1, D))]
        out_specs.append(_full((8, D)))
        out_shape.append(jax.ShapeDtypeStruct((8, D), F32))
        operands += tuple(head)
    return _host_call(
        body, comm, name=f"ffn_fwd_l{layer}", grid=(t // tm,),
        in_specs=in_specs, out_specs=out_specs, out_shape=out_shape,
        scratch_shapes=[wsc, wsc, wsc, pltpu.SemaphoreType.DMA((3 * NDEV,))],
        operands=operands)


def _ffn_bwd(dxo, x, y, a, b, vec, wsrc, layer, tm, comm=None):
    t = x.shape[0]
    n = t // tm
    g_gu, off_g, off_u, g_d, off_d = wsrc

    def body(dxo_ref, x_ref, y_ref, a_ref, b_ref, vec_ref, ggu_ref, gd_ref,
             dxi_ref, h_ref, dy_ref, s_ref, da_ref, db_ref, sums_ref, wg_s, wu_s, wd_s, sems):
        i = pl.program_id(0)

        @pl.when(i == 0)
        def _():
            _load_weight_rows([(ggu_ref, off_g, FS, wg_s), (ggu_ref, off_u, FS, wu_s), (gd_ref, off_d, FS, wd_s)], sems)
            sums_ref[...] = jnp.zeros_like(sums_ref)

        dxo_m = dxo_ref[...]
        sums_ref[0:1, :] += _rowsum(dxo_m * y_ref[...].astype(F32))
        dyb = (vec_ref[3:4] * dxo_m).astype(BF16)
        dy_ref[...] = dyb
        for ch in range(FF // MXU_N):
            cols = slice(ch * MXU_N, (ch + 1) * MXU_N)
            ds = _nt(dyb, wd_s[cols, :]).astype(BF16)
            av = a_ref[:, cols]
            bv = b_ref[:, cols]
            sg = _sig(av)
            sl = av * sg
            s_ref[:, cols] = sl * bv
            db_ref[:, cols] = ds * sl
            da_ref[:, cols] = (ds * bv) * (sg * (1.0 + av * (1.0 - sg)))
        dh = _nn(da_ref[...], wg_s[...]) + _nn(db_ref[...], wu_s[...])
        g, sc, sh = vec_ref[0:1], vec_ref[1:2], vec_ref[2:3]
        h, nrm, r = _norm_fwd(x_ref[...], g, sc, sh)
        h_ref[...] = h.astype(BF16)
        dxi_ref[...] = dxo_m + _norm_bwd(dh, nrm, r, g, sc)
        sums_ref[1:2, :] += _rowsum(dh)
        sums_ref[2:3, :] += _rowsum(dh * nrm)

        @pl.when(i == n - 1)
        def _():
            p = sums_ref[2:3, :]
            sums_ref[3:4, :] = p * (1.0 + sc)
            sums_ref[2:3, :] = p * g

    wsc = pltpu.VMEM((FF, D), BF16)
    big, small = jax.ShapeDtypeStruct((t, FF), BF16), jax.ShapeDtypeStruct((t, D), BF16)
    return _host_call(
        body, comm, name=f"ffn_bwd_l{layer}", grid=(n,),
        in_specs=[_tile(tm, D), _tile(tm, D), _tile(tm, D), _tile(tm, FF), _tile(tm, FF), _full((8, D)), ANY, ANY],
        out_specs=[_tile(tm, D), _tile(tm, D), _tile(tm, D), _tile(tm, FF), _tile(tm, FF), _tile(tm, FF),
                   _full((8, D))],
        out_shape=[jax.ShapeDtypeStruct((t, D), F32), small, small, big, big, big, jax.ShapeDtypeStruct((8, D), F32)],
        scratch_shapes=[wsc, wsc, wsc, pltpu.SemaphoreType.DMA((3 * NDEV,))],
        operands=(dxo, x, y, a, b, vec, g_gu, g_d))


def _wgrad(lhs, rhs, name, tk):
    t, m = lhs.shape
    n = t // tk

    def body(l_ref, r_ref, o_ref, acc):
        i = pl.program_id(0)

        @pl.when(i == 0)
        def _():
            acc[...] = jnp.zeros_like(acc)

        acc[...] += _tn(l_ref[...], r_ref[...])

        @pl.when(i == n - 1)
        def _():
            o_ref[...] = acc[...].astype(BF16)

    return pl.pallas_call(
        body, name=name, grid=(n,),
        in_specs=[_tile(tk, m), _tile(tk, D)], out_specs=_full((m, D)),
        out_shape=jax.ShapeDtypeStruct((m, D), BF16),
        scratch_shapes=[pltpu.VMEM((m, D), F32)],
        compiler_params=_cparams("arbitrary"),
    )(lhs, rhs)


ADAM_ROWS = LROWS // 5


def _adam_big(recv, w, m, v):
    def body(r_ref, w_ref, m_ref, v_ref, g_ref, d_ref, mo_ref, vo_ref):
        g = r_ref[0, 0].astype(F32)
        for s in range(1, NDEV):
            g = g + r_ref[s, 0].astype(F32)
        delta, m2, v2 = _adam(w_ref[0], g, m_ref[0], v_ref[0])
        g_ref[0], d_ref[0], mo_ref[0], vo_ref[0] = g, delta, m2, v2

    blk = pl.BlockSpec((1, ADAM_ROWS, D), lambda l, j: (l, j, 0))
    sds = jax.ShapeDtypeStruct(w.shape, F32)
    return pl.pallas_call(
        body, name="adam_big", grid=(DEPTH, LROWS // ADAM_ROWS),
        in_specs=[pl.BlockSpec((NDEV, 1, ADAM_ROWS, D), lambda l, j: (0, l, j, 0)), blk, blk, blk],
        out_specs=[blk] * 4, out_shape=[sds] * 4,
        compiler_params=_cparams("arbitrary", "arbitrary"),
    )(recv, w, m, v)


def _sum_small(gathered, rows):
    def body(g_ref, o_ref):
        acc = g_ref[0:rows, :]
        for s in range(1, NDEV):
            acc = acc + g_ref[s * rows:(s + 1) * rows, :]
        o_ref[...] = acc

    return pl.pallas_call(
        body, name="sum_small",
        in_specs=[pl.BlockSpec(memory_space=pltpu.VMEM)], out_specs=pl.BlockSpec(memory_space=pltpu.VMEM),
        out_shape=jax.ShapeDtypeStruct((rows, D), F32),
        compiler_params=pltpu.CompilerParams(vmem_limit_bytes=VMEM_LIMIT),
    )(gathered)


def _adam_small(params):
    k = len(params)

    def body(*refs):
        ins, outs = refs[:4 * k], refs[4 * k:]
        for j in range(k):
            w_ref, g_ref, m_ref, v_ref = ins[4 * j:4 * j + 4]
            delta, m2, v2 = _adam(w_ref[...], g_ref[...], m_ref[...], v_ref[...])
            outs[3 * j][...], outs[3 * j + 1][...], outs[3 * j + 2][...] = delta, m2, v2

    flat = [a for p in params for a in p]
    shapes = [jax.ShapeDtypeStruct(p[0].shape, F32) for p in params for _ in range(3)]
    vm = pl.BlockSpec(memory_space=pltpu.VMEM)
    res = pl.pallas_call(
        body, name="adam_small", in_specs=[vm] * len(flat), out_specs=[vm] * len(shapes), out_shape=shapes,
        compiler_params=pltpu.CompilerParams(vmem_limit_bytes=VMEM_LIMIT),
    )(*flat)
    return [tuple(res[3 * j:3 * j + 3]) for j in range(k)]


def _pack(ab_in, ab_out, pw1, pw2, wg, wu, wd):
    ins = jnp.swapaxes(jnp.stack([ab_in[0], pw1[0], ab_in[1], pw1[1]]), 1, 2)
    outs = jnp.stack([ab_out[0], pw2[0], ab_out[1], pw2[1]])
    return jnp.concatenate([ins, outs, jnp.swapaxes(wg, 1, 2), jnp.swapaxes(wu, 1, 2), wd], axis=1)


def _unpack(p):
    ins = jnp.swapaxes(p[:, OFF_IN:OFF_OUT], 1, 2)
    outs = p[:, OFF_OUT:OFF_G]
    return (ins[0::2], outs[0::2], ins[1::2], outs[1::2], jnp.swapaxes(p[:, OFF_G:OFF_U], 1, 2),
            jnp.swapaxes(p[:, OFF_U:OFF_D], 1, 2), p[:, OFF_D:LROWS])


def _unshard(flat, lead, per):
    k = len(lead)
    a = flat.reshape((NDEV,) + tuple(lead) + (per,))
    a = jnp.transpose(a, tuple(range(1, k + 1)) + (0, k + 1))
    return a.reshape(tuple(lead) + (NDEV * per,))


def _rows_of(a):
    f = a.reshape(-1)
    pad = (-f.shape[0]) % D
    if pad:
        f = jnp.concatenate([f, jnp.zeros((pad,), f.dtype)])
    return f.reshape(-1, D)


def _pad_rows(a, rows):
    return jnp.concatenate([a, jnp.zeros((rows - a.shape[0],) + a.shape[1:], a.dtype)], axis=0)


def kernel(x, c, norm_mix_g, norm_ffn_g, w_mod, b_mod, ab_w_in, ab_conv, ab_w_pool, ab_pool_scale, ab_w_out, cf_w_pw1, cf_b_pw1, cf_w_dw, cf_b_dw, cf_ln_g, cf_ln_b, cf_w_pw2, cf_b_pw2, ffn_w_gate, ffn_w_up, ffn_w_down, final_norm_g, loss_target, m_norm_mix_g, m_norm_ffn_g, m_w_mod, m_b_mod, m_ab_w_in, m_ab_conv, m_ab_w_pool, m_ab_pool_scale, m_ab_w_out, m_cf_w_pw1, m_cf_b_pw1, m_cf_w_dw, m_cf_b_dw, m_cf_ln_g, m_cf_ln_b, m_cf_w_pw2, m_cf_b_pw2, m_ffn_w_gate, m_ffn_w_up, m_ffn_w_down, m_final_norm_g, v_norm_mix_g, v_norm_ffn_g, v_w_mod, v_b_mod, v_ab_w_in, v_ab_conv, v_ab_w_pool, v_ab_pool_scale, v_ab_w_out, v_cf_w_pw1, v_cf_b_pw1, v_cf_w_dw, v_cf_b_dw, v_cf_ln_g, v_cf_ln_b, v_cf_w_pw2, v_cf_b_pw2, v_ffn_w_gate, v_ffn_w_up, v_ffn_w_down, v_final_norm_g):
    t = x.shape[1]
    tm = 512 if t % 512 == 0 else t // 2
    tk = 1024 if t % 1024 == 0 else t // 2
    tmo = tm
    tmb = tm // 2
    me = 4 * lax.axis_index("x") + 2 * lax.axis_index("y") + lax.axis_index("c")
    xs, tgt = x[0], loss_target[0]

    w_pack = _pack(ab_w_in, ab_w_out, cf_w_pw1, cf_w_pw2, ffn_w_gate, ffn_w_up, ffn_w_down)
    p16 = w_pack.astype(BF16)

    sharded = [ab_conv, cf_b_pw1, cf_w_dw, cf_b_dw, cf_ln_g, cf_ln_b, cf_b_pw2]
    flat = jnp.concatenate([a.reshape(-1) for a in sharded])
    n_flat = flat.shape[0]
    g1, g_io0 = _head_comm(jnp.concatenate([_pad_rows(c, 8), _pad_rows(_rows_of(flat), 16)], axis=0),
                           p16[0, OFF_IN:OFF_G])
    c_all = g1[:, 0, :]
    flat_all = g1[:, 8:, :].reshape(NDEV, -1)[:, :n_flat]
    full, o = [], 0
    for a in sharded:
        lead, per = a.shape[:-1], a.shape[-1]
        size = a.size
        full.append(_unshard(flat_all[:, o:o + size], lead, per))
        o += size
    ab_conv_f, b_pw1_f, w_dw_f, b_dw_f, ln_g_f, ln_b_f, b_pw2_f = full

    b_sl = lax.dynamic_slice_in_dim(b_mod, me * MODW, MODW, axis=1).reshape(DEPTH, 1, MODW)
    mod_part, c_act = _mod_fwd(c_all, w_mod, b_sl)
    g2 = _gather_small(mod_part, "gather_mod").reshape(NDEV, NDEV, DEPTH, MODW)
    mod = jnp.transpose(lax.dynamic_index_in_dim(g2, me, axis=1, keepdims=False), (1, 0, 2)).reshape(DEPTH, N_MOD, D)
    zeros4 = jnp.zeros((4, D), F32)

    def vec_of(g, layer, k):
        return jnp.concatenate([g[layer][None], mod[layer, k + 1][None], mod[layer, k][None],
                                mod[layer, k + 2][None], zeros4], axis=0)

    vmix = [vec_of(norm_mix_g, l, 0) for l in range(DEPTH)]
    vffn = [vec_of(norm_ffn_g, l, 3) for l in range(DEPTH)]

    wp16 = ab_w_pool.astype(BF16)
    wdw32 = [_pad_rows(w_dw_f[i], 32) for i in range(2)]
    sm_odd = [jnp.concatenate([b_dw_f[i][None], ln_g_f[i][None], ln_b_f[i][None], b_pw2_f[i][None], zeros4], axis=0)
              for i in range(2)]
    zero_bias = jnp.zeros((1, 2 * D), F32)

    saved = []
    xc = xs
    gw = [g_io0]
    wsrc = []
    for l in range(DEPTH):
        i = l // 2
        if l == 0:
            u, g_gu0 = _inproj(xc, vmix[l], zero_bias, gw[l], l, tm, _gather_comm(p16[0, OFF_G:OFF_D]))
            x_mid, y_mix, g_d0 = _mix_even_fwd(u, xc, vmix[l], ab_conv_f[i], wp16[i], ab_pool_scale[i][None], gw[l],
                                               l, tm, _gather_comm(p16[0, OFF_D:LROWS]))
            wsrc.append((g_gu0, 0, FS, g_d0, 0))
            z2 = None
        elif l % 2 == 0:
            u = _inproj(xc, vmix[l], zero_bias, gw[l], l, tm)
            x_mid, y_mix = _mix_even_fwd(u, xc, vmix[l], ab_conv_f[i], wp16[i], ab_pool_scale[i][None], gw[l], l, tm)
            z2 = None
        else:
            u = _inproj(xc, vmix[l], b_pw1_f[i][None], gw[l], l, tm)
            x_mid, y_mix, z2 = _mix_odd_fwd(u, xc, vmix[l], wdw32[i], sm_odd[i], gw[l], l, tmo)
        if l + 1 < DEPTH:
            x_out, a, b, y_ffn, g_next = _ffn_fwd(x_mid, vffn[l], wsrc[l], l, tm, _gather_comm(p16[l + 1]))
            gw.append(g_next)
            wsrc.append((g_next, OFF_G, OFF_U, g_next, OFF_D))
        else:
            x_out = None
            dx, a, b, y_ffn, fsum = _ffn_fwd(x_mid, vffn[l], wsrc[l], l, tmb, head=(tgt, final_norm_g[None]))
        saved.append((xc, u, y_mix, z2, x_mid, a, b, y_ffn))
        xc = x_out

    loss = lax.psum(jnp.sum(fsum[1]), ("x", "y", "c"))
    d_final_g = fsum[0]

    recv = _empty_recv()
    late_specs = [(OFF_U, FS), (OFF_IN, ROWS_IN), (OFF_OUT, ROWS_OUT)]
    pending = None
    dmod = [None] * DEPTH
    d_mix_g, d_ffn_g = [None] * DEPTH, [None] * DEPTH
    d_conv, d_pool, d_pscale = [None] * 2, [None] * 2, [None] * 2
    d_bpw1, d_wdw, d_bdw, d_lng, d_lnb, d_bpw2 = ([None] * 2 for _ in range(6))
    for l in reversed(range(DEPTH)):
        i = l // 2
        x_in, u, y_mix, z2, x_mid, a, b, y_ffn = saved[l]
        if pending is None:
            dx_mid, h2, dy, s, da, db, s_f = _ffn_bwd(dx, x_mid, y_ffn, a, b, vffn[l], wsrc[l], l, tmb)
        else:
            dx_mid, h2, dy, s, da, db, s_f, recv = _ffn_bwd(dx, x_mid, y_ffn, a, b, vffn[l], wsrc[l], l, tmb,
                                                            _scatter_comm(pending, late_specs, recv, l + 1))
        g_down = _wgrad(s, dy, f"wgrad_down_l{l}", tk)
        g_gate = _wgrad(da, h2, f"wgrad_gate_l{l}", tk)
        gu_comm = _scatter_comm([g_down, g_gate], [(OFF_D, FS), (OFF_G, FS)], recv, l)
        g_up = _wgrad(db, h2, f"wgrad_up_l{l}", tk)
        d_ffn_g[l] = s_f[3]
        mod_ffn = [s_f[1], s_f[2], s_f[0]]
        if l % 2 == 0:
            dx, du, h, cat, dym, s_m, dwp, recv = _mix_even_bwd(dx_mid, u, x_in, y_mix, vmix[l], ab_conv_f[i], wp16[i],
                                                                ab_pool_scale[i][None], gw[l], l, tm, gu_comm)
            g_out = _wgrad(cat, dym, f"wgrad_out_l{l}", tk)
            d_conv[i], d_pool[i], d_pscale[i] = s_m[4:7, :DA], dwp, s_m[7, :DA]
            mod_mix = [s_m[1], s_m[2], s_m[0]]
            d_mix_g[l] = s_m[3]
        else:
            dym, z3, dz2, s_1 = _mix_odd_bwd1(dx_mid, y_mix, z2, vmix[l], sm_odd[i], gw[l], l, tm)
            dx, du, h, s_2, dwdw, recv = _mix_odd_bwd2(dz2, u, x_in, dx_mid, vmix[l], wdw32[i], gw[l], l, tmo, gu_comm)
            g_out = _wgrad(z3, dym, f"wgrad_out_l{l}", tk)
            d_bpw1[i], d_wdw[i], d_bdw[i] = s_2[8:10].reshape(2 * D), dwdw[:CONF_K], s_1[7]
            d_lng[i], d_lnb[i], d_bpw2[i] = s_1[5], s_1[6], s_1[4]
            mod_mix = [s_2[1], s_2[2], s_1[0]]
            d_mix_g[l] = s_2[3]
        dmod[l] = jnp.stack(mod_mix + mod_ffn)
        pending = [g_up, _wgrad(du, h, f"wgrad_in_l{l}", tk), g_out]
    grad_x = dx[None]

    small = [jnp.stack(dmod), jnp.stack(d_mix_g), jnp.stack(d_ffn_g), d_final_g, jnp.stack(d_pool),
             jnp.stack(d_pscale), jnp.stack(d_conv), jnp.stack(d_bpw1), jnp.stack(d_wdw), jnp.stack(d_bdw),
             jnp.stack(d_lng), jnp.stack(d_lnb), jnp.stack(d_bpw2)]
    small_rows = [_rows_of(a) for a in small]
    n_rows = sum(a.shape[0] for a in small_rows)
    pad_rows = -(-n_rows // 8) * 8
    recv, g3 = _tail_comm(pending, late_specs, recv, 0, _pad_rows(jnp.concatenate(small_rows, axis=0), pad_rows))
    g3 = g3.reshape(NDEV * pad_rows, D)
    summed = _sum_small(g3, pad_rows)

    m_pack = _pack(m_ab_w_in, m_ab_w_out, m_cf_w_pw1, m_cf_w_pw2, m_ffn_w_gate, m_ffn_w_up, m_ffn_w_down)
    v_pack = _pack(v_ab_w_in, v_ab_w_out, v_cf_w_pw1, v_cf_w_pw2, v_ffn_w_gate, v_ffn_w_up, v_ffn_w_down)
    big = [_unpack(p) for p in _adam_big(recv, w_pack, m_pack, v_pack)]

    outs, o = [], 0
    for a, r in zip(small, small_rows):
        outs.append(summed[o:o + r.shape[0]].reshape(-1)[:a.size].reshape(a.shape))
        o += r.shape[0]
    (g_bmod, g_mix_g, g_ffn_g, g_final, g_pool, g_pscale, g_conv, g_bpw1, g_wdw, g_bdw, g_lng, g_lnb, g_bpw2) = outs
    g_bmod = g_bmod.reshape(DEPTH, N_MOD * D)

    def my_shard(a):
        per = a.shape[-1] // NDEV
        return lax.dynamic_slice_in_dim(a, me * per, per, axis=a.ndim - 1)

    g_conv, g_bpw1, g_wdw, g_bdw, g_lng, g_lnb, g_bpw2 = [
        my_shard(a) for a in (g_conv, g_bpw1, g_wdw, g_bdw, g_lng, g_lnb, g_bpw2)]

    dmod_all = g3.reshape(NDEV, pad_rows, D)[:, :DEPTH * N_MOD, :].reshape(NDEV, DEPTH, N_MOD * D)
    dmod_mine = jnp.transpose(lax.dynamic_slice_in_dim(dmod_all, me * MODW, MODW, axis=2), (1, 0, 2))
    g_wmod, d_wmod, nm_wmod, nv_wmod = _mod_bwd_adam(c_act.T, dmod_mine, w_mod, m_w_mod, v_w_mod)

    small_params = [
        (norm_mix_g, g_mix_g, m_norm_mix_g, v_norm_mix_g), (norm_ffn_g, g_ffn_g, m_norm_ffn_g, v_norm_ffn_g),
        (b_mod, g_bmod, m_b_mod, v_b_mod), (ab_conv, g_conv, m_ab_conv, v_ab_conv),
        (ab_w_pool, g_pool, m_ab_w_pool, v_ab_w_pool), (ab_pool_scale, g_pscale, m_ab_pool_scale, v_ab_pool_scale),
        (cf_b_pw1, g_bpw1, m_cf_b_pw1, v_cf_b_pw1), (cf_w_dw, g_wdw, m_cf_w_dw, v_cf_w_dw),
        (cf_b_dw, g_bdw, m_cf_b_dw, v_cf_b_dw), (cf_ln_g, g_lng, m_cf_ln_g, v_cf_ln_g),
        (cf_ln_b, g_lnb, m_cf_ln_b, v_cf_ln_b), (cf_b_pw2, g_bpw2, m_cf_b_pw2, v_cf_b_pw2),
        (final_norm_g, g_final, m_final_norm_g, v_final_norm_g)]

    def two_d(a):
        return a.reshape(-1, a.shape[-1])

    upd = _adam_small([tuple(two_d(a) for a in p) for p in small_params])
    upd = [tuple(r.reshape(p[0].shape) for r in u) for u, p in zip(upd, small_params)]
    (s_mix, s_ffn, s_bmod, s_conv, s_pool, s_pscale, s_bpw1, s_wdw, s_bdw, s_lng, s_lnb, s_bpw2, s_final) = upd
    small_g = [p[1] for p in small_params]
    (q_mix, q_ffn, q_bmod, q_conv, q_pool, q_pscale, q_bpw1, q_wdw, q_bdw, q_lng, q_lnb, q_bpw2, q_final) = small_g

    def ordered(k):
        ab_in, ab_out, pw1, pw2, wg, wu, wd = big[k]
        if k == 0:
            sm = dict(mix=q_mix, ffn=q_ffn, bmod=q_bmod, conv=q_conv, pool=q_pool, pscale=q_pscale, bpw1=q_bpw1,
                      wdw=q_wdw, bdw=q_bdw, lng=q_lng, lnb=q_lnb, bpw2=q_bpw2, final=q_final)
            wmod = g_wmod
        else:
            j = k - 1
            sm = dict(mix=s_mix[j], ffn=s_ffn[j], bmod=s_bmod[j], conv=s_conv[j], pool=s_pool[j], pscale=s_pscale[j],
                      bpw1=s_bpw1[j], wdw=s_wdw[j], bdw=s_bdw[j], lng=s_lng[j], lnb=s_lnb[j], bpw2=s_bpw2[j],
                      final=s_final[j])
            wmod = (d_wmod, nm_wmod, nv_wmod)[j]
        return [sm["mix"], sm["ffn"], wmod, sm["bmod"], ab_in, sm["conv"], sm["pool"], sm["pscale"], ab_out,
                pw1, sm["bpw1"], sm["wdw"], sm["bdw"], sm["lng"], sm["lnb"], pw2, sm["bpw2"], wg, wu, wd, sm["final"]]

    return (loss, grad_x, *ordered(0), *ordered(1), *ordered(2), *ordered(3))
```

```python
import functools

import jax
import jax.numpy as jnp
from jax import lax
from jax.experimental import pallas as pl
from jax.experimental.pallas import tpu as pltpu

F32 = jnp.float32
BF16 = jnp.bfloat16
MESH = pl.DeviceIdType.MESH

NDEV = 8
DEPTH = 4
D = 1024
FF = 2816
FS = FF // NDEV
DA = 512
PG = 128
POOL = ((2, 1, 0), (4, 2, 1), (8, 4, 3), (16, 8, 7))
CONF_K = 31
CONF_L = 15
N_MOD = 6
MODW = N_MOD * D // NDEV
RMS_EPS = 1e-6
LN_EPS = 1e-5

ROWS_IN, ROWS_OUT = 2 * D // NDEV, D // NDEV
OFF_IN, OFF_OUT = 0, ROWS_IN
OFF_G = OFF_OUT + ROWS_OUT
OFF_U = OFF_G + FS
OFF_D = OFF_U + FS
LROWS = OFF_D + FS

MXU_N = 256
HALO = 16
VMEM_LIMIT = 60 * 1024 * 1024

ADAM_LR, ADAM_B1, ADAM_B2, ADAM_EPS, ADAM_WD, ADAM_STEP = 1e-3, 0.9, 0.999, 1e-8, 0.01, 10
ADAM_C1 = 1.0 / (1.0 - ADAM_B1 ** ADAM_STEP)
ADAM_C2 = 1.0 / (1.0 - ADAM_B2 ** ADAM_STEP)


def _nn(a, b):
    return jnp.dot(a, b, preferred_element_type=F32)


def _nt(a, b):
    return lax.dot_general(a, b, (((1,), (1,)), ((), ())), preferred_element_type=F32)


def _tn(a, b):
    return lax.dot_general(a, b, (((0,), (0,)), ((), ())), preferred_element_type=F32)


def _sig(x):
    return 1.0 / (1.0 + jnp.exp(-x))


def _rowsum(x):
    return jnp.sum(x, axis=0, keepdims=True)


def _lanemean(x):
    return jnp.mean(x, axis=-1, keepdims=True)


def _norm_fwd(x, g, sc, sh):
    r = lax.rsqrt(_lanemean(x * x) + RMS_EPS)
    n = x * r
    return n * (g * (1.0 + sc)) + sh, n, r


def _norm_bwd(dh, n, r, g, sc):
    dn = dh * (g * (1.0 + sc))
    return r * (dn - n * _lanemean(dn * n))


def _adam(w, g, m, v):
    m2 = ADAM_B1 * m + (1.0 - ADAM_B1) * g
    v2 = ADAM_B2 * v + (1.0 - ADAM_B2) * (g * g)
    delta = -ADAM_LR * ((m2 * ADAM_C1) / (jnp.sqrt(v2 * ADAM_C2) + ADAM_EPS) + ADAM_WD * w)
    return delta, m2, v2


def _load_weights(g_ref, specs, sems):
    cps = [pltpu.make_async_copy(g_ref.at[:, pl.ds(off, rows), :], dst, sems.at[k])
           for k, (off, rows, dst) in enumerate(specs)]
    for cp in cps:
        cp.start()
    for cp in cps:
        cp.wait()


def _load_weight_rows(specs, sems):
    cps = [pltpu.make_async_copy(g_ref.at[d, pl.ds(off, rows), :], dst.at[pl.ds(d * rows, rows), :],
                                 sems.at[NDEV * k + d])
           for k, (g_ref, off, rows, dst) in enumerate(specs) for d in range(NDEV)]
    for cp in cps:
        cp.start()
    for cp in cps:
        cp.wait()


def _cparams(*sem):
    return pltpu.CompilerParams(dimension_semantics=sem if sem else None, vmem_limit_bytes=VMEM_LIMIT)


def _tile(tm, w):
    return pl.BlockSpec((tm, w), lambda i: (i, 0))


def _full(shape):
    nd = len(shape)
    return pl.BlockSpec(shape, lambda i: (0,) * nd)


def _halo_specs(tm, w, total_rows):
    tb = tm // HALO
    nb = total_rows // HALO
    prev = pl.BlockSpec((HALO, w), lambda i: (jnp.maximum(i * tb - 1, 0), 0))
    nxt = pl.BlockSpec((HALO, w), lambda i: (jnp.minimum((i + 1) * tb, nb - 1), 0))
    return prev, nxt


ANY = pl.BlockSpec(memory_space=pl.ANY)


def _peers():
    x, y, c = lax.axis_index("x"), lax.axis_index("y"), lax.axis_index("c")
    return x, y, c


def _gather_small(v, name):
    m_per, n = v.shape

    def body(x_ref, out_ref, send_sems, recv_sems, local_sem):
        x, y, c = _peers()
        me, sibling = (x, y, c), (x, y, 1 - c)
        chips = [(1 - x, y), (x, 1 - y), (1 - x, 1 - y)]

        def rows(px, py, pc):
            return out_ref.at[pl.ds((4 * px + 2 * py + pc) * m_per, m_per), :]

        def copy(k, block, to, src=None):
            return pltpu.make_async_remote_copy(
                src_ref=rows(*block) if src is None else src, dst_ref=rows(*block),
                send_sem=send_sems.at[k], recv_sem=recv_sems.at[k], device_id=to, device_id_type=MESH)

        mine = pltpu.make_async_copy(x_ref, rows(*me), local_sem)
        mine.start()
        first = [copy(0, me, sibling, src=x_ref)]
        first += [copy(1 + j, me, (*chip, c), src=x_ref) for j, chip in enumerate(chips)]
        for cp in first:
            cp.start()
        passed = [copy(4 + j, (*chip, c), sibling) for j, chip in enumerate(chips)]
        for j, chip in enumerate(chips):
            copy(1 + j, (*chip, c), me).wait_recv()
            passed[j].start()
        copy(0, sibling, me).wait_recv()
        for j, chip in enumerate(chips):
            copy(4 + j, (*chip, 1 - c), me).wait_recv()
        for cp in first + passed:
            cp.wait_send()
        mine.wait()

    return pl.pallas_call(
        body, name=name,
        out_shape=jax.ShapeDtypeStruct((NDEV * m_per, n), v.dtype),
        in_specs=[pl.BlockSpec(memory_space=pltpu.VMEM)],
        out_specs=pl.BlockSpec(memory_space=pltpu.VMEM),
        scratch_shapes=[pltpu.SemaphoreType.DMA((7,)), pltpu.SemaphoreType.DMA((7,)), pltpu.SemaphoreType.DMA],
        compiler_params=pltpu.CompilerParams(vmem_limit_bytes=VMEM_LIMIT),
    )(v)


class _Comm:
    def __init__(self, ins, outs, aliases, bind):
        self.ins, self.outs, self.aliases, self.bind = ins, outs, aliases, bind


COMM_SEMS = [pltpu.SemaphoreType.DMA((7,)), pltpu.SemaphoreType.DMA((7,)), pltpu.SemaphoreType.DMA]


def _gather_hooks(p_ref, out_ref, send_sems, recv_sems, local_sem):
    def parts():
        x, y, c = _peers()
        me, sibling = (x, y, c), (x, y, 1 - c)
        chips = [(1 - x, y), (x, 1 - y), (1 - x, 1 - y)]

        def slab(px, py, pc):
            return out_ref.at[4 * px + 2 * py + pc]

        def copy(k, block, to, src=None):
            return pltpu.make_async_remote_copy(
                src_ref=slab(*block) if src is None else src, dst_ref=slab(*block),
                send_sem=send_sems.at[k], recv_sem=recv_sems.at[k], device_id=to, device_id_type=MESH)

        def mine():
            return pltpu.make_async_copy(p_ref, slab(*me), local_sem)

        def first():
            return [copy(0, me, sibling, src=p_ref)] + [copy(1 + j, me, (*chip, c), src=p_ref)
                                                        for j, chip in enumerate(chips)]

        def passed():
            return [copy(4 + j, (*chip, c), sibling) for j, chip in enumerate(chips)]

        def from_chips():
            return [copy(1 + j, (*chip, c), me) for j, chip in enumerate(chips)]

        def from_sibling():
            return [copy(0, sibling, me)] + [copy(4 + j, (*chip, 1 - c), me) for j, chip in enumerate(chips)]

        return mine, first, passed, from_chips, from_sibling

    def start():
        mine, first, _, _, _ = parts()
        mine().start()
        for cp in first():
            cp.start()

    def middle():
        _, _, passed, from_chips, _ = parts()
        for arrived, onward in zip(from_chips(), passed()):
            arrived.wait_recv()
            onward.start()

    def end():
        mine, first, passed, _, from_sibling = parts()
        for cp in from_sibling():
            cp.wait_recv()
        for cp in first() + passed():
            cp.wait_send()
        mine().wait()

    return start, middle, end


def _gather_comm(p):
    return _Comm([p], [jax.ShapeDtypeStruct((NDEV,) + p.shape, p.dtype)], {},
                 lambda cins, couts, sems: _gather_hooks(cins[0], couts[0], *sems))


def _head_comm(cond, p):
    def body(c_ref, p_ref, co_ref, po_ref, *sems):
        c_hooks = _gather_hooks(c_ref, co_ref, *sems[:3])
        p_hooks = _gather_hooks(p_ref, po_ref, *sems[3:])
        for step in range(3):
            c_hooks[step]()
            p_hooks[step]()

    return pl.pallas_call(
        body, name="head_comm",
        out_shape=[jax.ShapeDtypeStruct((NDEV,) + cond.shape, cond.dtype),
                   jax.ShapeDtypeStruct((NDEV,) + p.shape, p.dtype)],
        in_specs=[ANY, ANY], out_specs=[ANY, ANY], scratch_shapes=COMM_SEMS + COMM_SEMS,
    )(cond, p)


def _scatter_hooks(src_refs, specs, r_ref, layer, send_sems, recv_sems, local_sem):
    total = sum(rows for _, rows in specs)

    def start():
        x, y, c = _peers()
        me = 4 * x + 2 * y + c

        def part(k, dev):
            off, rows = specs[k]
            src = src_refs[k].at[pl.ds(pl.multiple_of(dev * rows, 16), rows), :]
            return src, r_ref.at[me, layer, pl.ds(off, rows), :]

        for k in range(len(specs)):
            src, dst = part(k, me)
            pltpu.make_async_copy(src, dst, local_sem).start()
        for r in range(1, NDEV):
            px = 1 - x if r & 4 else x
            py = 1 - y if r & 2 else y
            pc = 1 - c if r & 1 else c
            for k in range(len(specs)):
                src, dst = part(k, 4 * px + 2 * py + pc)
                pltpu.make_async_remote_copy(
                    src_ref=src, dst_ref=dst, send_sem=send_sems.at[r - 1], recv_sem=recv_sems.at[r - 1],
                    device_id=(px, py, pc), device_id_type=MESH).start()

    def end():
        x, y, c = _peers()
        whole = r_ref.at[0, layer, pl.ds(0, total), :]
        for r in range(1, NDEV):
            done = pltpu.make_async_remote_copy(
                src_ref=whole, dst_ref=whole, send_sem=send_sems.at[r - 1], recv_sem=recv_sems.at[r - 1],
                device_id=(x, y, c), device_id_type=MESH)
            done.wait_recv()
            done.wait_send()
        pltpu.make_async_copy(whole, whole, local_sem).wait()

    return start, None, end


def _scatter_comm(srcs, specs, recv, layer):
    k = len(srcs)
    return _Comm(list(srcs) + [recv], [jax.ShapeDtypeStruct(recv.shape, recv.dtype)], {k: 0},
                 lambda cins, couts, sems: _scatter_hooks(cins[:k], specs, couts[0], layer, *sems))


def _tail_comm(srcs, specs, recv, layer, small):
    k = len(srcs)

    def body(*refs):
        src_refs, small_ref, r_ref, g_ref, sems = refs[:k], refs[k + 1], refs[k + 2], refs[k + 3], refs[k + 4:]
        s_start, _, s_end = _scatter_hooks(src_refs, specs, r_ref, layer, *sems[:3])
        g_start, g_middle, g_end = _gather_hooks(small_ref, g_ref, *sems[3:])
        s_start()
        g_start()
        g_middle()
        g_end()
        s_end()

    return pl.pallas_call(
        body, name="tail_comm",
        out_shape=[jax.ShapeDtypeStruct(recv.shape, recv.dtype),
                   jax.ShapeDtypeStruct((NDEV,) + small.shape, small.dtype)],
        in_specs=[ANY] * (k + 2), out_specs=[ANY, ANY], scratch_shapes=COMM_SEMS + COMM_SEMS,
        input_output_aliases={k: 0},
    )(*srcs, recv, small)


def _empty_recv():
    def body(o_ref):
        del o_ref

    return pl.pallas_call(body, name="recv_buffer", out_specs=ANY,
                          out_shape=jax.ShapeDtypeStruct((NDEV, DEPTH, LROWS, D), BF16))()


def _host_call(inner, comm, *, name, grid, in_specs, out_specs, out_shape, scratch_shapes, operands):
    if comm is None:
        return pl.pallas_call(
            inner, name=name, grid=grid, in_specs=in_specs, out_specs=out_specs, out_shape=out_shape,
            scratch_shapes=scratch_shapes, compiler_params=_cparams("arbitrary"))(*operands)
    n_in, n_out, n_s = len(in_specs), len(out_specs), len(scratch_shapes)
    k_in, k_out = len(comm.ins), len(comm.outs)
    steps = grid[0]

    def body(*refs):
        ins, cins = refs[:n_in], refs[n_in:n_in + k_in]
        o0 = n_in + k_in
        outs, couts = refs[o0:o0 + n_out], refs[o0 + n_out:o0 + n_out + k_out]
        s0 = o0 + n_out + k_out
        scr, sems = refs[s0:s0 + n_s], refs[s0 + n_s:]
        start, middle, end = comm.bind(cins, couts, sems)
        i = pl.program_id(0)
        pl.when(i == 0)(start)
        if middle is not None:
            pl.when(i == steps * 3 // 4)(middle)
        inner(*ins, *outs, *scr)
        pl.when(i == steps - 1)(end)

    return pl.pallas_call(
        body, name=name, grid=grid, in_specs=list(in_specs) + [ANY] * k_in,
        out_specs=list(out_specs) + [ANY] * k_out, out_shape=list(out_shape) + list(comm.outs),
        scratch_shapes=list(scratch_shapes) + COMM_SEMS,
        input_output_aliases={n_in + a: n_out + b for a, b in comm.aliases.items()},
        compiler_params=_cparams("arbitrary"))(*operands, *comm.ins)


def _mod_fwd(c_all, w_mod, b_sl):
    def body(c_ref, w_ref, b_ref, o_ref, ca_ref):
        cv = c_ref[...]
        ca = cv * _sig(cv)
        ca_ref[...] = ca
        o_ref[...] = jnp.dot(ca, w_ref[0], preferred_element_type=F32, precision=lax.Precision.HIGHEST) + b_ref[0]

    return pl.pallas_call(
        body, name="mod_fwd", grid=(DEPTH,),
        in_specs=[_full((NDEV, D)), pl.BlockSpec((1, D, MODW), lambda l: (l, 0, 0)),
                  pl.BlockSpec((1, 1, MODW), lambda l: (l, 0, 0))],
        out_specs=[pl.BlockSpec((NDEV, MODW), lambda l: (0, l)), _full((NDEV, D))],
        out_shape=[jax.ShapeDtypeStruct((NDEV, DEPTH * MODW), F32), jax.ShapeDtypeStruct((NDEV, D), F32)],
        compiler_params=_cparams("arbitrary"),
    )(c_all, w_mod, b_sl)


def _mod_bwd_adam(ca_t, dmod, w, m, v):
    def body(ct_ref, dm_ref, w_ref, m_ref, v_ref, g_ref, d_ref, mo_ref, vo_ref):
        g = jnp.dot(ct_ref[...], dm_ref[0], preferred_element_type=F32, precision=lax.Precision.HIGHEST)
        delta, m2, v2 = _adam(w_ref[0], g, m_ref[0], v_ref[0])
        g_ref[0], d_ref[0], mo_ref[0], vo_ref[0] = g, delta, m2, v2

    blk = pl.BlockSpec((1, D, MODW), lambda l: (l, 0, 0))
    sds = jax.ShapeDtypeStruct(w.shape, F32)
    return pl.pallas_call(
        body, name="mod_bwd_adam", grid=(DEPTH,),
        in_specs=[_full((D, NDEV)), pl.BlockSpec((1, NDEV, MODW), lambda l: (l, 0, 0)), blk, blk, blk],
        out_specs=[blk] * 4, out_shape=[sds] * 4,
        compiler_params=_cparams("arbitrary"),
    )(ca_t, dmod, w, m, v)


def _inproj(x, vec, bias, gw, layer, tm, comm=None):
    t = x.shape[0]

    def body(x_ref, vec_ref, b_ref, g_ref, u_ref, w_s, sems):
        @pl.when(pl.program_id(0) == 0)
        def _():
            _load_weights(g_ref, [(OFF_IN, ROWS_IN, w_s)], sems)

        h, _, _ = _norm_fwd(x_ref[...], vec_ref[0:1], vec_ref[1:2], vec_ref[2:3])
        w = w_s[...].reshape(2 * D, D)
        u_ref[...] = (_nt(h.astype(BF16), w) + b_ref[...]).astype(BF16)

    res = _host_call(
        body, comm, name=f"inproj_l{layer}", grid=(t // tm,),
        in_specs=[_tile(tm, D), _full((8, D)), _full((1, 2 * D)), ANY],
        out_specs=[_tile(tm, 2 * D)], out_shape=[jax.ShapeDtypeStruct((t, 2 * D), BF16)],
        scratch_shapes=[pltpu.VMEM((NDEV, ROWS_IN, D), BF16), pltpu.SemaphoreType.DMA((1,))],
        operands=(x, vec, bias, gw))
    return res[0] if comm is None else res


def _fill_even(qe, pe, be, part_ref, lo, rows, valid):
    cg = part_ref[:, DA:2 * DA].astype(F32)
    v = part_ref[:, 2 * DA:3 * DA].astype(F32)
    q = cg * v
    p = part_ref[:, 3 * DA:4 * DA].astype(F32)
    if valid is not None:
        q = jnp.where(valid, q, 0.0)
        p = jnp.where(valid, p, 0.0)
    qe[lo:lo + rows, :] = q
    pe[lo:lo + rows, :] = p
    if be is not None:
        b = part_ref[:, 0:DA].astype(F32)
        be[lo:lo + rows, :] = b if valid is None else jnp.where(valid, b, 0.0)


def _conv3(ca_ref, qe, tm):
    return (ca_ref[0:1] * qe[HALO - 1:HALO - 1 + tm] + ca_ref[1:2] * qe[HALO:HALO + tm]
            + ca_ref[2:3] * qe[HALO + 1:HALO + 1 + tm])


def _pool_counts(t0, rows, first_row, left, right, t):
    tg = t0 + first_row + lax.broadcasted_iota(jnp.int32, (rows, 1), 0)
    cnt = jnp.minimum(tg + right, t - 1) - jnp.maximum(tg - left, 0) + 1
    return jnp.maximum(cnt, 1).astype(F32)


def _pool_minus_id(pe, gi, left, right, inv_cnt, tm):
    c0 = gi * PG
    s = pe[HALO - left:HALO - left + tm, c0:c0 + PG]
    for j in range(-left + 1, right + 1):
        s = s + pe[HALO + j:HALO + j + tm, c0:c0 + PG]
    return s * inv_cnt - pe[HALO:HALO + tm, c0:c0 + PG]


def _mix_even_fwd(u, x, vec, ca, wp, ps, gw, layer, tm, comm=None):
    t = x.shape[0]
    n = t // tm
    e = tm + 2 * HALO

    def body(u_ref, up_ref, un_ref, x_ref, vec_ref, ca_ref, wp_ref, ps_ref, g_ref, xo_ref, y_ref, w_s, qe, pe, sems):
        i = pl.program_id(0)

        @pl.when(i == 0)
        def _():
            _load_weights(g_ref, [(OFF_OUT, ROWS_OUT, w_s)], sems)

        _fill_even(qe, pe, None, up_ref, 0, HALO, i > 0)
        _fill_even(qe, pe, None, u_ref, HALO, tm, None)
        _fill_even(qe, pe, None, un_ref, HALO + tm, HALO, i < n - 1)
        ya = u_ref[:, 0:DA].astype(F32) * _conv3(ca_ref, qe, tm)
        parts = [ya]
        for gi, (_, left, right) in enumerate(POOL):
            inv = 1.0 / _pool_counts(i * tm, tm, 0, left, right, t)
            pm = _pool_minus_id(pe, gi, left, right, inv, tm)
            parts.append(_nn(pm.astype(BF16), wp_ref[gi]) * ps_ref[0:1, gi * PG:(gi + 1) * PG])
        cat = jnp.concatenate(parts, axis=-1).astype(BF16)
        y = _nn(cat, w_s[...].reshape(D, D))
        y_ref[...] = y.astype(BF16)
        xo_ref[...] = x_ref[...] + vec_ref[3:4] * y

    prev, nxt = _halo_specs(tm, 2 * D, t)
    return _host_call(
        body, comm, name=f"mix_even_fwd_l{layer}", grid=(n,),
        in_specs=[_tile(tm, 2 * D), prev, nxt, _tile(tm, D), _full((8, D)), _full((3, DA)),
                  _full((4, PG, PG)), _full((1, DA)), ANY],
        out_specs=[_tile(tm, D), _tile(tm, D)],
        out_shape=[jax.ShapeDtypeStruct((t, D), F32), jax.ShapeDtypeStruct((t, D), BF16)],
        scratch_shapes=[pltpu.VMEM((NDEV, ROWS_OUT, D), BF16), pltpu.VMEM((e, DA), F32), pltpu.VMEM((e, DA), F32),
                        pltpu.SemaphoreType.DMA((1,))],
        operands=(u, u, u, x, vec, ca, wp, ps, gw))


def _mix_even_bwd(dxo, u, x, y, vec, ca, wp, ps, gw, layer, tm, comm=None):
    t = x.shape[0]
    n = t // tm
    e = tm + 2 * HALO

    def body(dxo_ref, dp_ref, dn_ref, u_ref, up_ref, un_ref, x_ref, y_ref, vec_ref, ca_ref, wp_ref, ps_ref, g_ref,
             dxi_ref, du_ref, h_ref, cat_ref, dy_ref, sums_ref, dwp_ref,
             wo_s, wi_s, dye, qe, pe, be, dce, epe, sems):
        i = pl.program_id(0)

        @pl.when(i == 0)
        def _():
            _load_weights(g_ref, [(OFF_OUT, ROWS_OUT, wo_s), (OFF_IN, ROWS_IN, wi_s)], sems)
            sums_ref[...] = jnp.zeros_like(sums_ref)
            dwp_ref[...] = jnp.zeros_like(dwp_ref)

        gate = vec_ref[3:4]
        dxo_m = dxo_ref[...]
        dye[0:HALO, :] = jnp.where(i > 0, gate * dp_ref[...], 0.0).astype(BF16)
        dye[HALO:HALO + tm, :] = (gate * dxo_m).astype(BF16)
        dye[HALO + tm:e, :] = jnp.where(i < n - 1, gate * dn_ref[...], 0.0).astype(BF16)
        _fill_even(qe, pe, be, up_ref, 0, HALO, i > 0)
        _fill_even(qe, pe, be, u_ref, HALO, tm, None)
        _fill_even(qe, pe, be, un_ref, HALO + tm, HALO, i < n - 1)
        sums_ref[0:1, :] += _rowsum(dxo_m * y_ref[...].astype(F32))

        dcat = _nt(dye[...], wo_s[...].reshape(D, D))
        dce[...] = dcat[:, 0:DA] * be[...]
        cq = _conv3(ca_ref, qe, tm)
        bg = be[HALO:HALO + tm]
        dc_m = dce[HALO:HALO + tm]
        dbg = dcat[HALO:HALO + tm, 0:DA] * cq
        dq = (ca_ref[0:1] * dce[HALO + 1:HALO + 1 + tm] + ca_ref[1:2] * dc_m
              + ca_ref[2:3] * dce[HALO - 1:HALO - 1 + tm])
        cg = u_ref[:, DA:2 * DA].astype(F32)
        v = u_ref[:, 2 * DA:3 * DA].astype(F32)
        for k in range(3):
            sums_ref[4 + k:5 + k, 0:DA] += _rowsum(dc_m * qe[HALO - 1 + k:HALO - 1 + k + tm])
        du_parts = [dbg, dq * v, dq * cg]
        cat_parts = [bg * cq]
        for gi, (_, left, right) in enumerate(POOL):
            c0 = gi * PG
            scale = ps_ref[0:1, c0:c0 + PG]
            dyb = dcat[:, DA + c0:DA + c0 + PG]
            dybs = (dyb * scale).astype(BF16)
            dpm = _nt(dybs, wp_ref[gi])
            inv_e = 1.0 / _pool_counts(i * tm, e, -HALO, left, right, t)
            epe[:, c0:c0 + PG] = dpm * inv_e
            s_adj = epe[HALO - right:HALO - right + tm, c0:c0 + PG]
            for j in range(-right + 1, left + 1):
                s_adj = s_adj + epe[HALO + j:HALO + j + tm, c0:c0 + PG]
            du_parts.append(s_adj - dpm[HALO:HALO + tm])
            inv_m = 1.0 / _pool_counts(i * tm, tm, 0, left, right, t)
            pm = _pool_minus_id(pe, gi, left, right, inv_m, tm).astype(BF16)
            ybpre = _nn(pm, wp_ref[gi])
            sums_ref[7:8, c0:c0 + PG] += _rowsum(dyb[HALO:HALO + tm] * ybpre)
            dwp_ref[gi] += _tn(pm, dybs[HALO:HALO + tm])
            cat_parts.append(ybpre * scale)
        du = jnp.concatenate(du_parts, axis=-1).astype(BF16)
        du_ref[...] = du
        cat_ref[...] = jnp.concatenate(cat_parts, axis=-1).astype(BF16)
        dy_ref[...] = dye[HALO:HALO + tm, :]
        dh = _nn(du, wi_s[...].reshape(2 * D, D))
        g, sc, sh = vec_ref[0:1], vec_ref[1:2], vec_ref[2:3]
        h, nrm, r = _norm_fwd(x_ref[...], g, sc, sh)
        h_ref[...] = h.astype(BF16)
        dxi_ref[...] = dxo_m + _norm_bwd(dh, nrm, r, g, sc)
        sums_ref[1:2, :] += _rowsum(dh)
        sums_ref[2:3, :] += _rowsum(dh * nrm)

        @pl.when(i == n - 1)
        def _():
            p = sums_ref[2:3, :]
            sums_ref[3:4, :] = p * (1.0 + sc)
            sums_ref[2:3, :] = p * g

    prev_u, nxt_u = _halo_specs(tm, 2 * D, t)
    prev_d, nxt_d = _halo_specs(tm, D, t)
    return _host_call(
        body, comm, name=f"mix_even_bwd_l{layer}", grid=(n,),
        in_specs=[_tile(tm, D), prev_d, nxt_d, _tile(tm, 2 * D), prev_u, nxt_u, _tile(tm, D), _tile(tm, D),
                  _full((8, D)), _full((3, DA)), _full((4, PG, PG)), _full((1, DA)), ANY],
        out_specs=[_tile(tm, D), _tile(tm, 2 * D), _tile(tm, D), _tile(tm, D), _tile(tm, D),
                   _full((16, D)), _full((4, PG, PG))],
        out_shape=[jax.ShapeDtypeStruct((t, D), F32), jax.ShapeDtypeStruct((t, 2 * D), BF16),
                   jax.ShapeDtypeStruct((t, D), BF16), jax.ShapeDtypeStruct((t, D), BF16),
                   jax.ShapeDtypeStruct((t, D), BF16), jax.ShapeDtypeStruct((16, D), F32),
                   jax.ShapeDtypeStruct((4, PG, PG), F32)],
        scratch_shapes=[pltpu.VMEM((NDEV, ROWS_OUT, D), BF16), pltpu.VMEM((NDEV, ROWS_IN, D), BF16),
                        pltpu.VMEM((e, D), BF16), pltpu.VMEM((e, DA), F32), pltpu.VMEM((e, DA), F32),
                        pltpu.VMEM((e, DA), F32), pltpu.VMEM((e, DA), F32), pltpu.VMEM((e, DA), F32),
                        pltpu.SemaphoreType.DMA((2,))],
        operands=(dxo, dxo, dxo, u, u, u, x, y, vec, ca, wp, ps, gw))


def _fill_glu(ze, part_ref, lo, rows, valid):
    a = part_ref[:, 0:D].astype(F32)
    g = part_ref[:, D:2 * D].astype(F32)
    z = a * _sig(g)
    ze[lo:lo + rows, :] = z if valid is None else jnp.where(valid, z, 0.0)


SHIFT_ROWS = 24


def _shifted_copies(dst, src, tm):
    rows = tm + SHIFT_ROWS
    for j in range(8):
        dst[j, :, :] = src[j:j + rows, :]


def _shifted(dst, shift, tm):
    lo = shift // 8 * 8
    return dst[shift % 8, lo:lo + tm, :]


def _layer_norm_parts(z2):
    mu = _lanemean(z2)
    d = z2 - mu
    rstd = lax.rsqrt(_lanemean(d * d) + LN_EPS)
    return d * rstd, rstd


def _mix_odd_fwd(u, x, vec, wdw, sm, gw, layer, tm):
    t = x.shape[0]
    n = t // tm
    e = tm + 2 * HALO

    def body(u_ref, up_ref, un_ref, x_ref, vec_ref, wdw_ref, sm_ref, g_ref, xo_ref, y_ref, z2_ref, w_s, ze, zsh, sems):
        i = pl.program_id(0)

        @pl.when(i == 0)
        def _():
            _load_weights(g_ref, [(OFF_OUT, ROWS_OUT, w_s)], sems)

        _fill_glu(ze, up_ref, 0, HALO, i > 0)
        _fill_glu(ze, u_ref, HALO, tm, None)
        _fill_glu(ze, un_ref, HALO + tm, HALO, i < n - 1)
        _shifted_copies(zsh, ze, tm)
        z2 = sm_ref[0:1] + wdw_ref[0:1] * _shifted(zsh, 1, tm)
        for k in range(1, CONF_K):
            z2 = z2 + wdw_ref[k:k + 1] * _shifted(zsh, 1 + k, tm)
        z2_ref[...] = z2.astype(BF16)
        zn, _ = _layer_norm_parts(z2)
        lo = zn * sm_ref[1:2] + sm_ref[2:3]
        z3 = lo * _sig(lo)
        y = _nn(z3.astype(BF16), w_s[...].reshape(D, D)) + sm_ref[3:4]
        y_ref[...] = y.astype(BF16)
        xo_ref[...] = x_ref[...] + vec_ref[3:4] * y

    prev, nxt = _halo_specs(tm, 2 * D, t)
    return pl.pallas_call(
        body, name=f"mix_odd_fwd_l{layer}", grid=(n,),
        in_specs=[_tile(tm, 2 * D), prev, nxt, _tile(tm, D), _full((8, D)), _full((32, D)), _full((8, D)), ANY],
        out_specs=[_tile(tm, D), _tile(tm, D), _tile(tm, D)],
        out_shape=[jax.ShapeDtypeStruct((t, D), F32), jax.ShapeDtypeStruct((t, D), BF16),
                   jax.ShapeDtypeStruct((t, D), BF16)],
        scratch_shapes=[pltpu.VMEM((NDEV, ROWS_OUT, D), BF16), pltpu.VMEM((e, D), F32),
                        pltpu.VMEM((8, tm + SHIFT_ROWS, D), F32), pltpu.SemaphoreType.DMA((1,))],
        compiler_params=_cparams("arbitrary"),
    )(u, u, u, x, vec, wdw, sm, gw)


def _mix_odd_bwd1(dxo, y, z2, vec, sm, gw, layer, tm):
    t = dxo.shape[0]
    n = t // tm

    def body(dxo_ref, y_ref, z2_ref, vec_ref, sm_ref, g_ref, dy_ref, z3_ref, dz2_ref, sums_ref, w_s, sems):
        i = pl.program_id(0)

        @pl.when(i == 0)
        def _():
            _load_weights(g_ref, [(OFF_OUT, ROWS_OUT, w_s)], sems)
            sums_ref[...] = jnp.zeros_like(sums_ref)

        dxo_m = dxo_ref[...]
        dy = vec_ref[3:4] * dxo_m
        dyb = dy.astype(BF16)
        dy_ref[...] = dyb
        sums_ref[0:1, :] += _rowsum(dxo_m * y_ref[...].astype(F32))
        sums_ref[4:5, :] += _rowsum(dy)
        dz3 = _nt(dyb, w_s[...].reshape(D, D))
        zn, rstd = _layer_norm_parts(z2_ref[...].astype(F32))
        lo = zn * sm_ref[1:2] + sm_ref[2:3]
        sg = _sig(lo)
        z3_ref[...] = (lo * sg).astype(BF16)
        dlo = dz3 * (sg * (1.0 + lo * (1.0 - sg)))
        sums_ref[5:6, :] += _rowsum(dlo * zn)
        sums_ref[6:7, :] += _rowsum(dlo)
        dzn = dlo * sm_ref[1:2]
        dz2 = rstd * (dzn - _lanemean(dzn) - zn * _lanemean(dzn * zn))
        sums_ref[7:8, :] += _rowsum(dz2)
        dz2_ref[...] = dz2.astype(BF16)

    return pl.pallas_call(
        body, name=f"mix_odd_bwd1_l{layer}", grid=(n,),
        in_specs=[_tile(tm, D), _tile(tm, D), _tile(tm, D), _full((8, D)), _full((8, D)), ANY],
        out_specs=[_tile(tm, D), _tile(tm, D), _tile(tm, D), _full((16, D))],
        out_shape=[jax.ShapeDtypeStruct((t, D), BF16)] * 3 + [jax.ShapeDtypeStruct((16, D), F32)],
        scratch_shapes=[pltpu.VMEM((NDEV, ROWS_OUT, D), BF16), pltpu.SemaphoreType.DMA((1,))],
        compiler_params=_cparams("arbitrary"),
    )(dxo, y, z2, vec, sm, gw)


def _mix_odd_bwd2(dz2, u, x, dxo, vec, wdw, gw, layer, tm, comm=None):
    t = x.shape[0]
    n = t // tm
    e = tm + 2 * HALO

    def body(dz_ref, dzp_ref, dzn_ref, u_ref, x_ref, dxo_ref, vec_ref, wdw_ref, g_ref,
             dxi_ref, du_ref, h_ref, sums_ref, dw_ref, w_s, de, zsh, sems):
        i = pl.program_id(0)

        @pl.when(i == 0)
        def _():
            _load_weights(g_ref, [(OFF_IN, ROWS_IN, w_s)], sems)
            sums_ref[...] = jnp.zeros_like(sums_ref)
            dw_ref[...] = jnp.zeros_like(dw_ref)

        de[0:HALO, :] = jnp.where(i > 0, dzp_ref[...].astype(F32), 0.0)
        de[HALO:HALO + tm, :] = dz_ref[...].astype(F32)
        de[HALO + tm:e, :] = jnp.where(i < n - 1, dzn_ref[...].astype(F32), 0.0)
        a = u_ref[:, 0:D].astype(F32)
        gg = u_ref[:, D:2 * D].astype(F32)
        sg = _sig(gg)
        z = a * sg
        _shifted_copies(zsh, de, tm)
        dz = None
        for k in range(CONF_K):
            shifted = _shifted(zsh, CONF_K - k, tm)
            term = wdw_ref[k:k + 1] * shifted
            dz = term if dz is None else dz + term
            dw_ref[k:k + 1, :] += _rowsum(z * shifted)
        da = dz * sg
        dg = dz * a * (sg * (1.0 - sg))
        sums_ref[8:9, :] += _rowsum(da)
        sums_ref[9:10, :] += _rowsum(dg)
        du = jnp.concatenate([da, dg], axis=-1).astype(BF16)
        du_ref[...] = du
        dh = _nn(du, w_s[...].reshape(2 * D, D))
        g, sc, sh = vec_ref[0:1], vec_ref[1:2], vec_ref[2:3]
        h, nrm, r = _norm_fwd(x_ref[...], g, sc, sh)
        h_ref[...] = h.astype(BF16)
        dxi_ref[...] = dxo_ref[...] + _norm_bwd(dh, nrm, r, g, sc)
        sums_ref[1:2, :] += _rowsum(dh)
        sums_ref[2:3, :] += _rowsum(dh * nrm)

        @pl.when(i == n - 1)
        def _():
            p = sums_ref[2:3, :]
            sums_ref[3:4, :] = p * (1.0 + sc)
            sums_ref[2:3, :] = p * g

    prev_d, nxt_d = _halo_specs(tm, D, t)
    return _host_call(
        body, comm, name=f"mix_odd_bwd2_l{layer}", grid=(n,),
        in_specs=[_tile(tm, D), prev_d, nxt_d, _tile(tm, 2 * D), _tile(tm, D), _tile(tm, D),
                  _full((8, D)), _full((32, D)), ANY],
        out_specs=[_tile(tm, D), _tile(tm, 2 * D), _tile(tm, D), _full((16, D)), _full((32, D))],
        out_shape=[jax.ShapeDtypeStruct((t, D), F32), jax.ShapeDtypeStruct((t, 2 * D), BF16),
                   jax.ShapeDtypeStruct((t, D), BF16), jax.ShapeDtypeStruct((16, D), F32),
                   jax.ShapeDtypeStruct((32, D), F32)],
        scratch_shapes=[pltpu.VMEM((NDEV, ROWS_IN, D), BF16), pltpu.VMEM((e, D), F32),
                        pltpu.VMEM((8, tm + SHIFT_ROWS, D), F32),
                        pltpu.SemaphoreType.DMA((1,))],
        operands=(dz2, dz2, dz2, u, x, dxo, vec, wdw, gw))


FCH = FF // 2


def _loss_head(x, tgt, g):
    r = lax.rsqrt(_lanemean(x * x) + RMS_EPS)
    nrm = x * r
    err = nrm * g - tgt
    dout = err * (1.0 / D)
    dn = dout * g
    return r * (dn - nrm * _lanemean(dn * nrm)), _rowsum(err * err) * (0.5 / D), _rowsum(dout * nrm)


def _ffn_fwd(x, vec, wsrc, layer, tm, comm=None, head=None):
    t = x.shape[0]
    g_gu, off_g, off_u, g_d, off_d = wsrc

    def body(*refs):
        if head is None:
            x_ref, vec_ref, ggu_ref, gd_ref, xo_ref, a_ref, b_ref, y_ref, wg_s, wu_s, wd_s, s_s, sems = refs
        else:
            (x_ref, vec_ref, ggu_ref, gd_ref, t_ref, gf_ref, xo_ref, a_ref, b_ref, y_ref, fsum_ref,
             wg_s, wu_s, wd_s, s_s, sems) = refs

        @pl.when(pl.program_id(0) == 0)
        def _():
            _load_weight_rows([(ggu_ref, off_g, FS, wg_s), (ggu_ref, off_u, FS, wu_s), (gd_ref, off_d, FS, wd_s)], sems)
            if head is not None:
                fsum_ref[...] = jnp.zeros_like(fsum_ref)

        xv = x_ref[...]
        h, _, _ = _norm_fwd(xv, vec_ref[0:1], vec_ref[1:2], vec_ref[2:3])
        hb = h.astype(BF16)
        for ch in range(FF // MXU_N):
            rows = slice(ch * MXU_N, (ch + 1) * MXU_N)
            a = _nt(hb, wg_s[rows, :])
            b = _nt(hb, wu_s[rows, :])
            a_ref[:, rows] = a.astype(BF16)
            b_ref[:, rows] = b.astype(BF16)
            s_s[:, rows] = ((a * _sig(a)) * b).astype(BF16)
        y = _nn(s_s[...], wd_s[...])
        y_ref[...] = y.astype(BF16)
        x_out = xv + vec_ref[3:4] * y
        if head is None:
            xo_ref[...] = x_out
        else:
            dx, loss_row, dg_row = _loss_head(x_out, t_ref[...], gf_ref[...])
            xo_ref[...] = dx
            fsum_ref[0:1, :] += dg_row
            fsum_ref[1:2, :] += loss_row

    wsc = pltpu.VMEM((FF, D), BF16)
    in_specs = [_tile(tm, D), _full((8, D)), ANY, ANY]
    out_specs = [_tile(tm, D), _tile(tm, FF), _tile(tm, FF), _tile(tm, D)]
    out_shape = [jax.ShapeDtypeStruct((t, D), F32), jax.ShapeDtypeStruct((t, FF), BF16),
                 jax.ShapeDtypeStruct((t, FF), BF16), jax.ShapeDtypeStruct((t, D), BF16)]
    operands = (x, vec, g_gu, g_d)
    if head is not None:
        in_specs += [_tile(tm, D), _full((1, D))]
        out_specs.append(_full((8, D)))
        out_shape.append(jax.ShapeDtypeStruct((8, D), F32))
        operands += tuple(head)
    return _host_call(
        body, comm, name=f"ffn_fwd_l{layer}", grid=(t // tm,),
        in_specs=in_specs, out_specs=out_specs, out_shape=out_shape,
        scratch_shapes=[wsc, wsc, wsc, pltpu.VMEM((tm, FF), BF16), pltpu.SemaphoreType.DMA((3 * NDEV,))],
        operands=operands)


def _ffn_bwd(dxo, x, y, a, b, vec, wsrc, layer, tm, comm=None):
    t = x.shape[0]
    n = t // tm
    g_gu, off_g, off_u, g_d, off_d = wsrc

    def body(dxo_ref, x_ref, y_ref, a_ref, b_ref, vec_ref, ggu_ref, gd_ref,
             dxi_ref, h_ref, dy_ref, s_ref, da_ref, db_ref, sums_ref, wg_s, wu_s, wd_s, sems):
        i = pl.program_id(0)

        @pl.when(i == 0)
        def _():
            _load_weight_rows([(ggu_ref, off_g, FS, wg_s), (ggu_ref, off_u, FS, wu_s), (gd_ref, off_d, FS, wd_s)], sems)
            sums_ref[...] = jnp.zeros_like(sums_ref)

        dxo_m = dxo_ref[...]
        sums_ref[0:1, :] += _rowsum(dxo_m * y_ref[...].astype(F32))
        dyb = (vec_ref[3:4] * dxo_m).astype(BF16)
        dy_ref[...] = dyb
        for ch in range(FF // MXU_N):
            cols = slice(ch * MXU_N, (ch + 1) * MXU_N)
            ds = _nt(dyb, wd_s[cols, :]).astype(BF16)
            av = a_ref[:, cols]
            bv = b_ref[:, cols]
            sg = _sig(av)
            sl = av * sg
            s_ref[:, cols] = sl * bv
            db_ref[:, cols] = ds * sl
            da_ref[:, cols] = (ds * bv) * (sg * (1.0 + av * (1.0 - sg)))
        dh = _nn(da_ref[...], wg_s[...]) + _nn(db_ref[...], wu_s[...])
        g, sc, sh = vec_ref[0:1], vec_ref[1:2], vec_ref[2:3]
        h, nrm, r = _norm_fwd(x_ref[...], g, sc, sh)
        h_ref[...] = h.astype(BF16)
        dxi_ref[...] = dxo_m + _norm_bwd(dh, nrm, r, g, sc)
        sums_ref[1:2, :] += _rowsum(dh)
        sums_ref[2:3, :] += _rowsum(dh * nrm)

        @pl.when(i == n - 1)
        def _():
            p = sums_ref[2:3, :]
            sums_ref[3:4, :] = p * (1.0 + sc)
            sums_ref[2:3, :] = p * g

    wsc = pltpu.VMEM((FF, D), BF16)
    big, small = jax.ShapeDtypeStruct((t, FF), BF16), jax.ShapeDtypeStruct((t, D), BF16)
    return _host_call(
        body, comm, name=f"ffn_bwd_l{layer}", grid=(n,),
        in_specs=[_tile(tm, D), _tile(tm, D), _tile(tm, D), _tile(tm, FF), _tile(tm, FF), _full((8, D)), ANY, ANY],
        out_specs=[_tile(tm, D), _tile(tm, D), _tile(tm, D), _tile(tm, FF), _tile(tm, FF), _tile(tm, FF),
                   _full((8, D))],
        out_shape=[jax.ShapeDtypeStruct((t, D), F32), small, small, big, big, big, jax.ShapeDtypeStruct((8, D), F32)],
        scratch_shapes=[wsc, wsc, wsc, pltpu.SemaphoreType.DMA((3 * NDEV,))],
        operands=(dxo, x, y, a, b, vec, g_gu, g_d))


def _wgrad(lhs, rhs, name, tk):
    t, m = lhs.shape
    n = t // tk

    def body(l_ref, r_ref, o_ref, acc):
        i = pl.program_id(0)

        @pl.when(i == 0)
        def _():
            acc[...] = jnp.zeros_like(acc)

        acc[...] += _tn(l_ref[...], r_ref[...])

        @pl.when(i == n - 1)
        def _():
            o_ref[...] = acc[...].astype(BF16)

    return pl.pallas_call(
        body, name=name, grid=(n,),
        in_specs=[_tile(tk, m), _tile(tk, D)], out_specs=_full((m, D)),
        out_shape=jax.ShapeDtypeStruct((m, D), BF16),
        scratch_shapes=[pltpu.VMEM((m, D), F32)],
        compiler_params=_cparams("arbitrary"),
    )(lhs, rhs)


ADAM_ROWS = LROWS // 5


def _adam_big(recv, w, m, v):
    def body(r_ref, w_ref, m_ref, v_ref, g_ref, d_ref, mo_ref, vo_ref):
        g = r_ref[0, 0].astype(F32)
        for s in range(1, NDEV):
            g = g + r_ref[s, 0].astype(F32)
        delta, m2, v2 = _adam(w_ref[0], g, m_ref[0], v_ref[0])
        g_ref[0], d_ref[0], mo_ref[0], vo_ref[0] = g, delta, m2, v2

    blk = pl.BlockSpec((1, ADAM_ROWS, D), lambda l, j: (l, j, 0))
    sds = jax.ShapeDtypeStruct(w.shape, F32)
    return pl.pallas_call(
        body, name="adam_big", grid=(DEPTH, LROWS // ADAM_ROWS),
        in_specs=[pl.BlockSpec((NDEV, 1, ADAM_ROWS, D), lambda l, j: (0, l, j, 0)), blk, blk, blk],
        out_specs=[blk] * 4, out_shape=[sds] * 4,
        compiler_params=_cparams("arbitrary", "arbitrary"),
    )(recv, w, m, v)


def _sum_small(gathered, rows):
    def body(g_ref, o_ref):
        acc = g_ref[0:rows, :]
        for s in range(1, NDEV):
            acc = acc + g_ref[s * rows:(s + 1) * rows, :]
        o_ref[...] = acc

    return pl.pallas_call(
        body, name="sum_small",
        in_specs=[pl.BlockSpec(memory_space=pltpu.VMEM)], out_specs=pl.BlockSpec(memory_space=pltpu.VMEM),
        out_shape=jax.ShapeDtypeStruct((rows, D), F32),
        compiler_params=pltpu.CompilerParams(vmem_limit_bytes=VMEM_LIMIT),
    )(gathered)


def _adam_small(params):
    k = len(params)

    def body(*refs):
        ins, outs = refs[:4 * k], refs[4 * k:]
        for j in range(k):
            w_ref, g_ref, m_ref, v_ref = ins[4 * j:4 * j + 4]
            delta, m2, v2 = _adam(w_ref[...], g_ref[...], m_ref[...], v_ref[...])
            outs[3 * j][...], outs[3 * j + 1][...], outs[3 * j + 2][...] = delta, m2, v2

    flat = [a for p in params for a in p]
    shapes = [jax.ShapeDtypeStruct(p[0].shape, F32) for p in params for _ in range(3)]
    vm = pl.BlockSpec(memory_space=pltpu.VMEM)
    res = pl.pallas_call(
        body, name="adam_small", in_specs=[vm] * len(flat), out_specs=[vm] * len(shapes), out_shape=shapes,
        compiler_params=pltpu.CompilerParams(vmem_limit_bytes=VMEM_LIMIT),
    )(*flat)
    return [tuple(res[3 * j:3 * j + 3]) for j in range(k)]


def _pack(ab_in, ab_out, pw1, pw2, wg, wu, wd):
    ins = jnp.swapaxes(jnp.stack([ab_in[0], pw1[0], ab_in[1], pw1[1]]), 1, 2)
    outs = jnp.stack([ab_out[0], pw2[0], ab_out[1], pw2[1]])
    return jnp.concatenate([ins, outs, jnp.swapaxes(wg, 1, 2), jnp.swapaxes(wu, 1, 2), wd], axis=1)


def _unpack(p):
    ins = jnp.swapaxes(p[:, OFF_IN:OFF_OUT], 1, 2)
    outs = p[:, OFF_OUT:OFF_G]
    return (ins[0::2], outs[0::2], ins[1::2], outs[1::2], jnp.swapaxes(p[:, OFF_G:OFF_U], 1, 2),
            jnp.swapaxes(p[:, OFF_U:OFF_D], 1, 2), p[:, OFF_D:LROWS])


def _unshard(flat, lead, per):
    k = len(lead)
    a = flat.reshape((NDEV,) + tuple(lead) + (per,))
    a = jnp.transpose(a, tuple(range(1, k + 1)) + (0, k + 1))
    return a.reshape(tuple(lead) + (NDEV * per,))


def _rows_of(a):
    f = a.reshape(-1)
    pad = (-f.shape[0]) % D
    if pad:
        f = jnp.concatenate([f, jnp.zeros((pad,), f.dtype)])
    return f.reshape(-1, D)


def _pad_rows(a, rows):
    return jnp.concatenate([a, jnp.zeros((rows - a.shape[0],) + a.shape[1:], a.dtype)], axis=0)


def kernel(x, c, norm_mix_g, norm_ffn_g, w_mod, b_mod, ab_w_in, ab_conv, ab_w_pool, ab_pool_scale, ab_w_out, cf_w_pw1, cf_b_pw1, cf_w_dw, cf_b_dw, cf_ln_g, cf_ln_b, cf_w_pw2, cf_b_pw2, ffn_w_gate, ffn_w_up, ffn_w_down, final_norm_g, loss_target, m_norm_mix_g, m_norm_ffn_g, m_w_mod, m_b_mod, m_ab_w_in, m_ab_conv, m_ab_w_pool, m_ab_pool_scale, m_ab_w_out, m_cf_w_pw1, m_cf_b_pw1, m_cf_w_dw, m_cf_b_dw, m_cf_ln_g, m_cf_ln_b, m_cf_w_pw2, m_cf_b_pw2, m_ffn_w_gate, m_ffn_w_up, m_ffn_w_down, m_final_norm_g, v_norm_mix_g, v_norm_ffn_g, v_w_mod, v_b_mod, v_ab_w_in, v_ab_conv, v_ab_w_pool, v_ab_pool_scale, v_ab_w_out, v_cf_w_pw1, v_cf_b_pw1, v_cf_w_dw, v_cf_b_dw, v_cf_ln_g, v_cf_ln_b, v_cf_w_pw2, v_cf_b_pw2, v_ffn_w_gate, v_ffn_w_up, v_ffn_w_down, v_final_norm_g):
    t = x.shape[1]
    tm = 512 if t % 512 == 0 else t // 2
    tk = 1024 if t % 1024 == 0 else t // 2
    tmo = tm
    tmb = tm // 2
    me = 4 * lax.axis_index("x") + 2 * lax.axis_index("y") + lax.axis_index("c")
    xs, tgt = x[0], loss_target[0]

    w_pack = _pack(ab_w_in, ab_w_out, cf_w_pw1, cf_w_pw2, ffn_w_gate, ffn_w_up, ffn_w_down)
    p16 = w_pack.astype(BF16)

    sharded = [ab_conv, cf_b_pw1, cf_w_dw, cf_b_dw, cf_ln_g, cf_ln_b, cf_b_pw2]
    flat = jnp.concatenate([a.reshape(-1) for a in sharded])
    n_flat = flat.shape[0]
    g1, g_io0 = _head_comm(jnp.concatenate([_pad_rows(c, 8), _pad_rows(_rows_of(flat), 16)], axis=0),
                           p16[0, OFF_IN:OFF_G])
    c_all = g1[:, 0, :]
    flat_all = g1[:, 8:, :].reshape(NDEV, -1)[:, :n_flat]
    full, o = [], 0
    for a in sharded:
        lead, per = a.shape[:-1], a.shape[-1]
        size = a.size
        full.append(_unshard(flat_all[:, o:o + size], lead, per))
        o += size
    ab_conv_f, b_pw1_f, w_dw_f, b_dw_f, ln_g_f, ln_b_f, b_pw2_f = full

    b_sl = lax.dynamic_slice_in_dim(b_mod, me * MODW, MODW, axis=1).reshape(DEPTH, 1, MODW)
    mod_part, c_act = _mod_fwd(c_all, w_mod, b_sl)
    g2 = _gather_small(mod_part, "gather_mod").reshape(NDEV, NDEV, DEPTH, MODW)
    mod = jnp.transpose(lax.dynamic_index_in_dim(g2, me, axis=1, keepdims=False), (1, 0, 2)).reshape(DEPTH, N_MOD, D)
    zeros4 = jnp.zeros((4, D), F32)

    def vec_of(g, layer, k):
        return jnp.concatenate([g[layer][None], mod[layer, k + 1][None], mod[layer, k][None],
                                mod[layer, k + 2][None], zeros4], axis=0)

    vmix = [vec_of(norm_mix_g, l, 0) for l in range(DEPTH)]
    vffn = [vec_of(norm_ffn_g, l, 3) for l in range(DEPTH)]

    wp16 = ab_w_pool.astype(BF16)
    wdw32 = [_pad_rows(w_dw_f[i], 32) for i in range(2)]
    sm_odd = [jnp.concatenate([b_dw_f[i][None], ln_g_f[i][None], ln_b_f[i][None], b_pw2_f[i][None], zeros4], axis=0)
              for i in range(2)]
    zero_bias = jnp.zeros((1, 2 * D), F32)

    saved = []
    xc = xs
    gw = [g_io0]
    wsrc = []
    for l in range(DEPTH):
        i = l // 2
        if l == 0:
            u, g_gu0 = _inproj(xc, vmix[l], zero_bias, gw[l], l, tm, _gather_comm(p16[0, OFF_G:OFF_D]))
            x_mid, y_mix, g_d0 = _mix_even_fwd(u, xc, vmix[l], ab_conv_f[i], wp16[i], ab_pool_scale[i][None], gw[l],
                                               l, tm, _gather_comm(p16[0, OFF_D:LROWS]))
            wsrc.append((g_gu0, 0, FS, g_d0, 0))
            z2 = None
        elif l % 2 == 0:
            u = _inproj(xc, vmix[l], zero_bias, gw[l], l, tm)
            x_mid, y_mix = _mix_even_fwd(u, xc, vmix[l], ab_conv_f[i], wp16[i], ab_pool_scale[i][None], gw[l], l, tm)
            z2 = None
        else:
            u = _inproj(xc, vmix[l], b_pw1_f[i][None], gw[l], l, tm)
            x_mid, y_mix, z2 = _mix_odd_fwd(u, xc, vmix[l], wdw32[i], sm_odd[i], gw[l], l, tmo)
        if l + 1 < DEPTH:
            x_out, a, b, y_ffn, g_next = _ffn_fwd(x_mid, vffn[l], wsrc[l], l, tm, _gather_comm(p16[l + 1]))
            gw.append(g_next)
            wsrc.append((g_next, OFF_G, OFF_U, g_next, OFF_D))
        else:
            x_out = None
            dx, a, b, y_ffn, fsum = _ffn_fwd(x_mid, vffn[l], wsrc[l], l, tmb, head=(tgt, final_norm_g[None]))
        saved.append((xc, u, y_mix, z2, x_mid, a, b, y_ffn))
        xc = x_out

    loss = lax.psum(jnp.sum(fsum[1]), ("x", "y", "c"))
    d_final_g = fsum[0]

    recv = _empty_recv()
    late_specs = [(OFF_U, FS), (OFF_IN, ROWS_IN), (OFF_OUT, ROWS_OUT)]
    pending = None
    dmod = [None] * DEPTH
    d_mix_g, d_ffn_g = [None] * DEPTH, [None] * DEPTH
    d_conv, d_pool, d_pscale = [None] * 2, [None] * 2, [None] * 2
    d_bpw1, d_wdw, d_bdw, d_lng, d_lnb, d_bpw2 = ([None] * 2 for _ in range(6))
    for l in reversed(range(DEPTH)):
        i = l // 2
        x_in, u, y_mix, z2, x_mid, a, b, y_ffn = saved[l]
        if pending is None:
            dx_mid, h2, dy, s, da, db, s_f = _ffn_bwd(dx, x_mid, y_ffn, a, b, vffn[l], wsrc[l], l, tmb)
        else:
            dx_mid, h2, dy, s, da, db, s_f, recv = _ffn_bwd(dx, x_mid, y_ffn, a, b, vffn[l], wsrc[l], l, tmb,
                                                            _scatter_comm(pending, late_specs, recv, l + 1))
        g_down = _wgrad(s, dy, f"wgrad_down_l{l}", tk)
        g_gate = _wgrad(da, h2, f"wgrad_gate_l{l}", tk)
        gu_comm = _scatter_comm([g_down, g_gate], [(OFF_D, FS), (OFF_G, FS)], recv, l)
        g_up = _wgrad(db, h2, f"wgrad_up_l{l}", tk)
        d_ffn_g[l] = s_f[3]
        mod_ffn = [s_f[1], s_f[2], s_f[0]]
        if l % 2 == 0:
            dx, du, h, cat, dym, s_m, dwp, recv = _mix_even_bwd(dx_mid, u, x_in, y_mix, vmix[l], ab_conv_f[i], wp16[i],
                                                                ab_pool_scale[i][None], gw[l], l, tm, gu_comm)
            g_out = _wgrad(cat, dym, f"wgrad_out_l{l}", tk)
            d_conv[i], d_pool[i], d_pscale[i] = s_m[4:7, :DA], dwp, s_m[7, :DA]
            mod_mix = [s_m[1], s_m[2], s_m[0]]
            d_mix_g[l] = s_m[3]
        else:
            dym, z3, dz2, s_1 = _mix_odd_bwd1(dx_mid, y_mix, z2, vmix[l], sm_odd[i], gw[l], l, tm)
            dx, du, h, s_2, dwdw, recv = _mix_odd_bwd2(dz2, u, x_in, dx_mid, vmix[l], wdw32[i], gw[l], l, tmo, gu_comm)
            g_out = _wgrad(z3, dym, f"wgrad_out_l{l}", tk)
            d_bpw1[i], d_wdw[i], d_bdw[i] = s_2[8:10].reshape(2 * D), dwdw[:CONF_K], s_1[7]
            d_lng[i], d_lnb[i], d_bpw2[i] = s_1[5], s_1[6], s_1[4]
            mod_mix = [s_2[1], s_2[2], s_1[0]]
            d_mix_g[l] = s_2[3]
        dmod[l] = jnp.stack(mod_mix + mod_ffn)
        pending = [g_up, _wgrad(du, h, f"wgrad_in_l{l}", tk), g_out]
    grad_x = dx[None]

    small = [jnp.stack(dmod), jnp.stack(d_mix_g), jnp.stack(d_ffn_g), d_final_g, jnp.stack(d_pool),
             jnp.stack(d_pscale), jnp.stack(d_conv), jnp.stack(d_bpw1), jnp.stack(d_wdw), jnp.stack(d_bdw),
             jnp.stack(d_lng), jnp.stack(d_lnb), jnp.stack(d_bpw2)]
    small_rows = [_rows_of(a) for a in small]
    n_rows = sum(a.shape[0] for a in small_rows)
    pad_rows = -(-n_rows // 8) * 8
    recv, g3 = _tail_comm(pending, late_specs, recv, 0, _pad_rows(jnp.concatenate(small_rows, axis=0), pad_rows))
    g3 = g3.reshape(NDEV * pad_rows, D)
    summed = _sum_small(g3, pad_rows)

    m_pack = _pack(m_ab_w_in, m_ab_w_out, m_cf_w_pw1, m_cf_w_pw2, m_ffn_w_gate, m_ffn_w_up, m_ffn_w_down)
    v_pack = _pack(v_ab_w_in, v_ab_w_out, v_cf_w_pw1, v_cf_w_pw2, v_ffn_w_gate, v_ffn_w_up, v_ffn_w_down)
    big = [_unpack(p) for p in _adam_big(recv, w_pack, m_pack, v_pack)]

    outs, o = [], 0
    for a, r in zip(small, small_rows):
        outs.append(summed[o:o + r.shape[0]].reshape(-1)[:a.size].reshape(a.shape))
        o += r.shape[0]
    (g_bmod, g_mix_g, g_ffn_g, g_final, g_pool, g_pscale, g_conv, g_bpw1, g_wdw, g_bdw, g_lng, g_lnb, g_bpw2) = outs
    g_bmod = g_bmod.reshape(DEPTH, N_MOD * D)

    def my_shard(a):
        per = a.shape[-1] // NDEV
        return lax.dynamic_slice_in_dim(a, me * per, per, axis=a.ndim - 1)

    g_conv, g_bpw1, g_wdw, g_bdw, g_lng, g_lnb, g_bpw2 = [
        my_shard(a) for a in (g_conv, g_bpw1, g_wdw, g_bdw, g_lng, g_lnb, g_bpw2)]

    dmod_all = g3.reshape(NDEV, pad_rows, D)[:, :DEPTH * N_MOD, :].reshape(NDEV, DEPTH, N_MOD * D)
    dmod_mine = jnp.transpose(lax.dynamic_slice_in_dim(dmod_all, me * MODW, MODW, axis=2), (1, 0, 2))
    g_wmod, d_wmod, nm_wmod, nv_wmod = _mod_bwd_adam(c_act.T, dmod_mine, w_mod, m_w_mod, v_w_mod)

    small_params = [
        (norm_mix_g, g_mix_g, m_norm_mix_g, v_norm_mix_g), (norm_ffn_g, g_ffn_g, m_norm_ffn_g, v_norm_ffn_g),
        (b_mod, g_bmod, m_b_mod, v_b_mod), (ab_conv, g_conv, m_ab_conv, v_ab_conv),
        (ab_w_pool, g_pool, m_ab_w_pool, v_ab_w_pool), (ab_pool_scale, g_pscale, m_ab_pool_scale, v_ab_pool_scale),
        (cf_b_pw1, g_bpw1, m_cf_b_pw1, v_cf_b_pw1), (cf_w_dw, g_wdw, m_cf_w_dw, v_cf_w_dw),
        (cf_b_dw, g_bdw, m_cf_b_dw, v_cf_b_dw), (cf_ln_g, g_lng, m_cf_ln_g, v_cf_ln_g),
        (cf_ln_b, g_lnb, m_cf_ln_b, v_cf_ln_b), (cf_b_pw2, g_bpw2, m_cf_b_pw2, v_cf_b_pw2),
        (final_norm_g, g_final, m_final_norm_g, v_final_norm_g)]

    def two_d(a):
        return a.reshape(-1, a.shape[-1])

    upd = _adam_small([tuple(two_d(a) for a in p) for p in small_params])
    upd = [tuple(r.reshape(p[0].shape) for r in u) for u, p in zip(upd, small_params)]
    (s_mix, s_ffn, s_bmod, s_conv, s_pool, s_pscale, s_bpw1, s_wdw, s_bdw, s_lng, s_lnb, s_bpw2, s_final) = upd
    small_g = [p[1] for p in small_params]
    (q_mix, q_ffn, q_bmod, q_conv, q_pool, q_pscale, q_bpw1, q_wdw, q_bdw, q_lng, q_lnb, q_bpw2, q_final) = small_g

    def ordered(k):
        ab_in, ab_out, pw1, pw2, wg, wu, wd = big[k]
        if k == 0:
            sm = dict(mix=q_mix, ffn=q_ffn, bmod=q_bmod, conv=q_conv, pool=q_pool, pscale=q_pscale, bpw1=q_bpw1,
                      wdw=q_wdw, bdw=q_bdw, lng=q_lng, lnb=q_lnb, bpw2=q_bpw2, final=q_final)
            wmod = g_wmod
        else:
            j = k - 1
            sm = dict(mix=s_mix[j], ffn=s_ffn[j], bmod=s_bmod[j], conv=s_conv[j], pool=s_pool[j], pscale=s_pscale[j],
                      bpw1=s_bpw1[j], wdw=s_wdw[j], bdw=s_bdw[j], lng=s_lng[j], lnb=s_lnb[j], bpw2=s_bpw2[j],
                      final=s_final[j])
            wmod = (d_wmod, nm_wmod, nv_wmod)[j]
        return [sm["mix"], sm["ffn"], wmod, sm["bmod"], ab_in, sm["conv"], sm["pool"], sm["pscale"], ab_out,
                pw1, sm["bpw1"], sm["wdw"], sm["bdw"], sm["lng"], sm["lnb"], pw2, sm["bpw2"], wg, wu, wd, sm["final"]]

    return (loss, grad_x, *ordered(0), *ordered(1), *ordered(2), *ordered(3))
```

```python
import functools

import jax
import jax.numpy as jnp
from jax import lax
from jax.experimental import pallas as pl
from jax.experimental.pallas import tpu as pltpu

F32 = jnp.float32
BF16 = jnp.bfloat16
MESH = pl.DeviceIdType.MESH

NDEV = 8
DEPTH = 4
D = 1024
FF = 2816
FS = FF // NDEV
DA = 512
PG = 128
POOL = ((2, 1, 0), (4, 2, 1), (8, 4, 3), (16, 8, 7))
CONF_K = 31
CONF_L = 15
N_MOD = 6
MODW = N_MOD * D // NDEV
RMS_EPS = 1e-6
LN_EPS = 1e-5

ROWS_IN, ROWS_OUT = 2 * D // NDEV, D // NDEV
OFF_IN, OFF_OUT = 0, ROWS_IN
OFF_G = OFF_OUT + ROWS_OUT
OFF_U = OFF_G + FS
OFF_D = OFF_U + FS
LROWS = OFF_D + FS

MXU_N = 256
HALO = 16
VMEM_LIMIT = 60 * 1024 * 1024

ADAM_LR, ADAM_B1, ADAM_B2, ADAM_EPS, ADAM_WD, ADAM_STEP = 1e-3, 0.9, 0.999, 1e-8, 0.01, 10
ADAM_C1 = 1.0 / (1.0 - ADAM_B1 ** ADAM_STEP)
ADAM_C2 = 1.0 / (1.0 - ADAM_B2 ** ADAM_STEP)


def _nn(a, b):
    return jnp.dot(a, b, preferred_element_type=F32)


def _nt(a, b):
    return lax.dot_general(a, b, (((1,), (1,)), ((), ())), preferred_element_type=F32)


def _tn(a, b):
    return lax.dot_general(a, b, (((0,), (0,)), ((), ())), preferred_element_type=F32)


def _sig(x):
    return 1.0 / (1.0 + jnp.exp(-x))


def _rowsum(x):
    return jnp.sum(x, axis=0, keepdims=True)


def _lanemean(x):
    return jnp.mean(x, axis=-1, keepdims=True)


def _norm_fwd(x, g, sc, sh):
    r = lax.rsqrt(_lanemean(x * x) + RMS_EPS)
    n = x * r
    return n * (g * (1.0 + sc)) + sh, n, r


def _norm_bwd(dh, n, r, g, sc):
    dn = dh * (g * (1.0 + sc))
    return r * (dn - n * _lanemean(dn * n))


def _adam(w, g, m, v):
    m2 = ADAM_B1 * m + (1.0 - ADAM_B1) * g
    v2 = ADAM_B2 * v + (1.0 - ADAM_B2) * (g * g)
    delta = -ADAM_LR * ((m2 * ADAM_C1) / (jnp.sqrt(v2 * ADAM_C2) + ADAM_EPS) + ADAM_WD * w)
    return delta, m2, v2


def _load_weights(g_ref, specs, sems):
    cps = [pltpu.make_async_copy(g_ref.at[:, pl.ds(off, rows), :], dst, sems.at[k])
           for k, (off, rows, dst) in enumerate(specs)]
    for cp in cps:
        cp.start()
    for cp in cps:
        cp.wait()


def _load_weight_rows(specs, sems):
    cps = [pltpu.make_async_copy(g_ref.at[d, pl.ds(off, rows), :], dst.at[pl.ds(d * rows, rows), :],
                                 sems.at[NDEV * k + d])
           for k, (g_ref, off, rows, dst) in enumerate(specs) for d in range(NDEV)]
    for cp in cps:
        cp.start()
    for cp in cps:
        cp.wait()


def _cparams(*sem):
    return pltpu.CompilerParams(dimension_semantics=sem if sem else None, vmem_limit_bytes=VMEM_LIMIT)


def _tile(tm, w):
    return pl.BlockSpec((tm, w), lambda i: (i, 0))


def _full(shape):
    nd = len(shape)
    return pl.BlockSpec(shape, lambda i: (0,) * nd)


def _halo_specs(tm, w, total_rows):
    tb = tm // HALO
    nb = total_rows // HALO
    prev = pl.BlockSpec((HALO, w), lambda i: (jnp.maximum(i * tb - 1, 0), 0))
    nxt = pl.BlockSpec((HALO, w), lambda i: (jnp.minimum((i + 1) * tb, nb - 1), 0))
    return prev, nxt


ANY = pl.BlockSpec(memory_space=pl.ANY)


def _peers():
    x, y, c = lax.axis_index("x"), lax.axis_index("y"), lax.axis_index("c")
    return x, y, c


def _gather_small(v, name):
    m_per, n = v.shape

    def body(x_ref, out_ref, send_sems, recv_sems, local_sem):
        x, y, c = _peers()
        me, sibling = (x, y, c), (x, y, 1 - c)
        chips = [(1 - x, y), (x, 1 - y), (1 - x, 1 - y)]

        def rows(px, py, pc):
            return out_ref.at[pl.ds((4 * px + 2 * py + pc) * m_per, m_per), :]

        def copy(k, block, to, src=None):
            return pltpu.make_async_remote_copy(
                src_ref=rows(*block) if src is None else src, dst_ref=rows(*block),
                send_sem=send_sems.at[k], recv_sem=recv_sems.at[k], device_id=to, device_id_type=MESH)

        mine = pltpu.make_async_copy(x_ref, rows(*me), local_sem)
        mine.start()
        first = [copy(0, me, sibling, src=x_ref)]
        first += [copy(1 + j, me, (*chip, c), src=x_ref) for j, chip in enumerate(chips)]
        for cp in first:
            cp.start()
        passed = [copy(4 + j, (*chip, c), sibling) for j, chip in enumerate(chips)]
        for j, chip in enumerate(chips):
            copy(1 + j, (*chip, c), me).wait_recv()
            passed[j].start()
        copy(0, sibling, me).wait_recv()
        for j, chip in enumerate(chips):
            copy(4 + j, (*chip, 1 - c), me).wait_recv()
        for cp in first + passed:
            cp.wait_send()
        mine.wait()

    return pl.pallas_call(
        body, name=name,
        out_shape=jax.ShapeDtypeStruct((NDEV * m_per, n), v.dtype),
        in_specs=[pl.BlockSpec(memory_space=pltpu.VMEM)],
        out_specs=pl.BlockSpec(memory_space=pltpu.VMEM),
        scratch_shapes=[pltpu.SemaphoreType.DMA((7,)), pltpu.SemaphoreType.DMA((7,)), pltpu.SemaphoreType.DMA],
        compiler_params=pltpu.CompilerParams(vmem_limit_bytes=VMEM_LIMIT),
    )(v)


class _Comm:
    def __init__(self, ins, outs, aliases, bind):
        self.ins, self.outs, self.aliases, self.bind = ins, outs, aliases, bind


COMM_SEMS = [pltpu.SemaphoreType.DMA((7,)), pltpu.SemaphoreType.DMA((7,)), pltpu.SemaphoreType.DMA]


def _gather_hooks(p_ref, out_ref, send_sems, recv_sems, local_sem):
    def parts():
        x, y, c = _peers()
        me, sibling = (x, y, c), (x, y, 1 - c)
        chips = [(1 - x, y), (x, 1 - y), (1 - x, 1 - y)]

        def slab(px, py, pc):
            return out_ref.at[4 * px + 2 * py + pc]

        def copy(k, block, to, src=None):
            return pltpu.make_async_remote_copy(
                src_ref=slab(*block) if src is None else src, dst_ref=slab(*block),
                send_sem=send_sems.at[k], recv_sem=recv_sems.at[k], device_id=to, device_id_type=MESH)

        def mine():
            return pltpu.make_async_copy(p_ref, slab(*me), local_sem)

        def first():
            return [copy(0, me, sibling, src=p_ref)] + [copy(1 + j, me, (*chip, c), src=p_ref)
                                                        for j, chip in enumerate(chips)]

        def passed():
            return [copy(4 + j, (*chip, c), sibling) for j, chip in enumerate(chips)]

        def from_chips():
            return [copy(1 + j, (*chip, c), me) for j, chip in enumerate(chips)]

        def from_sibling():
            return [copy(0, sibling, me)] + [copy(4 + j, (*chip, 1 - c), me) for j, chip in enumerate(chips)]

        return mine, first, passed, from_chips, from_sibling

    def start():
        mine, first, _, _, _ = parts()
        mine().start()
        for cp in first():
            cp.start()

    def middle():
        _, _, passed, from_chips, _ = parts()
        for arrived, onward in zip(from_chips(), passed()):
            arrived.wait_recv()
            onward.start()

    def end():
        mine, first, passed, _, from_sibling = parts()
        for cp in from_sibling():
            cp.wait_recv()
        for cp in first() + passed():
            cp.wait_send()
        mine().wait()

    return start, middle, end


def _gather_comm(p):
    return _Comm([p], [jax.ShapeDtypeStruct((NDEV,) + p.shape, p.dtype)], {},
                 lambda cins, couts, sems: _gather_hooks(cins[0], couts[0], *sems))


def _head_comm(cond, p):
    def body(c_ref, p_ref, co_ref, po_ref, *sems):
        c_hooks = _gather_hooks(c_ref, co_ref, *sems[:3])
        p_hooks = _gather_hooks(p_ref, po_ref, *sems[3:])
        for step in range(3):
            c_hooks[step]()
            p_hooks[step]()

    return pl.pallas_call(
        body, name="head_comm",
        out_shape=[jax.ShapeDtypeStruct((NDEV,) + cond.shape, cond.dtype),
                   jax.ShapeDtypeStruct((NDEV,) + p.shape, p.dtype)],
        in_specs=[ANY, ANY], out_specs=[ANY, ANY], scratch_shapes=COMM_SEMS + COMM_SEMS,
    )(cond, p)


def _scatter_hooks(src_refs, specs, r_ref, layer, send_sems, recv_sems, local_sem):
    total = sum(rows for _, rows in specs)

    def start():
        x, y, c = _peers()
        me = 4 * x + 2 * y + c

        def part(k, dev):
            off, rows = specs[k]
            src = src_refs[k].at[pl.ds(pl.multiple_of(dev * rows, 16), rows), :]
            return src, r_ref.at[me, layer, pl.ds(off, rows), :]

        for k in range(len(specs)):
            src, dst = part(k, me)
            pltpu.make_async_copy(src, dst, local_sem).start()
        for r in range(1, NDEV):
            px = 1 - x if r & 4 else x
            py = 1 - y if r & 2 else y
            pc = 1 - c if r & 1 else c
            for k in range(len(specs)):
                src, dst = part(k, 4 * px + 2 * py + pc)
                pltpu.make_async_remote_copy(
                    src_ref=src, dst_ref=dst, send_sem=send_sems.at[r - 1], recv_sem=recv_sems.at[r - 1],
                    device_id=(px, py, pc), device_id_type=MESH).start()

    def end():
        x, y, c = _peers()
        whole = r_ref.at[0, layer, pl.ds(0, total), :]
        for r in range(1, NDEV):
            done = pltpu.make_async_remote_copy(
                src_ref=whole, dst_ref=whole, send_sem=send_sems.at[r - 1], recv_sem=recv_sems.at[r - 1],
                device_id=(x, y, c), device_id_type=MESH)
            done.wait_recv()
            done.wait_send()
        pltpu.make_async_copy(whole, whole, local_sem).wait()

    return start, None, end


def _scatter_comm(srcs, specs, recv, layer):
    k = len(srcs)
    return _Comm(list(srcs) + [recv], [jax.ShapeDtypeStruct(recv.shape, recv.dtype)], {k: 0},
                 lambda cins, couts, sems: _scatter_hooks(cins[:k], specs, couts[0], layer, *sems))


def _tail_comm(srcs, specs, recv, layer, small):
    k = len(srcs)

    def body(*refs):
        src_refs, small_ref, r_ref, g_ref, sems = refs[:k], refs[k + 1], refs[k + 2], refs[k + 3], refs[k + 4:]
        s_start, _, s_end = _scatter_hooks(src_refs, specs, r_ref, layer, *sems[:3])
        g_start, g_middle, g_end = _gather_hooks(small_ref, g_ref, *sems[3:])
        s_start()
        g_start()
        g_middle()
        g_end()
        s_end()

    return pl.pallas_call(
        body, name="tail_comm",
        out_shape=[jax.ShapeDtypeStruct(recv.shape, recv.dtype),
                   jax.ShapeDtypeStruct((NDEV,) + small.shape, small.dtype)],
        in_specs=[ANY] * (k + 2), out_specs=[ANY, ANY], scratch_shapes=COMM_SEMS + COMM_SEMS,
        input_output_aliases={k: 0},
    )(*srcs, recv, small)


def _empty_recv():
    def body(o_ref):
        del o_ref

    return pl.pallas_call(body, name="recv_buffer", out_specs=ANY,
                          out_shape=jax.ShapeDtypeStruct((NDEV, DEPTH, LROWS, D), BF16))()


def _host_call(inner, comm, *, name, grid, in_specs, out_specs, out_shape, scratch_shapes, operands):
    if comm is None:
        return pl.pallas_call(
            inner, name=name, grid=grid, in_specs=in_specs, out_specs=out_specs, out_shape=out_shape,
            scratch_shapes=scratch_shapes, compiler_params=_cparams("arbitrary"))(*operands)
    n_in, n_out, n_s = len(in_specs), len(out_specs), len(scratch_shapes)
    k_in, k_out = len(comm.ins), len(comm.outs)
    steps = grid[0]

    def body(*refs):
        ins, cins = refs[:n_in], refs[n_in:n_in + k_in]
        o0 = n_in + k_in
        outs, couts = refs[o0:o0 + n_out], refs[o0 + n_out:o0 + n_out + k_out]
        s0 = o0 + n_out + k_out
        scr, sems = refs[s0:s0 + n_s], refs[s0 + n_s:]
        start, middle, end = comm.bind(cins, couts, sems)
        i = pl.program_id(0)
        pl.when(i == 0)(start)
        if middle is not None:
            pl.when(i == steps * 3 // 4)(middle)
        inner(*ins, *outs, *scr)
        pl.when(i == steps - 1)(end)

    return pl.pallas_call(
        body, name=name, grid=grid, in_specs=list(in_specs) + [ANY] * k_in,
        out_specs=list(out_specs) + [ANY] * k_out, out_shape=list(out_shape) + list(comm.outs),
        scratch_shapes=list(scratch_shapes) + COMM_SEMS,
        input_output_aliases={n_in + a: n_out + b for a, b in comm.aliases.items()},
        compiler_params=_cparams("arbitrary"))(*operands, *comm.ins)


def _mod_fwd(c_all, w_mod, b_sl):
    def body(c_ref, w_ref, b_ref, o_ref, ca_ref):
        cv = c_ref[...]
        ca = cv * _sig(cv)
        ca_ref[...] = ca
        o_ref[...] = jnp.dot(ca, w_ref[0], preferred_element_type=F32, precision=lax.Precision.HIGHEST) + b_ref[0]

    return pl.pallas_call(
        body, name="mod_fwd", grid=(DEPTH,),
        in_specs=[_full((NDEV, D)), pl.BlockSpec((1, D, MODW), lambda l: (l, 0, 0)),
                  pl.BlockSpec((1, 1, MODW), lambda l: (l, 0, 0))],
        out_specs=[pl.BlockSpec((NDEV, MODW), lambda l: (0, l)), _full((NDEV, D))],
        out_shape=[jax.ShapeDtypeStruct((NDEV, DEPTH * MODW), F32), jax.ShapeDtypeStruct((NDEV, D), F32)],
        compiler_params=_cparams("arbitrary"),
    )(c_all, w_mod, b_sl)


def _mod_bwd_adam(ca_t, dmod, w, m, v):
    def body(ct_ref, dm_ref, w_ref, m_ref, v_ref, g_ref, d_ref, mo_ref, vo_ref):
        g = jnp.dot(ct_ref[...], dm_ref[0], preferred_element_type=F32, precision=lax.Precision.HIGHEST)
        delta, m2, v2 = _adam(w_ref[0], g, m_ref[0], v_ref[0])
        g_ref[0], d_ref[0], mo_ref[0], vo_ref[0] = g, delta, m2, v2

    blk = pl.BlockSpec((1, D, MODW), lambda l: (l, 0, 0))
    sds = jax.ShapeDtypeStruct(w.shape, F32)
    return pl.pallas_call(
        body, name="mod_bwd_adam", grid=(DEPTH,),
        in_specs=[_full((D, NDEV)), pl.BlockSpec((1, NDEV, MODW), lambda l: (l, 0, 0)), blk, blk, blk],
        out_specs=[blk] * 4, out_shape=[sds] * 4,
        compiler_params=_cparams("arbitrary"),
    )(ca_t, dmod, w, m, v)


def _inproj(x, vec, bias, gw, layer, tm, comm=None):
    t = x.shape[0]

    def body(x_ref, vec_ref, b_ref, g_ref, u_ref, w_s, sems):
        @pl.when(pl.program_id(0) == 0)
        def _():
            _load_weights(g_ref, [(OFF_IN, ROWS_IN, w_s)], sems)

        h, _, _ = _norm_fwd(x_ref[...], vec_ref[0:1], vec_ref[1:2], vec_ref[2:3])
        w = w_s[...].reshape(2 * D, D)
        u_ref[...] = (_nt(h.astype(BF16), w) + b_ref[...]).astype(BF16)

    res = _host_call(
        body, comm, name=f"inproj_l{layer}", grid=(t // tm,),
        in_specs=[_tile(tm, D), _full((8, D)), _full((1, 2 * D)), ANY],
        out_specs=[_tile(tm, 2 * D)], out_shape=[jax.ShapeDtypeStruct((t, 2 * D), BF16)],
        scratch_shapes=[pltpu.VMEM((NDEV, ROWS_IN, D), BF16), pltpu.SemaphoreType.DMA((1,))],
        operands=(x, vec, bias, gw))
    return res[0] if comm is None else res


def _fill_even(qe, pe, be, part_ref, lo, rows, valid):
    cg = part_ref[:, DA:2 * DA].astype(F32)
    v = part_ref[:, 2 * DA:3 * DA].astype(F32)
    q = cg * v
    p = part_ref[:, 3 * DA:4 * DA].astype(F32)
    if valid is not None:
        q = jnp.where(valid, q, 0.0)
        p = jnp.where(valid, p, 0.0)
    qe[lo:lo + rows, :] = q
    pe[lo:lo + rows, :] = p
    if be is not None:
        b = part_ref[:, 0:DA].astype(F32)
        be[lo:lo + rows, :] = b if valid is None else jnp.where(valid, b, 0.0)


def _conv3(ca_ref, qe, tm):
    return (ca_ref[0:1] * qe[HALO - 1:HALO - 1 + tm] + ca_ref[1:2] * qe[HALO:HALO + tm]
            + ca_ref[2:3] * qe[HALO + 1:HALO + 1 + tm])


def _pool_counts(t0, rows, first_row, left, right, t):
    tg = t0 + first_row + lax.broadcasted_iota(jnp.int32, (rows, 1), 0)
    cnt = jnp.minimum(tg + right, t - 1) - jnp.maximum(tg - left, 0) + 1
    return jnp.maximum(cnt, 1).astype(F32)


def _pool_minus_id(pe, gi, left, right, inv_cnt, tm):
    c0 = gi * PG
    s = pe[HALO - left:HALO - left + tm, c0:c0 + PG]
    for j in range(-left + 1, right + 1):
        s = s + pe[HALO + j:HALO + j + tm, c0:c0 + PG]
    return s * inv_cnt - pe[HALO:HALO + tm, c0:c0 + PG]


def _mix_even_fwd(u, x, vec, ca, wp, ps, gw, layer, tm, comm=None):
    t = x.shape[0]
    n = t // tm
    e = tm + 2 * HALO

    def body(u_ref, up_ref, un_ref, x_ref, vec_ref, ca_ref, wp_ref, ps_ref, g_ref, xo_ref, y_ref, w_s, qe, pe, sems):
        i = pl.program_id(0)

        @pl.when(i == 0)
        def _():
            _load_weights(g_ref, [(OFF_OUT, ROWS_OUT, w_s)], sems)

        _fill_even(qe, pe, None, up_ref, 0, HALO, i > 0)
        _fill_even(qe, pe, None, u_ref, HALO, tm, None)
        _fill_even(qe, pe, None, un_ref, HALO + tm, HALO, i < n - 1)
        ya = u_ref[:, 0:DA].astype(F32) * _conv3(ca_ref, qe, tm)
        parts = [ya]
        for gi, (_, left, right) in enumerate(POOL):
            inv = 1.0 / _pool_counts(i * tm, tm, 0, left, right, t)
            pm = _pool_minus_id(pe, gi, left, right, inv, tm)
            parts.append(_nn(pm.astype(BF16), wp_ref[gi]) * ps_ref[0:1, gi * PG:(gi + 1) * PG])
        cat = jnp.concatenate(parts, axis=-1).astype(BF16)
        y = _nn(cat, w_s[...].reshape(D, D))
        y_ref[...] = y.astype(BF16)
        xo_ref[...] = x_ref[...] + vec_ref[3:4] * y

    prev, nxt = _halo_specs(tm, 2 * D, t)
    return _host_call(
        body, comm, name=f"mix_even_fwd_l{layer}", grid=(n,),
        in_specs=[_tile(tm, 2 * D), prev, nxt, _tile(tm, D), _full((8, D)), _full((3, DA)),
                  _full((4, PG, PG)), _full((1, DA)), ANY],
        out_specs=[_tile(tm, D), _tile(tm, D)],
        out_shape=[jax.ShapeDtypeStruct((t, D), F32), jax.ShapeDtypeStruct((t, D), BF16)],
        scratch_shapes=[pltpu.VMEM((NDEV, ROWS_OUT, D), BF16), pltpu.VMEM((e, DA), F32), pltpu.VMEM((e, DA), F32),
                        pltpu.SemaphoreType.DMA((1,))],
        operands=(u, u, u, x, vec, ca, wp, ps, gw))


def _mix_even_bwd(dxo, u, x, y, vec, ca, wp, ps, gw, layer, tm, comm=None):
    t = x.shape[0]
    n = t // tm
    e = tm + 2 * HALO

    def body(dxo_ref, dp_ref, dn_ref, u_ref, up_ref, un_ref, x_ref, y_ref, vec_ref, ca_ref, wp_ref, ps_ref, g_ref,
             dxi_ref, du_ref, h_ref, cat_ref, dy_ref, sums_ref, dwp_ref,
             wo_s, wi_s, dye, qe, pe, be, dce, epe, sems):
        i = pl.program_id(0)

        @pl.when(i == 0)
        def _():
            _load_weights(g_ref, [(OFF_OUT, ROWS_OUT, wo_s), (OFF_IN, ROWS_IN, wi_s)], sems)
            sums_ref[...] = jnp.zeros_like(sums_ref)
            dwp_ref[...] = jnp.zeros_like(dwp_ref)

        gate = vec_ref[3:4]
        dxo_m = dxo_ref[...]
        dye[0:HALO, :] = jnp.where(i > 0, gate * dp_ref[...], 0.0).astype(BF16)
        dye[HALO:HALO + tm, :] = (gate * dxo_m).astype(BF16)
        dye[HALO + tm:e, :] = jnp.where(i < n - 1, gate * dn_ref[...], 0.0).astype(BF16)
        _fill_even(qe, pe, be, up_ref, 0, HALO, i > 0)
        _fill_even(qe, pe, be, u_ref, HALO, tm, None)
        _fill_even(qe, pe, be, un_ref, HALO + tm, HALO, i < n - 1)
        sums_ref[0:1, :] += _rowsum(dxo_m * y_ref[...].astype(F32))

        dcat = _nt(dye[...], wo_s[...].reshape(D, D))
        dce[...] = dcat[:, 0:DA] * be[...]
        cq = _conv3(ca_ref, qe, tm)
        bg = be[HALO:HALO + tm]
        dc_m = dce[HALO:HALO + tm]
        dbg = dcat[HALO:HALO + tm, 0:DA] * cq
        dq = (ca_ref[0:1] * dce[HALO + 1:HALO + 1 + tm] + ca_ref[1:2] * dc_m
              + ca_ref[2:3] * dce[HALO - 1:HALO - 1 + tm])
        cg = u_ref[:, DA:2 * DA].astype(F32)
        v = u_ref[:, 2 * DA:3 * DA].astype(F32)
        for k in range(3):
            sums_ref[4 + k:5 + k, 0:DA] += _rowsum(dc_m * qe[HALO - 1 + k:HALO - 1 + k + tm])
        du_parts = [dbg, dq * v, dq * cg]
        cat_parts = [bg * cq]
        for gi, (_, left, right) in enumerate(POOL):
            c0 = gi * PG
            scale = ps_ref[0:1, c0:c0 + PG]
            dyb = dcat[:, DA + c0:DA + c0 + PG]
            dybs = (dyb * scale).astype(BF16)
            dpm = _nt(dybs, wp_ref[gi])
            inv_e = 1.0 / _pool_counts(i * tm, e, -HALO, left, right, t)
            epe[:, c0:c0 + PG] = dpm * inv_e
            s_adj = epe[HALO - right:HALO - right + tm, c0:c0 + PG]
            for j in range(-right + 1, left + 1):
                s_adj = s_adj + epe[HALO + j:HALO + j + tm, c0:c0 + PG]
            du_parts.append(s_adj - dpm[HALO:HALO + tm])
            inv_m = 1.0 / _pool_counts(i * tm, tm, 0, left, right, t)
            pm = _pool_minus_id(pe, gi, left, right, inv_m, tm).astype(BF16)
            ybpre = _nn(pm, wp_ref[gi])
            sums_ref[7:8, c0:c0 + PG] += _rowsum(dyb[HALO:HALO + tm] * ybpre)
            dwp_ref[gi] += _tn(pm, dybs[HALO:HALO + tm])
            cat_parts.append(ybpre * scale)
        du = jnp.concatenate(du_parts, axis=-1).astype(BF16)
        du_ref[...] = du
        cat_ref[...] = jnp.concatenate(cat_parts, axis=-1).astype(BF16)
        dy_ref[...] = dye[HALO:HALO + tm, :]
        dh = _nn(du, wi_s[...].reshape(2 * D, D))
        g, sc, sh = vec_ref[0:1], vec_ref[1:2], vec_ref[2:3]
        h, nrm, r = _norm_fwd(x_ref[...], g, sc, sh)
        h_ref[...] = h.astype(BF16)
        dxi_ref[...] = dxo_m + _norm_bwd(dh, nrm, r, g, sc)
        sums_ref[1:2, :] += _rowsum(dh)
        sums_ref[2:3, :] += _rowsum(dh * nrm)

        @pl.when(i == n - 1)
        def _():
            p = sums_ref[2:3, :]
            sums_ref[3:4, :] = p * (1.0 + sc)
            sums_ref[2:3, :] = p * g

    prev_u, nxt_u = _halo_specs(tm, 2 * D, t)
    prev_d, nxt_d = _halo_specs(tm, D, t)
    return _host_call(
        body, comm, name=f"mix_even_bwd_l{layer}", grid=(n,),
        in_specs=[_tile(tm, D), prev_d, nxt_d, _tile(tm, 2 * D), prev_u, nxt_u, _tile(tm, D), _tile(tm, D),
                  _full((8, D)), _full((3, DA)), _full((4, PG, PG)), _full((1, DA)), ANY],
        out_specs=[_tile(tm, D), _tile(tm, 2 * D), _tile(tm, D), _tile(tm, D), _tile(tm, D),
                   _full((16, D)), _full((4, PG, PG))],
        out_shape=[jax.ShapeDtypeStruct((t, D), F32), jax.ShapeDtypeStruct((t, 2 * D), BF16),
                   jax.ShapeDtypeStruct((t, D), BF16), jax.ShapeDtypeStruct((t, D), BF16),
                   jax.ShapeDtypeStruct((t, D), BF16), jax.ShapeDtypeStruct((16, D), F32),
                   jax.ShapeDtypeStruct((4, PG, PG), F32)],
        scratch_shapes=[pltpu.VMEM((NDEV, ROWS_OUT, D), BF16), pltpu.VMEM((NDEV, ROWS_IN, D), BF16),
                        pltpu.VMEM((e, D), BF16), pltpu.VMEM((e, DA), F32), pltpu.VMEM((e, DA), F32),
                        pltpu.VMEM((e, DA), F32), pltpu.VMEM((e, DA), F32), pltpu.VMEM((e, DA), F32),
                        pltpu.SemaphoreType.DMA((2,))],
        operands=(dxo, dxo, dxo, u, u, u, x, y, vec, ca, wp, ps, gw))


def _fill_glu(ze, part_ref, lo, rows, valid):
    a = part_ref[:, 0:D].astype(F32)
    g = part_ref[:, D:2 * D].astype(F32)
    z = a * _sig(g)
    ze[lo:lo + rows, :] = z if valid is None else jnp.where(valid, z, 0.0)


SHIFT_ROWS = 24


def _shifted_copies(dst, src, tm):
    rows = tm + SHIFT_ROWS
    for j in range(8):
        dst[j, :, :] = src[j:j + rows, :]


def _shifted(dst, shift, tm):
    lo = shift // 8 * 8
    return dst[shift % 8, lo:lo + tm, :]


def _layer_norm_parts(z2):
    mu = _lanemean(z2)
    d = z2 - mu
    rstd = lax.rsqrt(_lanemean(d * d) + LN_EPS)
    return d * rstd, rstd


def _mix_odd_fwd(u, x, vec, wdw, sm, gw, layer, tm):
    t = x.shape[0]
    n = t // tm
    e = tm + 2 * HALO

    def body(u_ref, up_ref, un_ref, x_ref, vec_ref, wdw_ref, sm_ref, g_ref, xo_ref, y_ref, z2_ref, w_s, ze, zsh, sems):
        i = pl.program_id(0)

        @pl.when(i == 0)
        def _():
            _load_weights(g_ref, [(OFF_OUT, ROWS_OUT, w_s)], sems)

        _fill_glu(ze, up_ref, 0, HALO, i > 0)
        _fill_glu(ze, u_ref, HALO, tm, None)
        _fill_glu(ze, un_ref, HALO + tm, HALO, i < n - 1)
        _shifted_copies(zsh, ze, tm)
        z2 = sm_ref[0:1] + wdw_ref[0:1] * _shifted(zsh, 1, tm)
        for k in range(1, CONF_K):
            z2 = z2 + wdw_ref[k:k + 1] * _shifted(zsh, 1 + k, tm)
        z2_ref[...] = z2.astype(BF16)
        zn, _ = _layer_norm_parts(z2)
        lo = zn * sm_ref[1:2] + sm_ref[2:3]
        z3 = lo * _sig(lo)
        y = _nn(z3.astype(BF16), w_s[...].reshape(D, D)) + sm_ref[3:4]
        y_ref[...] = y.astype(BF16)
        xo_ref[...] = x_ref[...] + vec_ref[3:4] * y

    prev, nxt = _halo_specs(tm, 2 * D, t)
    return pl.pallas_call(
        body, name=f"mix_odd_fwd_l{layer}", grid=(n,),
        in_specs=[_tile(tm, 2 * D), prev, nxt, _tile(tm, D), _full((8, D)), _full((32, D)), _full((8, D)), ANY],
        out_specs=[_tile(tm, D), _tile(tm, D), _tile(tm, D)],
        out_shape=[jax.ShapeDtypeStruct((t, D), F32), jax.ShapeDtypeStruct((t, D), BF16),
                   jax.ShapeDtypeStruct((t, D), BF16)],
        scratch_shapes=[pltpu.VMEM((NDEV, ROWS_OUT, D), BF16), pltpu.VMEM((e, D), F32),
                        pltpu.VMEM((8, tm + SHIFT_ROWS, D), F32), pltpu.SemaphoreType.DMA((1,))],
        compiler_params=_cparams("arbitrary"),
    )(u, u, u, x, vec, wdw, sm, gw)


def _mix_odd_bwd1(dxo, y, z2, vec, sm, gw, layer, tm):
    t = dxo.shape[0]
    n = t // tm

    def body(dxo_ref, y_ref, z2_ref, vec_ref, sm_ref, g_ref, dy_ref, z3_ref, dz2_ref, sums_ref, w_s, sems):
        i = pl.program_id(0)

        @pl.when(i == 0)
        def _():
            _load_weights(g_ref, [(OFF_OUT, ROWS_OUT, w_s)], sems)
            sums_ref[...] = jnp.zeros_like(sums_ref)

        dxo_m = dxo_ref[...]
        dy = vec_ref[3:4] * dxo_m
        dyb = dy.astype(BF16)
        dy_ref[...] = dyb
        sums_ref[0:1, :] += _rowsum(dxo_m * y_ref[...].astype(F32))
        sums_ref[4:5, :] += _rowsum(dy)
        dz3 = _nt(dyb, w_s[...].reshape(D, D))
        zn, rstd = _layer_norm_parts(z2_ref[...].astype(F32))
        lo = zn * sm_ref[1:2] + sm_ref[2:3]
        sg = _sig(lo)
        z3_ref[...] = (lo * sg).astype(BF16)
        dlo = dz3 * (sg * (1.0 + lo * (1.0 - sg)))
        sums_ref[5:6, :] += _rowsum(dlo * zn)
        sums_ref[6:7, :] += _rowsum(dlo)
        dzn = dlo * sm_ref[1:2]
        dz2 = rstd * (dzn - _lanemean(dzn) - zn * _lanemean(dzn * zn))
        sums_ref[7:8, :] += _rowsum(dz2)
        dz2_ref[...] = dz2.astype(BF16)

    return pl.pallas_call(
        body, name=f"mix_odd_bwd1_l{layer}", grid=(n,),
        in_specs=[_tile(tm, D), _tile(tm, D), _tile(tm, D), _full((8, D)), _full((8, D)), ANY],
        out_specs=[_tile(tm, D), _tile(tm, D), _tile(tm, D), _full((16, D))],
        out_shape=[jax.ShapeDtypeStruct((t, D), BF16)] * 3 + [jax.ShapeDtypeStruct((16, D), F32)],
        scratch_shapes=[pltpu.VMEM((NDEV, ROWS_OUT, D), BF16), pltpu.SemaphoreType.DMA((1,))],
        compiler_params=_cparams("arbitrary"),
    )(dxo, y, z2, vec, sm, gw)


def _mix_odd_bwd2(dz2, u, x, dxo, vec, wdw, gw, layer, tm, comm=None):
    t = x.shape[0]
    n = t // tm
    e = tm + 2 * HALO

    def body(dz_ref, dzp_ref, dzn_ref, u_ref, x_ref, dxo_ref, vec_ref, wdw_ref, g_ref,
             dxi_ref, du_ref, h_ref, sums_ref, dw_ref, w_s, de, zsh, sems):
        i = pl.program_id(0)

        @pl.when(i == 0)
        def _():
            _load_weights(g_ref, [(OFF_IN, ROWS_IN, w_s)], sems)
            sums_ref[...] = jnp.zeros_like(sums_ref)
            dw_ref[...] = jnp.zeros_like(dw_ref)

        de[0:HALO, :] = jnp.where(i > 0, dzp_ref[...].astype(F32), 0.0)
        de[HALO:HALO + tm, :] = dz_ref[...].astype(F32)
        de[HALO + tm:e, :] = jnp.where(i < n - 1, dzn_ref[...].astype(F32), 0.0)
        a = u_ref[:, 0:D].astype(F32)
        gg = u_ref[:, D:2 * D].astype(F32)
        sg = _sig(gg)
        z = a * sg
        _shifted_copies(zsh, de, tm)
        dz = None
        for k in range(CONF_K):
            shifted = _shifted(zsh, CONF_K - k, tm)
            term = wdw_ref[k:k + 1] * shifted
            dz = term if dz is None else dz + term
            dw_ref[k:k + 1, :] += _rowsum(z * shifted)
        da = dz * sg
        dg = dz * a * (sg * (1.0 - sg))
        sums_ref[8:9, :] += _rowsum(da)
        sums_ref[9:10, :] += _rowsum(dg)
        du = jnp.concatenate([da, dg], axis=-1).astype(BF16)
        du_ref[...] = du
        dh = _nn(du, w_s[...].reshape(2 * D, D))
        g, sc, sh = vec_ref[0:1], vec_ref[1:2], vec_ref[2:3]
        h, nrm, r = _norm_fwd(x_ref[...], g, sc, sh)
        h_ref[...] = h.astype(BF16)
        dxi_ref[...] = dxo_ref[...] + _norm_bwd(dh, nrm, r, g, sc)
        sums_ref[1:2, :] += _rowsum(dh)
        sums_ref[2:3, :] += _rowsum(dh * nrm)

        @pl.when(i == n - 1)
        def _():
            p = sums_ref[2:3, :]
            sums_ref[3:4, :] = p * (1.0 + sc)
            sums_ref[2:3, :] = p * g

    prev_d, nxt_d = _halo_specs(tm, D, t)
    return _host_call(
        body, comm, name=f"mix_odd_bwd2_l{layer}", grid=(n,),
        in_specs=[_tile(tm, D), prev_d, nxt_d, _tile(tm, 2 * D), _tile(tm, D), _tile(tm, D),
                  _full((8, D)), _full((32, D)), ANY],
        out_specs=[_tile(tm, D), _tile(tm, 2 * D), _tile(tm, D), _full((16, D)), _full((32, D))],
        out_shape=[jax.ShapeDtypeStruct((t, D), F32), jax.ShapeDtypeStruct((t, 2 * D), BF16),
                   jax.ShapeDtypeStruct((t, D), BF16), jax.ShapeDtypeStruct((16, D), F32),
                   jax.ShapeDtypeStruct((32, D), F32)],
        scratch_shapes=[pltpu.VMEM((NDEV, ROWS_IN, D), BF16), pltpu.VMEM((e, D), F32),
                        pltpu.VMEM((8, tm + SHIFT_ROWS, D), F32),
                        pltpu.SemaphoreType.DMA((1,))],
        operands=(dz2, dz2, dz2, u, x, dxo, vec, wdw, gw))


FCH = FF // 2


def _loss_head(x, tgt, g):
    r = lax.rsqrt(_lanemean(x * x) + RMS_EPS)
    nrm = x * r
    err = nrm * g - tgt
    dout = err * (1.0 / D)
    dn = dout * g
    return r * (dn - nrm * _lanemean(dn * nrm)), _rowsum(err * err) * (0.5 / D), _rowsum(dout * nrm)


def _ffn_fwd(x, vec, wsrc, layer, tm, comm=None, head=None):
    t = x.shape[0]
    g_a, off_g, g_b, off_u, off_d = wsrc

    def body(*refs):
        if head is None:
            x_ref, vec_ref, ga_ref, gb_ref, xo_ref, a_ref, b_ref, y_ref, wg_s, wu_s, wd_s, s_s, sems = refs
        else:
            (x_ref, vec_ref, ga_ref, gb_ref, t_ref, gf_ref, xo_ref, a_ref, b_ref, y_ref, fsum_ref,
             wg_s, wu_s, wd_s, s_s, sems) = refs

        @pl.when(pl.program_id(0) == 0)
        def _():
            _load_weight_rows([(ga_ref, off_g, FS, wg_s), (gb_ref, off_u, FS, wu_s), (gb_ref, off_d, FS, wd_s)], sems)
            if head is not None:
                fsum_ref[...] = jnp.zeros_like(fsum_ref)

        xv = x_ref[...]
        h, _, _ = _norm_fwd(xv, vec_ref[0:1], vec_ref[1:2], vec_ref[2:3])
        hb = h.astype(BF16)
        for ch in range(FF // MXU_N):
            rows = slice(ch * MXU_N, (ch + 1) * MXU_N)
            a = _nt(hb, wg_s[rows, :])
            b = _nt(hb, wu_s[rows, :])
            a_ref[:, rows] = a.astype(BF16)
            b_ref[:, rows] = b.astype(BF16)
            s_s[:, rows] = ((a * _sig(a)) * b).astype(BF16)
        y = _nn(s_s[...], wd_s[...])
        y_ref[...] = y.astype(BF16)
        x_out = xv + vec_ref[3:4] * y
        if head is None:
            xo_ref[...] = x_out
        else:
            dx, loss_row, dg_row = _loss_head(x_out, t_ref[...], gf_ref[...])
            xo_ref[...] = dx
            fsum_ref[0:1, :] += dg_row
            fsum_ref[1:2, :] += loss_row

    wsc = pltpu.VMEM((FF, D), BF16)
    in_specs = [_tile(tm, D), _full((8, D)), ANY, ANY]
    out_specs = [_tile(tm, D), _tile(tm, FF), _tile(tm, FF), _tile(tm, D)]
    out_shape = [jax.ShapeDtypeStruct((t, D), F32), jax.ShapeDtypeStruct((t, FF), BF16),
                 jax.ShapeDtypeStruct((t, FF), BF16), jax.ShapeDtypeStruct((t, D), BF16)]
    operands = (x, vec, g_a, g_b)
    if head is not None:
        in_specs += [_tile(tm, D), _full((1, D))]
        out_specs.append(_full((8, D)))
        out_shape.append(jax.ShapeDtypeStruct((8, D), F32))
        operands += tuple(head)
    return _host_call(
        body, comm, name=f"ffn_fwd_l{layer}", grid=(t // tm,),
        in_specs=in_specs, out_specs=out_specs, out_shape=out_shape,
        scratch_shapes=[wsc, wsc, wsc, pltpu.VMEM((tm, FF), BF16), pltpu.SemaphoreType.DMA((3 * NDEV,))],
        operands=operands)


def _ffn_bwd(dxo, x, y, a, b, vec, wsrc, layer, tm, comm=None):
    t = x.shape[0]
    n = t // tm
    g_a, off_g, g_b, off_u, off_d = wsrc

    def body(dxo_ref, x_ref, y_ref, a_ref, b_ref, vec_ref, ga_ref, gb_ref,
             dxi_ref, h_ref, dy_ref, s_ref, da_ref, db_ref, sums_ref, wg_s, wu_s, wd_s, sems):
        i = pl.program_id(0)

        @pl.when(i == 0)
        def _():
            _load_weight_rows([(ga_ref, off_g, FS, wg_s), (gb_ref, off_u, FS, wu_s), (gb_ref, off_d, FS, wd_s)], sems)
            sums_ref[...] = jnp.zeros_like(sums_ref)

        dxo_m = dxo_ref[...]
        sums_ref[0:1, :] += _rowsum(dxo_m * y_ref[...].astype(F32))
        dyb = (vec_ref[3:4] * dxo_m).astype(BF16)
        dy_ref[...] = dyb
        for ch in range(FF // MXU_N):
            cols = slice(ch * MXU_N, (ch + 1) * MXU_N)
            ds = _nt(dyb, wd_s[cols, :]).astype(BF16)
            av = a_ref[:, cols]
            bv = b_ref[:, cols]
            sg = _sig(av)
            sl = av * sg
            s_ref[:, cols] = sl * bv
            db_ref[:, cols] = ds * sl
            da_ref[:, cols] = (ds * bv) * (sg * (1.0 + av * (1.0 - sg)))
        dh = _nn(da_ref[...], wg_s[...]) + _nn(db_ref[...], wu_s[...])
        g, sc, sh = vec_ref[0:1], vec_ref[1:2], vec_ref[2:3]
        h, nrm, r = _norm_fwd(x_ref[...], g, sc, sh)
        h_ref[...] = h.astype(BF16)
        dxi_ref[...] = dxo_m + _norm_bwd(dh, nrm, r, g, sc)
        sums_ref[1:2, :] += _rowsum(dh)
        sums_ref[2:3, :] += _rowsum(dh * nrm)

        @pl.when(i == n - 1)
        def _():
            p = sums_ref[2:3, :]
            sums_ref[3:4, :] = p * (1.0 + sc)
            sums_ref[2:3, :] = p * g

    wsc = pltpu.VMEM((FF, D), BF16)
    big, small = jax.ShapeDtypeStruct((t, FF), BF16), jax.ShapeDtypeStruct((t, D), BF16)
    return _host_call(
        body, comm, name=f"ffn_bwd_l{layer}", grid=(n,),
        in_specs=[_tile(tm, D), _tile(tm, D), _tile(tm, D), _tile(tm, FF), _tile(tm, FF), _full((8, D)), ANY, ANY],
        out_specs=[_tile(tm, D), _tile(tm, D), _tile(tm, D), _tile(tm, FF), _tile(tm, FF), _tile(tm, FF),
                   _full((8, D))],
        out_shape=[jax.ShapeDtypeStruct((t, D), F32), small, small, big, big, big, jax.ShapeDtypeStruct((8, D), F32)],
        scratch_shapes=[wsc, wsc, wsc, pltpu.SemaphoreType.DMA((3 * NDEV,))],
        operands=(dxo, x, y, a, b, vec, g_a, g_b))


def _wgrad(lhs, rhs, name, tk):
    t, m = lhs.shape
    n = t // tk

    def body(l_ref, r_ref, o_ref, acc):
        i = pl.program_id(0)

        @pl.when(i == 0)
        def _():
            acc[...] = jnp.zeros_like(acc)

        acc[...] += _tn(l_ref[...], r_ref[...])

        @pl.when(i == n - 1)
        def _():
            o_ref[...] = acc[...].astype(BF16)

    return pl.pallas_call(
        body, name=name, grid=(n,),
        in_specs=[_tile(tk, m), _tile(tk, D)], out_specs=_full((m, D)),
        out_shape=jax.ShapeDtypeStruct((m, D), BF16),
        scratch_shapes=[pltpu.VMEM((m, D), F32)],
        compiler_params=_cparams("arbitrary"),
    )(lhs, rhs)


ADAM_ROWS = LROWS // 5


def _adam_big(recv, w, m, v):
    def body(r_ref, w_ref, m_ref, v_ref, g_ref, d_ref, mo_ref, vo_ref):
        g = r_ref[0, 0].astype(F32)
        for s in range(1, NDEV):
            g = g + r_ref[s, 0].astype(F32)
        delta, m2, v2 = _adam(w_ref[0], g, m_ref[0], v_ref[0])
        g_ref[0], d_ref[0], mo_ref[0], vo_ref[0] = g, delta, m2, v2

    blk = pl.BlockSpec((1, ADAM_ROWS, D), lambda l, j: (l, j, 0))
    sds = jax.ShapeDtypeStruct(w.shape, F32)
    return pl.pallas_call(
        body, name="adam_big", grid=(DEPTH, LROWS // ADAM_ROWS),
        in_specs=[pl.BlockSpec((NDEV, 1, ADAM_ROWS, D), lambda l, j: (0, l, j, 0)), blk, blk, blk],
        out_specs=[blk] * 4, out_shape=[sds] * 4,
        compiler_params=_cparams("arbitrary", "arbitrary"),
    )(recv, w, m, v)


def _sum_small(gathered, rows):
    def body(g_ref, o_ref):
        acc = g_ref[0:rows, :]
        for s in range(1, NDEV):
            acc = acc + g_ref[s * rows:(s + 1) * rows, :]
        o_ref[...] = acc

    return pl.pallas_call(
        body, name="sum_small",
        in_specs=[pl.BlockSpec(memory_space=pltpu.VMEM)], out_specs=pl.BlockSpec(memory_space=pltpu.VMEM),
        out_shape=jax.ShapeDtypeStruct((rows, D), F32),
        compiler_params=pltpu.CompilerParams(vmem_limit_bytes=VMEM_LIMIT),
    )(gathered)


def _adam_small(params):
    k = len(params)

    def body(*refs):
        ins, outs = refs[:4 * k], refs[4 * k:]
        for j in range(k):
            w_ref, g_ref, m_ref, v_ref = ins[4 * j:4 * j + 4]
            delta, m2, v2 = _adam(w_ref[...], g_ref[...], m_ref[...], v_ref[...])
            outs[3 * j][...], outs[3 * j + 1][...], outs[3 * j + 2][...] = delta, m2, v2

    flat = [a for p in params for a in p]
    shapes = [jax.ShapeDtypeStruct(p[0].shape, F32) for p in params for _ in range(3)]
    vm = pl.BlockSpec(memory_space=pltpu.VMEM)
    res = pl.pallas_call(
        body, name="adam_small", in_specs=[vm] * len(flat), out_specs=[vm] * len(shapes), out_shape=shapes,
        compiler_params=pltpu.CompilerParams(vmem_limit_bytes=VMEM_LIMIT),
    )(*flat)
    return [tuple(res[3 * j:3 * j + 3]) for j in range(k)]


def _pack(ab_in, ab_out, pw1, pw2, wg, wu, wd):
    ins = jnp.swapaxes(jnp.stack([ab_in[0], pw1[0], ab_in[1], pw1[1]]), 1, 2)
    outs = jnp.stack([ab_out[0], pw2[0], ab_out[1], pw2[1]])
    return jnp.concatenate([ins, outs, jnp.swapaxes(wg, 1, 2), jnp.swapaxes(wu, 1, 2), wd], axis=1)


def _unpack(p):
    ins = jnp.swapaxes(p[:, OFF_IN:OFF_OUT], 1, 2)
    outs = p[:, OFF_OUT:OFF_G]
    return (ins[0::2], outs[0::2], ins[1::2], outs[1::2], jnp.swapaxes(p[:, OFF_G:OFF_U], 1, 2),
            jnp.swapaxes(p[:, OFF_U:OFF_D], 1, 2), p[:, OFF_D:LROWS])


def _unshard(flat, lead, per):
    k = len(lead)
    a = flat.reshape((NDEV,) + tuple(lead) + (per,))
    a = jnp.transpose(a, tuple(range(1, k + 1)) + (0, k + 1))
    return a.reshape(tuple(lead) + (NDEV * per,))


def _rows_of(a):
    f = a.reshape(-1)
    pad = (-f.shape[0]) % D
    if pad:
        f = jnp.concatenate([f, jnp.zeros((pad,), f.dtype)])
    return f.reshape(-1, D)


def _pad_rows(a, rows):
    return jnp.concatenate([a, jnp.zeros((rows - a.shape[0],) + a.shape[1:], a.dtype)], axis=0)


def kernel(x, c, norm_mix_g, norm_ffn_g, w_mod, b_mod, ab_w_in, ab_conv, ab_w_pool, ab_pool_scale, ab_w_out, cf_w_pw1, cf_b_pw1, cf_w_dw, cf_b_dw, cf_ln_g, cf_ln_b, cf_w_pw2, cf_b_pw2, ffn_w_gate, ffn_w_up, ffn_w_down, final_norm_g, loss_target, m_norm_mix_g, m_norm_ffn_g, m_w_mod, m_b_mod, m_ab_w_in, m_ab_conv, m_ab_w_pool, m_ab_pool_scale, m_ab_w_out, m_cf_w_pw1, m_cf_b_pw1, m_cf_w_dw, m_cf_b_dw, m_cf_ln_g, m_cf_ln_b, m_cf_w_pw2, m_cf_b_pw2, m_ffn_w_gate, m_ffn_w_up, m_ffn_w_down, m_final_norm_g, v_norm_mix_g, v_norm_ffn_g, v_w_mod, v_b_mod, v_ab_w_in, v_ab_conv, v_ab_w_pool, v_ab_pool_scale, v_ab_w_out, v_cf_w_pw1, v_cf_b_pw1, v_cf_w_dw, v_cf_b_dw, v_cf_ln_g, v_cf_ln_b, v_cf_w_pw2, v_cf_b_pw2, v_ffn_w_gate, v_ffn_w_up, v_ffn_w_down, v_final_norm_g):
    t = x.shape[1]
    tm = 512 if t % 512 == 0 else t // 2
    tk = 1024 if t % 1024 == 0 else t // 2
    tmo = tm
    tmb = tm // 2
    me = 4 * lax.axis_index("x") + 2 * lax.axis_index("y") + lax.axis_index("c")
    xs, tgt = x[0], loss_target[0]

    w_pack = _pack(ab_w_in, ab_w_out, cf_w_pw1, cf_w_pw2, ffn_w_gate, ffn_w_up, ffn_w_down)
    p16 = w_pack.astype(BF16)

    sharded = [ab_conv, cf_b_pw1, cf_w_dw, cf_b_dw, cf_ln_g, cf_ln_b, cf_b_pw2]
    flat = jnp.concatenate([a.reshape(-1) for a in sharded])
    n_flat = flat.shape[0]
    g1, g_io0 = _head_comm(jnp.concatenate([_pad_rows(c, 8), _pad_rows(_rows_of(flat), 16)], axis=0),
                           p16[0, OFF_IN:OFF_G])
    c_all = g1[:, 0, :]
    flat_all = g1[:, 8:, :].reshape(NDEV, -1)[:, :n_flat]
    full, o = [], 0
    for a in sharded:
        lead, per = a.shape[:-1], a.shape[-1]
        size = a.size
        full.append(_unshard(flat_all[:, o:o + size], lead, per))
        o += size
    ab_conv_f, b_pw1_f, w_dw_f, b_dw_f, ln_g_f, ln_b_f, b_pw2_f = full

    b_sl = lax.dynamic_slice_in_dim(b_mod, me * MODW, MODW, axis=1).reshape(DEPTH, 1, MODW)
    mod_part, c_act = _mod_fwd(c_all, w_mod, b_sl)
    g2 = _gather_small(mod_part, "gather_mod").reshape(NDEV, NDEV, DEPTH, MODW)
    mod = jnp.transpose(lax.dynamic_index_in_dim(g2, me, axis=1, keepdims=False), (1, 0, 2)).reshape(DEPTH, N_MOD, D)
    zeros4 = jnp.zeros((4, D), F32)

    def vec_of(g, layer, k):
        return jnp.concatenate([g[layer][None], mod[layer, k + 1][None], mod[layer, k][None],
                                mod[layer, k + 2][None], zeros4], axis=0)

    vmix = [vec_of(norm_mix_g, l, 0) for l in range(DEPTH)]
    vffn = [vec_of(norm_ffn_g, l, 3) for l in range(DEPTH)]

    wp16 = ab_w_pool.astype(BF16)
    wdw32 = [_pad_rows(w_dw_f[i], 32) for i in range(2)]
    sm_odd = [jnp.concatenate([b_dw_f[i][None], ln_g_f[i][None], ln_b_f[i][None], b_pw2_f[i][None], zeros4], axis=0)
              for i in range(2)]
    zero_bias = jnp.zeros((1, 2 * D), F32)

    saved = []
    xc = xs
    gw = [g_io0]
    wsrc = []
    for l in range(DEPTH):
        i = l // 2
        if l == 0:
            u, g_g0 = _inproj(xc, vmix[l], zero_bias, gw[l], l, tm, _gather_comm(p16[0, OFF_G:OFF_U]))
            x_mid, y_mix, g_ud0 = _mix_even_fwd(u, xc, vmix[l], ab_conv_f[i], wp16[i], ab_pool_scale[i][None], gw[l],
                                                l, tm, _gather_comm(p16[0, OFF_U:LROWS]))
            wsrc.append((g_g0, 0, g_ud0, 0, FS))
            z2 = None
        elif l % 2 == 0:
            u = _inproj(xc, vmix[l], zero_bias, gw[l], l, tm)
            x_mid, y_mix = _mix_even_fwd(u, xc, vmix[l], ab_conv_f[i], wp16[i], ab_pool_scale[i][None], gw[l], l, tm)
            z2 = None
        else:
            u = _inproj(xc, vmix[l], b_pw1_f[i][None], gw[l], l, tm)
            x_mid, y_mix, z2 = _mix_odd_fwd(u, xc, vmix[l], wdw32[i], sm_odd[i], gw[l], l, tmo)
        if l + 1 < DEPTH:
            x_out, a, b, y_ffn, g_next = _ffn_fwd(x_mid, vffn[l], wsrc[l], l, tm, _gather_comm(p16[l + 1]))
            gw.append(g_next)
            wsrc.append((g_next, OFF_G, g_next, OFF_U, OFF_D))
        else:
            x_out = None
            dx, a, b, y_ffn, fsum = _ffn_fwd(x_mid, vffn[l], wsrc[l], l, tmb, head=(tgt, final_norm_g[None]))
        saved.append((xc, u, y_mix, z2, x_mid, a, b, y_ffn))
        xc = x_out

    loss = lax.psum(jnp.sum(fsum[1]), ("x", "y", "c"))
    d_final_g = fsum[0]

    recv = _empty_recv()
    late_specs = [(OFF_U, FS), (OFF_IN, ROWS_IN), (OFF_OUT, ROWS_OUT)]
    pending = None
    dmod = [None] * DEPTH
    d_mix_g, d_ffn_g = [None] * DEPTH, [None] * DEPTH
    d_conv, d_pool, d_pscale = [None] * 2, [None] * 2, [None] * 2
    d_bpw1, d_wdw, d_bdw, d_lng, d_lnb, d_bpw2 = ([None] * 2 for _ in range(6))
    for l in reversed(range(DEPTH)):
        i = l // 2
        x_in, u, y_mix, z2, x_mid, a, b, y_ffn = saved[l]
        if pending is None:
            dx_mid, h2, dy, s, da, db, s_f = _ffn_bwd(dx, x_mid, y_ffn, a, b, vffn[l], wsrc[l], l, tmb)
        else:
            dx_mid, h2, dy, s, da, db, s_f, recv = _ffn_bwd(dx, x_mid, y_ffn, a, b, vffn[l], wsrc[l], l, tmb,
                                                            _scatter_comm(pending, late_specs, recv, l + 1))
        g_down = _wgrad(s, dy, f"wgrad_down_l{l}", tk)
        g_gate = _wgrad(da, h2, f"wgrad_gate_l{l}", tk)
        gu_comm = _scatter_comm([g_down, g_gate], [(OFF_D, FS), (OFF_G, FS)], recv, l)
        g_up = _wgrad(db, h2, f"wgrad_up_l{l}", tk)
        d_ffn_g[l] = s_f[3]
        mod_ffn = [s_f[1], s_f[2], s_f[0]]
        if l % 2 == 0:
            dx, du, h, cat, dym, s_m, dwp, recv = _mix_even_bwd(dx_mid, u, x_in, y_mix, vmix[l], ab_conv_f[i], wp16[i],
                                                                ab_pool_scale[i][None], gw[l], l, tm, gu_comm)
            g_out = _wgrad(cat, dym, f"wgrad_out_l{l}", tk)
            d_conv[i], d_pool[i], d_pscale[i] = s_m[4:7, :DA], dwp, s_m[7, :DA]
            mod_mix = [s_m[1], s_m[2], s_m[0]]
            d_mix_g[l] = s_m[3]
        else:
            dym, z3, dz2, s_1 = _mix_odd_bwd1(dx_mid, y_mix, z2, vmix[l], sm_odd[i], gw[l], l, tm)
            dx, du, h, s_2, dwdw, recv = _mix_odd_bwd2(dz2, u, x_in, dx_mid, vmix[l], wdw32[i], gw[l], l, tmo, gu_comm)
            g_out = _wgrad(z3, dym, f"wgrad_out_l{l}", tk)
            d_bpw1[i], d_wdw[i], d_bdw[i] = s_2[8:10].reshape(2 * D), dwdw[:CONF_K], s_1[7]
            d_lng[i], d_lnb[i], d_bpw2[i] = s_1[5], s_1[6], s_1[4]
            mod_mix = [s_2[1], s_2[2], s_1[0]]
            d_mix_g[l] = s_2[3]
        dmod[l] = jnp.stack(mod_mix + mod_ffn)
        pending = [g_up, _wgrad(du, h, f"wgrad_in_l{l}", tk), g_out]
    grad_x = dx[None]

    small = [jnp.stack(dmod), jnp.stack(d_mix_g), jnp.stack(d_ffn_g), d_final_g, jnp.stack(d_pool),
             jnp.stack(d_pscale), jnp.stack(d_conv), jnp.stack(d_bpw1), jnp.stack(d_wdw), jnp.stack(d_bdw),
             jnp.stack(d_lng), jnp.stack(d_lnb), jnp.stack(d_bpw2)]
    small_rows = [_rows_of(a) for a in small]
    n_rows = sum(a.shape[0] for a in small_rows)
    pad_rows = -(-n_rows // 8) * 8
    recv, g3 = _tail_comm(pending, late_specs, recv, 0, _pad_rows(jnp.concatenate(small_rows, axis=0), pad_rows))
    g3 = g3.reshape(NDEV * pad_rows, D)
    summed = _sum_small(g3, pad_rows)

    m_pack = _pack(m_ab_w_in, m_ab_w_out, m_cf_w_pw1, m_cf_w_pw2, m_ffn_w_gate, m_ffn_w_up, m_ffn_w_down)
    v_pack = _pack(v_ab_w_in, v_ab_w_out, v_cf_w_pw1, v_cf_w_pw2, v_ffn_w_gate, v_ffn_w_up, v_ffn_w_down)
    big = [_unpack(p) for p in _adam_big(recv, w_pack, m_pack, v_pack)]

    outs, o = [], 0
    for a, r in zip(small, small_rows):
        outs.append(summed[o:o + r.shape[0]].reshape(-1)[:a.size].reshape(a.shape))
        o += r.shape[0]
    (g_bmod, g_mix_g, g_ffn_g, g_final, g_pool, g_pscale, g_conv, g_bpw1, g_wdw, g_bdw, g_lng, g_lnb, g_bpw2) = outs
    g_bmod = g_bmod.reshape(DEPTH, N_MOD * D)

    def my_shard(a):
        per = a.shape[-1] // NDEV
        return lax.dynamic_slice_in_dim(a, me * per, per, axis=a.ndim - 1)

    g_conv, g_bpw1, g_wdw, g_bdw, g_lng, g_lnb, g_bpw2 = [
        my_shard(a) for a in (g_conv, g_bpw1, g_wdw, g_bdw, g_lng, g_lnb, g_bpw2)]

    dmod_all = g3.reshape(NDEV, pad_rows, D)[:, :DEPTH * N_MOD, :].reshape(NDEV, DEPTH, N_MOD * D)
    dmod_mine = jnp.transpose(lax.dynamic_slice_in_dim(dmod_all, me * MODW, MODW, axis=2), (1, 0, 2))
    g_wmod, d_wmod, nm_wmod, nv_wmod = _mod_bwd_adam(c_act.T, dmod_mine, w_mod, m_w_mod, v_w_mod)

    small_params = [
        (norm_mix_g, g_mix_g, m_norm_mix_g, v_norm_mix_g), (norm_ffn_g, g_ffn_g, m_norm_ffn_g, v_norm_ffn_g),
        (b_mod, g_bmod, m_b_mod, v_b_mod), (ab_conv, g_conv, m_ab_conv, v_ab_conv),
        (ab_w_pool, g_pool, m_ab_w_pool, v_ab_w_pool), (ab_pool_scale, g_pscale, m_ab_pool_scale, v_ab_pool_scale),
        (cf_b_pw1, g_bpw1, m_cf_b_pw1, v_cf_b_pw1), (cf_w_dw, g_wdw, m_cf_w_dw, v_cf_w_dw),
        (cf_b_dw, g_bdw, m_cf_b_dw, v_cf_b_dw), (cf_ln_g, g_lng, m_cf_ln_g, v_cf_ln_g),
        (cf_ln_b, g_lnb, m_cf_ln_b, v_cf_ln_b), (cf_b_pw2, g_bpw2, m_cf_b_pw2, v_cf_b_pw2),
        (final_norm_g, g_final, m_final_norm_g, v_final_norm_g)]

    def two_d(a):
        return a.reshape(-1, a.shape[-1])

    upd = _adam_small([tuple(two_d(a) for a in p) for p in small_params])
    upd = [tuple(r.reshape(p[0].shape) for r in u) for u, p in zip(upd, small_params)]
    (s_mix, s_ffn, s_bmod, s_conv, s_pool, s_pscale, s_bpw1, s_wdw, s_bdw, s_lng, s_lnb, s_bpw2, s_final) = upd
    small_g = [p[1] for p in small_params]
    (q_mix, q_ffn, q_bmod, q_conv, q_pool, q_pscale, q_bpw1, q_wdw, q_bdw, q_lng, q_lnb, q_bpw2, q_final) = small_g

    def ordered(k):
        ab_in, ab_out, pw1, pw2, wg, wu, wd = big[k]
        if k == 0:
            sm = dict(mix=q_mix, ffn=q_ffn, bmod=q_bmod, conv=q_conv, pool=q_pool, pscale=q_pscale, bpw1=q_bpw1,
                      wdw=q_wdw, bdw=q_bdw, lng=q_lng, lnb=q_lnb, bpw2=q_bpw2, final=q_final)
            wmod = g_wmod
        else:
            j = k - 1
            sm = dict(mix=s_mix[j], ffn=s_ffn[j], bmod=s_bmod[j], conv=s_conv[j], pool=s_pool[j], pscale=s_pscale[j],
                      bpw1=s_bpw1[j], wdw=s_wdw[j], bdw=s_bdw[j], lng=s_lng[j], lnb=s_lnb[j], bpw2=s_bpw2[j],
                      final=s_final[j])
            wmod = (d_wmod, nm_wmod, nv_wmod)[j]
        return [sm["mix"], sm["ffn"], wmod, sm["bmod"], ab_in, sm["conv"], sm["pool"], sm["pscale"], ab_out,
                pw1, sm["bpw1"], sm["wdw"], sm["bdw"], sm["lng"], sm["lnb"], pw2, sm["bpw2"], wg, wu, wd, sm["final"]]

    return (loss, grad_x, *ordered(0), *ordered(1), *ordered(2), *ordered(3))
```

```python
import functools

import jax
import jax.numpy as jnp
from jax import lax
from jax.experimental import pallas as pl
from jax.experimental.pallas import tpu as pltpu

F32 = jnp.float32
BF16 = jnp.bfloat16
MESH = pl.DeviceIdType.MESH

NDEV = 8
DEPTH = 4
D = 1024
FF = 2816
FS = FF // NDEV
DA = 512
PG = 128
POOL = ((2, 1, 0), (4, 2, 1), (8, 4, 3), (16, 8, 7))
CONF_K = 31
CONF_L = 15
N_MOD = 6
MODW = N_MOD * D // NDEV
RMS_EPS = 1e-6
LN_EPS = 1e-5

ROWS_IN, ROWS_OUT = 2 * D // NDEV, D // NDEV
OFF_IN, OFF_OUT = 0, ROWS_IN
OFF_G = OFF_OUT + ROWS_OUT
OFF_U = OFF_G + FS
OFF_D = OFF_U + FS
LROWS = OFF_D + FS

MXU_N = 256
HALO = 16
VMEM_LIMIT = 60 * 1024 * 1024

ADAM_LR, ADAM_B1, ADAM_B2, ADAM_EPS, ADAM_WD, ADAM_STEP = 1e-3, 0.9, 0.999, 1e-8, 0.01, 10
ADAM_C1 = 1.0 / (1.0 - ADAM_B1 ** ADAM_STEP)
ADAM_C2 = 1.0 / (1.0 - ADAM_B2 ** ADAM_STEP)


def _nn(a, b):
    return jnp.dot(a, b, preferred_element_type=F32)


def _nt(a, b):
    return lax.dot_general(a, b, (((1,), (1,)), ((), ())), preferred_element_type=F32)


def _tn(a, b):
    return lax.dot_general(a, b, (((0,), (0,)), ((), ())), preferred_element_type=F32)


def _sig(x):
    return 1.0 / (1.0 + jnp.exp(-x))


def _rowsum(x):
    return jnp.sum(x, axis=0, keepdims=True)


def _lanemean(x):
    return jnp.mean(x, axis=-1, keepdims=True)


def _norm_fwd(x, g, sc, sh):
    r = lax.rsqrt(_lanemean(x * x) + RMS_EPS)
    n = x * r
    return n * (g * (1.0 + sc)) + sh, n, r


def _norm_bwd(dh, n, r, g, sc):
    dn = dh * (g * (1.0 + sc))
    return r * (dn - n * _lanemean(dn * n))


def _adam(w, g, m, v):
    m2 = ADAM_B1 * m + (1.0 - ADAM_B1) * g
    v2 = ADAM_B2 * v + (1.0 - ADAM_B2) * (g * g)
    delta = -ADAM_LR * ((m2 * ADAM_C1) / (jnp.sqrt(v2 * ADAM_C2) + ADAM_EPS) + ADAM_WD * w)
    return delta, m2, v2


def _load_weights(g_ref, specs, sems):
    cps = [pltpu.make_async_copy(g_ref.at[:, pl.ds(off, rows), :], dst, sems.at[k])
           for k, (off, rows, dst) in enumerate(specs)]
    for cp in cps:
        cp.start()
    for cp in cps:
        cp.wait()


def _load_weight_rows(specs, sems):
    cps = [pltpu.make_async_copy(g_ref.at[d, pl.ds(off, rows), :], dst.at[pl.ds(d * rows, rows), :],
                                 sems.at[NDEV * k + d])
           for k, (g_ref, off, rows, dst) in enumerate(specs) for d in range(NDEV)]
    for cp in cps:
        cp.start()
    for cp in cps:
        cp.wait()


def _cparams(*sem):
    return pltpu.CompilerParams(dimension_semantics=sem if sem else None, vmem_limit_bytes=VMEM_LIMIT)


def _tile(tm, w):
    return pl.BlockSpec((tm, w), lambda i: (i, 0))


def _full(shape):
    nd = len(shape)
    return pl.BlockSpec(shape, lambda i: (0,) * nd)


def _halo_specs(tm, w, total_rows):
    tb = tm // HALO
    nb = total_rows // HALO
    prev = pl.BlockSpec((HALO, w), lambda i: (jnp.maximum(i * tb - 1, 0), 0))
    nxt = pl.BlockSpec((HALO, w), lambda i: (jnp.minimum((i + 1) * tb, nb - 1), 0))
    return prev, nxt


ANY = pl.BlockSpec(memory_space=pl.ANY)


def _peers():
    x, y, c = lax.axis_index("x"), lax.axis_index("y"), lax.axis_index("c")
    return x, y, c


def _gather_small(v, name):
    m_per, n = v.shape

    def body(x_ref, out_ref, send_sems, recv_sems, local_sem):
        x, y, c = _peers()
        me, sibling = (x, y, c), (x, y, 1 - c)
        chips = [(1 - x, y), (x, 1 - y), (1 - x, 1 - y)]

        def rows(px, py, pc):
            return out_ref.at[pl.ds((4 * px + 2 * py + pc) * m_per, m_per), :]

        def copy(k, block, to, src=None):
            return pltpu.make_async_remote_copy(
                src_ref=rows(*block) if src is None else src, dst_ref=rows(*block),
                send_sem=send_sems.at[k], recv_sem=recv_sems.at[k], device_id=to, device_id_type=MESH)

        mine = pltpu.make_async_copy(x_ref, rows(*me), local_sem)
        mine.start()
        first = [copy(0, me, sibling, src=x_ref)]
        first += [copy(1 + j, me, (*chip, c), src=x_ref) for j, chip in enumerate(chips)]
        for cp in first:
            cp.start()
        passed = [copy(4 + j, (*chip, c), sibling) for j, chip in enumerate(chips)]
        for j, chip in enumerate(chips):
            copy(1 + j, (*chip, c), me).wait_recv()
            passed[j].start()
        copy(0, sibling, me).wait_recv()
        for j, chip in enumerate(chips):
            copy(4 + j, (*chip, 1 - c), me).wait_recv()
        for cp in first + passed:
            cp.wait_send()
        mine.wait()

    return pl.pallas_call(
        body, name=name,
        out_shape=jax.ShapeDtypeStruct((NDEV * m_per, n), v.dtype),
        in_specs=[pl.BlockSpec(memory_space=pltpu.VMEM)],
        out_specs=pl.BlockSpec(memory_space=pltpu.VMEM),
        scratch_shapes=[pltpu.SemaphoreType.DMA((7,)), pltpu.SemaphoreType.DMA((7,)), pltpu.SemaphoreType.DMA],
        compiler_params=pltpu.CompilerParams(vmem_limit_bytes=VMEM_LIMIT),
    )(v)


class _Comm:
    def __init__(self, ins, outs, aliases, bind):
        self.ins, self.outs, self.aliases, self.bind = ins, outs, aliases, bind


COMM_SEMS = [pltpu.SemaphoreType.DMA((7,)), pltpu.SemaphoreType.DMA((7,)), pltpu.SemaphoreType.DMA]


def _gather_hooks(p_ref, out_ref, send_sems, recv_sems, local_sem):
    def parts():
        x, y, c = _peers()
        me, sibling = (x, y, c), (x, y, 1 - c)
        chips = [(1 - x, y), (x, 1 - y), (1 - x, 1 - y)]

        def slab(px, py, pc):
            return out_ref.at[4 * px + 2 * py + pc]

        def copy(k, block, to, src=None):
            return pltpu.make_async_remote_copy(
                src_ref=slab(*block) if src is None else src, dst_ref=slab(*block),
                send_sem=send_sems.at[k], recv_sem=recv_sems.at[k], device_id=to, device_id_type=MESH)

        def mine():
            return pltpu.make_async_copy(p_ref, slab(*me), local_sem)

        def first():
            return [copy(0, me, sibling, src=p_ref)] + [copy(1 + j, me, (*chip, c), src=p_ref)
                                                        for j, chip in enumerate(chips)]

        def passed():
            return [copy(4 + j, (*chip, c), sibling) for j, chip in enumerate(chips)]

        def from_chips():
            return [copy(1 + j, (*chip, c), me) for j, chip in enumerate(chips)]

        def from_sibling():
            return [copy(0, sibling, me)] + [copy(4 + j, (*chip, 1 - c), me) for j, chip in enumerate(chips)]

        return mine, first, passed, from_chips, from_sibling

    def start():
        mine, first, _, _, _ = parts()
        mine().start()
        for cp in first():
            cp.start()

    def middle():
        _, _, passed, from_chips, _ = parts()
        for arrived, onward in zip(from_chips(), passed()):
            arrived.wait_recv()
            onward.start()

    def end():
        mine, first, passed, _, from_sibling = parts()
        for cp in from_sibling():
            cp.wait_recv()
        for cp in first() + passed():
            cp.wait_send()
        mine().wait()

    return start, middle, end


def _gather_comm(p):
    return _Comm([p], [jax.ShapeDtypeStruct((NDEV,) + p.shape, p.dtype)], {},
                 lambda cins, couts, sems: _gather_hooks(cins[0], couts[0], *sems))


def _head_comm(cond, p):
    def body(c_ref, p_ref, co_ref, po_ref, *sems):
        c_hooks = _gather_hooks(c_ref, co_ref, *sems[:3])
        p_hooks = _gather_hooks(p_ref, po_ref, *sems[3:])
        for step in range(3):
            c_hooks[step]()
            p_hooks[step]()

    return pl.pallas_call(
        body, name="head_comm",
        out_shape=[jax.ShapeDtypeStruct((NDEV,) + cond.shape, cond.dtype),
                   jax.ShapeDtypeStruct((NDEV,) + p.shape, p.dtype)],
        in_specs=[ANY, ANY], out_specs=[ANY, ANY], scratch_shapes=COMM_SEMS + COMM_SEMS,
    )(cond, p)


def _scatter_hooks(src_refs, specs, r_ref, layer, send_sems, recv_sems, local_sem):
    total = sum(rows for _, rows in specs)

    def start():
        x, y, c = _peers()
        me = 4 * x + 2 * y + c

        def part(k, dev):
            off, rows = specs[k]
            src = src_refs[k].at[pl.ds(pl.multiple_of(dev * rows, 16), rows), :]
            return src, r_ref.at[me, layer, pl.ds(off, rows), :]

        for k in range(len(specs)):
            src, dst = part(k, me)
            pltpu.make_async_copy(src, dst, local_sem).start()
        for r in range(1, NDEV):
            px = 1 - x if r & 4 else x
            py = 1 - y if r & 2 else y
            pc = 1 - c if r & 1 else c
            for k in range(len(specs)):
                src, dst = part(k, 4 * px + 2 * py + pc)
                pltpu.make_async_remote_copy(
                    src_ref=src, dst_ref=dst, send_sem=send_sems.at[r - 1], recv_sem=recv_sems.at[r - 1],
                    device_id=(px, py, pc), device_id_type=MESH).start()

    def end():
        x, y, c = _peers()
        whole = r_ref.at[0, layer, pl.ds(0, total), :]
        for r in range(1, NDEV):
            done = pltpu.make_async_remote_copy(
                src_ref=whole, dst_ref=whole, send_sem=send_sems.at[r - 1], recv_sem=recv_sems.at[r - 1],
                device_id=(x, y, c), device_id_type=MESH)
            done.wait_recv()
            done.wait_send()
        pltpu.make_async_copy(whole, whole, local_sem).wait()

    return start, None, end


def _scatter_comm(srcs, specs, recv, layer):
    k = len(srcs)
    return _Comm(list(srcs) + [recv], [jax.ShapeDtypeStruct(recv.shape, recv.dtype)], {k: 0},
                 lambda cins, couts, sems: _scatter_hooks(cins[:k], specs, couts[0], layer, *sems))


def _tail_comm(srcs, specs, recv, layer, small):
    k = len(srcs)

    def body(*refs):
        src_refs, small_ref, r_ref, g_ref, sems = refs[:k], refs[k + 1], refs[k + 2], refs[k + 3], refs[k + 4:]
        s_start, _, s_end = _scatter_hooks(src_refs, specs, r_ref, layer, *sems[:3])
        g_start, g_middle, g_end = _gather_hooks(small_ref, g_ref, *sems[3:])
        s_start()
        g_start()
        g_middle()
        g_end()
        s_end()

    return pl.pallas_call(
        body, name="tail_comm",
        out_shape=[jax.ShapeDtypeStruct(recv.shape, recv.dtype),
                   jax.ShapeDtypeStruct((NDEV,) + small.shape, small.dtype)],
        in_specs=[ANY] * (k + 2), out_specs=[ANY, ANY], scratch_shapes=COMM_SEMS + COMM_SEMS,
        input_output_aliases={k: 0},
    )(*srcs, recv, small)


def _empty_recv():
    def body(o_ref):
        del o_ref

    return pl.pallas_call(body, name="recv_buffer", out_specs=ANY,
                          out_shape=jax.ShapeDtypeStruct((NDEV, DEPTH, LROWS, D), BF16))()


def _host_call(inner, comm, *, name, grid, in_specs, out_specs, out_shape, scratch_shapes, operands):
    if comm is None:
        return pl.pallas_call(
            inner, name=name, grid=grid, in_specs=in_specs, out_specs=out_specs, out_shape=out_shape,
            scratch_shapes=scratch_shapes, compiler_params=_cparams("arbitrary"))(*operands)
    n_in, n_out, n_s = len(in_specs), len(out_specs), len(scratch_shapes)
    k_in, k_out = len(comm.ins), len(comm.outs)
    steps = grid[0]

    def body(*refs):
        ins, cins = refs[:n_in], refs[n_in:n_in + k_in]
        o0 = n_in + k_in
        outs, couts = refs[o0:o0 + n_out], refs[o0 + n_out:o0 + n_out + k_out]
        s0 = o0 + n_out + k_out
        scr, sems = refs[s0:s0 + n_s], refs[s0 + n_s:]
        start, middle, end = comm.bind(cins, couts, sems)
        i = pl.program_id(0)
        pl.when(i == 0)(start)
        if middle is not None:
            pl.when(i == steps * 3 // 4)(middle)
        inner(*ins, *outs, *scr)
        pl.when(i == steps - 1)(end)

    return pl.pallas_call(
        body, name=name, grid=grid, in_specs=list(in_specs) + [ANY] * k_in,
        out_specs=list(out_specs) + [ANY] * k_out, out_shape=list(out_shape) + list(comm.outs),
        scratch_shapes=list(scratch_shapes) + COMM_SEMS,
        input_output_aliases={n_in + a: n_out + b for a, b in comm.aliases.items()},
        compiler_params=_cparams("arbitrary"))(*operands, *comm.ins)


def _mod_fwd(c_all, w_mod, b_sl):
    def body(c_ref, w_ref, b_ref, o_ref, ca_ref):
        cv = c_ref[...]
        ca = cv * _sig(cv)
        ca_ref[...] = ca
        o_ref[...] = jnp.dot(ca, w_ref[0], preferred_element_type=F32, precision=lax.Precision.HIGHEST) + b_ref[0]

    return pl.pallas_call(
        body, name="mod_fwd", grid=(DEPTH,),
        in_specs=[_full((NDEV, D)), pl.BlockSpec((1, D, MODW), lambda l: (l, 0, 0)),
                  pl.BlockSpec((1, 1, MODW), lambda l: (l, 0, 0))],
        out_specs=[pl.BlockSpec((NDEV, MODW), lambda l: (0, l)), _full((NDEV, D))],
        out_shape=[jax.ShapeDtypeStruct((NDEV, DEPTH * MODW), F32), jax.ShapeDtypeStruct((NDEV, D), F32)],
        compiler_params=_cparams("arbitrary"),
    )(c_all, w_mod, b_sl)


def _mod_bwd_adam(ca_t, dmod, w, m, v):
    def body(ct_ref, dm_ref, w_ref, m_ref, v_ref, g_ref, d_ref, mo_ref, vo_ref):
        g = jnp.dot(ct_ref[...], dm_ref[0], preferred_element_type=F32, precision=lax.Precision.HIGHEST)
        delta, m2, v2 = _adam(w_ref[0], g, m_ref[0], v_ref[0])
        g_ref[0], d_ref[0], mo_ref[0], vo_ref[0] = g, delta, m2, v2

    blk = pl.BlockSpec((1, D, MODW), lambda l: (l, 0, 0))
    sds = jax.ShapeDtypeStruct(w.shape, F32)
    return pl.pallas_call(
        body, name="mod_bwd_adam", grid=(DEPTH,),
        in_specs=[_full((D, NDEV)), pl.BlockSpec((1, NDEV, MODW), lambda l: (l, 0, 0)), blk, blk, blk],
        out_specs=[blk] * 4, out_shape=[sds] * 4,
        compiler_params=_cparams("arbitrary"),
    )(ca_t, dmod, w, m, v)


def _inproj(x, vec, bias, gw, layer, tm, comm=None):
    t = x.shape[0]

    def body(x_ref, vec_ref, b_ref, g_ref, u_ref, w_s, sems):
        @pl.when(pl.program_id(0) == 0)
        def _():
            _load_weights(g_ref, [(OFF_IN, ROWS_IN, w_s)], sems)

        h, _, _ = _norm_fwd(x_ref[...], vec_ref[0:1], vec_ref[1:2], vec_ref[2:3])
        w = w_s[...].reshape(2 * D, D)
        u_ref[...] = (_nt(h.astype(BF16), w) + b_ref[...]).astype(BF16)

    res = _host_call(
        body, comm, name=f"inproj_l{layer}", grid=(t // tm,),
        in_specs=[_tile(tm, D), _full((8, D)), _full((1, 2 * D)), ANY],
        out_specs=[_tile(tm, 2 * D)], out_shape=[jax.ShapeDtypeStruct((t, 2 * D), BF16)],
        scratch_shapes=[pltpu.VMEM((NDEV, ROWS_IN, D), BF16), pltpu.SemaphoreType.DMA((1,))],
        operands=(x, vec, bias, gw))
    return res[0] if comm is None else res


def _fill_even(qe, pe, be, part_ref, lo, rows, valid):
    cg = part_ref[:, DA:2 * DA].astype(F32)
    v = part_ref[:, 2 * DA:3 * DA].astype(F32)
    q = cg * v
    p = part_ref[:, 3 * DA:4 * DA].astype(F32)
    if valid is not None:
        q = jnp.where(valid, q, 0.0)
        p = jnp.where(valid, p, 0.0)
    qe[lo:lo + rows, :] = q
    pe[lo:lo + rows, :] = p
    if be is not None:
        b = part_ref[:, 0:DA].astype(F32)
        be[lo:lo + rows, :] = b if valid is None else jnp.where(valid, b, 0.0)


def _conv3(ca_ref, qe, tm):
    return (ca_ref[0:1] * qe[HALO - 1:HALO - 1 + tm] + ca_ref[1:2] * qe[HALO:HALO + tm]
            + ca_ref[2:3] * qe[HALO + 1:HALO + 1 + tm])


def _pool_counts(t0, rows, first_row, left, right, t):
    tg = t0 + first_row + lax.broadcasted_iota(jnp.int32, (rows, 1), 0)
    cnt = jnp.minimum(tg + right, t - 1) - jnp.maximum(tg - left, 0) + 1
    return jnp.maximum(cnt, 1).astype(F32)


def _pool_minus_id(pe, gi, left, right, inv_cnt, tm):
    c0 = gi * PG
    s = pe[HALO - left:HALO - left + tm, c0:c0 + PG]
    for j in range(-left + 1, right + 1):
        s = s + pe[HALO + j:HALO + j + tm, c0:c0 + PG]
    return s * inv_cnt - pe[HALO:HALO + tm, c0:c0 + PG]


def _mix_even_fwd(u, x, vec, ca, wp, ps, gw, layer, tm, comm=None):
    t = x.shape[0]
    n = t // tm
    e = tm + 2 * HALO

    def body(u_ref, up_ref, un_ref, x_ref, vec_ref, ca_ref, wp_ref, ps_ref, g_ref, xo_ref, y_ref, w_s, qe, pe, sems):
        i = pl.program_id(0)

        @pl.when(i == 0)
        def _():
            _load_weights(g_ref, [(OFF_OUT, ROWS_OUT, w_s)], sems)

        _fill_even(qe, pe, None, up_ref, 0, HALO, i > 0)
        _fill_even(qe, pe, None, u_ref, HALO, tm, None)
        _fill_even(qe, pe, None, un_ref, HALO + tm, HALO, i < n - 1)
        ya = u_ref[:, 0:DA].astype(F32) * _conv3(ca_ref, qe, tm)
        parts = [ya]
        for gi, (_, left, right) in enumerate(POOL):
            inv = 1.0 / _pool_counts(i * tm, tm, 0, left, right, t)
            pm = _pool_minus_id(pe, gi, left, right, inv, tm)
            parts.append(_nn(pm.astype(BF16), wp_ref[gi]) * ps_ref[0:1, gi * PG:(gi + 1) * PG])
        cat = jnp.concatenate(parts, axis=-1).astype(BF16)
        y = _nn(cat, w_s[...].reshape(D, D))
        y_ref[...] = y.astype(BF16)
        xo_ref[...] = x_ref[...] + vec_ref[3:4] * y

    prev, nxt = _halo_specs(tm, 2 * D, t)
    return _host_call(
        body, comm, name=f"mix_even_fwd_l{layer}", grid=(n,),
        in_specs=[_tile(tm, 2 * D), prev, nxt, _tile(tm, D), _full((8, D)), _full((3, DA)),
                  _full((4, PG, PG)), _full((1, DA)), ANY],
        out_specs=[_tile(tm, D), _tile(tm, D)],
        out_shape=[jax.ShapeDtypeStruct((t, D), F32), jax.ShapeDtypeStruct((t, D), BF16)],
        scratch_shapes=[pltpu.VMEM((NDEV, ROWS_OUT, D), BF16), pltpu.VMEM((e, DA), F32), pltpu.VMEM((e, DA), F32),
                        pltpu.SemaphoreType.DMA((1,))],
        operands=(u, u, u, x, vec, ca, wp, ps, gw))


def _mix_even_bwd(dxo, u, x, y, vec, ca, wp, ps, gw, layer, tm, comm=None):
    t = x.shape[0]
    n = t // tm
    e = tm + 2 * HALO

    def body(dxo_ref, dp_ref, dn_ref, u_ref, up_ref, un_ref, x_ref, y_ref, vec_ref, ca_ref, wp_ref, ps_ref, g_ref,
             dxi_ref, du_ref, h_ref, cat_ref, dy_ref, sums_ref, dwp_ref,
             wo_s, wi_s, dye, qe, pe, be, dce, epe, sems):
        i = pl.program_id(0)

        @pl.when(i == 0)
        def _():
            _load_weights(g_ref, [(OFF_OUT, ROWS_OUT, wo_s), (OFF_IN, ROWS_IN, wi_s)], sems)
            sums_ref[...] = jnp.zeros_like(sums_ref)
            dwp_ref[...] = jnp.zeros_like(dwp_ref)

        gate = vec_ref[3:4]
        dxo_m = dxo_ref[...]
        dye[0:HALO, :] = jnp.where(i > 0, gate * dp_ref[...], 0.0).astype(BF16)
        dye[HALO:HALO + tm, :] = (gate * dxo_m).astype(BF16)
        dye[HALO + tm:e, :] = jnp.where(i < n - 1, gate * dn_ref[...], 0.0).astype(BF16)
        _fill_even(qe, pe, be, up_ref, 0, HALO, i > 0)
        _fill_even(qe, pe, be, u_ref, HALO, tm, None)
        _fill_even(qe, pe, be, un_ref, HALO + tm, HALO, i < n - 1)
        sums_ref[0:1, :] += _rowsum(dxo_m * y_ref[...].astype(F32))

        dcat = _nt(dye[...], wo_s[...].reshape(D, D))
        dce[...] = dcat[:, 0:DA] * be[...]
        cq = _conv3(ca_ref, qe, tm)
        bg = be[HALO:HALO + tm]
        dc_m = dce[HALO:HALO + tm]
        dbg = dcat[HALO:HALO + tm, 0:DA] * cq
        dq = (ca_ref[0:1] * dce[HALO + 1:HALO + 1 + tm] + ca_ref[1:2] * dc_m
              + ca_ref[2:3] * dce[HALO - 1:HALO - 1 + tm])
        cg = u_ref[:, DA:2 * DA].astype(F32)
        v = u_ref[:, 2 * DA:3 * DA].astype(F32)
        for k in range(3):
            sums_ref[4 + k:5 + k, 0:DA] += _rowsum(dc_m * qe[HALO - 1 + k:HALO - 1 + k + tm])
        du_parts = [dbg, dq * v, dq * cg]
        cat_parts = [bg * cq]
        for gi, (_, left, right) in enumerate(POOL):
            c0 = gi * PG
            scale = ps_ref[0:1, c0:c0 + PG]
            dyb = dcat[:, DA + c0:DA + c0 + PG]
            dybs = (dyb * scale).astype(BF16)
            dpm = _nt(dybs, wp_ref[gi])
            inv_e = 1.0 / _pool_counts(i * tm, e, -HALO, left, right, t)
            epe[:, c0:c0 + PG] = dpm * inv_e
            s_adj = epe[HALO - right:HALO - right + tm, c0:c0 + PG]
            for j in range(-right + 1, left + 1):
                s_adj = s_adj + epe[HALO + j:HALO + j + tm, c0:c0 + PG]
            du_parts.append(s_adj - dpm[HALO:HALO + tm])
            inv_m = 1.0 / _pool_counts(i * tm, tm, 0, left, right, t)
            pm = _pool_minus_id(pe, gi, left, right, inv_m, tm).astype(BF16)
            ybpre = _nn(pm, wp_ref[gi])
            sums_ref[7:8, c0:c0 + PG] += _rowsum(dyb[HALO:HALO + tm] * ybpre)
            dwp_ref[gi] += _tn(pm, dybs[HALO:HALO + tm])
            cat_parts.append(ybpre * scale)
        du = jnp.concatenate(du_parts, axis=-1).astype(BF16)
        du_ref[...] = du
        cat_ref[...] = jnp.concatenate(cat_parts, axis=-1).astype(BF16)
        dy_ref[...] = dye[HALO:HALO + tm, :]
        dh = _nn(du, wi_s[...].reshape(2 * D, D))
        g, sc, sh = vec_ref[0:1], vec_ref[1:2], vec_ref[2:3]
        h, nrm, r = _norm_fwd(x_ref[...], g, sc, sh)
        h_ref[...] = h.astype(BF16)
        dxi_ref[...] = dxo_m + _norm_bwd(dh, nrm, r, g, sc)
        sums_ref[1:2, :] += _rowsum(dh)
        sums_ref[2:3, :] += _rowsum(dh * nrm)

        @pl.when(i == n - 1)
        def _():
            p = sums_ref[2:3, :]
            sums_ref[3:4, :] = p * (1.0 + sc)
            sums_ref[2:3, :] = p * g

    prev_u, nxt_u = _halo_specs(tm, 2 * D, t)
    prev_d, nxt_d = _halo_specs(tm, D, t)
    return _host_call(
        body, comm, name=f"mix_even_bwd_l{layer}", grid=(n,),
        in_specs=[_tile(tm, D), prev_d, nxt_d, _tile(tm, 2 * D), prev_u, nxt_u, _tile(tm, D), _tile(tm, D),
                  _full((8, D)), _full((3, DA)), _full((4, PG, PG)), _full((1, DA)), ANY],
        out_specs=[_tile(tm, D), _tile(tm, 2 * D), _tile(tm, D), _tile(tm, D), _tile(tm, D),
                   _full((16, D)), _full((4, PG, PG))],
        out_shape=[jax.ShapeDtypeStruct((t, D), F32), jax.ShapeDtypeStruct((t, 2 * D), BF16),
                   jax.ShapeDtypeStruct((t, D), BF16), jax.ShapeDtypeStruct((t, D), BF16),
                   jax.ShapeDtypeStruct((t, D), BF16), jax.ShapeDtypeStruct((16, D), F32),
                   jax.ShapeDtypeStruct((4, PG, PG), F32)],
        scratch_shapes=[pltpu.VMEM((NDEV, ROWS_OUT, D), BF16), pltpu.VMEM((NDEV, ROWS_IN, D), BF16),
                        pltpu.VMEM((e, D), BF16), pltpu.VMEM((e, DA), F32), pltpu.VMEM((e, DA), F32),
                        pltpu.VMEM((e, DA), F32), pltpu.VMEM((e, DA), F32), pltpu.VMEM((e, DA), F32),
                        pltpu.SemaphoreType.DMA((2,))],
        operands=(dxo, dxo, dxo, u, u, u, x, y, vec, ca, wp, ps, gw))


def _fill_glu(ze, part_ref, lo, rows, valid):
    a = part_ref[:, 0:D].astype(F32)
    g = part_ref[:, D:2 * D].astype(F32)
    z = a * _sig(g)
    ze[lo:lo + rows, :] = z if valid is None else jnp.where(valid, z, 0.0)


SHIFT_ROWS = 24


def _shifted_copies(dst, src, tm):
    rows = tm + SHIFT_ROWS
    for j in range(8):
        dst[j, :, :] = src[j:j + rows, :]


def _shifted(dst, shift, tm):
    lo = shift // 8 * 8
    return dst[shift % 8, lo:lo + tm, :]


def _layer_norm_parts(z2):
    mu = _lanemean(z2)
    d = z2 - mu
    rstd = lax.rsqrt(_lanemean(d * d) + LN_EPS)
    return d * rstd, rstd


def _mix_odd_fwd(u, x, vec, wdw, sm, gw, layer, tm):
    t = x.shape[0]
    n = t // tm
    e = tm + 2 * HALO

    def body(u_ref, up_ref, un_ref, x_ref, vec_ref, wdw_ref, sm_ref, g_ref, xo_ref, y_ref, z2_ref, w_s, ze, zsh, sems):
        i = pl.program_id(0)

        @pl.when(i == 0)
        def _():
            _load_weights(g_ref, [(OFF_OUT, ROWS_OUT, w_s)], sems)

        _fill_glu(ze, up_ref, 0, HALO, i > 0)
        _fill_glu(ze, u_ref, HALO, tm, None)
        _fill_glu(ze, un_ref, HALO + tm, HALO, i < n - 1)
        _shifted_copies(zsh, ze, tm)
        z2 = sm_ref[0:1] + wdw_ref[0:1] * _shifted(zsh, 1, tm)
        for k in range(1, CONF_K):
            z2 = z2 + wdw_ref[k:k + 1] * _shifted(zsh, 1 + k, tm)
        z2_ref[...] = z2.astype(BF16)
        zn, _ = _layer_norm_parts(z2)
        lo = zn * sm_ref[1:2] + sm_ref[2:3]
        z3 = lo * _sig(lo)
        y = _nn(z3.astype(BF16), w_s[...].reshape(D, D)) + sm_ref[3:4]
        y_ref[...] = y.astype(BF16)
        xo_ref[...] = x_ref[...] + vec_ref[3:4] * y

    prev, nxt = _halo_specs(tm, 2 * D, t)
    return pl.pallas_call(
        body, name=f"mix_odd_fwd_l{layer}", grid=(n,),
        in_specs=[_tile(tm, 2 * D), prev, nxt, _tile(tm, D), _full((8, D)), _full((32, D)), _full((8, D)), ANY],
        out_specs=[_tile(tm, D), _tile(tm, D), _tile(tm, D)],
        out_shape=[jax.ShapeDtypeStruct((t, D), F32), jax.ShapeDtypeStruct((t, D), BF16),
                   jax.ShapeDtypeStruct((t, D), BF16)],
        scratch_shapes=[pltpu.VMEM((NDEV, ROWS_OUT, D), BF16), pltpu.VMEM((e, D), F32),
                        pltpu.VMEM((8, tm + SHIFT_ROWS, D), F32), pltpu.SemaphoreType.DMA((1,))],
        compiler_params=_cparams("arbitrary"),
    )(u, u, u, x, vec, wdw, sm, gw)


def _mix_odd_bwd1(dxo, y, z2, vec, sm, gw, layer, tm):
    t = dxo.shape[0]
    n = t // tm

    def body(dxo_ref, y_ref, z2_ref, vec_ref, sm_ref, g_ref, dy_ref, z3_ref, dz2_ref, sums_ref, w_s, sems):
        i = pl.program_id(0)

        @pl.when(i == 0)
        def _():
            _load_weights(g_ref, [(OFF_OUT, ROWS_OUT, w_s)], sems)
            sums_ref[...] = jnp.zeros_like(sums_ref)

        dxo_m = dxo_ref[...]
        dy = vec_ref[3:4] * dxo_m
        dyb = dy.astype(BF16)
        dy_ref[...] = dyb
        sums_ref[0:1, :] += _rowsum(dxo_m * y_ref[...].astype(F32))
        sums_ref[4:5, :] += _rowsum(dy)
        dz3 = _nt(dyb, w_s[...].reshape(D, D))
        zn, rstd = _layer_norm_parts(z2_ref[...].astype(F32))
        lo = zn * sm_ref[1:2] + sm_ref[2:3]
        sg = _sig(lo)
        z3_ref[...] = (lo * sg).astype(BF16)
        dlo = dz3 * (sg * (1.0 + lo * (1.0 - sg)))
        sums_ref[5:6, :] += _rowsum(dlo * zn)
        sums_ref[6:7, :] += _rowsum(dlo)
        dzn = dlo * sm_ref[1:2]
        dz2 = rstd * (dzn - _lanemean(dzn) - zn * _lanemean(dzn * zn))
        sums_ref[7:8, :] += _rowsum(dz2)
        dz2_ref[...] = dz2.astype(BF16)

    return pl.pallas_call(
        body, name=f"mix_odd_bwd1_l{layer}", grid=(n,),
        in_specs=[_tile(tm, D), _tile(tm, D), _tile(tm, D), _full((8, D)), _full((8, D)), ANY],
        out_specs=[_tile(tm, D), _tile(tm, D), _tile(tm, D), _full((16, D))],
        out_shape=[jax.ShapeDtypeStruct((t, D), BF16)] * 3 + [jax.ShapeDtypeStruct((16, D), F32)],
        scratch_shapes=[pltpu.VMEM((NDEV, ROWS_OUT, D), BF16), pltpu.SemaphoreType.DMA((1,))],
        compiler_params=_cparams("arbitrary"),
    )(dxo, y, z2, vec, sm, gw)


def _mix_odd_bwd2(dz2, u, x, dxo, vec, wdw, gw, layer, tm, comm=None):
    t = x.shape[0]
    n = t // tm
    e = tm + 2 * HALO

    def body(dz_ref, dzp_ref, dzn_ref, u_ref, x_ref, dxo_ref, vec_ref, wdw_ref, g_ref,
             dxi_ref, du_ref, h_ref, sums_ref, dw_ref, w_s, de, zsh, sems):
        i = pl.program_id(0)

        @pl.when(i == 0)
        def _():
            _load_weights(g_ref, [(OFF_IN, ROWS_IN, w_s)], sems)
            sums_ref[...] = jnp.zeros_like(sums_ref)
            dw_ref[...] = jnp.zeros_like(dw_ref)

        de[0:HALO, :] = jnp.where(i > 0, dzp_ref[...].astype(F32), 0.0)
        de[HALO:HALO + tm, :] = dz_ref[...].astype(F32)
        de[HALO + tm:e, :] = jnp.where(i < n - 1, dzn_ref[...].astype(F32), 0.0)
        a = u_ref[:, 0:D].astype(F32)
        gg = u_ref[:, D:2 * D].astype(F32)
        sg = _sig(gg)
        z = a * sg
        _shifted_copies(zsh, de, tm)
        dz = None
        for k in range(CONF_K):
            shifted = _shifted(zsh, CONF_K - k, tm)
            term = wdw_ref[k:k + 1] * shifted
            dz = term if dz is None else dz + term
            dw_ref[k:k + 1, :] += _rowsum(z * shifted)
        da = dz * sg
        dg = dz * a * (sg * (1.0 - sg))
        sums_ref[8:9, :] += _rowsum(da)
        sums_ref[9:10, :] += _rowsum(dg)
        du = jnp.concatenate([da, dg], axis=-1).astype(BF16)
        du_ref[...] = du
        dh = _nn(du, w_s[...].reshape(2 * D, D))
        g, sc, sh = vec_ref[0:1], vec_ref[1:2], vec_ref[2:3]
        h, nrm, r = _norm_fwd(x_ref[...], g, sc, sh)
        h_ref[...] = h.astype(BF16)
        dxi_ref[...] = dxo_ref[...] + _norm_bwd(dh, nrm, r, g, sc)
        sums_ref[1:2, :] += _rowsum(dh)
        sums_ref[2:3, :] += _rowsum(dh * nrm)

        @pl.when(i == n - 1)
        def _():
            p = sums_ref[2:3, :]
            sums_ref[3:4, :] = p * (1.0 + sc)
            sums_ref[2:3, :] = p * g

    prev_d, nxt_d = _halo_specs(tm, D, t)
    return _host_call(
        body, comm, name=f"mix_odd_bwd2_l{layer}", grid=(n,),
        in_specs=[_tile(tm, D), prev_d, nxt_d, _tile(tm, 2 * D), _tile(tm, D), _tile(tm, D),
                  _full((8, D)), _full((32, D)), ANY],
        out_specs=[_tile(tm, D), _tile(tm, 2 * D), _tile(tm, D), _full((16, D)), _full((32, D))],
        out_shape=[jax.ShapeDtypeStruct((t, D), F32), jax.ShapeDtypeStruct((t, 2 * D), BF16),
                   jax.ShapeDtypeStruct((t, D), BF16), jax.ShapeDtypeStruct((16, D), F32),
                   jax.ShapeDtypeStruct((32, D), F32)],
        scratch_shapes=[pltpu.VMEM((NDEV, ROWS_IN, D), BF16), pltpu.VMEM((e, D), F32),
                        pltpu.VMEM((8, tm + SHIFT_ROWS, D), F32),
                        pltpu.SemaphoreType.DMA((1,))],
        operands=(dz2, dz2, dz2, u, x, dxo, vec, wdw, gw))


FCH = FF // 2


def _loss_head(x, tgt, g):
    r = lax.rsqrt(_lanemean(x * x) + RMS_EPS)
    nrm = x * r
    err = nrm * g - tgt
    dout = err * (1.0 / D)
    dn = dout * g
    return r * (dn - nrm * _lanemean(dn * nrm)), _rowsum(err * err) * (0.5 / D), _rowsum(dout * nrm)


def _ffn_fwd(x, vec, wsrc, layer, tm, comm=None, head=None):
    t = x.shape[0]
    g_a, off_g, g_b, off_u, off_d = wsrc

    def body(*refs):
        if head is None:
            x_ref, vec_ref, ga_ref, gb_ref, xo_ref, a_ref, b_ref, y_ref, wg_s, wu_s, wd_s, s_s, sems = refs
        else:
            (x_ref, vec_ref, ga_ref, gb_ref, t_ref, gf_ref, xo_ref, a_ref, b_ref, y_ref, fsum_ref,
             wg_s, wu_s, wd_s, s_s, sems) = refs

        @pl.when(pl.program_id(0) == 0)
        def _():
            _load_weight_rows([(ga_ref, off_g, FS, wg_s), (gb_ref, off_u, FS, wu_s), (gb_ref, off_d, FS, wd_s)], sems)
            if head is not None:
                fsum_ref[...] = jnp.zeros_like(fsum_ref)

        xv = x_ref[...]
        h, _, _ = _norm_fwd(xv, vec_ref[0:1], vec_ref[1:2], vec_ref[2:3])
        hb = h.astype(BF16)
        for ch in range(FF // MXU_N):
            rows = slice(ch * MXU_N, (ch + 1) * MXU_N)
            a = _nt(hb, wg_s[rows, :])
            b = _nt(hb, wu_s[rows, :])
            a_ref[:, rows] = a.astype(BF16)
            b_ref[:, rows] = b.astype(BF16)
            s_s[:, rows] = ((a * _sig(a)) * b).astype(BF16)
        y = _nn(s_s[...], wd_s[...])
        y_ref[...] = y.astype(BF16)
        x_out = xv + vec_ref[3:4] * y
        if head is None:
            xo_ref[...] = x_out
        else:
            dx, loss_row, dg_row = _loss_head(x_out, t_ref[...], gf_ref[...])
            xo_ref[...] = dx
            fsum_ref[0:1, :] += dg_row
            fsum_ref[1:2, :] += loss_row

    wsc = pltpu.VMEM((FF, D), BF16)
    in_specs = [_tile(tm, D), _full((8, D)), ANY, ANY]
    out_specs = [_tile(tm, D), _tile(tm, FF), _tile(tm, FF), _tile(tm, D)]
    out_shape = [jax.ShapeDtypeStruct((t, D), F32), jax.ShapeDtypeStruct((t, FF), BF16),
                 jax.ShapeDtypeStruct((t, FF), BF16), jax.ShapeDtypeStruct((t, D), BF16)]
    operands = (x, vec, g_a, g_b)
    if head is not None:
        in_specs += [_tile(tm, D), _full((1, D))]
        out_specs.append(_full((8, D)))
        out_shape.append(jax.ShapeDtypeStruct((8, D), F32))
        operands += tuple(head)
    return _host_call(
        body, comm, name=f"ffn_fwd_l{layer}", grid=(t // tm,),
        in_specs=in_specs, out_specs=out_specs, out_shape=out_shape,
        scratch_shapes=[wsc, wsc, wsc, pltpu.VMEM((tm, FF), BF16), pltpu.SemaphoreType.DMA((3 * NDEV,))],
        operands=operands)


def _ffn_bwd(dxo, x, y, a, b, vec, wsrc, layer, tm, comm=None):
    t = x.shape[0]
    n = t // tm
    g_a, off_g, g_b, off_u, off_d = wsrc

    def body(dxo_ref, x_ref, y_ref, a_ref, b_ref, vec_ref, ga_ref, gb_ref,
             dxi_ref, h_ref, dy_ref, s_ref, da_ref, db_ref, sums_ref, wg_s, wu_s, wd_s, sems):
        i = pl.program_id(0)

        @pl.when(i == 0)
        def _():
            _load_weight_rows([(ga_ref, off_g, FS, wg_s), (gb_ref, off_u, FS, wu_s), (gb_ref, off_d, FS, wd_s)], sems)
            sums_ref[...] = jnp.zeros_like(sums_ref)

        dxo_m = dxo_ref[...]
        sums_ref[0:1, :] += _rowsum(dxo_m * y_ref[...].astype(F32))
        dyb = (vec_ref[3:4] * dxo_m).astype(BF16)
        dy_ref[...] = dyb
        for ch in range(FF // MXU_N):
            cols = slice(ch * MXU_N, (ch + 1) * MXU_N)
            ds = _nt(dyb, wd_s[cols, :]).astype(BF16)
            av = a_ref[:, cols]
            bv = b_ref[:, cols]
            sg = _sig(av)
            sl = av * sg
            s_ref[:, cols] = sl * bv
            db_ref[:, cols] = ds * sl
            da_ref[:, cols] = (ds * bv) * (sg * (1.0 + av * (1.0 - sg)))
        dh = _nn(da_ref[...], wg_s[...]) + _nn(db_ref[...], wu_s[...])
        g, sc, sh = vec_ref[0:1], vec_ref[1:2], vec_ref[2:3]
        h, nrm, r = _norm_fwd(x_ref[...], g, sc, sh)
        h_ref[...] = h.astype(BF16)
        dxi_ref[...] = dxo_m + _norm_bwd(dh, nrm, r, g, sc)
        sums_ref[1:2, :] += _rowsum(dh)
        sums_ref[2:3, :] += _rowsum(dh * nrm)

        @pl.when(i == n - 1)
        def _():
            p = sums_ref[2:3, :]
            sums_ref[3:4, :] = p * (1.0 + sc)
            sums_ref[2:3, :] = p * g

    wsc = pltpu.VMEM((FF, D), BF16)
    big, small = jax.ShapeDtypeStruct((t, FF), BF16), jax.ShapeDtypeStruct((t, D), BF16)
    return _host_call(
        body, comm, name=f"ffn_bwd_l{layer}", grid=(n,),
        in_specs=[_tile(tm, D), _tile(tm, D), _tile(tm, D), _tile(tm, FF), _tile(tm, FF), _full((8, D)), ANY, ANY],
        out_specs=[_tile(tm, D), _tile(tm, D), _tile(tm, D), _tile(tm, FF), _tile(tm, FF), _tile(tm, FF),
                   _full((8, D))],
        out_shape=[jax.ShapeDtypeStruct((t, D), F32), small, small, big, big, big, jax.ShapeDtypeStruct((8, D), F32)],
        scratch_shapes=[wsc, wsc, wsc, pltpu.SemaphoreType.DMA((3 * NDEV,))],
        operands=(dxo, x, y, a, b, vec, g_a, g_b))


def _wgrad(lhs, rhs, name, tk):
    t, m = lhs.shape
    n = t // tk

    def body(l_ref, r_ref, o_ref, acc):
        i = pl.program_id(0)

        @pl.when(i == 0)
        def _():
            acc[...] = jnp.zeros_like(acc)

        acc[...] += _tn(l_ref[...], r_ref[...])

        @pl.when(i == n - 1)
        def _():
            o_ref[...] = acc[...].astype(BF16)

    return pl.pallas_call(
        body, name=name, grid=(n,),
        in_specs=[_tile(tk, m), _tile(tk, D)], out_specs=_full((m, D)),
        out_shape=jax.ShapeDtypeStruct((m, D), BF16),
        scratch_shapes=[pltpu.VMEM((m, D), F32)],
        compiler_params=_cparams("arbitrary"),
    )(lhs, rhs)


ADAM_ROWS = LROWS // 5


def _adam_big(recv, w, m, v):
    def body(r_ref, w_ref, m_ref, v_ref, g_ref, d_ref, mo_ref, vo_ref):
        g = r_ref[0, 0].astype(F32)
        for s in range(1, NDEV):
            g = g + r_ref[s, 0].astype(F32)
        delta, m2, v2 = _adam(w_ref[0], g, m_ref[0], v_ref[0])
        g_ref[0], d_ref[0], mo_ref[0], vo_ref[0] = g, delta, m2, v2

    blk = pl.BlockSpec((1, ADAM_ROWS, D), lambda l, j: (l, j, 0))
    sds = jax.ShapeDtypeStruct(w.shape, F32)
    return pl.pallas_call(
        body, name="adam_big", grid=(DEPTH, LROWS // ADAM_ROWS),
        in_specs=[pl.BlockSpec((NDEV, 1, ADAM_ROWS, D), lambda l, j: (0, l, j, 0)), blk, blk, blk],
        out_specs=[blk] * 4, out_shape=[sds] * 4,
        compiler_params=_cparams("arbitrary", "arbitrary"),
    )(recv, w, m, v)


def _sum_small(gathered, rows):
    def body(g_ref, o_ref):
        acc = g_ref[0:rows, :]
        for s in range(1, NDEV):
            acc = acc + g_ref[s * rows:(s + 1) * rows, :]
        o_ref[...] = acc

    return pl.pallas_call(
        body, name="sum_small",
        in_specs=[pl.BlockSpec(memory_space=pltpu.VMEM)], out_specs=pl.BlockSpec(memory_space=pltpu.VMEM),
        out_shape=jax.ShapeDtypeStruct((rows, D), F32),
        compiler_params=pltpu.CompilerParams(vmem_limit_bytes=VMEM_LIMIT),
    )(gathered)


def _adam_small(params):
    k = len(params)

    def body(*refs):
        ins, outs = refs[:4 * k], refs[4 * k:]
        for j in range(k):
            w_ref, g_ref, m_ref, v_ref = ins[4 * j:4 * j + 4]
            delta, m2, v2 = _adam(w_ref[...], g_ref[...], m_ref[...], v_ref[...])
            outs[3 * j][...], outs[3 * j + 1][...], outs[3 * j + 2][...] = delta, m2, v2

    flat = [a for p in params for a in p]
    shapes = [jax.ShapeDtypeStruct(p[0].shape, F32) for p in params for _ in range(3)]
    vm = pl.BlockSpec(memory_space=pltpu.VMEM)
    res = pl.pallas_call(
        body, name="adam_small", in_specs=[vm] * len(flat), out_specs=[vm] * len(shapes), out_shape=shapes,
        compiler_params=pltpu.CompilerParams(vmem_limit_bytes=VMEM_LIMIT),
    )(*flat)
    return [tuple(res[3 * j:3 * j + 3]) for j in range(k)]


def _pack(ab_in, ab_out, pw1, pw2, wg, wu, wd):
    ins = jnp.swapaxes(jnp.stack([ab_in[0], pw1[0], ab_in[1], pw1[1]]), 1, 2)
    outs = jnp.stack([ab_out[0], pw2[0], ab_out[1], pw2[1]])
    return jnp.concatenate([ins, outs, jnp.swapaxes(wg, 1, 2), jnp.swapaxes(wu, 1, 2), wd], axis=1)


def _unpack(p):
    ins = jnp.swapaxes(p[:, OFF_IN:OFF_OUT], 1, 2)
    outs = p[:, OFF_OUT:OFF_G]
    return (ins[0::2], outs[0::2], ins[1::2], outs[1::2], jnp.swapaxes(p[:, OFF_G:OFF_U], 1, 2),
            jnp.swapaxes(p[:, OFF_U:OFF_D], 1, 2), p[:, OFF_D:LROWS])


def _unshard(flat, lead, per):
    k = len(lead)
    a = flat.reshape((NDEV,) + tuple(lead) + (per,))
    a = jnp.transpose(a, tuple(range(1, k + 1)) + (0, k + 1))
    return a.reshape(tuple(lead) + (NDEV * per,))


def _rows_of(a):
    f = a.reshape(-1)
    pad = (-f.shape[0]) % D
    if pad:
        f = jnp.concatenate([f, jnp.zeros((pad,), f.dtype)])
    return f.reshape(-1, D)


def _pad_rows(a, rows):
    return jnp.concatenate([a, jnp.zeros((rows - a.shape[0],) + a.shape[1:], a.dtype)], axis=0)


def kernel(x, c, norm_mix_g, norm_ffn_g, w_mod, b_mod, ab_w_in, ab_conv, ab_w_pool, ab_pool_scale, ab_w_out, cf_w_pw1, cf_b_pw1, cf_w_dw, cf_b_dw, cf_ln_g, cf_ln_b, cf_w_pw2, cf_b_pw2, ffn_w_gate, ffn_w_up, ffn_w_down, final_norm_g, loss_target, m_norm_mix_g, m_norm_ffn_g, m_w_mod, m_b_mod, m_ab_w_in, m_ab_conv, m_ab_w_pool, m_ab_pool_scale, m_ab_w_out, m_cf_w_pw1, m_cf_b_pw1, m_cf_w_dw, m_cf_b_dw, m_cf_ln_g, m_cf_ln_b, m_cf_w_pw2, m_cf_b_pw2, m_ffn_w_gate, m_ffn_w_up, m_ffn_w_down, m_final_norm_g, v_norm_mix_g, v_norm_ffn_g, v_w_mod, v_b_mod, v_ab_w_in, v_ab_conv, v_ab_w_pool, v_ab_pool_scale, v_ab_w_out, v_cf_w_pw1, v_cf_b_pw1, v_cf_w_dw, v_cf_b_dw, v_cf_ln_g, v_cf_ln_b, v_cf_w_pw2, v_cf_b_pw2, v_ffn_w_gate, v_ffn_w_up, v_ffn_w_down, v_final_norm_g):
    t = x.shape[1]
    tm = 512 if t % 512 == 0 else t // 2
    tk = 1024 if t % 1024 == 0 else t // 2
    tmo = tm
    tmb = tm // 2
    me = 4 * lax.axis_index("x") + 2 * lax.axis_index("y") + lax.axis_index("c")
    xs, tgt = x[0], loss_target[0]

    w_pack = _pack(ab_w_in, ab_w_out, cf_w_pw1, cf_w_pw2, ffn_w_gate, ffn_w_up, ffn_w_down)
    p16 = w_pack.astype(BF16)

    sharded = [ab_conv, cf_b_pw1, cf_w_dw, cf_b_dw, cf_ln_g, cf_ln_b, cf_b_pw2]
    flat = jnp.concatenate([a.reshape(-1) for a in sharded])
    n_flat = flat.shape[0]
    g1, g_io0 = _head_comm(jnp.concatenate([_pad_rows(c, 8), _pad_rows(_rows_of(flat), 16)], axis=0),
                           p16[0, OFF_IN:OFF_G])
    c_all = g1[:, 0, :]
    flat_all = g1[:, 8:, :].reshape(NDEV, -1)[:, :n_flat]
    full, o = [], 0
    for a in sharded:
        lead, per = a.shape[:-1], a.shape[-1]
        size = a.size
        full.append(_unshard(flat_all[:, o:o + size], lead, per))
        o += size
    ab_conv_f, b_pw1_f, w_dw_f, b_dw_f, ln_g_f, ln_b_f, b_pw2_f = full

    b_sl = lax.dynamic_slice_in_dim(b_mod, me * MODW, MODW, axis=1).reshape(DEPTH, 1, MODW)
    mod_part, c_act = _mod_fwd(c_all, w_mod, b_sl)
    g2 = _gather_small(mod_part, "gather_mod").reshape(NDEV, NDEV, DEPTH, MODW)
    mod = jnp.transpose(lax.dynamic_index_in_dim(g2, me, axis=1, keepdims=False), (1, 0, 2)).reshape(DEPTH, N_MOD, D)
    zeros4 = jnp.zeros((4, D), F32)

    def vec_of(g, layer, k):
        return jnp.concatenate([g[layer][None], mod[layer, k + 1][None], mod[layer, k][None],
                                mod[layer, k + 2][None], zeros4], axis=0)

    vmix = [vec_of(norm_mix_g, l, 0) for l in range(DEPTH)]
    vffn = [vec_of(norm_ffn_g, l, 3) for l in range(DEPTH)]

    wp16 = ab_w_pool.astype(BF16)
    wdw32 = [_pad_rows(w_dw_f[i], 32) for i in range(2)]
    sm_odd = [jnp.concatenate([b_dw_f[i][None], ln_g_f[i][None], ln_b_f[i][None], b_pw2_f[i][None], zeros4], axis=0)
              for i in range(2)]
    zero_bias = jnp.zeros((1, 2 * D), F32)

    saved = []
    xc = xs
    gw = [g_io0]
    wsrc = []
    for l in range(DEPTH):
        i = l // 2
        if l == 0:
            u, g_g0 = _inproj(xc, vmix[l], zero_bias, gw[l], l, tm, _gather_comm(p16[0, OFF_G:OFF_U]))
            x_mid, y_mix, g_ud0 = _mix_even_fwd(u, xc, vmix[l], ab_conv_f[i], wp16[i], ab_pool_scale[i][None], gw[l],
                                                l, tm, _gather_comm(p16[0, OFF_U:LROWS]))
            wsrc.append((g_g0, 0, g_ud0, 0, FS))
            z2 = None
        elif l % 2 == 0:
            u = _inproj(xc, vmix[l], zero_bias, gw[l], l, tm)
            x_mid, y_mix = _mix_even_fwd(u, xc, vmix[l], ab_conv_f[i], wp16[i], ab_pool_scale[i][None], gw[l], l, tm)
            z2 = None
        else:
            u = _inproj(xc, vmix[l], b_pw1_f[i][None], gw[l], l, tm)
            x_mid, y_mix, z2 = _mix_odd_fwd(u, xc, vmix[l], wdw32[i], sm_odd[i], gw[l], l, tmo)
        if l + 1 < DEPTH:
            x_out, a, b, y_ffn, g_next = _ffn_fwd(x_mid, vffn[l], wsrc[l], l, tm, _gather_comm(p16[l + 1]))
            gw.append(g_next)
            wsrc.append((g_next, OFF_G, g_next, OFF_U, OFF_D))
        else:
            x_out = None
            dx, a, b, y_ffn, fsum = _ffn_fwd(x_mid, vffn[l], wsrc[l], l, tm, head=(tgt, final_norm_g[None]))
        saved.append((xc, u, y_mix, z2, x_mid, a, b, y_ffn))
        xc = x_out

    loss = lax.psum(jnp.sum(fsum[1]), ("x", "y", "c"))
    d_final_g = fsum[0]

    recv = _empty_recv()
    late_specs = [(OFF_U, FS), (OFF_IN, ROWS_IN), (OFF_OUT, ROWS_OUT)]
    pending = None
    dmod = [None] * DEPTH
    d_mix_g, d_ffn_g = [None] * DEPTH, [None] * DEPTH
    d_conv, d_pool, d_pscale = [None] * 2, [None] * 2, [None] * 2
    d_bpw1, d_wdw, d_bdw, d_lng, d_lnb, d_bpw2 = ([None] * 2 for _ in range(6))
    for l in reversed(range(DEPTH)):
        i = l // 2
        x_in, u, y_mix, z2, x_mid, a, b, y_ffn = saved[l]
        if pending is None:
            dx_mid, h2, dy, s, da, db, s_f = _ffn_bwd(dx, x_mid, y_ffn, a, b, vffn[l], wsrc[l], l, tmb)
        else:
            dx_mid, h2, dy, s, da, db, s_f, recv = _ffn_bwd(dx, x_mid, y_ffn, a, b, vffn[l], wsrc[l], l, tmb,
                                                            _scatter_comm(pending, late_specs, recv, l + 1))
        g_down = _wgrad(s, dy, f"wgrad_down_l{l}", tk)
        g_gate = _wgrad(da, h2, f"wgrad_gate_l{l}", tk)
        gu_comm = _scatter_comm([g_down, g_gate], [(OFF_D, FS), (OFF_G, FS)], recv, l)
        g_up = _wgrad(db, h2, f"wgrad_up_l{l}", tk)
        d_ffn_g[l] = s_f[3]
        mod_ffn = [s_f[1], s_f[2], s_f[0]]
        if l % 2 == 0:
            dx, du, h, cat, dym, s_m, dwp, recv = _mix_even_bwd(dx_mid, u, x_in, y_mix, vmix[l], ab_conv_f[i], wp16[i],
                                                                ab_pool_scale[i][None], gw[l], l, tm, gu_comm)
            g_out = _wgrad(cat, dym, f"wgrad_out_l{l}", tk)
            d_conv[i], d_pool[i], d_pscale[i] = s_m[4:7, :DA], dwp, s_m[7, :DA]
            mod_mix = [s_m[1], s_m[2], s_m[0]]
            d_mix_g[l] = s_m[3]
        else:
            dym, z3, dz2, s_1 = _mix_odd_bwd1(dx_mid, y_mix, z2, vmix[l], sm_odd[i], gw[l], l, tm)
            dx, du, h, s_2, dwdw, recv = _mix_odd_bwd2(dz2, u, x_in, dx_mid, vmix[l], wdw32[i], gw[l], l, tmo, gu_comm)
            g_out = _wgrad(z3, dym, f"wgrad_out_l{l}", tk)
            d_bpw1[i], d_wdw[i], d_bdw[i] = s_2[8:10].reshape(2 * D), dwdw[:CONF_K], s_1[7]
            d_lng[i], d_lnb[i], d_bpw2[i] = s_1[5], s_1[6], s_1[4]
            mod_mix = [s_2[1], s_2[2], s_1[0]]
            d_mix_g[l] = s_2[3]
        dmod[l] = jnp.stack(mod_mix + mod_ffn)
        pending = [g_up, _wgrad(du, h, f"wgrad_in_l{l}", tk), g_out]
    grad_x = dx[None]

    small = [jnp.stack(dmod), jnp.stack(d_mix_g), jnp.stack(d_ffn_g), d_final_g, jnp.stack(d_pool),
             jnp.stack(d_pscale), jnp.stack(d_conv), jnp.stack(d_bpw1), jnp.stack(d_wdw), jnp.stack(d_bdw),
             jnp.stack(d_lng), jnp.stack(d_lnb), jnp.stack(d_bpw2)]
    small_rows = [_rows_of(a) for a in small]
    n_rows = sum(a.shape[0] for a in small_rows)
    pad_rows = -(-n_rows // 8) * 8
    recv, g3 = _tail_comm(pending, late_specs, recv, 0, _pad_rows(jnp.concatenate(small_rows, axis=0), pad_rows))
    g3 = g3.reshape(NDEV * pad_rows, D)
    summed = _sum_small(g3, pad_rows)

    m_pack = _pack(m_ab_w_in, m_ab_w_out, m_cf_w_pw1, m_cf_w_pw2, m_ffn_w_gate, m_ffn_w_up, m_ffn_w_down)
    v_pack = _pack(v_ab_w_in, v_ab_w_out, v_cf_w_pw1, v_cf_w_pw2, v_ffn_w_gate, v_ffn_w_up, v_ffn_w_down)
    big = [_unpack(p) for p in _adam_big(recv, w_pack, m_pack, v_pack)]

    outs, o = [], 0
    for a, r in zip(small, small_rows):
        outs.append(summed[o:o + r.shape[0]].reshape(-1)[:a.size].reshape(a.shape))
        o += r.shape[0]
    (g_bmod, g_mix_g, g_ffn_g, g_final, g_pool, g_pscale, g_conv, g_bpw1, g_wdw, g_bdw, g_lng, g_lnb, g_bpw2) = outs
    g_bmod = g_bmod.reshape(DEPTH, N_MOD * D)

    def my_shard(a):
        per = a.shape[-1] // NDEV
        return lax.dynamic_slice_in_dim(a, me * per, per, axis=a.ndim - 1)

    g_conv, g_bpw1, g_wdw, g_bdw, g_lng, g_lnb, g_bpw2 = [
        my_shard(a) for a in (g_conv, g_bpw1, g_wdw, g_bdw, g_lng, g_lnb, g_bpw2)]

    dmod_all = g3.reshape(NDEV, pad_rows, D)[:, :DEPTH * N_MOD, :].reshape(NDEV, DEPTH, N_MOD * D)
    dmod_mine = jnp.transpose(lax.dynamic_slice_in_dim(dmod_all, me * MODW, MODW, axis=2), (1, 0, 2))
    g_wmod, d_wmod, nm_wmod, nv_wmod = _mod_bwd_adam(c_act.T, dmod_mine, w_mod, m_w_mod, v_w_mod)

    small_params = [
        (norm_mix_g, g_mix_g, m_norm_mix_g, v_norm_mix_g), (norm_ffn_g, g_ffn_g, m_norm_ffn_g, v_norm_ffn_g),
        (b_mod, g_bmod, m_b_mod, v_b_mod), (ab_conv, g_conv, m_ab_conv, v_ab_conv),
        (ab_w_pool, g_pool, m_ab_w_pool, v_ab_w_pool), (ab_pool_scale, g_pscale, m_ab_pool_scale, v_ab_pool_scale),
        (cf_b_pw1, g_bpw1, m_cf_b_pw1, v_cf_b_pw1), (cf_w_dw, g_wdw, m_cf_w_dw, v_cf_w_dw),
        (cf_b_dw, g_bdw, m_cf_b_dw, v_cf_b_dw), (cf_ln_g, g_lng, m_cf_ln_g, v_cf_ln_g),
        (cf_ln_b, g_lnb, m_cf_ln_b, v_cf_ln_b), (cf_b_pw2, g_bpw2, m_cf_b_pw2, v_cf_b_pw2),
        (final_norm_g, g_final, m_final_norm_g, v_final_norm_g)]

    def two_d(a):
        return a.reshape(-1, a.shape[-1])

    upd = _adam_small([tuple(two_d(a) for a in p) for p in small_params])
    upd = [tuple(r.reshape(p[0].shape) for r in u) for u, p in zip(upd, small_params)]
    (s_mix, s_ffn, s_bmod, s_conv, s_pool, s_pscale, s_bpw1, s_wdw, s_bdw, s_lng, s_lnb, s_bpw2, s_final) = upd
    small_g = [p[1] for p in small_params]
    (q_mix, q_ffn, q_bmod, q_conv, q_pool, q_pscale, q_bpw1, q_wdw, q_bdw, q_lng, q_lnb, q_bpw2, q_final) = small_g

    def ordered(k):
        ab_in, ab_out, pw1, pw2, wg, wu, wd = big[k]
        if k == 0:
            sm = dict(mix=q_mix, ffn=q_ffn, bmod=q_bmod, conv=q_conv, pool=q_pool, pscale=q_pscale, bpw1=q_bpw1,
                      wdw=q_wdw, bdw=q_bdw, lng=q_lng, lnb=q_lnb, bpw2=q_bpw2, final=q_final)
            wmod = g_wmod
        else:
            j = k - 1
            sm = dict(mix=s_mix[j], ffn=s_ffn[j], bmod=s_bmod[j], conv=s_conv[j], pool=s_pool[j], pscale=s_pscale[j],
                      bpw1=s_bpw1[j], wdw=s_wdw[j], bdw=s_bdw[j], lng=s_lng[j], lnb=s_lnb[j], bpw2=s_bpw2[j],
                      final=s_final[j])
            wmod = (d_wmod, nm_wmod, nv_wmod)[j]
        return [sm["mix"], sm["ffn"], wmod, sm["bmod"], ab_in, sm["conv"], sm["pool"], sm["pscale"], ab_out,
                pw1, sm["bpw1"], sm["wdw"], sm["bdw"], sm["lng"], sm["lnb"], pw2, sm["bpw2"], wg, wu, wd, sm["final"]]

    return (loss, grad_x, *ordered(0), *ordered(1), *ordered(2), *ordered(3))
```
